```python
import math
import jax, jax.numpy as jnp
from jax import lax
import numpy as np

D_MODEL = 1024
BATCH = 2
SEQ = 8192
DEPTH = 1
DEC_BATCH = 128
DEC_SEQ = 4
PAST_LEN = 16384
PAGE_SIZE = 128

HEAD_DIM = 64
N_HEADS = D_MODEL // 128
N_KV_HEADS = N_HEADS // 4
GQA_GROUP = N_HEADS // N_KV_HEADS
WINDOW = 128
ATTN_WIDTH = N_HEADS * HEAD_DIM
KV_WIDTH = N_KV_HEADS * HEAD_DIM
ROPE_THETA = 10000.0
SSM_WIDTH = D_MODEL // 2
SSM_GROUP_CH = 16
SSM_GROUPS = SSM_WIDTH // SSM_GROUP_CH
SSM_STATE = 64
D_FF = ((8 * D_MODEL // 3 + 127) // 128) * 128
IN_COLS = ATTN_WIDTH + 2 * KV_WIDTH + SSM_WIDTH + 2 * D_MODEL
SPLIT_POINTS = [ATTN_WIDTH, ATTN_WIDTH + KV_WIDTH, ATTN_WIDTH + 2 * KV_WIDTH,
                ATTN_WIDTH + 2 * KV_WIDTH + SSM_WIDTH,
                ATTN_WIDTH + 2 * KV_WIDTH + SSM_WIDTH + D_MODEL]
RMS_EPS = 1e-6
MASK_VALUE = -1e30

kernel_name = "macaron_griffin_swa_s5_decode_step"


def rms_norm(x, g):
    xf = x.astype(jnp.float32)
    y = xf * lax.rsqrt(jnp.mean(xf * xf, axis=-1, keepdims=True) + RMS_EPS) * g.astype(jnp.float32)
    return y.astype(x.dtype)


def swiglu(x, w1, w3, w2):
    return (jax.nn.silu(x @ w1) * (x @ w3)) @ w2


def rope(x, pos):
    half = HEAD_DIM // 2
    inv = ROPE_THETA ** (-2.0 * jnp.arange(half, dtype=jnp.float32) / HEAD_DIM)
    ang = pos[:, None] * inv[None, :]
    cos = jnp.cos(ang)[:, None, :]
    sin = jnp.sin(ang)[:, None, :]
    xf = x.astype(jnp.float32)
    x1, x2 = xf[..., :half], xf[..., half:]
    return jnp.concatenate([x1 * cos - x2 * sin, x2 * cos + x1 * sin], axis=-1).astype(x.dtype)


def sink_attend(scores, mask, sinks, v, av_eq):
    sink = sinks.astype(jnp.float32).reshape(N_KV_HEADS, GQA_GROUP, 1, 1)
    s = jnp.where(mask, scores, MASK_VALUE)
    m = jnp.maximum(jnp.max(s, axis=-1, keepdims=True), sink)
    p = jnp.exp(s - m)
    denom = jnp.sum(p, axis=-1, keepdims=True) + jnp.exp(sink - m)
    probs = (p / denom).astype(v.dtype)
    return jnp.einsum(av_eq, probs, v)


def swa_prompt(q, k, v, sinks):
    B, L = q.shape[:2]
    nb = L // WINDOW
    qb = q.reshape(B, nb, WINDOW, N_KV_HEADS, GQA_GROUP, HEAD_DIM)
    pad = ((0, 0), (WINDOW, 0), (0, 0), (0, 0))
    kp = jnp.pad(k, pad).reshape(B, nb + 1, WINDOW, N_KV_HEADS, HEAD_DIM)
    vp = jnp.pad(v, pad).reshape(B, nb + 1, WINDOW, N_KV_HEADS, HEAD_DIM)
    kb = jnp.concatenate([kp[:, :-1], kp[:, 1:]], axis=2)
    vb = jnp.concatenate([vp[:, :-1], vp[:, 1:]], axis=2)
    scores = jnp.einsum('bnqkgd,bnskd->bnkgqs', qb, kb).astype(jnp.float32)
    qi = jnp.arange(WINDOW)[:, None]
    sj = jnp.arange(2 * WINDOW)[None, :]
    diff = WINDOW + qi - sj
    kpos = (jnp.arange(nb)[:, None, None] - 1) * WINDOW + sj[None]
    mask = (diff >= 0)[None] & (diff < WINDOW)[None] & (kpos >= 0)
    mask = mask[None, :, None, None, :, :]
    out = sink_attend(scores, mask, sinks, vb, 'bnkgqs,bnskd->bnqkgd')
    return out.reshape(B, L, ATTN_WIDTH)


def swa_sample(q, k_new, v_new, k_buf, v_buf, sinks):
    B, T = q.shape[:2]
    kk = jnp.concatenate([k_buf.astype(k_new.dtype), k_new], axis=1)
    vv = jnp.concatenate([v_buf.astype(v_new.dtype), v_new], axis=1)
    qg = q.reshape(B, T, N_KV_HEADS, GQA_GROUP, HEAD_DIM)
    scores = jnp.einsum('btkgd,bskd->bkgts', qg, kk).astype(jnp.float32)
    diff = WINDOW + jnp.arange(T)[:, None] - jnp.arange(WINDOW + T)[None, :]
    mask = (diff >= 0) & (diff < WINDOW)
    out = sink_attend(scores, mask, sinks, vv, 'bkgts,bskd->btkgd')
    return out.reshape(B, T, ATTN_WIDTH), kk[:, T:], vv[:, T:]


def _ssm_combine(e1, e2):
    a1r, a1i, b1r, b1i = e1
    a2r, a2i, b2r, b2i = e2
    return (a2r * a1r - a2i * a1i,
            a2r * a1i + a2i * a1r,
            a2r * b1r - a2i * b1i + b2r,
            a2r * b1i + a2i * b1r + b2i)


def s5_scan(u, x0_re, x0_im, a_re, a_im, log_dt, b_re, b_im, c_re, c_im, d_skip):
    f32 = jnp.float32
    u = u.astype(f32)
    a_re, a_im = a_re.astype(f32), a_im.astype(f32)
    dt = jnp.exp(log_dt.astype(f32))[:, None]
    mag = jnp.exp(dt * a_re)
    ab_re, ab_im = mag * jnp.cos(dt * a_im), mag * jnp.sin(dt * a_im)
    den = a_re * a_re + a_im * a_im
    nr, ni = ab_re - 1.0, ab_im
    f_re = (nr * a_re + ni * a_im) / den
    f_im = (ni * a_re - nr * a_im) / den
    b_re, b_im = b_re.astype(f32), b_im.astype(f32)
    bb_re = f_re[..., None] * b_re - f_im[..., None] * b_im
    bb_im = f_re[..., None] * b_im + f_im[..., None] * b_re
    bu_re = jnp.einsum('btgc,gnc->btgn', u, bb_re)
    bu_im = jnp.einsum('btgc,gnc->btgn', u, bb_im)
    ar = jnp.broadcast_to(ab_re, bu_re.shape)
    ai = jnp.broadcast_to(ab_im, bu_re.shape)
    ac_re, ac_im, xs_re, xs_im = lax.associative_scan(_ssm_combine, (ar, ai, bu_re, bu_im), axis=1)
    if x0_re is not None:
        x0r = x0_re.astype(f32)[:, None]
        x0i = x0_im.astype(f32)[:, None]
        xs_re = xs_re + ac_re * x0r - ac_im * x0i
        xs_im = xs_im + ac_re * x0i + ac_im * x0r
    y = (jnp.einsum('btgn,gcn->btgc', xs_re, c_re.astype(f32))
         - jnp.einsum('btgn,gcn->btgc', xs_im, c_im.astype(f32))
         + d_skip.astype(f32) * u)
    return y, xs_re[:, -1], xs_im[:, -1]


def trunk_layer(x, pos0, k_buf, v_buf, s_re, s_im, p):
    B, T, _ = x.shape
    x = x + 0.5 * swiglu(rms_norm(x, p['ffn1_norm']), p['ffn1_w1'], p['ffn1_w3'], p['ffn1_w2'])
    h = rms_norm(x, p['mix_norm'])
    proj = h @ p['w_in']
    q, k, v, u, ga, gs = jnp.split(proj, SPLIT_POINTS, axis=-1)
    q = q.reshape(B, T, N_HEADS, HEAD_DIM)
    k = k.reshape(B, T, N_KV_HEADS, HEAD_DIM)
    v = v.reshape(B, T, N_KV_HEADS, HEAD_DIM)
    pos = (pos0 + jnp.arange(T)).astype(jnp.float32)
    q = rope(rms_norm(q, p['q_norm']), pos) * (HEAD_DIM ** -0.5)
    k = rope(rms_norm(k, p['k_norm']), pos)
    if k_buf is None:
        attn = swa_prompt(q, k, v, p['attn_sinks'])
        new_k, new_v = k[:, T - WINDOW:], v[:, T - WINDOW:]
    else:
        attn, new_k, new_v = swa_sample(q, k, v, k_buf, v_buf, p['attn_sinks'])
    y_ssm, new_re, new_im = s5_scan(u.reshape(B, T, SSM_GROUPS, SSM_GROUP_CH), s_re, s_im,
                                    p['ssm_a_re'], p['ssm_a_im'], p['ssm_log_dt'],
                                    p['ssm_b_re'], p['ssm_b_im'], p['ssm_c_re'], p['ssm_c_im'],
                                    p['ssm_d'])
    z = jax.nn.gelu(y_ssm.reshape(B, T, SSM_WIDTH).astype(x.dtype))
    ssm = z * jax.nn.sigmoid(z @ p['w_glu'] + p['b_glu'])
    merged = (jax.nn.sigmoid(ga) * (attn @ p['w_attn_out'])
              + jax.nn.sigmoid(gs) * (ssm @ p['w_ssm_out']))
    x = x + merged @ p['w_out']
    x = x + 0.5 * swiglu(rms_norm(x, p['ffn2_norm']), p['ffn2_w1'], p['ffn2_w3'], p['ffn2_w2'])
    return x, new_k, new_v, new_re, new_im


def setup_inputs(seed: int = 0) -> dict:
    key = jax.random.key(seed)
    ks = iter(jax.random.split(key, 48))
    f32 = jnp.float32

    def nrm(shape, scale):
        return jax.random.normal(next(ks), shape, f32) * scale

    L, D, G, N, CH = DEPTH, D_MODEL, SSM_GROUPS, SSM_STATE, SSM_GROUP_CH
    return {
        'x_prompt': nrm((BATCH, SEQ, D), 1.0),
        'x_sample': nrm((DEC_BATCH, DEC_SEQ, D), 1.0),
        'cache_k': nrm((L, DEC_BATCH, WINDOW, N_KV_HEADS, HEAD_DIM), 1.0),
        'cache_v': nrm((L, DEC_BATCH, WINDOW, N_KV_HEADS, HEAD_DIM), 1.0),
        'state_ssm_re': nrm((L, DEC_BATCH, G, N), 0.5),
        'state_ssm_im': nrm((L, DEC_BATCH, G, N), 0.5),
        'ffn1_norm': 1.0 + nrm((L, D), 0.02),
        'ffn1_w1': nrm((L, D, D_FF), D ** -0.5),
        'ffn1_w3': nrm((L, D, D_FF), D ** -0.5),
        'ffn1_w2': nrm((L, D_FF, D), D_FF ** -0.5),
        'mix_norm': 1.0 + nrm((L, D), 0.02),
        'w_in': nrm((L, D, IN_COLS), D ** -0.5),
        'q_norm': 1.0 + nrm((L, HEAD_DIM), 0.02),
        'k_norm': 1.0 + nrm((L, HEAD_DIM), 0.02),
        'attn_sinks': nrm((L, N_HEADS), 0.5),
        'w_attn_out': nrm((L, ATTN_WIDTH, D), ATTN_WIDTH ** -0.5),
        'ssm_a_re': -0.5 + nrm((L, G, N), 0.01),
        'ssm_a_im': math.pi * jnp.arange(N, dtype=f32) + nrm((L, G, N), 0.01),
        'ssm_log_dt': jax.random.uniform(next(ks), (L, G), f32, math.log(1e-3), math.log(1e-1)),
        'ssm_b_re': nrm((L, G, N, CH), (2 * CH) ** -0.5),
        'ssm_b_im': nrm((L, G, N, CH), (2 * CH) ** -0.5),
        'ssm_c_re': nrm((L, G, CH, N), (2 * N) ** -0.5),
        'ssm_c_im': nrm((L, G, CH, N), (2 * N) ** -0.5),
        'ssm_d': 1.0 + nrm((L, G, CH), 0.1),
        'w_glu': nrm((L, SSM_WIDTH, SSM_WIDTH), SSM_WIDTH ** -0.5),
        'b_glu': nrm((L, SSM_WIDTH), 0.02),
        'w_ssm_out': nrm((L, SSM_WIDTH, D), SSM_WIDTH ** -0.5),
        'w_out': nrm((L, D, D), D ** -0.5),
        'ffn2_norm': 1.0 + nrm((L, D), 0.02),
        'ffn2_w1': nrm((L, D, D_FF), D ** -0.5),
        'ffn2_w3': nrm((L, D, D_FF), D ** -0.5),
        'ffn2_w2': nrm((L, D_FF, D), D_FF ** -0.5),
    }


def reference(x_prompt, x_sample, cache_k, cache_v, state_ssm_re, state_ssm_im,
              ffn1_norm, ffn1_w1, ffn1_w3, ffn1_w2, mix_norm, w_in, q_norm, k_norm,
              attn_sinks, w_attn_out, ssm_a_re, ssm_a_im, ssm_log_dt, ssm_b_re, ssm_b_im,
              ssm_c_re, ssm_c_im, ssm_d, w_glu, b_glu, w_ssm_out, w_out,
              ffn2_norm, ffn2_w1, ffn2_w3, ffn2_w2):
    xp, xs = x_prompt, x_sample
    kp_l, vp_l, rp_l, ip_l = [], [], [], []
    ks_l, vs_l, rs_l, is_l = [], [], [], []
    for l in range(DEPTH):
        p = dict(ffn1_norm=ffn1_norm[l], ffn1_w1=ffn1_w1[l], ffn1_w3=ffn1_w3[l], ffn1_w2=ffn1_w2[l],
                 mix_norm=mix_norm[l], w_in=w_in[l], q_norm=q_norm[l], k_norm=k_norm[l],
                 attn_sinks=attn_sinks[l], w_attn_out=w_attn_out[l],
                 ssm_a_re=ssm_a_re[l], ssm_a_im=ssm_a_im[l], ssm_log_dt=ssm_log_dt[l],
                 ssm_b_re=ssm_b_re[l], ssm_b_im=ssm_b_im[l], ssm_c_re=ssm_c_re[l], ssm_c_im=ssm_c_im[l],
                 ssm_d=ssm_d[l], w_glu=w_glu[l], b_glu=b_glu[l], w_ssm_out=w_ssm_out[l], w_out=w_out[l],
                 ffn2_norm=ffn2_norm[l], ffn2_w1=ffn2_w1[l], ffn2_w3=ffn2_w3[l], ffn2_w2=ffn2_w2[l])
        xp, kp, vp, rp, ip = trunk_layer(xp, 0, None, None, None, None, p)
        xs, ks_, vs_, rs_, is_ = trunk_layer(xs, PAST_LEN, cache_k[l], cache_v[l],
                                             state_ssm_re[l], state_ssm_im[l], p)
        kp_l.append(kp); vp_l.append(vp); rp_l.append(rp); ip_l.append(ip)
        ks_l.append(ks_); vs_l.append(vs_); rs_l.append(rs_); is_l.append(is_)
    new_k_prompt, new_v_prompt = jnp.stack(kp_l), jnp.stack(vp_l)
    new_re_prompt, new_im_prompt = jnp.stack(rp_l), jnp.stack(ip_l)
    new_k_sample, new_v_sample = jnp.stack(ks_l), jnp.stack(vs_l)
    new_re_sample, new_im_sample = jnp.stack(rs_l), jnp.stack(is_l)
    return (xp, xs, new_k_prompt, new_v_prompt, new_re_prompt, new_im_prompt,
            new_k_sample, new_v_sample, new_re_sample, new_im_sample)
```

```python
import functools
import math

import jax
import jax.numpy as jnp
from jax import lax
from jax.experimental import pallas as pl
from jax.experimental.pallas import tpu as pltpu

F32 = jnp.float32
BF16 = jnp.bfloat16

D_MODEL = 1024
HEAD_DIM = 64
N_HEADS = 8
N_KV = 2
WINDOW = 128
ATTN_W = N_HEADS * HEAD_DIM
KV_W = N_KV * HEAD_DIM
SSM_W = 512
SSM_G = 32
SSM_CH = 16
SSM_N = 64
SSM_S = SSM_G * SSM_N
D_FF = 2816
IN_COLS = ATTN_W + 2 * KV_W + SSM_W + 2 * D_MODEL
ROPE_THETA = 10000.0
RMS_EPS = 1e-6
MASK_VALUE = -1e30
PAST_LEN = 16384

LANES = 128
SUBLANES = 8
MXU_DIM = 256

FRONT_TM = 256
MIX_TM = 256
SCAN_SEG = MIX_TM // SUBLANES
FF_CHUNK = 256
SCAN_LC = 512
SAMPLE_BS = 8
VMEM_LIMIT = 56 * 1024 * 1024


def _const_spec(shape):
    nd = len(shape)
    return pl.BlockSpec(shape, lambda *_: (0,) * nd, pipeline_mode=pl.Buffered(1))


def _rms(x, w):
    ms = jnp.mean(x * x, axis=-1, keepdims=True)
    return x * lax.rsqrt(ms + RMS_EPS) * w


def _swiglu_residual(x, nw_ref, w1_ref, w3_ref, w2_ref):
    h = _rms(x, nw_ref[...]).astype(BF16)
    acc = None
    for c0 in range(0, D_FF, FF_CHUNK):
        a = jnp.dot(h, w1_ref[:, c0:c0 + FF_CHUNK], preferred_element_type=F32)
        b = jnp.dot(h, w3_ref[:, c0:c0 + FF_CHUNK], preferred_element_type=F32)
        g = (a * jax.nn.sigmoid(a) * b).astype(BF16)
        t = jnp.dot(g, w2_ref[c0:c0 + FF_CHUNK, :], preferred_element_type=F32)
        acc = t if acc is None else acc + t
    return x + 0.5 * acc


def _head_norm_rope(t, ones_ref, g_ref, cos, sinlo, sinhi, scale):
    ssum = jnp.dot((t * t).astype(BF16), ones_ref[...], preferred_element_type=F32)
    tn = t * lax.rsqrt(ssum * (1.0 / HEAD_DIM) + RMS_EPS) * g_ref[...]
    up = pltpu.roll(tn, HEAD_DIM // 2, axis=1)
    dn = pltpu.roll(tn, LANES - HEAD_DIM // 2, axis=1)
    r = tn * cos + dn * sinlo + up * sinhi
    return r * scale if scale != 1.0 else r


def _front_kernel(x_ref, cos_ref, sinlo_ref, sinhi_ref,
                  n1_ref, w1_ref, w3_ref, w2_ref, nm_ref, win_ref, qn_ref, kn_ref, ones_ref,
                  x1_ref, q_ref, k_ref, v_ref, u_ref, sga_ref, sgs_ref):
    x1 = _swiglu_residual(x_ref[...], n1_ref, w1_ref, w3_ref, w2_ref)
    x1_ref[...] = x1
    h = _rms(x1, nm_ref[...]).astype(BF16)
    cos, sinlo, sinhi = cos_ref[...], sinlo_ref[...], sinhi_ref[...]
    o = 0
    for j in range(ATTN_W // LANES):
        t = jnp.dot(h, win_ref[:, o:o + LANES], preferred_element_type=F32)
        q_ref[:, j * LANES:(j + 1) * LANES] = _head_norm_rope(
            t, ones_ref, qn_ref, cos, sinlo, sinhi, HEAD_DIM ** -0.5).astype(q_ref.dtype)
        o += LANES
    t = jnp.dot(h, win_ref[:, o:o + KV_W], preferred_element_type=F32)
    k_ref[...] = _head_norm_rope(t, ones_ref, kn_ref, cos, sinlo, sinhi, 1.0)
    o += KV_W
    v_ref[...] = jnp.dot(h, win_ref[:, o:o + KV_W], preferred_element_type=F32)
    o += KV_W
    u_ref[...] = jnp.dot(h, win_ref[:, o:o + SSM_W], preferred_element_type=F32)
    o += SSM_W
    ga = jnp.dot(h, win_ref[:, o:o + D_MODEL], preferred_element_type=F32)
    sga_ref[...] = jax.nn.sigmoid(ga).astype(BF16)
    o += D_MODEL
    gs = jnp.dot(h, win_ref[:, o:o + D_MODEL], preferred_element_type=F32)
    sgs_ref[...] = jax.nn.sigmoid(gs).astype(BF16)


def _front(x, tabs, p, q_dtype):
    n = x.shape[0]
    tm = FRONT_TM
    row = lambda w: pl.BlockSpec((tm, w), lambda i: (i, 0))
    consts = [p['n1'], p['w1a'], p['w3a'], p['w2a'], p['nm'], p['win'], p['qn'], p['kn'], p['ones']]
    outs = [(D_MODEL, F32), (ATTN_W, q_dtype), (KV_W, F32), (KV_W, F32), (SSM_W, F32),
            (D_MODEL, BF16), (D_MODEL, BF16)]
    return pl.pallas_call(
        _front_kernel,
        grid=(n // tm,),
        in_specs=[row(D_MODEL), row(LANES), row(LANES), row(LANES)] + [_const_spec(c.shape) for c in consts],
        out_specs=[row(w) for w, _ in outs],
        out_shape=[jax.ShapeDtypeStruct((n, w), dt) for w, dt in outs],
        compiler_params=pltpu.CompilerParams(dimension_semantics=("arbitrary",),
                                             vmem_limit_bytes=VMEM_LIMIT),
        name="front",
    )(x, *tabs, *consts)


def _ffn_kernel(x_ref, n_ref, w1_ref, w3_ref, w2_ref, y_ref):
    y_ref[...] = _swiglu_residual(x_ref[...], n_ref, w1_ref, w3_ref, w2_ref)


def _ffn(x, p):
    n = x.shape[0]
    tm = FRONT_TM
    consts = [p['n2'], p['w1b'], p['w3b'], p['w2b']]
    return pl.pallas_call(
        _ffn_kernel,
        grid=(n // tm,),
        in_specs=[pl.BlockSpec((tm, D_MODEL), lambda i: (i, 0))] + [_const_spec(c.shape) for c in consts],
        out_specs=pl.BlockSpec((tm, D_MODEL), lambda i: (i, 0)),
        out_shape=jax.ShapeDtypeStruct((n, D_MODEL), F32),
        compiler_params=pltpu.CompilerParams(dimension_semantics=("arbitrary",),
                                             vmem_limit_bytes=VMEM_LIMIT),
        name="ffn",
    )(x, *consts)


def _cmul(ar, ai, br, bi):
    return ar * br - ai * bi, ar * bi + ai * br


def _ssm_discretize(are_ref, aim_ref, ldt_ref):
    a_re, a_im = are_ref[...], aim_ref[...]
    dt = jnp.exp(ldt_ref[...])
    mag = jnp.exp(dt * a_re)
    ab_re = mag * jnp.cos(dt * a_im)
    ab_im = mag * jnp.sin(dt * a_im)
    den = a_re * a_re + a_im * a_im
    nr, ni = ab_re - 1.0, ab_im
    f_re = (nr * a_re + ni * a_im) / den
    f_im = (ni * a_re - nr * a_im) / den
    return ab_re, ab_im, f_re, f_im


def _build_wb(bbre_ref, bbim_ref, f_re, f_im, wbre_s, wbim_s):
    half = SSM_S // 2
    for hh in range(2):
        fr, fi = f_re[:, hh * half:(hh + 1) * half], f_im[:, hh * half:(hh + 1) * half]
        br, bi = bbre_ref[hh], bbim_ref[hh]
        wbre_s[hh] = (br * fr - bi * fi).astype(BF16)
        wbim_s[hh] = (br * fi + bi * fr).astype(BF16)


def _expand_state(ub, wbre_s, wbim_s, bur_s, bui_s):
    half_in, half_st = SSM_W // 2, SSM_S // 2
    for hh in range(2):
        uh = ub[:, hh * half_in:(hh + 1) * half_in]
        bur_s[:, hh * half_st:(hh + 1) * half_st] = jnp.dot(uh, wbre_s[hh], preferred_element_type=F32)
        bui_s[:, hh * half_st:(hh + 1) * half_st] = jnp.dot(uh, wbim_s[hh], preferred_element_type=F32)


def _project_state(xr_s, xi_s, ccat_ref):
    half_st = SSM_S // 2
    outs = []
    for hh in range(2):
        sl = slice(hh * half_st, (hh + 1) * half_st)
        xcat = jnp.concatenate([xr_s[:, sl].astype(BF16), xi_s[:, sl].astype(BF16)], axis=1)
        outs.append(jnp.dot(xcat, ccat_ref[hh], preferred_element_type=F32))
    return jnp.concatenate(outs, axis=1)


def _gated_merge(x1, attn_b, y_ssm, sga, sgs, wao_ref, wglu_ref, bglu_ref, wso_ref, wout_ref):
    z = jax.nn.gelu(y_ssm, approximate=True)
    gl = jnp.dot(z.astype(BF16), wglu_ref[...], preferred_element_type=F32) + bglu_ref[...]
    ssm = (z * jax.nn.sigmoid(gl)).astype(BF16)
    merged = (sga.astype(F32) * jnp.dot(attn_b, wao_ref[...], preferred_element_type=F32)
              + sgs.astype(F32) * jnp.dot(ssm, wso_ref[...], preferred_element_type=F32))
    return x1 + jnp.dot(merged.astype(BF16), wout_ref[...], preferred_element_type=F32)


def _replicate_heads(kv):
    sw = pltpu.roll(kv, HEAD_DIM, axis=1)
    lo = lax.broadcasted_iota(jnp.int32, kv.shape, 1) < HEAD_DIM
    return jnp.where(lo, kv, sw).astype(BF16), jnp.where(lo, sw, kv).astype(BF16)


def _window_attention(q, krep, vrep, sinks_ref, first):
    qb = q.shape[0]
    qf = q.astype(F32)
    lo = lax.broadcasted_iota(jnp.int32, (qb, LANES), 1) < HEAD_DIM
    rows = 2 * (N_HEADS // N_KV // 2) * qb
    qi = lax.broadcasted_iota(jnp.int32, (rows, 2 * WINDOW), 0) & (qb - 1)
    sj = lax.broadcasted_iota(jnp.int32, (rows, 2 * WINDOW), 1)
    valid = (sj <= WINDOW + qi) & (sj > qi) & ((sj >= WINDOW) | jnp.logical_not(first))
    out_tiles = []
    for kv in range(N_KV):
        parts, sink_parts = [], []
        for jj in range(2):
            j = 2 * kv + jj
            t = qf[:, j * LANES:(j + 1) * LANES]
            parts += [jnp.where(lo, t, 0.0), jnp.where(lo, 0.0, t)]
            for hh in range(2):
                sink_parts.append(jnp.full((qb, 1), sinks_ref[2 * j + hh], F32))
        qs = jnp.concatenate(parts, axis=0).astype(BF16)
        sink = jnp.concatenate(sink_parts, axis=0)
        s = lax.dot_general(qs, krep[kv], (((1,), (1,)), ((), ())), preferred_element_type=F32)
        s = jnp.where(valid, s, MASK_VALUE)
        m = jnp.maximum(jnp.max(s, axis=-1, keepdims=True), sink)
        e = jnp.exp(s - m)
        denom = jnp.sum(e, axis=-1, keepdims=True) + jnp.exp(sink - m)
        probs = (e * (1.0 / denom)).astype(BF16)
        o = jnp.dot(probs, vrep[kv], preferred_element_type=F32)
        for jj in range(2):
            oa = o[(2 * jj) * qb:(2 * jj + 1) * qb]
            ob = o[(2 * jj + 1) * qb:(2 * jj + 2) * qb]
            out_tiles.append(jnp.where(lo, oa, ob))
    return jnp.concatenate(out_tiles, axis=1)


def _mix_prompt_kernel(x1_ref, q_ref, k_ref, v_ref, u_ref, sga_ref, sgs_ref, sinks_ref,
                       wao_ref, wglu_ref, bglu_ref, wso_ref, wout_ref,
                       are_ref, aim_ref, ldt_ref, bbre_ref, bbim_ref, ccat_ref, d_ref, perm_ref, permt_ref,
                       x2_ref, sre_ref, sim_ref,
                       wbre_s, wbim_s, apr_s, api_s, a1_s, aseg_s, carry_s, cin_s,
                       bur_s, bui_s, kprev_s, vprev_s):
    b, i = pl.program_id(0), pl.program_id(1)
    nt = pl.num_programs(1)

    @pl.when((b == 0) & (i == 0))
    def _():
        ab_re, ab_im, f_re, f_im = _ssm_discretize(are_ref, aim_ref, ldt_ref)
        _build_wb(bbre_ref, bbim_ref, f_re, f_im, wbre_s, wbim_s)
        a1_s[0:1, :] = ab_re
        a1_s[1:2, :] = ab_im
        pr, pi = ab_re, ab_im
        for j in range(SCAN_SEG):
            apr_s[j * SUBLANES:(j + 1) * SUBLANES, :] = jnp.broadcast_to(pr, (SUBLANES, SSM_S))
            api_s[j * SUBLANES:(j + 1) * SUBLANES, :] = jnp.broadcast_to(pi, (SUBLANES, SSM_S))
            if j == SCAN_SEG - 1:
                aseg_s[0:1, :] = pr
                aseg_s[1:2, :] = pi
            pr, pi = _cmul(pr, pi, ab_re, ab_im)

    @pl.when(i == 0)
    def _():
        carry_s[...] = jnp.zeros_like(carry_s)
        kprev_s[...] = jnp.zeros_like(kprev_s)
        vprev_s[...] = jnp.zeros_like(vprev_s)

    u = u_ref[...]
    up = jnp.dot(perm_ref[...], u.astype(BF16), preferred_element_type=F32).astype(BF16)
    _expand_state(up, wbre_s, wbim_s, bur_s, bui_s)
    for c0 in range(0, SSM_S, SCAN_LC):
        cs = slice(c0, c0 + SCAN_LC)
        ar = jnp.broadcast_to(a1_s[0:1, cs], (SUBLANES, SCAN_LC))
        ai = jnp.broadcast_to(a1_s[1:2, cs], (SUBLANES, SCAN_LC))

        def step(j, carry, cs=cs, ar=ar, ai=ai):
            xr, xi = carry
            r0 = pl.multiple_of(j * SUBLANES, SUBLANES)
            nr = ar * xr - ai * xi + bur_s[pl.ds(r0, SUBLANES), cs]
            ni = ar * xi + ai * xr + bui_s[pl.ds(r0, SUBLANES), cs]
            bur_s[pl.ds(r0, SUBLANES), cs] = nr
            bui_s[pl.ds(r0, SUBLANES), cs] = ni
            return nr, ni

        lax.fori_loop(1, SCAN_SEG, step, (bur_s[0:SUBLANES, cs], bui_s[0:SUBLANES, cs]))
    last = MIX_TM - SUBLANES
    sr, si = aseg_s[0:1, :], aseg_s[1:2, :]
    cr, ci = carry_s[0:1, :], carry_s[1:2, :]
    for s in range(SUBLANES):
        cin_s[0, s:s + 1, :] = cr
        cin_s[1, s:s + 1, :] = ci
        pr, pi = _cmul(sr, si, cr, ci)
        cr = pr + bur_s[last + s:last + s + 1, :]
        ci = pi + bui_s[last + s:last + s + 1, :]
    carry_s[0:1, :] = cr
    carry_s[1:2, :] = ci
    cinr, cini = cin_s[0], cin_s[1]

    def fix(j, _):
        r0 = pl.multiple_of(j * SUBLANES, SUBLANES)
        pr, pi = apr_s[pl.ds(r0, SUBLANES), :], api_s[pl.ds(r0, SUBLANES), :]
        bur_s[pl.ds(r0, SUBLANES), :] += pr * cinr - pi * cini
        bui_s[pl.ds(r0, SUBLANES), :] += pr * cini + pi * cinr
        return 0

    lax.fori_loop(0, SCAN_SEG, fix, 0)
    yp = _project_state(bur_s, bui_s, ccat_ref)
    y_hi = yp.astype(BF16)
    y_lo = (yp - y_hi.astype(F32)).astype(BF16)
    y_ssm = (jnp.dot(permt_ref[...], y_hi, preferred_element_type=F32)
             + jnp.dot(permt_ref[...], y_lo, preferred_element_type=F32) + d_ref[...] * u)

    @pl.when(i == nt - 1)
    def _():
        sre_ref[...] = carry_s[0:1, :]
        sim_ref[...] = carry_s[1:2, :]

    k0, k1 = _replicate_heads(k_ref[...])
    v0, v1 = _replicate_heads(v_ref[...])
    attn = []
    for blk in range(MIX_TM // WINDOW):
        cur = slice(blk * WINDOW, (blk + 1) * WINDOW)
        if blk == 0:
            prev = [kprev_s[0], kprev_s[1], vprev_s[0], vprev_s[1]]
            first = i == 0
        else:
            ps = slice((blk - 1) * WINDOW, blk * WINDOW)
            prev = [k0[ps], k1[ps], v0[ps], v1[ps]]
            first = False
        krep = [jnp.concatenate([prev[0], k0[cur]], axis=0), jnp.concatenate([prev[1], k1[cur]], axis=0)]
        vrep = [jnp.concatenate([prev[2], v0[cur]], axis=0), jnp.concatenate([prev[3], v1[cur]], axis=0)]
        attn.append(_window_attention(q_ref[cur, :], krep, vrep, sinks_ref, first))
    tail = slice(MIX_TM - WINDOW, MIX_TM)
    kprev_s[0], kprev_s[1] = k0[tail], k1[tail]
    vprev_s[0], vprev_s[1] = v0[tail], v1[tail]
    attn_b = jnp.concatenate(attn, axis=0).astype(BF16)

    x2_ref[...] = _gated_merge(x1_ref[...], attn_b, y_ssm, sga_ref[...], sgs_ref[...],
                               wao_ref, wglu_ref, bglu_ref, wso_ref, wout_ref)


def _mix_prompt(fr, p, batch, seq):
    x1, q, k, v, u, sga, sgs = fr
    tm = MIX_TM
    nt = seq // tm
    row = lambda w: pl.BlockSpec((tm, w), lambda b, i: (b * nt + i, 0))
    consts = [p['wao'], p['wglu'], p['bglu'], p['wso'], p['wout'],
              p['are'], p['aim'], p['ldt'], p['bbre'], p['bbim'], p['ccat'], p['dskip'], p['perm'], p['permt']]
    st_spec = pl.BlockSpec((None, 1, SSM_S), lambda b, i: (b, 0, 0))
    scratch = [pltpu.VMEM((2, MXU_DIM, SSM_S // 2), BF16), pltpu.VMEM((2, MXU_DIM, SSM_S // 2), BF16),
               pltpu.VMEM((tm, SSM_S), F32), pltpu.VMEM((tm, SSM_S), F32),
               pltpu.VMEM((SUBLANES, SSM_S), F32), pltpu.VMEM((SUBLANES, SSM_S), F32),
               pltpu.VMEM((SUBLANES, SSM_S), F32), pltpu.VMEM((2, SUBLANES, SSM_S), F32),
               pltpu.VMEM((tm, SSM_S), F32), pltpu.VMEM((tm, SSM_S), F32),
               pltpu.VMEM((N_KV, WINDOW, LANES), BF16), pltpu.VMEM((N_KV, WINDOW, LANES), BF16)]
    return pl.pallas_call(
        _mix_prompt_kernel,
        grid=(batch, nt),
        in_specs=[row(D_MODEL), row(ATTN_W), row(KV_W), row(KV_W), row(SSM_W), row(D_MODEL), row(D_MODEL),
                  pl.BlockSpec(memory_space=pltpu.SMEM)] + [_const_spec(c.shape) for c in consts],
        out_specs=[row(D_MODEL), st_spec, st_spec],
        out_shape=[jax.ShapeDtypeStruct((batch * seq, D_MODEL), F32),
                   jax.ShapeDtypeStruct((batch, 1, SSM_S), F32),
                   jax.ShapeDtypeStruct((batch, 1, SSM_S), F32)],
        scratch_shapes=scratch,
        compiler_params=pltpu.CompilerParams(dimension_semantics=("arbitrary", "arbitrary"),
                                             vmem_limit_bytes=VMEM_LIMIT),
        name="mix_prompt",
    )(x1, q, k, v, u, sga, sgs, p['sinks'], *consts)


def _attn_sample_kernel(q_ref, kn_ref, vn_ref, ck_ref, cv_ref, sinks_ref, o_ref, q8_s, kc_s, vc_s):
    steps = q_ref.shape[0]
    q8_s[...] = jnp.zeros_like(q8_s)
    kc_s[...] = jnp.zeros_like(kc_s)
    vc_s[...] = jnp.zeros_like(vc_s)
    for s in range(SAMPLE_BS):
        kc_s[0:WINDOW, :] = ck_ref[s]
        vc_s[0:WINDOW, :] = cv_ref[s]
        for t in range(steps):
            q8_s[t:t + 1, :] = q_ref[t, s:s + 1, :]
            kc_s[WINDOW + t:WINDOW + t + 1, :] = kn_ref[t, s:s + 1, :]
            vc_s[WINDOW + t:WINDOW + t + 1, :] = vn_ref[t, s:s + 1, :]
        k0, k1 = _replicate_heads(kc_s[...])
        v0, v1 = _replicate_heads(vc_s[...])
        o = _window_attention(q8_s[...], [k0, k1], [v0, v1], sinks_ref, False)
        for t in range(steps):
            o_ref[t, s:s + 1, :] = o[t:t + 1, :]


def _attn_sample(q, kn, vn, ck, cv, sinks, steps, nseq):
    bs = SAMPLE_BS
    blk3 = lambda w: pl.BlockSpec((steps, bs, w), lambda i: (0, i, 0))
    cache = pl.BlockSpec((bs, WINDOW, KV_W), lambda i: (i, 0, 0))
    return pl.pallas_call(
        _attn_sample_kernel,
        grid=(nseq // bs,),
        in_specs=[blk3(ATTN_W), blk3(KV_W), blk3(KV_W), cache, cache, pl.BlockSpec(memory_space=pltpu.SMEM)],
        out_specs=blk3(ATTN_W),
        out_shape=jax.ShapeDtypeStruct((steps, nseq, ATTN_W), F32),
        scratch_shapes=[pltpu.VMEM((SUBLANES, ATTN_W), F32), pltpu.VMEM((2 * WINDOW, KV_W), F32),
                        pltpu.VMEM((2 * WINDOW, KV_W), F32)],
        compiler_params=pltpu.CompilerParams(dimension_semantics=("arbitrary",)),
        name="attn_sample",
    )(q.reshape(steps, nseq, ATTN_W), kn.reshape(steps, nseq, KV_W), vn.reshape(steps, nseq, KV_W),
      ck, cv, sinks)


def _mix_sample_kernel(x1_ref, attn_ref, u_ref, sga_ref, sgs_ref, x0r_ref, x0i_ref,
                       wao_ref, wglu_ref, bglu_ref, wso_ref, wout_ref,
                       are_ref, aim_ref, ldt_ref, bbre_ref, bbim_ref, ccat_ref, d_ref,
                       x2_ref, sre_ref, sim_ref,
                       wbre_s, wbim_s, bur_s, bui_s):
    nseq = x0r_ref.shape[0]
    steps = x1_ref.shape[0] // nseq
    ab_re, ab_im, f_re, f_im = _ssm_discretize(are_ref, aim_ref, ldt_ref)
    _build_wb(bbre_ref, bbim_ref, f_re, f_im, wbre_s, wbim_s)
    u = u_ref[...]
    _expand_state(u.astype(BF16), wbre_s, wbim_s, bur_s, bui_s)
    for c0 in range(0, SSM_S, LANES):
        cs = slice(c0, c0 + LANES)
        ar, ai = ab_re[:, cs], ab_im[:, cs]
        xr, xi = x0r_ref[:, cs], x0i_ref[:, cs]
        for t in range(steps):
            rs = slice(t * nseq, (t + 1) * nseq)
            nr = ar * xr - ai * xi + bur_s[rs, cs]
            ni = ar * xi + ai * xr + bui_s[rs, cs]
            bur_s[rs, cs] = nr
            bui_s[rs, cs] = ni
            xr, xi = nr, ni
        sre_ref[:, cs] = xr
        sim_ref[:, cs] = xi
    y_ssm = _project_state(bur_s, bui_s, ccat_ref) + d_ref[...] * u
    x2_ref[...] = _gated_merge(x1_ref[...], attn_ref[...].astype(BF16), y_ssm, sga_ref[...], sgs_ref[...],
                               wao_ref, wglu_ref, bglu_ref, wso_ref, wout_ref)


def _mix_sample(x1, attn, u, sga, sgs, x0r, x0i, p):
    n, nseq = x1.shape[0], x0r.shape[0]
    args = [x1, attn, u, sga, sgs, x0r, x0i, p['wao'], p['wglu'], p['bglu'], p['wso'], p['wout'],
            p['are'], p['aim'], p['ldt'], p['bbre'], p['bbim'], p['ccat'], p['dskip']]
    return pl.pallas_call(
        _mix_sample_kernel,
        grid=(1,),
        in_specs=[_const_spec(a.shape) for a in args],
        out_specs=[pl.BlockSpec((n, D_MODEL), lambda i: (0, 0)), pl.BlockSpec((nseq, SSM_S), lambda i: (0, 0)),
                   pl.BlockSpec((nseq, SSM_S), lambda i: (0, 0))],
        out_shape=[jax.ShapeDtypeStruct((n, D_MODEL), F32),
                   jax.ShapeDtypeStruct((nseq, SSM_S), F32),
                   jax.ShapeDtypeStruct((nseq, SSM_S), F32)],
        scratch_shapes=[pltpu.VMEM((2, MXU_DIM, SSM_S // 2), BF16), pltpu.VMEM((2, MXU_DIM, SSM_S // 2), BF16),
                        pltpu.VMEM((n, SSM_S), F32), pltpu.VMEM((n, SSM_S), F32)],
        compiler_params=pltpu.CompilerParams(dimension_semantics=("arbitrary",),
                                             vmem_limit_bytes=VMEM_LIMIT),
        name="mix_sample",
    )(*args)


def _rope_tables(pos):
    half = HEAD_DIM // 2
    inv = ROPE_THETA ** (-2.0 * jnp.arange(half, dtype=F32) / HEAD_DIM)
    ang = pos.astype(F32)[:, None] * inv[None, :]
    cos, sin = jnp.cos(ang), jnp.sin(ang)
    zero = jnp.zeros_like(sin)
    reps = LANES // HEAD_DIM
    return (jnp.tile(jnp.concatenate([cos, cos], axis=1), (1, reps)),
            jnp.tile(jnp.concatenate([-sin, zero], axis=1), (1, reps)),
            jnp.tile(jnp.concatenate([zero, sin], axis=1), (1, reps)))


def _block_diag_halves(m):
    g, r, c = m.shape
    hg = g // 2
    eye = jnp.eye(hg, dtype=m.dtype)
    mh = m.reshape(2, hg, r, c)
    return (mh[:, :, :, None, :] * eye[None, :, None, :, None]).reshape(2, hg * r, hg * c)


def _layout_params(ffn1_norm, ffn1_w1, ffn1_w3, ffn1_w2, mix_norm, w_in, q_norm, k_norm, attn_sinks,
                   w_attn_out, ssm_a_re, ssm_a_im, ssm_log_dt, ssm_b_re, ssm_b_im, ssm_c_re, ssm_c_im,
                   ssm_d, w_glu, b_glu, w_ssm_out, w_out, ffn2_norm, ffn2_w1, ffn2_w3, ffn2_w2):
    l = 0
    row = lambda a: a.reshape(1, -1).astype(F32)
    head_ones = jnp.kron(jnp.eye(LANES // HEAD_DIM, dtype=F32), jnp.ones((HEAD_DIM, HEAD_DIM), F32))
    r = jnp.arange(MIX_TM)
    tok = (r % SUBLANES) * SCAN_SEG + r // SUBLANES
    perm = (tok[:, None] == jnp.arange(MIX_TM)[None, :]).astype(BF16)
    c_cat = jnp.concatenate([_block_diag_halves(jnp.swapaxes(ssm_c_re[l], 1, 2)),
                             -_block_diag_halves(jnp.swapaxes(ssm_c_im[l], 1, 2))], axis=1)
    return dict(
        n1=row(ffn1_norm[l]), w1a=ffn1_w1[l].astype(BF16), w3a=ffn1_w3[l].astype(BF16), w2a=ffn1_w2[l].astype(BF16),
        nm=row(mix_norm[l]), win=w_in[l].astype(BF16),
        qn=row(jnp.tile(q_norm[l], LANES // HEAD_DIM)), kn=row(jnp.tile(k_norm[l], LANES // HEAD_DIM)),
        ones=head_ones.astype(BF16), sinks=attn_sinks[l].astype(F32),
        wao=w_attn_out[l].astype(BF16), wglu=w_glu[l].astype(BF16), bglu=row(b_glu[l]),
        wso=w_ssm_out[l].astype(BF16), wout=w_out[l].astype(BF16),
        are=row(ssm_a_re[l]), aim=row(ssm_a_im[l]), ldt=row(jnp.repeat(ssm_log_dt[l], SSM_N)),
        bbre=_block_diag_halves(jnp.swapaxes(ssm_b_re[l], 1, 2).astype(F32)),
        bbim=_block_diag_halves(jnp.swapaxes(ssm_b_im[l], 1, 2).astype(F32)),
        ccat=c_cat.astype(BF16), dskip=row(ssm_d[l]), perm=perm, permt=perm.T,
        n2=row(ffn2_norm[l]), w1b=ffn2_w1[l].astype(BF16), w3b=ffn2_w3[l].astype(BF16), w2b=ffn2_w2[l].astype(BF16),
    )


def kernel(x_prompt, x_sample, cache_k, cache_v, state_ssm_re, state_ssm_im, ffn1_norm, ffn1_w1, ffn1_w3, ffn1_w2, mix_norm, w_in, q_norm, k_norm, attn_sinks, w_attn_out, ssm_a_re, ssm_a_im, ssm_log_dt, ssm_b_re, ssm_b_im, ssm_c_re, ssm_c_im, ssm_d, w_glu, b_glu, w_ssm_out, w_out, ffn2_norm, ffn2_w1, ffn2_w3, ffn2_w2):
    assert ffn1_norm.shape[0] == 1, "single trunk layer"
    batch, seq, _ = x_prompt.shape
    nseq, steps, _ = x_sample.shape
    p = _layout_params(ffn1_norm, ffn1_w1, ffn1_w3, ffn1_w2, mix_norm, w_in, q_norm, k_norm, attn_sinks,
                       w_attn_out, ssm_a_re, ssm_a_im, ssm_log_dt, ssm_b_re, ssm_b_im, ssm_c_re, ssm_c_im,
                       ssm_d, w_glu, b_glu, w_ssm_out, w_out, ffn2_norm, ffn2_w1, ffn2_w3, ffn2_w2)

    tabs_p = _rope_tables(jnp.tile(jnp.arange(seq), batch))
    fr = _front(x_prompt.reshape(batch * seq, D_MODEL), tabs_p, p, BF16)
    x2p, re_p, im_p = _mix_prompt(fr, p, batch, seq)
    y_prompt = _ffn(x2p, p).reshape(batch, seq, D_MODEL)
    k_p = fr[2].reshape(batch, seq, N_KV, HEAD_DIM)[:, seq - WINDOW:]
    v_p = fr[3].reshape(batch, seq, N_KV, HEAD_DIM)[:, seq - WINDOW:]

    xs = jnp.swapaxes(x_sample, 0, 1).reshape(steps * nseq, D_MODEL)
    tabs_s = _rope_tables(PAST_LEN + jnp.repeat(jnp.arange(steps), nseq))
    x1s, qs, ks, vs, us, sgas, sgss = _front(xs, tabs_s, p, F32)
    ck = cache_k[0].reshape(nseq, WINDOW, KV_W)
    cv = cache_v[0].reshape(nseq, WINDOW, KV_W)
    attn_s = _attn_sample(qs, ks, vs, ck, cv, p['sinks'], steps, nseq).reshape(steps * nseq, ATTN_W)
    x2s, re_s, im_s = _mix_sample(x1s, attn_s, us, sgas, sgss,
                                  state_ssm_re[0].reshape(nseq, SSM_S), state_ssm_im[0].reshape(nseq, SSM_S), p)
    y_sample = jnp.swapaxes(_ffn(x2s, p).reshape(steps, nseq, D_MODEL), 0, 1)
    k_new = jnp.swapaxes(ks.reshape(steps, nseq, KV_W), 0, 1)
    v_new = jnp.swapaxes(vs.reshape(steps, nseq, KV_W), 0, 1)
    k_s = jnp.concatenate([ck, k_new], axis=1)[:, steps:].reshape(nseq, WINDOW, N_KV, HEAD_DIM)
    v_s = jnp.concatenate([cv, v_new], axis=1)[:, steps:].reshape(nseq, WINDOW, N_KV, HEAD_DIM)

    st = lambda a, n: a.reshape(1, n, SSM_G, SSM_N)
    return (y_prompt, y_sample, k_p[None], v_p[None], st(re_p, batch), st(im_p, batch),
            k_s[None], v_s[None], st(re_s, nseq), st(im_s, nseq))
```

```python
import functools
import math

import jax
import jax.numpy as jnp
from jax import lax
from jax.experimental import pallas as pl
from jax.experimental.pallas import tpu as pltpu

F32 = jnp.float32
BF16 = jnp.bfloat16

D_MODEL = 1024
HEAD_DIM = 64
N_HEADS = 8
N_KV = 2
WINDOW = 128
ATTN_W = N_HEADS * HEAD_DIM
KV_W = N_KV * HEAD_DIM
SSM_W = 512
SSM_G = 32
SSM_CH = 16
SSM_N = 64
SSM_S = SSM_G * SSM_N
D_FF = 2816
IN_COLS = ATTN_W + 2 * KV_W + SSM_W + 2 * D_MODEL
ROPE_THETA = 10000.0
RMS_EPS = 1e-6
MASK_VALUE = -1e30
PAST_LEN = 16384

LANES = 128
SUBLANES = 8
MXU_DIM = 256

FRONT_TM = 256
MIX_TM = 256
SCAN_SEG = MIX_TM // SUBLANES
FF_CHUNK = 256
SCAN_LC = 512
SAMPLE_BS = 8
VMEM_LIMIT = 56 * 1024 * 1024


def _const_spec(shape):
    nd = len(shape)
    return pl.BlockSpec(shape, lambda *_: (0,) * nd, pipeline_mode=pl.Buffered(1))


def _rms(x, w):
    ms = jnp.mean(x * x, axis=-1, keepdims=True)
    return x * lax.rsqrt(ms + RMS_EPS) * w


def _swiglu_residual(x, nw_ref, w1_ref, w3_ref, w2_ref):
    h = _rms(x, nw_ref[...]).astype(BF16)
    acc = None
    for c0 in range(0, D_FF, FF_CHUNK):
        a = jnp.dot(h, w1_ref[:, c0:c0 + FF_CHUNK], preferred_element_type=F32)
        b = jnp.dot(h, w3_ref[:, c0:c0 + FF_CHUNK], preferred_element_type=F32)
        g = (a * jax.nn.sigmoid(a) * b).astype(BF16)
        t = jnp.dot(g, w2_ref[c0:c0 + FF_CHUNK, :], preferred_element_type=F32)
        acc = t if acc is None else acc + t
    return x + 0.5 * acc


def _head_norm_rope(t, ones_ref, g_ref, cos, sinlo, sinhi, scale):
    ssum = jnp.dot((t * t).astype(BF16), ones_ref[...], preferred_element_type=F32)
    tn = t * lax.rsqrt(ssum * (1.0 / HEAD_DIM) + RMS_EPS) * g_ref[...]
    up = pltpu.roll(tn, HEAD_DIM // 2, axis=1)
    dn = pltpu.roll(tn, LANES - HEAD_DIM // 2, axis=1)
    r = tn * cos + dn * sinlo + up * sinhi
    return r * scale if scale != 1.0 else r


def _front_kernel(x_ref, base_ref, off_ref,
                  n1_ref, w1_ref, w3_ref, w2_ref, nm_ref, win_ref, qn_ref, kn_ref, ones_ref,
                  x1_ref, q_ref, k_ref, v_ref, u_ref, sga_ref, sgs_ref):
    x1 = _swiglu_residual(x_ref[...], n1_ref, w1_ref, w3_ref, w2_ref)
    x1_ref[...] = x1
    h = _rms(x1, nm_ref[...]).astype(BF16)
    cb, sb = base_ref[0:1, :], base_ref[1:2, :]
    cos = cb * off_ref[0] - sb * off_ref[1]
    sinlo = sb * off_ref[2] + cb * off_ref[3]
    sinhi = sb * off_ref[4] + cb * off_ref[5]
    o = 0
    for j in range(ATTN_W // LANES):
        t = jnp.dot(h, win_ref[:, o:o + LANES], preferred_element_type=F32)
        q_ref[:, j * LANES:(j + 1) * LANES] = _head_norm_rope(
            t, ones_ref, qn_ref, cos, sinlo, sinhi, HEAD_DIM ** -0.5).astype(q_ref.dtype)
        o += LANES
    t = jnp.dot(h, win_ref[:, o:o + KV_W], preferred_element_type=F32)
    k_ref[...] = _head_norm_rope(t, ones_ref, kn_ref, cos, sinlo, sinhi, 1.0)
    o += KV_W
    v_ref[...] = jnp.dot(h, win_ref[:, o:o + KV_W], preferred_element_type=F32)
    o += KV_W
    u_ref[...] = jnp.dot(h, win_ref[:, o:o + SSM_W], preferred_element_type=F32)
    o += SSM_W
    ga = jnp.dot(h, win_ref[:, o:o + D_MODEL], preferred_element_type=F32)
    sga_ref[...] = jax.nn.sigmoid(ga).astype(BF16)
    o += D_MODEL
    gs = jnp.dot(h, win_ref[:, o:o + D_MODEL], preferred_element_type=F32)
    sgs_ref[...] = jax.nn.sigmoid(gs).astype(BF16)


def _front(x, tabs, p, q_dtype):
    n = x.shape[0]
    tm = FRONT_TM
    base_tab, off_tab = tabs
    row = lambda w: pl.BlockSpec((tm, w), lambda i: (i, 0))
    consts = [p['n1'], p['w1a'], p['w3a'], p['w2a'], p['nm'], p['win'], p['qn'], p['kn'], p['ones']]
    outs = [(D_MODEL, F32), (ATTN_W, q_dtype), (KV_W, F32), (KV_W, F32), (SSM_W, F32),
            (D_MODEL, BF16), (D_MODEL, BF16)]
    return pl.pallas_call(
        _front_kernel,
        grid=(n // tm,),
        in_specs=[row(D_MODEL), pl.BlockSpec((None, 2, LANES), lambda i: (i, 0, 0)), _const_spec(off_tab.shape)]
        + [_const_spec(c.shape) for c in consts],
        out_specs=[row(w) for w, _ in outs],
        out_shape=[jax.ShapeDtypeStruct((n, w), dt) for w, dt in outs],
        compiler_params=pltpu.CompilerParams(dimension_semantics=("arbitrary",),
                                             vmem_limit_bytes=VMEM_LIMIT),
        name="front",
    )(x, base_tab, off_tab, *consts)


def _ffn_kernel(x_ref, n_ref, w1_ref, w3_ref, w2_ref, y_ref):
    y_ref[...] = _swiglu_residual(x_ref[...], n_ref, w1_ref, w3_ref, w2_ref)


def _ffn(x, p):
    n = x.shape[0]
    tm = FRONT_TM
    consts = [p['n2'], p['w1b'], p['w3b'], p['w2b']]
    return pl.pallas_call(
        _ffn_kernel,
        grid=(n // tm,),
        in_specs=[pl.BlockSpec((tm, D_MODEL), lambda i: (i, 0))] + [_const_spec(c.shape) for c in consts],
        out_specs=pl.BlockSpec((tm, D_MODEL), lambda i: (i, 0)),
        out_shape=jax.ShapeDtypeStruct((n, D_MODEL), F32),
        compiler_params=pltpu.CompilerParams(dimension_semantics=("arbitrary",),
                                             vmem_limit_bytes=VMEM_LIMIT),
        name="ffn",
    )(x, *consts)


def _cmul(ar, ai, br, bi):
    return ar * br - ai * bi, ar * bi + ai * br


def _ssm_discretize(are_ref, aim_ref, ldt_ref):
    a_re, a_im = are_ref[...], aim_ref[...]
    dt = jnp.exp(ldt_ref[...])
    mag = jnp.exp(dt * a_re)
    ab_re = mag * jnp.cos(dt * a_im)
    ab_im = mag * jnp.sin(dt * a_im)
    den = a_re * a_re + a_im * a_im
    nr, ni = ab_re - 1.0, ab_im
    f_re = (nr * a_re + ni * a_im) / den
    f_im = (ni * a_re - nr * a_im) / den
    return ab_re, ab_im, f_re, f_im


def _build_wb(bbre_ref, bbim_ref, f_re, f_im, wbre_s, wbim_s):
    half = SSM_S // 2
    for hh in range(2):
        fr, fi = f_re[:, hh * half:(hh + 1) * half], f_im[:, hh * half:(hh + 1) * half]
        br, bi = bbre_ref[hh], bbim_ref[hh]
        wbre_s[hh] = (br * fr - bi * fi).astype(BF16)
        wbim_s[hh] = (br * fi + bi * fr).astype(BF16)


def _expand_state(ub, wbre_s, wbim_s, bur_s, bui_s):
    half_in, half_st = SSM_W // 2, SSM_S // 2
    for hh in range(2):
        uh = ub[:, hh * half_in:(hh + 1) * half_in]
        bur_s[:, hh * half_st:(hh + 1) * half_st] = jnp.dot(uh, wbre_s[hh], preferred_element_type=F32)
        bui_s[:, hh * half_st:(hh + 1) * half_st] = jnp.dot(uh, wbim_s[hh], preferred_element_type=F32)


def _project_state(xr_s, xi_s, ccat_ref):
    half_st = SSM_S // 2
    outs = []
    for hh in range(2):
        sl = slice(hh * half_st, (hh + 1) * half_st)
        xcat = jnp.concatenate([xr_s[:, sl].astype(BF16), xi_s[:, sl].astype(BF16)], axis=1)
        outs.append(jnp.dot(xcat, ccat_ref[hh], preferred_element_type=F32))
    return jnp.concatenate(outs, axis=1)


def _gated_merge(x1, attn_b, y_ssm, sga, sgs, wao_ref, wglu_ref, bglu_ref, wso_ref, wout_ref):
    z = jax.nn.gelu(y_ssm, approximate=True)
    gl = jnp.dot(z.astype(BF16), wglu_ref[...], preferred_element_type=F32) + bglu_ref[...]
    ssm = (z * jax.nn.sigmoid(gl)).astype(BF16)
    merged = (sga.astype(F32) * jnp.dot(attn_b, wao_ref[...], preferred_element_type=F32)
              + sgs.astype(F32) * jnp.dot(ssm, wso_ref[...], preferred_element_type=F32))
    return x1 + jnp.dot(merged.astype(BF16), wout_ref[...], preferred_element_type=F32)


def _replicate_heads(kv):
    sw = pltpu.roll(kv, HEAD_DIM, axis=1)
    lo = lax.broadcasted_iota(jnp.int32, kv.shape, 1) < HEAD_DIM
    return jnp.where(lo, kv, sw).astype(BF16), jnp.where(lo, sw, kv).astype(BF16)


def _window_attention(q, krep, vrep, sinks_ref, first):
    return _window_attention_batch([q], [krep], [vrep], sinks_ref, first)[0]


def _window_attention_batch(qs, kreps, vreps, sinks_ref, first):
    qb = qs[0].shape[0]
    lo = lax.broadcasted_iota(jnp.int32, (qb, LANES), 1) < HEAD_DIM
    rows = 2 * (N_HEADS // N_KV // 2) * qb
    scores, sink_parts = [], []
    for q, krep in zip(qs, kreps):
        qf = q.astype(F32)
        for kv in range(N_KV):
            parts = []
            for jj in range(2):
                j = 2 * kv + jj
                t = qf[:, j * LANES:(j + 1) * LANES]
                parts += [jnp.where(lo, t, 0.0), jnp.where(lo, 0.0, t)]
                if len(scores) < N_KV:
                    sink_parts += [jnp.full((qb, 1), sinks_ref[2 * j + hh], F32) for hh in range(2)]
            stacked = jnp.concatenate(parts, axis=0).astype(BF16)
            scores.append(lax.dot_general(stacked, krep[kv], (((1,), (1,)), ((), ())),
                                          preferred_element_type=F32))
    s = jnp.concatenate(scores, axis=0)
    sink = jnp.concatenate(sink_parts * len(qs), axis=0)
    qi = lax.broadcasted_iota(jnp.int32, s.shape, 0) & (qb - 1)
    sj = lax.broadcasted_iota(jnp.int32, s.shape, 1)
    valid = (sj <= WINDOW + qi) & (sj > qi) & ((sj >= WINDOW) | jnp.logical_not(first))
    s = jnp.where(valid, s, MASK_VALUE)
    m = jnp.maximum(jnp.max(s, axis=-1, keepdims=True), sink)
    e = jnp.exp(s - m)
    denom = jnp.sum(e, axis=-1, keepdims=True) + jnp.exp(sink - m)
    probs = (e * (1.0 / denom)).astype(BF16)
    outs = []
    for b, vrep in enumerate(vreps):
        tiles = []
        for kv in range(N_KV):
            r0 = (b * N_KV + kv) * rows
            o = jnp.dot(probs[r0:r0 + rows], vrep[kv], preferred_element_type=F32)
            for jj in range(2):
                oa = o[(2 * jj) * qb:(2 * jj + 1) * qb]
                ob = o[(2 * jj + 1) * qb:(2 * jj + 2) * qb]
                tiles.append(jnp.where(lo, oa, ob))
        outs.append(jnp.concatenate(tiles, axis=1))
    return outs


def _mix_prompt_kernel(x1_ref, q_ref, k_ref, v_ref, u_ref, sga_ref, sgs_ref, sinks_ref,
                       wao_ref, wglu_ref, bglu_ref, wso_ref, wout_ref,
                       are_ref, aim_ref, ldt_ref, bbre_ref, bbim_ref, ccat_ref, d_ref, perm_ref, permt_ref,
                       x2_ref, sre_ref, sim_ref,
                       wbre_s, wbim_s, apr_s, api_s, a1_s, aseg_s, carry_s, cin_s,
                       bur_s, bui_s, xb_s, kprev_s, vprev_s):
    b, i = pl.program_id(0), pl.program_id(1)
    nt = pl.num_programs(1)

    @pl.when((b == 0) & (i == 0))
    def _():
        ab_re, ab_im, f_re, f_im = _ssm_discretize(are_ref, aim_ref, ldt_ref)
        _build_wb(bbre_ref, bbim_ref, f_re, f_im, wbre_s, wbim_s)
        a1_s[0:1, :] = ab_re
        a1_s[1:2, :] = ab_im
        pr, pi = ab_re, ab_im
        for j in range(SCAN_SEG):
            apr_s[j * SUBLANES:(j + 1) * SUBLANES, :] = jnp.broadcast_to(pr, (SUBLANES, SSM_S))
            api_s[j * SUBLANES:(j + 1) * SUBLANES, :] = jnp.broadcast_to(pi, (SUBLANES, SSM_S))
            if j == SCAN_SEG - 1:
                aseg_s[0:1, :] = pr
                aseg_s[1:2, :] = pi
            pr, pi = _cmul(pr, pi, ab_re, ab_im)

    @pl.when(i == 0)
    def _():
        carry_s[...] = jnp.zeros_like(carry_s)
        kprev_s[...] = jnp.zeros_like(kprev_s)
        vprev_s[...] = jnp.zeros_like(vprev_s)

    u = u_ref[...]
    up = jnp.dot(perm_ref[...], u.astype(BF16), preferred_element_type=F32).astype(BF16)
    _expand_state(up, wbre_s, wbim_s, bur_s, bui_s)
    for c0 in range(0, SSM_S, SCAN_LC):
        cs = slice(c0, c0 + SCAN_LC)
        ar = jnp.broadcast_to(a1_s[0:1, cs], (SUBLANES, SCAN_LC))
        ai = jnp.broadcast_to(a1_s[1:2, cs], (SUBLANES, SCAN_LC))
        xr, xi = bur_s[0:SUBLANES, cs], bui_s[0:SUBLANES, cs]
        for j in range(1, SCAN_SEG):
            rs = slice(j * SUBLANES, (j + 1) * SUBLANES)
            xr, xi = (ar * xr - ai * xi + bur_s[rs, cs], ar * xi + ai * xr + bui_s[rs, cs])
            bur_s[rs, cs] = xr
            bui_s[rs, cs] = xi
    last = MIX_TM - SUBLANES
    sr, si = aseg_s[0:1, :], aseg_s[1:2, :]
    cr, ci = carry_s[0:1, :], carry_s[1:2, :]
    for s in range(SUBLANES):
        cin_s[0, s:s + 1, :] = cr
        cin_s[1, s:s + 1, :] = ci
        pr, pi = _cmul(sr, si, cr, ci)
        cr = pr + bur_s[last + s:last + s + 1, :]
        ci = pi + bui_s[last + s:last + s + 1, :]
    carry_s[0:1, :] = cr
    carry_s[1:2, :] = ci
    half_st = SSM_S // 2
    pack = 2 * SUBLANES
    for c0 in range(0, SSM_S, SCAN_LC):
        cs = slice(c0, c0 + SCAN_LC)
        cinr = jnp.concatenate([cin_s[0, :, cs]] * 2, axis=0)
        cini = jnp.concatenate([cin_s[1, :, cs]] * 2, axis=0)
        col = (c0 // half_st) * SSM_S + c0 % half_st
        for r0 in range(0, MIX_TM, pack):
            rs = slice(r0, r0 + pack)
            pr, pi = apr_s[rs, cs], api_s[rs, cs]
            xb_s[rs, col:col + SCAN_LC] = (bur_s[rs, cs] + pr * cinr - pi * cini).astype(BF16)
            xb_s[rs, col + half_st:col + half_st + SCAN_LC] = (bui_s[rs, cs] + pr * cini + pi * cinr).astype(BF16)
    yp = jnp.concatenate([jnp.dot(xb_s[:, hh * SSM_S:(hh + 1) * SSM_S], ccat_ref[hh], preferred_element_type=F32)
                          for hh in range(2)], axis=1)
    y_hi = yp.astype(BF16)
    y_lo = (yp - y_hi.astype(F32)).astype(BF16)
    y_ssm = (jnp.dot(permt_ref[...], y_hi, preferred_element_type=F32)
             + jnp.dot(permt_ref[...], y_lo, preferred_element_type=F32) + d_ref[...] * u)

    @pl.when(i == nt - 1)
    def _():
        sre_ref[...] = carry_s[0:1, :]
        sim_ref[...] = carry_s[1:2, :]

    k0, k1 = _replicate_heads(k_ref[...])
    v0, v1 = _replicate_heads(v_ref[...])
    attn = []
    for blk in range(MIX_TM // WINDOW):
        cur = slice(blk * WINDOW, (blk + 1) * WINDOW)
        if blk == 0:
            prev = [kprev_s[0], kprev_s[1], vprev_s[0], vprev_s[1]]
            first = i == 0
        else:
            ps = slice((blk - 1) * WINDOW, blk * WINDOW)
            prev = [k0[ps], k1[ps], v0[ps], v1[ps]]
            first = False
        krep = [jnp.concatenate([prev[0], k0[cur]], axis=0), jnp.concatenate([prev[1], k1[cur]], axis=0)]
        vrep = [jnp.concatenate([prev[2], v0[cur]], axis=0), jnp.concatenate([prev[3], v1[cur]], axis=0)]
        attn.append(_window_attention(q_ref[cur, :], krep, vrep, sinks_ref, first))
    tail = slice(MIX_TM - WINDOW, MIX_TM)
    kprev_s[0], kprev_s[1] = k0[tail], k1[tail]
    vprev_s[0], vprev_s[1] = v0[tail], v1[tail]
    attn_b = jnp.concatenate(attn, axis=0).astype(BF16)

    x2_ref[...] = _gated_merge(x1_ref[...], attn_b, y_ssm, sga_ref[...], sgs_ref[...],
                               wao_ref, wglu_ref, bglu_ref, wso_ref, wout_ref)


def _mix_prompt(fr, p, batch, seq):
    x1, q, k, v, u, sga, sgs = fr
    tm = MIX_TM
    nt = seq // tm
    row = lambda w: pl.BlockSpec((tm, w), lambda b, i: (b * nt + i, 0))
    consts = [p['wao'], p['wglu'], p['bglu'], p['wso'], p['wout'],
              p['are'], p['aim'], p['ldt'], p['bbre'], p['bbim'], p['ccat'], p['dskip'], p['perm'], p['permt']]
    st_spec = pl.BlockSpec((None, 1, SSM_S), lambda b, i: (b, 0, 0))
    scratch = [pltpu.VMEM((2, MXU_DIM, SSM_S // 2), BF16), pltpu.VMEM((2, MXU_DIM, SSM_S // 2), BF16),
               pltpu.VMEM((tm, SSM_S), F32), pltpu.VMEM((tm, SSM_S), F32),
               pltpu.VMEM((SUBLANES, SSM_S), F32), pltpu.VMEM((SUBLANES, SSM_S), F32),
               pltpu.VMEM((SUBLANES, SSM_S), F32), pltpu.VMEM((2, SUBLANES, SSM_S), F32),
               pltpu.VMEM((tm, SSM_S), F32), pltpu.VMEM((tm, SSM_S), F32), pltpu.VMEM((tm, 2 * SSM_S), BF16),
               pltpu.VMEM((N_KV, WINDOW, LANES), BF16), pltpu.VMEM((N_KV, WINDOW, LANES), BF16)]
    return pl.pallas_call(
        _mix_prompt_kernel,
        grid=(batch, nt),
        in_specs=[row(D_MODEL), row(ATTN_W), row(KV_W), row(KV_W), row(SSM_W), row(D_MODEL), row(D_MODEL),
                  pl.BlockSpec(memory_space=pltpu.SMEM)] + [_const_spec(c.shape) for c in consts],
        out_specs=[row(D_MODEL), st_spec, st_spec],
        out_shape=[jax.ShapeDtypeStruct((batch * seq, D_MODEL), F32),
                   jax.ShapeDtypeStruct((batch, 1, SSM_S), F32),
                   jax.ShapeDtypeStruct((batch, 1, SSM_S), F32)],
        scratch_shapes=scratch,
        compiler_params=pltpu.CompilerParams(dimension_semantics=("arbitrary", "arbitrary"),
                                             vmem_limit_bytes=VMEM_LIMIT),
        name="mix_prompt",
    )(x1, q, k, v, u, sga, sgs, p['sinks'], *consts)


def _attn_sample_kernel(q_ref, kn_ref, vn_ref, ck_ref, cv_ref, sinks_ref, o_ref, q8_s, kc_s, vc_s):
    steps = q_ref.shape[0]
    q8_s[...] = jnp.zeros_like(q8_s)
    kc_s[...] = jnp.zeros_like(kc_s)
    vc_s[...] = jnp.zeros_like(vc_s)
    qs, kreps, vreps = [], [], []
    for s in range(SAMPLE_BS):
        kc_s[s, 0:WINDOW, :] = ck_ref[s]
        vc_s[s, 0:WINDOW, :] = cv_ref[s]
        for t in range(steps):
            q8_s[s, t:t + 1, :] = q_ref[t, s:s + 1, :]
            kc_s[s, WINDOW + t:WINDOW + t + 1, :] = kn_ref[t, s:s + 1, :]
            vc_s[s, WINDOW + t:WINDOW + t + 1, :] = vn_ref[t, s:s + 1, :]
        qs.append(q8_s[s])
        kreps.append(_replicate_heads(kc_s[s]))
        vreps.append(_replicate_heads(vc_s[s]))
    outs = _window_attention_batch(qs, kreps, vreps, sinks_ref, False)
    for s in range(SAMPLE_BS):
        for t in range(steps):
            o_ref[t, s:s + 1, :] = outs[s][t:t + 1, :]


def _attn_sample(q, kn, vn, ck, cv, sinks, steps, nseq):
    bs = SAMPLE_BS
    blk3 = lambda w: pl.BlockSpec((steps, bs, w), lambda i: (0, i, 0))
    cache = pl.BlockSpec((bs, WINDOW, KV_W), lambda i: (i, 0, 0))
    return pl.pallas_call(
        _attn_sample_kernel,
        grid=(nseq // bs,),
        in_specs=[blk3(ATTN_W), blk3(KV_W), blk3(KV_W), cache, cache, pl.BlockSpec(memory_space=pltpu.SMEM)],
        out_specs=blk3(ATTN_W),
        out_shape=jax.ShapeDtypeStruct((steps, nseq, ATTN_W), F32),
        scratch_shapes=[pltpu.VMEM((bs, SUBLANES, ATTN_W), F32), pltpu.VMEM((bs, 2 * WINDOW, KV_W), F32),
                        pltpu.VMEM((bs, 2 * WINDOW, KV_W), F32)],
        compiler_params=pltpu.CompilerParams(dimension_semantics=("arbitrary",)),
        name="attn_sample",
    )(q.reshape(steps, nseq, ATTN_W), kn.reshape(steps, nseq, KV_W), vn.reshape(steps, nseq, KV_W),
      ck, cv, sinks)


def _mix_sample_kernel(x1_ref, attn_ref, u_ref, sga_ref, sgs_ref, x0r_ref, x0i_ref,
                       wao_ref, wglu_ref, bglu_ref, wso_ref, wout_ref,
                       are_ref, aim_ref, ldt_ref, bbre_ref, bbim_ref, ccat_ref, d_ref,
                       x2_ref, sre_ref, sim_ref,
                       wbre_s, wbim_s, bur_s, bui_s):
    nseq = x0r_ref.shape[0]
    steps = x1_ref.shape[0] // nseq
    ab_re, ab_im, f_re, f_im = _ssm_discretize(are_ref, aim_ref, ldt_ref)
    _build_wb(bbre_ref, bbim_ref, f_re, f_im, wbre_s, wbim_s)
    u = u_ref[...]
    _expand_state(u.astype(BF16), wbre_s, wbim_s, bur_s, bui_s)
    for c0 in range(0, SSM_S, LANES):
        cs = slice(c0, c0 + LANES)
        ar, ai = ab_re[:, cs], ab_im[:, cs]
        xr, xi = x0r_ref[:, cs], x0i_ref[:, cs]
        for t in range(steps):
            rs = slice(t * nseq, (t + 1) * nseq)
            nr = ar * xr - ai * xi + bur_s[rs, cs]
            ni = ar * xi + ai * xr + bui_s[rs, cs]
            bur_s[rs, cs] = nr
            bui_s[rs, cs] = ni
            xr, xi = nr, ni
        sre_ref[:, cs] = xr
        sim_ref[:, cs] = xi
    y_ssm = _project_state(bur_s, bui_s, ccat_ref) + d_ref[...] * u
    x2_ref[...] = _gated_merge(x1_ref[...], attn_ref[...].astype(BF16), y_ssm, sga_ref[...], sgs_ref[...],
                               wao_ref, wglu_ref, bglu_ref, wso_ref, wout_ref)


def _mix_sample(x1, attn, u, sga, sgs, x0r, x0i, p):
    n, nseq = x1.shape[0], x0r.shape[0]
    args = [x1, attn, u, sga, sgs, x0r, x0i, p['wao'], p['wglu'], p['bglu'], p['wso'], p['wout'],
            p['are'], p['aim'], p['ldt'], p['bbre'], p['bbim'], p['ccat'], p['dskip']]
    return pl.pallas_call(
        _mix_sample_kernel,
        grid=(1,),
        in_specs=[_const_spec(a.shape) for a in args],
        out_specs=[pl.BlockSpec((n, D_MODEL), lambda i: (0, 0)), pl.BlockSpec((nseq, SSM_S), lambda i: (0, 0)),
                   pl.BlockSpec((nseq, SSM_S), lambda i: (0, 0))],
        out_shape=[jax.ShapeDtypeStruct((n, D_MODEL), F32),
                   jax.ShapeDtypeStruct((nseq, SSM_S), F32),
                   jax.ShapeDtypeStruct((nseq, SSM_S), F32)],
        scratch_shapes=[pltpu.VMEM((2, MXU_DIM, SSM_S // 2), BF16), pltpu.VMEM((2, MXU_DIM, SSM_S // 2), BF16),
                        pltpu.VMEM((n, SSM_S), F32), pltpu.VMEM((n, SSM_S), F32)],
        compiler_params=pltpu.CompilerParams(dimension_semantics=("arbitrary",),
                                             vmem_limit_bytes=VMEM_LIMIT),
        name="mix_sample",
    )(*args)


def _rope_tables(base_pos, off_pos):
    half = HEAD_DIM // 2
    lane = jnp.arange(LANES)
    inv = ROPE_THETA ** (-2.0 * (lane % half).astype(F32) / HEAD_DIM)
    first = (lane % HEAD_DIM) < half
    ab = base_pos.astype(F32)[:, None] * inv[None, :]
    ao = off_pos.astype(F32)[:, None] * inv[None, :]
    base = jnp.stack([jnp.cos(ab), jnp.sin(ab)], axis=1)
    co, so = jnp.cos(ao), jnp.sin(ao)
    lo = jnp.where(first, -1.0, 0.0).astype(F32)
    hi = jnp.where(first, 0.0, 1.0).astype(F32)
    off = jnp.stack([co, so, co * lo, so * lo, co * hi, so * hi], axis=0)
    return base, off


def _block_diag_halves(m):
    g, r, c = m.shape
    hg = g // 2
    eye = jnp.eye(hg, dtype=m.dtype)
    mh = m.reshape(2, hg, r, c)
    return (mh[:, :, :, None, :] * eye[None, :, None, :, None]).reshape(2, hg * r, hg * c)


def _layout_params(ffn1_norm, ffn1_w1, ffn1_w3, ffn1_w2, mix_norm, w_in, q_norm, k_norm, attn_sinks,
                   w_attn_out, ssm_a_re, ssm_a_im, ssm_log_dt, ssm_b_re, ssm_b_im, ssm_c_re, ssm_c_im,
                   ssm_d, w_glu, b_glu, w_ssm_out, w_out, ffn2_norm, ffn2_w1, ffn2_w3, ffn2_w2):
    l = 0
    row = lambda a: a.reshape(1, -1).astype(F32)
    head_ones = jnp.kron(jnp.eye(LANES // HEAD_DIM, dtype=F32), jnp.ones((HEAD_DIM, HEAD_DIM), F32))
    r = jnp.arange(MIX_TM)
    tok = (r % SUBLANES) * SCAN_SEG + r // SUBLANES
    perm = (tok[:, None] == jnp.arange(MIX_TM)[None, :]).astype(BF16)
    c_cat = jnp.concatenate([_block_diag_halves(jnp.swapaxes(ssm_c_re[l], 1, 2)),
                             -_block_diag_halves(jnp.swapaxes(ssm_c_im[l], 1, 2))], axis=1)
    return dict(
        n1=row(ffn1_norm[l]), w1a=ffn1_w1[l].astype(BF16), w3a=ffn1_w3[l].astype(BF16), w2a=ffn1_w2[l].astype(BF16),
        nm=row(mix_norm[l]), win=w_in[l].astype(BF16),
        qn=row(jnp.tile(q_norm[l], LANES // HEAD_DIM)), kn=row(jnp.tile(k_norm[l], LANES // HEAD_DIM)),
        ones=head_ones.astype(BF16), sinks=attn_sinks[l].astype(F32),
        wao=w_attn_out[l].astype(BF16), wglu=w_glu[l].astype(BF16), bglu=row(b_glu[l]),
        wso=w_ssm_out[l].astype(BF16), wout=w_out[l].astype(BF16),
        are=row(ssm_a_re[l]), aim=row(ssm_a_im[l]), ldt=row(jnp.repeat(ssm_log_dt[l], SSM_N)),
        bbre=_block_diag_halves(jnp.swapaxes(ssm_b_re[l], 1, 2).astype(F32)),
        bbim=_block_diag_halves(jnp.swapaxes(ssm_b_im[l], 1, 2).astype(F32)),
        ccat=c_cat.astype(BF16), dskip=row(ssm_d[l]), perm=perm, permt=perm.T,
        n2=row(ffn2_norm[l]), w1b=ffn2_w1[l].astype(BF16), w3b=ffn2_w3[l].astype(BF16), w2b=ffn2_w2[l].astype(BF16),
    )


def kernel(x_prompt, x_sample, cache_k, cache_v, state_ssm_re, state_ssm_im, ffn1_norm, ffn1_w1, ffn1_w3, ffn1_w2, mix_norm, w_in, q_norm, k_norm, attn_sinks, w_attn_out, ssm_a_re, ssm_a_im, ssm_log_dt, ssm_b_re, ssm_b_im, ssm_c_re, ssm_c_im, ssm_d, w_glu, b_glu, w_ssm_out, w_out, ffn2_norm, ffn2_w1, ffn2_w3, ffn2_w2):
    assert ffn1_norm.shape[0] == 1, "single trunk layer"
    batch, seq, _ = x_prompt.shape
    nseq, steps, _ = x_sample.shape
    p = _layout_params(ffn1_norm, ffn1_w1, ffn1_w3, ffn1_w2, mix_norm, w_in, q_norm, k_norm, attn_sinks,
                       w_attn_out, ssm_a_re, ssm_a_im, ssm_log_dt, ssm_b_re, ssm_b_im, ssm_c_re, ssm_c_im,
                       ssm_d, w_glu, b_glu, w_ssm_out, w_out, ffn2_norm, ffn2_w1, ffn2_w3, ffn2_w2)

    tile0 = jnp.arange(batch * seq // FRONT_TM) * FRONT_TM
    tabs_p = _rope_tables(tile0 % seq, jnp.arange(FRONT_TM))
    fr = _front(x_prompt.reshape(batch * seq, D_MODEL), tabs_p, p, BF16)
    x2p, re_p, im_p = _mix_prompt(fr, p, batch, seq)
    y_prompt = _ffn(x2p, p).reshape(batch, seq, D_MODEL)
    tail_kv = lambda a: a.reshape(batch, seq, KV_W)[:, seq - WINDOW:].reshape(batch, WINDOW, N_KV, HEAD_DIM)
    k_p, v_p = tail_kv(fr[2]), tail_kv(fr[3])

    xs = jnp.swapaxes(x_sample, 0, 1).reshape(steps * nseq, D_MODEL)
    tile0 = jnp.arange(steps * nseq // FRONT_TM) * FRONT_TM
    tabs_s = _rope_tables(PAST_LEN + tile0 // nseq, jnp.arange(FRONT_TM) // nseq)
    x1s, qs, ks, vs, us, sgas, sgss = _front(xs, tabs_s, p, F32)
    ck = cache_k[0].reshape(nseq, WINDOW, KV_W)
    cv = cache_v[0].reshape(nseq, WINDOW, KV_W)
    attn_s = _attn_sample(qs, ks, vs, ck, cv, p['sinks'], steps, nseq).reshape(steps * nseq, ATTN_W)
    x2s, re_s, im_s = _mix_sample(x1s, attn_s, us, sgas, sgss,
                                  state_ssm_re[0].reshape(nseq, SSM_S), state_ssm_im[0].reshape(nseq, SSM_S), p)
    y_sample = jnp.swapaxes(_ffn(x2s, p).reshape(steps, nseq, D_MODEL), 0, 1)
    new_kv = lambda a: jnp.swapaxes(a.reshape(steps, nseq, N_KV, HEAD_DIM), 0, 1)
    k_s = jnp.concatenate([cache_k[0][:, steps:], new_kv(ks)], axis=1)
    v_s = jnp.concatenate([cache_v[0][:, steps:], new_kv(vs)], axis=1)

    st = lambda a, n: a.reshape(1, n, SSM_G, SSM_N)
    return (y_prompt, y_sample, k_p[None], v_p[None], st(re_p, batch), st(im_p, batch),
            k_s[None], v_s[None], st(re_s, nseq), st(im_s, nseq))
```

```python
import functools
import math

import jax
import jax.numpy as jnp
from jax import lax
from jax.experimental import pallas as pl
from jax.experimental.pallas import tpu as pltpu

F32 = jnp.float32
BF16 = jnp.bfloat16

D_MODEL = 1024
HEAD_DIM = 64
N_HEADS = 8
N_KV = 2
WINDOW = 128
ATTN_W = N_HEADS * HEAD_DIM
KV_W = N_KV * HEAD_DIM
SSM_W = 512
SSM_G = 32
SSM_CH = 16
SSM_N = 64
SSM_S = SSM_G * SSM_N
D_FF = 2816
IN_COLS = ATTN_W + 2 * KV_W + SSM_W + 2 * D_MODEL
ROPE_THETA = 10000.0
RMS_EPS = 1e-6
MASK_VALUE = -1e30
PAST_LEN = 16384

LANES = 128
SUBLANES = 8
MXU_DIM = 256

FRONT_TM = 512
MIX_TM = 256
SCAN_SEG = MIX_TM // SUBLANES
FF_CHUNK = 256
SCAN_LC = 512
SAMPLE_BS = 8
VMEM_LIMIT = 56 * 1024 * 1024


def _const_spec(shape):
    nd = len(shape)
    return pl.BlockSpec(shape, lambda *_: (0,) * nd, pipeline_mode=pl.Buffered(1))


def _rms(x, w):
    ms = jnp.mean(x * x, axis=-1, keepdims=True)
    return x * lax.rsqrt(ms + RMS_EPS) * w


def _swiglu_residual(x, nw_ref, w1_ref, w3_ref, w2_ref):
    h = _rms(x, nw_ref[...]).astype(BF16)
    acc = None
    for c0 in range(0, D_FF, FF_CHUNK):
        a = jnp.dot(h, w1_ref[:, c0:c0 + FF_CHUNK], preferred_element_type=F32)
        b = jnp.dot(h, w3_ref[:, c0:c0 + FF_CHUNK], preferred_element_type=F32)
        g = (a * jax.nn.sigmoid(a) * b).astype(BF16)
        t = jnp.dot(g, w2_ref[c0:c0 + FF_CHUNK, :], preferred_element_type=F32)
        acc = t if acc is None else acc + t
    return x + 0.5 * acc


def _head_norm_rope(t, ones_ref, g_ref, cos, sinlo, sinhi, scale):
    ssum = jnp.dot((t * t).astype(BF16), ones_ref[...], preferred_element_type=F32)
    tn = t * lax.rsqrt(ssum * (1.0 / HEAD_DIM) + RMS_EPS) * g_ref[...]
    up = pltpu.roll(tn, HEAD_DIM // 2, axis=1)
    dn = pltpu.roll(tn, LANES - HEAD_DIM // 2, axis=1)
    r = tn * cos + dn * sinlo + up * sinhi
    return r * scale if scale != 1.0 else r


def _front_kernel(x_ref, base_ref, off_ref,
                  n1_ref, w1_ref, w3_ref, w2_ref, nm_ref, win_ref, qn_ref, kn_ref, ones_ref,
                  x1_ref, q_ref, k_ref, v_ref, u_ref, sga_ref, sgs_ref):
    x1 = _swiglu_residual(x_ref[...], n1_ref, w1_ref, w3_ref, w2_ref)
    x1_ref[...] = x1
    h = _rms(x1, nm_ref[...]).astype(BF16)
    cb, sb = base_ref[0:1, :], base_ref[1:2, :]
    cos = cb * off_ref[0] - sb * off_ref[1]
    sinlo = sb * off_ref[2] + cb * off_ref[3]
    sinhi = sb * off_ref[4] + cb * off_ref[5]
    o = ATTN_W + 2 * KV_W + SSM_W
    qkvu = jnp.dot(h, win_ref[:, 0:o], preferred_element_type=F32)
    for j in range(ATTN_W // LANES):
        q_ref[:, j * LANES:(j + 1) * LANES] = _head_norm_rope(
            qkvu[:, j * LANES:(j + 1) * LANES], ones_ref, qn_ref, cos, sinlo, sinhi,
            HEAD_DIM ** -0.5).astype(q_ref.dtype)
    k_ref[...] = _head_norm_rope(qkvu[:, ATTN_W:ATTN_W + KV_W], ones_ref, kn_ref, cos, sinlo, sinhi, 1.0)
    v_ref[...] = qkvu[:, ATTN_W + KV_W:ATTN_W + 2 * KV_W]
    u_ref[...] = qkvu[:, ATTN_W + 2 * KV_W:o]
    ga = jnp.dot(h, win_ref[:, o:o + D_MODEL], preferred_element_type=F32)
    sga_ref[...] = jax.nn.sigmoid(ga).astype(BF16)
    o += D_MODEL
    gs = jnp.dot(h, win_ref[:, o:o + D_MODEL], preferred_element_type=F32)
    sgs_ref[...] = jax.nn.sigmoid(gs).astype(BF16)


def _front(x, tabs, p, q_dtype):
    n = x.shape[0]
    tm = FRONT_TM
    base_tab, off_tab = tabs
    row = lambda w: pl.BlockSpec((tm, w), lambda i: (i, 0))
    consts = [p['n1'], p['w1a'], p['w3a'], p['w2a'], p['nm'], p['win'], p['qn'], p['kn'], p['ones']]
    outs = [(D_MODEL, F32), (ATTN_W, q_dtype), (KV_W, F32), (KV_W, F32), (SSM_W, F32),
            (D_MODEL, BF16), (D_MODEL, BF16)]
    return pl.pallas_call(
        _front_kernel,
        grid=(n // tm,),
        in_specs=[row(D_MODEL), pl.BlockSpec((None, 2, LANES), lambda i: (i, 0, 0)), _const_spec(off_tab.shape)]
        + [_const_spec(c.shape) for c in consts],
        out_specs=[row(w) for w, _ in outs],
        out_shape=[jax.ShapeDtypeStruct((n, w), dt) for w, dt in outs],
        compiler_params=pltpu.CompilerParams(dimension_semantics=("arbitrary",),
                                             vmem_limit_bytes=VMEM_LIMIT),
        name="front",
    )(x, base_tab, off_tab, *consts)


def _ffn_kernel(x_ref, n_ref, w1_ref, w3_ref, w2_ref, y_ref):
    y_ref[...] = _swiglu_residual(x_ref[...], n_ref, w1_ref, w3_ref, w2_ref)


def _ffn(x, p):
    n = x.shape[0]
    tm = FRONT_TM
    consts = [p['n2'], p['w1b'], p['w3b'], p['w2b']]
    return pl.pallas_call(
        _ffn_kernel,
        grid=(n // tm,),
        in_specs=[pl.BlockSpec((tm, D_MODEL), lambda i: (i, 0))] + [_const_spec(c.shape) for c in consts],
        out_specs=pl.BlockSpec((tm, D_MODEL), lambda i: (i, 0)),
        out_shape=jax.ShapeDtypeStruct((n, D_MODEL), F32),
        compiler_params=pltpu.CompilerParams(dimension_semantics=("arbitrary",),
                                             vmem_limit_bytes=VMEM_LIMIT),
        name="ffn",
    )(x, *consts)


def _cmul(ar, ai, br, bi):
    return ar * br - ai * bi, ar * bi + ai * br


def _ssm_discretize(are_ref, aim_ref, ldt_ref):
    a_re, a_im = are_ref[...], aim_ref[...]
    dt = jnp.exp(ldt_ref[...])
    mag = jnp.exp(dt * a_re)
    ab_re = mag * jnp.cos(dt * a_im)
    ab_im = mag * jnp.sin(dt * a_im)
    den = a_re * a_re + a_im * a_im
    nr, ni = ab_re - 1.0, ab_im
    f_re = (nr * a_re + ni * a_im) / den
    f_im = (ni * a_re - nr * a_im) / den
    return ab_re, ab_im, f_re, f_im


def _build_wb(bbre_ref, bbim_ref, f_re, f_im, wbre_s, wbim_s):
    half = SSM_S // 2
    for hh in range(2):
        fr, fi = f_re[:, hh * half:(hh + 1) * half], f_im[:, hh * half:(hh + 1) * half]
        br, bi = bbre_ref[hh], bbim_ref[hh]
        wbre_s[hh] = (br * fr - bi * fi).astype(BF16)
        wbim_s[hh] = (br * fi + bi * fr).astype(BF16)


def _expand_state(ub, wbre_s, wbim_s, bur_s, bui_s):
    half_in, half_st = SSM_W // 2, SSM_S // 2
    for hh in range(2):
        uh = ub[:, hh * half_in:(hh + 1) * half_in]
        bur_s[:, hh * half_st:(hh + 1) * half_st] = jnp.dot(uh, wbre_s[hh], preferred_element_type=F32)
        bui_s[:, hh * half_st:(hh + 1) * half_st] = jnp.dot(uh, wbim_s[hh], preferred_element_type=F32)


def _project_state(xr_s, xi_s, ccat_ref):
    half_st = SSM_S // 2
    outs = []
    for hh in range(2):
        sl = slice(hh * half_st, (hh + 1) * half_st)
        xcat = jnp.concatenate([xr_s[:, sl].astype(BF16), xi_s[:, sl].astype(BF16)], axis=1)
        outs.append(jnp.dot(xcat, ccat_ref[hh], preferred_element_type=F32))
    return jnp.concatenate(outs, axis=1)


def _gated_attn(attn_b, sga, wao_ref):
    return sga.astype(F32) * jnp.dot(attn_b, wao_ref[...], preferred_element_type=F32)


def _gated_merge(x1, attn_gated, y_ssm, sgs, wglu_ref, bglu_ref, wso_ref, wout_ref):
    z = jax.nn.gelu(y_ssm, approximate=True)
    gl = jnp.dot(z.astype(BF16), wglu_ref[...], preferred_element_type=F32) + bglu_ref[...]
    ssm = (z * jax.nn.sigmoid(gl)).astype(BF16)
    merged = attn_gated + sgs.astype(F32) * jnp.dot(ssm, wso_ref[...], preferred_element_type=F32)
    return x1 + jnp.dot(merged.astype(BF16), wout_ref[...], preferred_element_type=F32)


def _replicate_heads(kv):
    sw = pltpu.roll(kv, HEAD_DIM, axis=1)
    lo = lax.broadcasted_iota(jnp.int32, kv.shape, 1) < HEAD_DIM
    return jnp.where(lo, kv, sw).astype(BF16), jnp.where(lo, sw, kv).astype(BF16)


def _attn_scores(qs, kreps):
    qb = qs[0].shape[0]
    lo = lax.broadcasted_iota(jnp.int32, (qb, LANES), 1) < HEAD_DIM
    scores = []
    for q, krep in zip(qs, kreps):
        if q.dtype != BF16 or qb % (2 * SUBLANES):
            q = q.astype(F32)
        zero = jnp.zeros((), q.dtype)
        for kv in range(N_KV):
            parts = []
            for jj in range(2):
                t = q[:, (2 * kv + jj) * LANES:(2 * kv + jj + 1) * LANES]
                parts += [jnp.where(lo, t, zero), jnp.where(lo, zero, t)]
            stacked = jnp.concatenate(parts, axis=0).astype(BF16)
            scores.append(lax.dot_general(stacked, krep[kv], (((1,), (1,)), ((), ())),
                                          preferred_element_type=F32))
    return jnp.concatenate(scores, axis=0)


def _attn_softmax(s, sinks_ref, qb, firsts):
    per_block = s.shape[0] // len(firsts)
    sink_parts = [jnp.full((qb, 1), sinks_ref[h], F32) for h in range(N_HEADS)]
    sink = jnp.concatenate(sink_parts * len(firsts), axis=0)
    qi = lax.broadcasted_iota(jnp.int32, (per_block, 2 * WINDOW), 0) & (qb - 1)
    sj = lax.broadcasted_iota(jnp.int32, (per_block, 2 * WINDOW), 1)
    band = (sj <= WINDOW + qi) & (sj > qi)
    valid = jnp.concatenate(
        [band if f is False else band & ((sj >= WINDOW) | jnp.logical_not(f)) for f in firsts], axis=0)
    s = jnp.where(valid, s, MASK_VALUE)
    m = jnp.maximum(jnp.max(s, axis=-1, keepdims=True), sink)
    e = jnp.exp(s - m)
    denom = jnp.sum(e, axis=-1, keepdims=True) + jnp.exp(sink - m)
    return (e * (1.0 / denom)).astype(BF16)


def _attn_values(probs, vreps, qb):
    lo = lax.broadcasted_iota(jnp.int32, (qb, LANES), 1) < HEAD_DIM
    rows = 2 * (N_HEADS // N_KV // 2) * qb
    outs = []
    for b, vrep in enumerate(vreps):
        tiles = []
        for kv in range(N_KV):
            r0 = (b * N_KV + kv) * rows
            o = jnp.dot(probs[r0:r0 + rows], vrep[kv], preferred_element_type=F32)
            for jj in range(2):
                oa = o[(2 * jj) * qb:(2 * jj + 1) * qb]
                ob = o[(2 * jj + 1) * qb:(2 * jj + 2) * qb]
                tiles.append(jnp.where(lo, oa, ob))
        outs.append(jnp.concatenate(tiles, axis=1))
    return outs


def _mix_prompt_kernel(x1_ref, q_ref, k_ref, v_ref, u_ref, sga_ref, sgs_ref, sinks_ref,
                       wao_ref, wglu_ref, bglu_ref, wso_ref, wout_ref,
                       are_ref, aim_ref, ldt_ref, bbre_ref, bbim_ref, ccat_ref, d_ref, perm_ref, permt_ref,
                       x2_ref, sre_ref, sim_ref,
                       wbre_s, wbim_s, apr_s, api_s, a1_s, aseg_s, carry_s, cin_s,
                       bur_s, bui_s, xb_s, kprev_s, vprev_s):
    b, i = pl.program_id(0), pl.program_id(1)
    nt = pl.num_programs(1)

    @pl.when((b == 0) & (i == 0))
    def _():
        ab_re, ab_im, f_re, f_im = _ssm_discretize(are_ref, aim_ref, ldt_ref)
        _build_wb(bbre_ref, bbim_ref, f_re, f_im, wbre_s, wbim_s)
        a1_s[0:1, :] = ab_re
        a1_s[1:2, :] = ab_im
        pr, pi = ab_re, ab_im
        for j in range(SCAN_SEG):
            apr_s[j * SUBLANES:(j + 1) * SUBLANES, :] = jnp.broadcast_to(pr, (SUBLANES, SSM_S))
            api_s[j * SUBLANES:(j + 1) * SUBLANES, :] = jnp.broadcast_to(pi, (SUBLANES, SSM_S))
            if j == SCAN_SEG - 1:
                aseg_s[0:1, :] = pr
                aseg_s[1:2, :] = pi
            pr, pi = _cmul(pr, pi, ab_re, ab_im)

    @pl.when(i == 0)
    def _():
        carry_s[...] = jnp.zeros_like(carry_s)
        kprev_s[...] = jnp.zeros_like(kprev_s)
        vprev_s[...] = jnp.zeros_like(vprev_s)

    k0, k1 = _replicate_heads(k_ref[...])
    v0, v1 = _replicate_heads(v_ref[...])
    qs, kreps, vreps, firsts = [], [], [], []
    for blk in range(MIX_TM // WINDOW):
        cur = slice(blk * WINDOW, (blk + 1) * WINDOW)
        if blk == 0:
            prev = [kprev_s[0], kprev_s[1], vprev_s[0], vprev_s[1]]
            firsts.append(i == 0)
        else:
            ps = slice((blk - 1) * WINDOW, blk * WINDOW)
            prev = [k0[ps], k1[ps], v0[ps], v1[ps]]
            firsts.append(False)
        qs.append(q_ref[cur, :])
        kreps.append([jnp.concatenate([prev[0], k0[cur]], axis=0), jnp.concatenate([prev[1], k1[cur]], axis=0)])
        vreps.append([jnp.concatenate([prev[2], v0[cur]], axis=0), jnp.concatenate([prev[3], v1[cur]], axis=0)])
    tail = slice(MIX_TM - WINDOW, MIX_TM)
    kprev_s[0], kprev_s[1] = k0[tail], k1[tail]
    vprev_s[0], vprev_s[1] = v0[tail], v1[tail]
    scores = _attn_scores(qs, kreps)

    half_in, half_st = SSM_W // 2, SSM_S // 2
    chunks = list(range(0, SSM_S, SCAN_LC))
    blk_rows = scores.shape[0] // len(qs)
    u = u_ref[...]
    up = jnp.dot(perm_ref[...], u.astype(BF16), preferred_element_type=F32).astype(BF16)

    def expand_half(hh):
        uh = up[:, hh * half_in:(hh + 1) * half_in]
        bur_s[:, hh * half_st:(hh + 1) * half_st] = jnp.dot(uh, wbre_s[hh], preferred_element_type=F32)
        bui_s[:, hh * half_st:(hh + 1) * half_st] = jnp.dot(uh, wbim_s[hh], preferred_element_type=F32)

    def softmax_block(bk):
        return _attn_softmax(scores[bk * blk_rows:(bk + 1) * blk_rows], sinks_ref, WINDOW, [firsts[bk]])

    def scan_chunk(c0):
        cs = slice(c0, c0 + SCAN_LC)
        ar = jnp.broadcast_to(a1_s[0:1, cs], (SUBLANES, SCAN_LC))
        ai = jnp.broadcast_to(a1_s[1:2, cs], (SUBLANES, SCAN_LC))
        xr, xi = bur_s[0:SUBLANES, cs], bui_s[0:SUBLANES, cs]
        for j in range(1, SCAN_SEG):
            rs = slice(j * SUBLANES, (j + 1) * SUBLANES)
            xr, xi = (ar * xr - ai * xi + bur_s[rs, cs], ar * xi + ai * xr + bui_s[rs, cs])
            bur_s[rs, cs] = xr
            bui_s[rs, cs] = xi

    expand_half(0)
    probs = [softmax_block(0)]
    expand_half(1)
    probs.append(softmax_block(1))
    scan_chunk(chunks[0])
    attn = _attn_values(probs[0], vreps[0:1], WINDOW)
    scan_chunk(chunks[1])
    attn += _attn_values(probs[1], vreps[1:2], WINDOW)
    scan_chunk(chunks[2])
    attn_gated = _gated_attn(jnp.concatenate(attn, axis=0).astype(BF16), sga_ref[...], wao_ref)
    scan_chunk(chunks[3])

    last = MIX_TM - SUBLANES
    sr, si = aseg_s[0:1, :], aseg_s[1:2, :]
    cr, ci = carry_s[0:1, :], carry_s[1:2, :]
    for s in range(SUBLANES):
        cin_s[0, s:s + 1, :] = cr
        cin_s[1, s:s + 1, :] = ci
        pr, pi = _cmul(sr, si, cr, ci)
        cr = pr + bur_s[last + s:last + s + 1, :]
        ci = pi + bui_s[last + s:last + s + 1, :]
    carry_s[0:1, :] = cr
    carry_s[1:2, :] = ci
    pack = 2 * SUBLANES

    def fix_chunk(c0):
        cs = slice(c0, c0 + SCAN_LC)
        cinr = jnp.concatenate([cin_s[0, :, cs]] * 2, axis=0)
        cini = jnp.concatenate([cin_s[1, :, cs]] * 2, axis=0)
        col = (c0 // half_st) * SSM_S + c0 % half_st
        for r0 in range(0, MIX_TM, pack):
            rs = slice(r0, r0 + pack)
            pr, pi = apr_s[rs, cs], api_s[rs, cs]
            xb_s[rs, col:col + SCAN_LC] = (bur_s[rs, cs] + pr * cinr - pi * cini).astype(BF16)
            xb_s[rs, col + half_st:col + half_st + SCAN_LC] = (bui_s[rs, cs] + pr * cini + pi * cinr).astype(BF16)

    def project_half(hh):
        return jnp.dot(xb_s[:, hh * SSM_S:(hh + 1) * SSM_S], ccat_ref[hh], preferred_element_type=F32)

    per_half = len(chunks) // 2
    for c0 in chunks[:per_half]:
        fix_chunk(c0)
    yp0 = project_half(0)
    for c0 in chunks[per_half:]:
        fix_chunk(c0)
    yp = jnp.concatenate([yp0, project_half(1)], axis=1)
    y_hi = yp.astype(BF16)
    y_lo = (yp - y_hi.astype(F32)).astype(BF16)
    y_ssm = (jnp.dot(permt_ref[...], y_hi, preferred_element_type=F32)
             + jnp.dot(permt_ref[...], y_lo, preferred_element_type=F32) + d_ref[...] * u)

    x2_ref[...] = _gated_merge(x1_ref[...], attn_gated, y_ssm, sgs_ref[...],
                               wglu_ref, bglu_ref, wso_ref, wout_ref)

    @pl.when(i == nt - 1)
    def _():
        sre_ref[...] = carry_s[0:1, :]
        sim_ref[...] = carry_s[1:2, :]


def _mix_prompt(fr, p, batch, seq):
    x1, q, k, v, u, sga, sgs = fr
    tm = MIX_TM
    nt = seq // tm
    row = lambda w: pl.BlockSpec((tm, w), lambda b, i: (b * nt + i, 0))
    consts = [p['wao'], p['wglu'], p['bglu'], p['wso'], p['wout'],
              p['are'], p['aim'], p['ldt'], p['bbre'], p['bbim'], p['ccat'], p['dskip'], p['perm'], p['permt']]
    st_spec = pl.BlockSpec((None, 1, SSM_S), lambda b, i: (b, 0, 0))
    scratch = [pltpu.VMEM((2, MXU_DIM, SSM_S // 2), BF16), pltpu.VMEM((2, MXU_DIM, SSM_S // 2), BF16),
               pltpu.VMEM((tm, SSM_S), F32), pltpu.VMEM((tm, SSM_S), F32),
               pltpu.VMEM((SUBLANES, SSM_S), F32), pltpu.VMEM((SUBLANES, SSM_S), F32),
               pltpu.VMEM((SUBLANES, SSM_S), F32), pltpu.VMEM((2, SUBLANES, SSM_S), F32),
               pltpu.VMEM((tm, SSM_S), F32), pltpu.VMEM((tm, SSM_S), F32), pltpu.VMEM((tm, 2 * SSM_S), BF16),
               pltpu.VMEM((N_KV, WINDOW, LANES), BF16), pltpu.VMEM((N_KV, WINDOW, LANES), BF16)]
    return pl.pallas_call(
        _mix_prompt_kernel,
        grid=(batch, nt),
        in_specs=[row(D_MODEL), row(ATTN_W), row(KV_W), row(KV_W), row(SSM_W), row(D_MODEL), row(D_MODEL),
                  pl.BlockSpec(memory_space=pltpu.SMEM)] + [_const_spec(c.shape) for c in consts],
        out_specs=[row(D_MODEL), st_spec, st_spec],
        out_shape=[jax.ShapeDtypeStruct((batch * seq, D_MODEL), F32),
                   jax.ShapeDtypeStruct((batch, 1, SSM_S), F32),
                   jax.ShapeDtypeStruct((batch, 1, SSM_S), F32)],
        scratch_shapes=scratch,
        compiler_params=pltpu.CompilerParams(dimension_semantics=("arbitrary", "arbitrary"),
                                             vmem_limit_bytes=VMEM_LIMIT),
        name="mix_prompt",
    )(x1, q, k, v, u, sga, sgs, p['sinks'], *consts)


def _attn_sample_kernel(q_ref, kn_ref, vn_ref, ck_ref, cv_ref, sinks_ref, o_ref, q8_s, kc_s, vc_s):
    steps = q_ref.shape[0]
    q8_s[...] = jnp.zeros_like(q8_s)
    kc_s[...] = jnp.zeros_like(kc_s)
    vc_s[...] = jnp.zeros_like(vc_s)
    qs, kreps, vreps = [], [], []
    for s in range(SAMPLE_BS):
        kc_s[s, 0:WINDOW, :] = ck_ref[s]
        vc_s[s, 0:WINDOW, :] = cv_ref[s]
        for t in range(steps):
            q8_s[s, t:t + 1, :] = q_ref[t, s:s + 1, :]
            kc_s[s, WINDOW + t:WINDOW + t + 1, :] = kn_ref[t, s:s + 1, :]
            vc_s[s, WINDOW + t:WINDOW + t + 1, :] = vn_ref[t, s:s + 1, :]
        qs.append(q8_s[s])
        kreps.append(_replicate_heads(kc_s[s]))
        vreps.append(_replicate_heads(vc_s[s]))
    probs = _attn_softmax(_attn_scores(qs, kreps), sinks_ref, SUBLANES, [False] * SAMPLE_BS)
    outs = _attn_values(probs, vreps, SUBLANES)
    for s in range(SAMPLE_BS):
        for t in range(steps):
            o_ref[t, s:s + 1, :] = outs[s][t:t + 1, :]


def _attn_sample(q, kn, vn, ck, cv, sinks, steps, nseq):
    bs = SAMPLE_BS
    blk3 = lambda w: pl.BlockSpec((steps, bs, w), lambda i: (0, i, 0))
    cache = pl.BlockSpec((bs, WINDOW, KV_W), lambda i: (i, 0, 0))
    return pl.pallas_call(
        _attn_sample_kernel,
        grid=(nseq // bs,),
        in_specs=[blk3(ATTN_W), blk3(KV_W), blk3(KV_W), cache, cache, pl.BlockSpec(memory_space=pltpu.SMEM)],
        out_specs=blk3(ATTN_W),
        out_shape=jax.ShapeDtypeStruct((steps, nseq, ATTN_W), F32),
        scratch_shapes=[pltpu.VMEM((bs, SUBLANES, ATTN_W), F32), pltpu.VMEM((bs, 2 * WINDOW, KV_W), F32),
                        pltpu.VMEM((bs, 2 * WINDOW, KV_W), F32)],
        compiler_params=pltpu.CompilerParams(dimension_semantics=("arbitrary",)),
        name="attn_sample",
    )(q.reshape(steps, nseq, ATTN_W), kn.reshape(steps, nseq, KV_W), vn.reshape(steps, nseq, KV_W),
      ck, cv, sinks)


def _mix_sample_kernel(x1_ref, attn_ref, u_ref, sga_ref, sgs_ref, x0r_ref, x0i_ref,
                       wao_ref, wglu_ref, bglu_ref, wso_ref, wout_ref,
                       are_ref, aim_ref, ldt_ref, bbre_ref, bbim_ref, ccat_ref, d_ref,
                       x2_ref, sre_ref, sim_ref,
                       wbre_s, wbim_s, bur_s, bui_s):
    nseq = x0r_ref.shape[0]
    steps = x1_ref.shape[0] // nseq
    ab_re, ab_im, f_re, f_im = _ssm_discretize(are_ref, aim_ref, ldt_ref)
    _build_wb(bbre_ref, bbim_ref, f_re, f_im, wbre_s, wbim_s)
    u = u_ref[...]
    _expand_state(u.astype(BF16), wbre_s, wbim_s, bur_s, bui_s)
    for c0 in range(0, SSM_S, LANES):
        cs = slice(c0, c0 + LANES)
        ar, ai = ab_re[:, cs], ab_im[:, cs]
        xr, xi = x0r_ref[:, cs], x0i_ref[:, cs]
        for t in range(steps):
            rs = slice(t * nseq, (t + 1) * nseq)
            nr = ar * xr - ai * xi + bur_s[rs, cs]
            ni = ar * xi + ai * xr + bui_s[rs, cs]
            bur_s[rs, cs] = nr
            bui_s[rs, cs] = ni
            xr, xi = nr, ni
        sre_ref[:, cs] = xr
        sim_ref[:, cs] = xi
    y_ssm = _project_state(bur_s, bui_s, ccat_ref) + d_ref[...] * u
    attn_gated = _gated_attn(attn_ref[...].astype(BF16), sga_ref[...], wao_ref)
    x2_ref[...] = _gated_merge(x1_ref[...], attn_gated, y_ssm, sgs_ref[...],
                               wglu_ref, bglu_ref, wso_ref, wout_ref)


def _mix_sample(x1, attn, u, sga, sgs, x0r, x0i, p):
    n, nseq = x1.shape[0], x0r.shape[0]
    args = [x1, attn, u, sga, sgs, x0r, x0i, p['wao'], p['wglu'], p['bglu'], p['wso'], p['wout'],
            p['are'], p['aim'], p['ldt'], p['bbre'], p['bbim'], p['ccat'], p['dskip']]
    return pl.pallas_call(
        _mix_sample_kernel,
        grid=(1,),
        in_specs=[_const_spec(a.shape) for a in args],
        out_specs=[pl.BlockSpec((n, D_MODEL), lambda i: (0, 0)), pl.BlockSpec((nseq, SSM_S), lambda i: (0, 0)),
                   pl.BlockSpec((nseq, SSM_S), lambda i: (0, 0))],
        out_shape=[jax.ShapeDtypeStruct((n, D_MODEL), F32),
                   jax.ShapeDtypeStruct((nseq, SSM_S), F32),
                   jax.ShapeDtypeStruct((nseq, SSM_S), F32)],
        scratch_shapes=[pltpu.VMEM((2, MXU_DIM, SSM_S // 2), BF16), pltpu.VMEM((2, MXU_DIM, SSM_S // 2), BF16),
                        pltpu.VMEM((n, SSM_S), F32), pltpu.VMEM((n, SSM_S), F32)],
        compiler_params=pltpu.CompilerParams(dimension_semantics=("arbitrary",),
                                             vmem_limit_bytes=VMEM_LIMIT),
        name="mix_sample",
    )(*args)


def _rope_tables(base_pos, off_pos):
    half = HEAD_DIM // 2
    lane = jnp.arange(LANES)
    inv = ROPE_THETA ** (-2.0 * (lane % half).astype(F32) / HEAD_DIM)
    first = (lane % HEAD_DIM) < half
    ab = base_pos.astype(F32)[:, None] * inv[None, :]
    ao = off_pos.astype(F32)[:, None] * inv[None, :]
    base = jnp.stack([jnp.cos(ab), jnp.sin(ab)], axis=1)
    co, so = jnp.cos(ao), jnp.sin(ao)
    lo = jnp.where(first, -1.0, 0.0).astype(F32)
    hi = jnp.where(first, 0.0, 1.0).astype(F32)
    off = jnp.stack([co, so, co * lo, so * lo, co * hi, so * hi], axis=0)
    return base, off


def _block_diag_halves(m):
    g, r, c = m.shape
    hg = g // 2
    eye = jnp.eye(hg, dtype=m.dtype)
    mh = m.reshape(2, hg, r, c)
    return (mh[:, :, :, None, :] * eye[None, :, None, :, None]).reshape(2, hg * r, hg * c)


def _layout_params(ffn1_norm, ffn1_w1, ffn1_w3, ffn1_w2, mix_norm, w_in, q_norm, k_norm, attn_sinks,
                   w_attn_out, ssm_a_re, ssm_a_im, ssm_log_dt, ssm_b_re, ssm_b_im, ssm_c_re, ssm_c_im,
                   ssm_d, w_glu, b_glu, w_ssm_out, w_out, ffn2_norm, ffn2_w1, ffn2_w3, ffn2_w2):
    l = 0
    row = lambda a: a.reshape(1, -1).astype(F32)
    head_ones = jnp.kron(jnp.eye(LANES // HEAD_DIM, dtype=F32), jnp.ones((HEAD_DIM, HEAD_DIM), F32))
    r = jnp.arange(MIX_TM)
    tok = (r % SUBLANES) * SCAN_SEG + r // SUBLANES
    perm = (tok[:, None] == jnp.arange(MIX_TM)[None, :]).astype(BF16)
    c_cat = jnp.concatenate([_block_diag_halves(jnp.swapaxes(ssm_c_re[l], 1, 2)),
                             -_block_diag_halves(jnp.swapaxes(ssm_c_im[l], 1, 2))], axis=1)
    return dict(
        n1=row(ffn1_norm[l]), w1a=ffn1_w1[l].astype(BF16), w3a=ffn1_w3[l].astype(BF16), w2a=ffn1_w2[l].astype(BF16),
        nm=row(mix_norm[l]), win=w_in[l].astype(BF16),
        qn=row(jnp.tile(q_norm[l], LANES // HEAD_DIM)), kn=row(jnp.tile(k_norm[l], LANES // HEAD_DIM)),
        ones=head_ones.astype(BF16), sinks=attn_sinks[l].astype(F32),
        wao=w_attn_out[l].astype(BF16), wglu=w_glu[l].astype(BF16), bglu=row(b_glu[l]),
        wso=w_ssm_out[l].astype(BF16), wout=w_out[l].astype(BF16),
        are=row(ssm_a_re[l]), aim=row(ssm_a_im[l]), ldt=row(jnp.repeat(ssm_log_dt[l], SSM_N)),
        bbre=_block_diag_halves(jnp.swapaxes(ssm_b_re[l], 1, 2).astype(F32)),
        bbim=_block_diag_halves(jnp.swapaxes(ssm_b_im[l], 1, 2).astype(F32)),
        ccat=c_cat.astype(BF16), dskip=row(ssm_d[l]), perm=perm, permt=perm.T,
        n2=row(ffn2_norm[l]), w1b=ffn2_w1[l].astype(BF16), w3b=ffn2_w3[l].astype(BF16), w2b=ffn2_w2[l].astype(BF16),
    )


def kernel(x_prompt, x_sample, cache_k, cache_v, state_ssm_re, state_ssm_im, ffn1_norm, ffn1_w1, ffn1_w3, ffn1_w2, mix_norm, w_in, q_norm, k_norm, attn_sinks, w_attn_out, ssm_a_re, ssm_a_im, ssm_log_dt, ssm_b_re, ssm_b_im, ssm_c_re, ssm_c_im, ssm_d, w_glu, b_glu, w_ssm_out, w_out, ffn2_norm, ffn2_w1, ffn2_w3, ffn2_w2):
    assert ffn1_norm.shape[0] == 1, "single trunk layer"
    batch, seq, _ = x_prompt.shape
    nseq, steps, _ = x_sample.shape
    p = _layout_params(ffn1_norm, ffn1_w1, ffn1_w3, ffn1_w2, mix_norm, w_in, q_norm, k_norm, attn_sinks,
                       w_attn_out, ssm_a_re, ssm_a_im, ssm_log_dt, ssm_b_re, ssm_b_im, ssm_c_re, ssm_c_im,
                       ssm_d, w_glu, b_glu, w_ssm_out, w_out, ffn2_norm, ffn2_w1, ffn2_w3, ffn2_w2)

    tile0 = jnp.arange(batch * seq // FRONT_TM) * FRONT_TM
    tabs_p = _rope_tables(tile0 % seq, jnp.arange(FRONT_TM))
    fr = _front(x_prompt.reshape(batch * seq, D_MODEL), tabs_p, p, BF16)
    x2p, re_p, im_p = _mix_prompt(fr, p, batch, seq)
    y_prompt = _ffn(x2p, p).reshape(batch, seq, D_MODEL)
    tail_kv = lambda a: a.reshape(batch, seq, KV_W)[:, seq - WINDOW:].reshape(batch, WINDOW, N_KV, HEAD_DIM)
    k_p, v_p = tail_kv(fr[2]), tail_kv(fr[3])

    xs = jnp.swapaxes(x_sample, 0, 1).reshape(steps * nseq, D_MODEL)
    tile0 = jnp.arange(steps * nseq // FRONT_TM) * FRONT_TM
    tabs_s = _rope_tables(PAST_LEN + tile0 // nseq, jnp.arange(FRONT_TM) // nseq)
    x1s, qs, ks, vs, us, sgas, sgss = _front(xs, tabs_s, p, F32)
    ck = cache_k[0].reshape(nseq, WINDOW, KV_W)
    cv = cache_v[0].reshape(nseq, WINDOW, KV_W)
    attn_s = _attn_sample(qs, ks, vs, ck, cv, p['sinks'], steps, nseq).reshape(steps * nseq, ATTN_W)
    x2s, re_s, im_s = _mix_sample(x1s, attn_s, us, sgas, sgss,
                                  state_ssm_re[0].reshape(nseq, SSM_S), state_ssm_im[0].reshape(nseq, SSM_S), p)
    y_sample = jnp.swapaxes(_ffn(x2s, p).reshape(steps, nseq, D_MODEL), 0, 1)
    new_kv = lambda a: jnp.swapaxes(a.reshape(steps, nseq, N_KV, HEAD_DIM), 0, 1)
    k_s = jnp.concatenate([cache_k[0][:, steps:], new_kv(ks)], axis=1)
    v_s = jnp.concatenate([cache_v[0][:, steps:], new_kv(vs)], axis=1)

    st = lambda a, n: a.reshape(1, n, SSM_G, SSM_N)
    return (y_prompt, y_sample, k_p[None], v_p[None], st(re_p, batch), st(im_p, batch),
            k_s[None], v_s[None], st(re_s, nseq), st(im_s, nseq))
```

```python
import jax
import jax.numpy as jnp
import numpy as np
from jax import lax
from jax.experimental import pallas as pl
from jax.experimental.pallas import tpu as pltpu

F32 = jnp.float32
BF16 = jnp.bfloat16

D_MODEL = 1024
HEAD_DIM = 64
N_HEADS = 8
N_KV = 2
WINDOW = 128
ATTN_W = N_HEADS * HEAD_DIM
KV_W = N_KV * HEAD_DIM
SSM_W = 512
SSM_G = 32
SSM_CH = 16
SSM_N = 64
SSM_S = SSM_G * SSM_N
D_FF = 2816
IN_COLS = ATTN_W + 2 * KV_W + SSM_W + 2 * D_MODEL
ROPE_THETA = 10000.0
RMS_EPS = 1e-6
MASK_VALUE = -1e30
PAST_LEN = 16384

LANES = 128
SUBLANES = 8
MXU_DIM = 256

FRONT_TM = 512
MIX_TM = 256
SCAN_SEG = MIX_TM // SUBLANES
FF_CHUNK = 256
SCAN_LC = 512
SAMPLE_BS = 8
VMEM_LIMIT = 56 * 1024 * 1024


def _const_spec(shape):
    nd = len(shape)
    return pl.BlockSpec(shape, lambda *_: (0,) * nd, pipeline_mode=pl.Buffered(1))


def _rms(x, w):
    ms = jnp.mean(x * x, axis=-1, keepdims=True)
    return x * lax.rsqrt(ms + RMS_EPS) * w


def _swiglu_residual(x, nw_ref, w1_ref, w3_ref, w2_ref):
    h = _rms(x, nw_ref[...]).astype(BF16)
    acc = None
    for c0 in range(0, D_FF, FF_CHUNK):
        a = jnp.dot(h, w1_ref[:, c0:c0 + FF_CHUNK], preferred_element_type=F32)
        b = jnp.dot(h, w3_ref[:, c0:c0 + FF_CHUNK], preferred_element_type=F32)
        g = (a * jax.nn.sigmoid(a) * b).astype(BF16)
        t = jnp.dot(g, w2_ref[c0:c0 + FF_CHUNK, :], preferred_element_type=F32)
        acc = t if acc is None else acc + t
    return x + 0.5 * acc


def _head_norm_rope(t, ones_ref, g_ref, cos, sinlo, sinhi, scale):
    ssum = jnp.dot((t * t).astype(BF16), ones_ref[...], preferred_element_type=F32)
    tn = t * lax.rsqrt(ssum * (1.0 / HEAD_DIM) + RMS_EPS) * g_ref[...]
    up = pltpu.roll(tn, HEAD_DIM // 2, axis=1)
    dn = pltpu.roll(tn, LANES - HEAD_DIM // 2, axis=1)
    r = tn * cos + dn * sinlo + up * sinhi
    return r * scale if scale != 1.0 else r


def _front_kernel(x_ref, base_ref, off_ref,
                  n1_ref, w1_ref, w3_ref, w2_ref, nm_ref, win_ref, qn_ref, kn_ref, ones_ref,
                  x1_ref, q_ref, k_ref, v_ref, u_ref, sga_ref, sgs_ref):
    x1 = _swiglu_residual(x_ref[...], n1_ref, w1_ref, w3_ref, w2_ref)
    x1_ref[...] = x1
    h = _rms(x1, nm_ref[...]).astype(BF16)
    cb, sb = base_ref[0:1, :], base_ref[1:2, :]
    cos = cb * off_ref[0] - sb * off_ref[1]
    sinlo = sb * off_ref[2] + cb * off_ref[3]
    sinhi = sb * off_ref[4] + cb * off_ref[5]
    o = ATTN_W + 2 * KV_W + SSM_W
    qkvu = jnp.dot(h, win_ref[:, 0:o], preferred_element_type=F32)
    for j in range(ATTN_W // LANES):
        q_ref[:, j * LANES:(j + 1) * LANES] = _head_norm_rope(
            qkvu[:, j * LANES:(j + 1) * LANES], ones_ref, qn_ref, cos, sinlo, sinhi,
            HEAD_DIM ** -0.5).astype(q_ref.dtype)
    k_ref[...] = _head_norm_rope(qkvu[:, ATTN_W:ATTN_W + KV_W], ones_ref, kn_ref, cos, sinlo, sinhi, 1.0)
    v_ref[...] = qkvu[:, ATTN_W + KV_W:ATTN_W + 2 * KV_W]
    u_ref[...] = qkvu[:, ATTN_W + 2 * KV_W:o]
    ga = jnp.dot(h, win_ref[:, o:o + D_MODEL], preferred_element_type=F32)
    sga_ref[...] = jax.nn.sigmoid(ga).astype(BF16)
    o += D_MODEL
    gs = jnp.dot(h, win_ref[:, o:o + D_MODEL], preferred_element_type=F32)
    sgs_ref[...] = jax.nn.sigmoid(gs).astype(BF16)


def _front(x, tabs, p, q_dtype):
    n = x.shape[0]
    tm = FRONT_TM
    base_tab, off_tab = tabs
    row = lambda w: pl.BlockSpec((tm, w), lambda i: (i, 0))
    consts = [p['n1'], p['w1a'], p['w3a'], p['w2a'], p['nm'], p['win'], p['qn'], p['kn'], p['ones']]
    outs = [(D_MODEL, F32), (ATTN_W, q_dtype), (KV_W, F32), (KV_W, F32), (SSM_W, F32),
            (D_MODEL, BF16), (D_MODEL, BF16)]
    return pl.pallas_call(
        _front_kernel,
        grid=(n // tm,),
        in_specs=[row(D_MODEL), pl.BlockSpec((None, 2, LANES), lambda i: (i, 0, 0)), _const_spec(off_tab.shape)]
        + [_const_spec(c.shape) for c in consts],
        out_specs=[row(w) for w, _ in outs],
        out_shape=[jax.ShapeDtypeStruct((n, w), dt) for w, dt in outs],
        compiler_params=pltpu.CompilerParams(dimension_semantics=("arbitrary",),
                                             vmem_limit_bytes=VMEM_LIMIT),
        name="front",
    )(x, base_tab, off_tab, *consts)


def _ffn_kernel(x_ref, n_ref, w1_ref, w3_ref, w2_ref, y_ref):
    y_ref[...] = _swiglu_residual(x_ref[...], n_ref, w1_ref, w3_ref, w2_ref)


def _ffn(x, p):
    n = x.shape[0]
    tm = FRONT_TM
    consts = [p['n2'], p['w1b'], p['w3b'], p['w2b']]
    return pl.pallas_call(
        _ffn_kernel,
        grid=(n // tm,),
        in_specs=[pl.BlockSpec((tm, D_MODEL), lambda i: (i, 0))] + [_const_spec(c.shape) for c in consts],
        out_specs=pl.BlockSpec((tm, D_MODEL), lambda i: (i, 0)),
        out_shape=jax.ShapeDtypeStruct((n, D_MODEL), F32),
        compiler_params=pltpu.CompilerParams(dimension_semantics=("arbitrary",),
                                             vmem_limit_bytes=VMEM_LIMIT),
        name="ffn",
    )(x, *consts)


def _cmul(ar, ai, br, bi):
    return ar * br - ai * bi, ar * bi + ai * br


def _ssm_discretize(are_ref, aim_ref, ldt_ref):
    a_re, a_im = are_ref[...], aim_ref[...]
    dt = jnp.exp(ldt_ref[...])
    mag = jnp.exp(dt * a_re)
    ab_re = mag * jnp.cos(dt * a_im)
    ab_im = mag * jnp.sin(dt * a_im)
    den = a_re * a_re + a_im * a_im
    nr, ni = ab_re - 1.0, ab_im
    f_re = (nr * a_re + ni * a_im) / den
    f_im = (ni * a_re - nr * a_im) / den
    return ab_re, ab_im, f_re, f_im


def _build_wb(bbre_ref, bbim_ref, f_re, f_im, wbre_s, wbim_s):
    half = SSM_S // 2
    for hh in range(2):
        fr, fi = f_re[:, hh * half:(hh + 1) * half], f_im[:, hh * half:(hh + 1) * half]
        br, bi = bbre_ref[hh], bbim_ref[hh]
        wbre_s[hh] = (br * fr - bi * fi).astype(BF16)
        wbim_s[hh] = (br * fi + bi * fr).astype(BF16)


def _expand_state(ub, wbre_s, wbim_s, bur_s, bui_s):
    half_in, half_st = SSM_W // 2, SSM_S // 2
    for hh in range(2):
        uh = ub[:, hh * half_in:(hh + 1) * half_in]
        bur_s[:, hh * half_st:(hh + 1) * half_st] = jnp.dot(uh, wbre_s[hh], preferred_element_type=F32)
        bui_s[:, hh * half_st:(hh + 1) * half_st] = jnp.dot(uh, wbim_s[hh], preferred_element_type=F32)


def _project_state(xr_s, xi_s, ccat_ref):
    half_st = SSM_S // 2
    outs = []
    for hh in range(2):
        sl = slice(hh * half_st, (hh + 1) * half_st)
        xcat = jnp.concatenate([xr_s[:, sl].astype(BF16), xi_s[:, sl].astype(BF16)], axis=1)
        outs.append(jnp.dot(xcat, ccat_ref[hh], preferred_element_type=F32))
    return jnp.concatenate(outs, axis=1)


def _gated_attn(attn_b, sga, wao_ref):
    return sga.astype(F32) * jnp.dot(attn_b, wao_ref[...], preferred_element_type=F32)


def _gated_merge(x1, attn_gated, y_ssm, sgs, wglu_ref, bglu_ref, wso_ref, wout_ref):
    z = jax.nn.gelu(y_ssm, approximate=True)
    gl = jnp.dot(z.astype(BF16), wglu_ref[...], preferred_element_type=F32) + bglu_ref[...]
    ssm = (z * jax.nn.sigmoid(gl)).astype(BF16)
    merged = attn_gated + sgs.astype(F32) * jnp.dot(ssm, wso_ref[...], preferred_element_type=F32)
    return x1 + jnp.dot(merged.astype(BF16), wout_ref[...], preferred_element_type=F32)


def _replicate_heads(kv):
    sw = pltpu.roll(kv, HEAD_DIM, axis=1)
    lo = lax.broadcasted_iota(jnp.int32, kv.shape, 1) < HEAD_DIM
    return jnp.where(lo, kv, sw).astype(BF16), jnp.where(lo, sw, kv).astype(BF16)


def _attn_scores(qs, kreps, keys_on_lanes=False):
    qb = qs[0].shape[0]
    lo = lax.broadcasted_iota(jnp.int32, (qb, LANES), 1) < HEAD_DIM
    dims = (((1,), (0,)), ((), ())) if keys_on_lanes else (((1,), (1,)), ((), ()))
    scores = []
    for q, krep in zip(qs, kreps):
        if q.dtype != BF16 or qb % (2 * SUBLANES):
            q = q.astype(F32)
        zero = jnp.zeros((), q.dtype)
        for kv in range(N_KV):
            parts = []
            for jj in range(2):
                t = q[:, (2 * kv + jj) * LANES:(2 * kv + jj + 1) * LANES]
                parts += [jnp.where(lo, t, zero), jnp.where(lo, zero, t)]
            stacked = jnp.concatenate(parts, axis=0).astype(BF16)
            scores.append(lax.dot_general(stacked, krep[kv], dims, preferred_element_type=F32))
    return jnp.concatenate(scores, axis=0)


def _attn_softmax(s, sinks_ref, qb, firsts):
    per_block = s.shape[0] // len(firsts)
    sink_parts = [jnp.full((qb, 1), sinks_ref[h], F32) for h in range(N_HEADS)]
    sink = jnp.concatenate(sink_parts * len(firsts), axis=0)
    qi = lax.broadcasted_iota(jnp.int32, (per_block, 2 * WINDOW), 0) & (qb - 1)
    sj = lax.broadcasted_iota(jnp.int32, (per_block, 2 * WINDOW), 1)
    band = (sj <= WINDOW + qi) & (sj > qi)
    valid = jnp.concatenate(
        [band if f is False else band & ((sj >= WINDOW) | jnp.logical_not(f)) for f in firsts], axis=0)
    s = jnp.where(valid, s, MASK_VALUE)
    m = jnp.maximum(jnp.max(s, axis=-1, keepdims=True), sink)
    e = jnp.exp(s - m)
    denom = jnp.sum(e, axis=-1, keepdims=True) + jnp.exp(sink - m)
    return (e * (1.0 / denom)).astype(BF16)


def _attn_values(probs, vreps, qb, keys_on_lanes=False):
    lo = lax.broadcasted_iota(jnp.int32, (qb, LANES), 1) < HEAD_DIM
    rows = 2 * (N_HEADS // N_KV // 2) * qb
    dims = (((1,), (1,)), ((), ())) if keys_on_lanes else (((1,), (0,)), ((), ()))
    outs = []
    for b, vrep in enumerate(vreps):
        tiles = []
        for kv in range(N_KV):
            r0 = (b * N_KV + kv) * rows
            o = lax.dot_general(probs[r0:r0 + rows], vrep[kv], dims, preferred_element_type=F32)
            for jj in range(2):
                oa = o[(2 * jj) * qb:(2 * jj + 1) * qb]
                ob = o[(2 * jj + 1) * qb:(2 * jj + 2) * qb]
                tiles.append(jnp.where(lo, oa, ob))
        outs.append(jnp.concatenate(tiles, axis=1))
    return outs


def _mix_prompt_kernel(x1_ref, q_ref, k_ref, v_ref, u_ref, sga_ref, sgs_ref, sinks_ref,
                       wao_ref, wglu_ref, bglu_ref, wso_ref, wout_ref,
                       are_ref, aim_ref, ldt_ref, bbre_ref, bbim_ref, ccat_ref, d_ref, perm_ref, permt_ref,
                       x2_ref, sre_ref, sim_ref,
                       wbre_s, wbim_s, apr_s, api_s, a1_s, aseg_s, carry_s, cin_s,
                       bur_s, bui_s, xb_s, kprev_s, vprev_s):
    b, i = pl.program_id(0), pl.program_id(1)
    nt = pl.num_programs(1)

    @pl.when((b == 0) & (i == 0))
    def _():
        ab_re, ab_im, f_re, f_im = _ssm_discretize(are_ref, aim_ref, ldt_ref)
        _build_wb(bbre_ref, bbim_ref, f_re, f_im, wbre_s, wbim_s)
        a1_s[0:1, :] = ab_re
        a1_s[1:2, :] = ab_im
        pr, pi = ab_re, ab_im
        for j in range(SCAN_SEG):
            apr_s[j * SUBLANES:(j + 1) * SUBLANES, :] = jnp.broadcast_to(pr, (SUBLANES, SSM_S))
            api_s[j * SUBLANES:(j + 1) * SUBLANES, :] = jnp.broadcast_to(pi, (SUBLANES, SSM_S))
            if j == SCAN_SEG - 1:
                aseg_s[0:1, :] = pr
                aseg_s[1:2, :] = pi
            pr, pi = _cmul(pr, pi, ab_re, ab_im)

    @pl.when(i == 0)
    def _():
        carry_s[...] = jnp.zeros_like(carry_s)
        kprev_s[...] = jnp.zeros_like(kprev_s)
        vprev_s[...] = jnp.zeros_like(vprev_s)

    k0, k1 = _replicate_heads(k_ref[...])
    v0, v1 = _replicate_heads(v_ref[...])
    qs, kreps, vreps, firsts = [], [], [], []
    for blk in range(MIX_TM // WINDOW):
        cur = slice(blk * WINDOW, (blk + 1) * WINDOW)
        if blk == 0:
            prev = [kprev_s[0], kprev_s[1], vprev_s[0], vprev_s[1]]
            firsts.append(i == 0)
        else:
            ps = slice((blk - 1) * WINDOW, blk * WINDOW)
            prev = [k0[ps], k1[ps], v0[ps], v1[ps]]
            firsts.append(False)
        qs.append(q_ref[cur, :])
        kreps.append([jnp.concatenate([prev[0], k0[cur]], axis=0), jnp.concatenate([prev[1], k1[cur]], axis=0)])
        vreps.append([jnp.concatenate([prev[2], v0[cur]], axis=0), jnp.concatenate([prev[3], v1[cur]], axis=0)])
    tail = slice(MIX_TM - WINDOW, MIX_TM)
    kprev_s[0], kprev_s[1] = k0[tail], k1[tail]
    vprev_s[0], vprev_s[1] = v0[tail], v1[tail]
    scores = _attn_scores(qs, kreps)

    half_in, half_st = SSM_W // 2, SSM_S // 2
    chunks = list(range(0, SSM_S, SCAN_LC))
    blk_rows = scores.shape[0] // len(qs)
    u = u_ref[...]
    up = jnp.dot(perm_ref[...], u.astype(BF16), preferred_element_type=F32).astype(BF16)

    def expand_half(hh):
        uh = up[:, hh * half_in:(hh + 1) * half_in]
        bur_s[:, hh * half_st:(hh + 1) * half_st] = jnp.dot(uh, wbre_s[hh], preferred_element_type=F32)
        bui_s[:, hh * half_st:(hh + 1) * half_st] = jnp.dot(uh, wbim_s[hh], preferred_element_type=F32)

    def softmax_block(bk):
        return _attn_softmax(scores[bk * blk_rows:(bk + 1) * blk_rows], sinks_ref, WINDOW, [firsts[bk]])

    def scan_chunk(c0):
        cs = slice(c0, c0 + SCAN_LC)
        ar = jnp.broadcast_to(a1_s[0:1, cs], (SUBLANES, SCAN_LC))
        ai = jnp.broadcast_to(a1_s[1:2, cs], (SUBLANES, SCAN_LC))
        xr, xi = bur_s[0:SUBLANES, cs], bui_s[0:SUBLANES, cs]
        for j in range(1, SCAN_SEG):
            rs = slice(j * SUBLANES, (j + 1) * SUBLANES)
            xr, xi = (ar * xr - ai * xi + bur_s[rs, cs], ar * xi + ai * xr + bui_s[rs, cs])
            bur_s[rs, cs] = xr
            bui_s[rs, cs] = xi

    expand_half(0)
    probs = [softmax_block(0)]
    expand_half(1)
    probs.append(softmax_block(1))
    scan_chunk(chunks[0])
    attn = _attn_values(probs[0], vreps[0:1], WINDOW)
    scan_chunk(chunks[1])
    attn += _attn_values(probs[1], vreps[1:2], WINDOW)
    scan_chunk(chunks[2])
    attn_gated = _gated_attn(jnp.concatenate(attn, axis=0).astype(BF16), sga_ref[...], wao_ref)
    scan_chunk(chunks[3])

    last = MIX_TM - SUBLANES
    sr, si = aseg_s[0:1, :], aseg_s[1:2, :]
    cr, ci = carry_s[0:1, :], carry_s[1:2, :]
    for s in range(SUBLANES):
        cin_s[0, s:s + 1, :] = cr
        cin_s[1, s:s + 1, :] = ci
        pr, pi = _cmul(sr, si, cr, ci)
        cr = pr + bur_s[last + s:last + s + 1, :]
        ci = pi + bui_s[last + s:last + s + 1, :]
    carry_s[0:1, :] = cr
    carry_s[1:2, :] = ci
    pack = 2 * SUBLANES

    def fix_chunk(c0):
        cs = slice(c0, c0 + SCAN_LC)
        cinr = jnp.concatenate([cin_s[0, :, cs]] * 2, axis=0)
        cini = jnp.concatenate([cin_s[1, :, cs]] * 2, axis=0)
        col = (c0 // half_st) * SSM_S + c0 % half_st
        for r0 in range(0, MIX_TM, pack):
            rs = slice(r0, r0 + pack)
            pr, pi = apr_s[rs, cs], api_s[rs, cs]
            xb_s[rs, col:col + SCAN_LC] = (bur_s[rs, cs] + pr * cinr - pi * cini).astype(BF16)
            xb_s[rs, col + half_st:col + half_st + SCAN_LC] = (bui_s[rs, cs] + pr * cini + pi * cinr).astype(BF16)

    def project_half(hh):
        return jnp.dot(xb_s[:, hh * SSM_S:(hh + 1) * SSM_S], ccat_ref[hh], preferred_element_type=F32)

    per_half = len(chunks) // 2
    for c0 in chunks[:per_half]:
        fix_chunk(c0)
    yp0 = project_half(0)
    for c0 in chunks[per_half:]:
        fix_chunk(c0)
    yp = jnp.concatenate([yp0, project_half(1)], axis=1)
    y_hi = yp.astype(BF16)
    y_lo = (yp - y_hi.astype(F32)).astype(BF16)
    y_ssm = (jnp.dot(permt_ref[...], y_hi, preferred_element_type=F32)
             + jnp.dot(permt_ref[...], y_lo, preferred_element_type=F32) + d_ref[...] * u)

    x2_ref[...] = _gated_merge(x1_ref[...], attn_gated, y_ssm, sgs_ref[...],
                               wglu_ref, bglu_ref, wso_ref, wout_ref)

    @pl.when(i == nt - 1)
    def _():
        sre_ref[...] = carry_s[0:1, :]
        sim_ref[...] = carry_s[1:2, :]


def _mix_prompt(fr, p, batch, seq):
    x1, q, k, v, u, sga, sgs = fr
    tm = MIX_TM
    nt = seq // tm
    row = lambda w: pl.BlockSpec((tm, w), lambda b, i: (b * nt + i, 0))
    consts = [p['wao'], p['wglu'], p['bglu'], p['wso'], p['wout'],
              p['are'], p['aim'], p['ldt'], p['bbre'], p['bbim'], p['ccat'], p['dskip'], p['perm'], p['permt']]
    st_spec = pl.BlockSpec((None, 1, SSM_S), lambda b, i: (b, 0, 0))
    scratch = [pltpu.VMEM((2, MXU_DIM, SSM_S // 2), BF16), pltpu.VMEM((2, MXU_DIM, SSM_S // 2), BF16),
               pltpu.VMEM((tm, SSM_S), F32), pltpu.VMEM((tm, SSM_S), F32),
               pltpu.VMEM((SUBLANES, SSM_S), F32), pltpu.VMEM((SUBLANES, SSM_S), F32),
               pltpu.VMEM((SUBLANES, SSM_S), F32), pltpu.VMEM((2, SUBLANES, SSM_S), F32),
               pltpu.VMEM((tm, SSM_S), F32), pltpu.VMEM((tm, SSM_S), F32), pltpu.VMEM((tm, 2 * SSM_S), BF16),
               pltpu.VMEM((N_KV, WINDOW, LANES), BF16), pltpu.VMEM((N_KV, WINDOW, LANES), BF16)]
    return pl.pallas_call(
        _mix_prompt_kernel,
        grid=(batch, nt),
        in_specs=[row(D_MODEL), row(ATTN_W), row(KV_W), row(KV_W), row(SSM_W), row(D_MODEL), row(D_MODEL),
                  pl.BlockSpec(memory_space=pltpu.SMEM)] + [_const_spec(c.shape) for c in consts],
        out_specs=[row(D_MODEL), st_spec, st_spec],
        out_shape=[jax.ShapeDtypeStruct((batch * seq, D_MODEL), F32),
                   jax.ShapeDtypeStruct((batch, 1, SSM_S), F32),
                   jax.ShapeDtypeStruct((batch, 1, SSM_S), F32)],
        scratch_shapes=scratch,
        compiler_params=pltpu.CompilerParams(dimension_semantics=("arbitrary", "arbitrary"),
                                             vmem_limit_bytes=VMEM_LIMIT),
        name="mix_prompt",
    )(x1, q, k, v, u, sga, sgs, p['sinks'], *consts)


def _attn_sample_kernel(q_ref, kn_ref, vn_ref, ckt_ref, cvt_ref, sinks_ref, o_ref, nkt_ref, nvt_ref,
                        q8_s, knp_s, vnp_s):
    steps = q_ref.shape[0]
    shift = WINDOW - steps
    keep = lax.broadcasted_iota(jnp.int32, (HEAD_DIM, WINDOW), 1) < shift
    q8_s[...] = jnp.zeros_like(q8_s)
    knp_s[...] = jnp.zeros_like(knp_s)
    vnp_s[...] = jnp.zeros_like(vnp_s)
    qs, kreps, vreps = [], [], []
    for s in range(SAMPLE_BS):
        for t in range(steps):
            q8_s[s, t:t + 1, :] = q_ref[t, s:s + 1, :]
            knp_s[s, t:t + 1, :] = kn_ref[t, s:s + 1, :]
            vnp_s[s, t:t + 1, :] = vn_ref[t, s:s + 1, :]
        news = [knp_s[s].T, vnp_s[s].T]
        ends = [pltpu.roll(n, shift, axis=1) for n in news]
        krep, vrep = [], []
        for kv in range(N_KV):
            hs = slice(kv * HEAD_DIM, (kv + 1) * HEAD_DIM)
            for cache_ref, out_ref, new, end, reps in ((ckt_ref, nkt_ref, news[0], ends[0], krep),
                                                       (cvt_ref, nvt_ref, news[1], ends[1], vrep)):
                old = cache_ref[s, kv]
                out_ref[s, kv] = jnp.where(keep, pltpu.roll(old, shift, axis=1), end[hs])
                both = jnp.concatenate([old, new[hs]], axis=1).astype(BF16)
                reps.append(jnp.concatenate([both, both], axis=0))
        qs.append(q8_s[s])
        kreps.append(krep)
        vreps.append(vrep)
    probs = _attn_softmax(_attn_scores(qs, kreps, keys_on_lanes=True), sinks_ref, SUBLANES, [False] * SAMPLE_BS)
    outs = _attn_values(probs, vreps, SUBLANES, keys_on_lanes=True)
    for s in range(SAMPLE_BS):
        for t in range(steps):
            o_ref[t, s:s + 1, :] = outs[s][t:t + 1, :]


def _attn_sample(q, kn, vn, ckt, cvt, sinks, steps, nseq):
    bs = SAMPLE_BS
    blk3 = lambda w: pl.BlockSpec((steps, bs, w), lambda i: (0, i, 0))
    cache = pl.BlockSpec((bs, N_KV, HEAD_DIM, WINDOW), lambda i: (i, 0, 0, 0))
    cache_shape = jax.ShapeDtypeStruct((nseq, N_KV, HEAD_DIM, WINDOW), F32)
    return pl.pallas_call(
        _attn_sample_kernel,
        grid=(nseq // bs,),
        in_specs=[blk3(ATTN_W), blk3(KV_W), blk3(KV_W), cache, cache, pl.BlockSpec(memory_space=pltpu.SMEM)],
        out_specs=[blk3(ATTN_W), cache, cache],
        out_shape=[jax.ShapeDtypeStruct((steps, nseq, ATTN_W), F32), cache_shape, cache_shape],
        scratch_shapes=[pltpu.VMEM((bs, SUBLANES, ATTN_W), F32), pltpu.VMEM((bs, LANES, KV_W), F32),
                        pltpu.VMEM((bs, LANES, KV_W), F32)],
        compiler_params=pltpu.CompilerParams(dimension_semantics=("arbitrary",)),
        name="attn_sample",
    )(q.reshape(steps, nseq, ATTN_W), kn.reshape(steps, nseq, KV_W), vn.reshape(steps, nseq, KV_W),
      ckt, cvt, sinks)


def _mix_sample_kernel(x1_ref, attn_ref, u_ref, sga_ref, sgs_ref, x0r_ref, x0i_ref,
                       wao_ref, wglu_ref, bglu_ref, wso_ref, wout_ref,
                       are_ref, aim_ref, ldt_ref, bbre_ref, bbim_ref, ccat_ref, d_ref,
                       x2_ref, sre_ref, sim_ref,
                       wbre_s, wbim_s, bur_s, bui_s):
    nseq = x0r_ref.shape[0]
    steps = x1_ref.shape[0] // nseq
    ab_re, ab_im, f_re, f_im = _ssm_discretize(are_ref, aim_ref, ldt_ref)
    _build_wb(bbre_ref, bbim_ref, f_re, f_im, wbre_s, wbim_s)
    u = u_ref[...]
    _expand_state(u.astype(BF16), wbre_s, wbim_s, bur_s, bui_s)
    for c0 in range(0, SSM_S, LANES):
        cs = slice(c0, c0 + LANES)
        ar, ai = ab_re[:, cs], ab_im[:, cs]
        xr, xi = x0r_ref[:, cs], x0i_ref[:, cs]
        for t in range(steps):
            rs = slice(t * nseq, (t + 1) * nseq)
            nr = ar * xr - ai * xi + bur_s[rs, cs]
            ni = ar * xi + ai * xr + bui_s[rs, cs]
            bur_s[rs, cs] = nr
            bui_s[rs, cs] = ni
            xr, xi = nr, ni
        sre_ref[:, cs] = xr
        sim_ref[:, cs] = xi
    y_ssm = _project_state(bur_s, bui_s, ccat_ref) + d_ref[...] * u
    attn_gated = _gated_attn(attn_ref[...].astype(BF16), sga_ref[...], wao_ref)
    x2_ref[...] = _gated_merge(x1_ref[...], attn_gated, y_ssm, sgs_ref[...],
                               wglu_ref, bglu_ref, wso_ref, wout_ref)


def _mix_sample(x1, attn, u, sga, sgs, x0r, x0i, p):
    n, nseq = x1.shape[0], x0r.shape[0]
    args = [x1, attn, u, sga, sgs, x0r, x0i, p['wao'], p['wglu'], p['bglu'], p['wso'], p['wout'],
            p['are'], p['aim'], p['ldt'], p['bbre'], p['bbim'], p['ccat'], p['dskip']]
    return pl.pallas_call(
        _mix_sample_kernel,
        grid=(1,),
        in_specs=[_const_spec(a.shape) for a in args],
        out_specs=[pl.BlockSpec((n, D_MODEL), lambda i: (0, 0)), pl.BlockSpec((nseq, SSM_S), lambda i: (0, 0)),
                   pl.BlockSpec((nseq, SSM_S), lambda i: (0, 0))],
        out_shape=[jax.ShapeDtypeStruct((n, D_MODEL), F32),
                   jax.ShapeDtypeStruct((nseq, SSM_S), F32),
                   jax.ShapeDtypeStruct((nseq, SSM_S), F32)],
        scratch_shapes=[pltpu.VMEM((2, MXU_DIM, SSM_S // 2), BF16), pltpu.VMEM((2, MXU_DIM, SSM_S // 2), BF16),
                        pltpu.VMEM((n, SSM_S), F32), pltpu.VMEM((n, SSM_S), F32)],
        compiler_params=pltpu.CompilerParams(dimension_semantics=("arbitrary",),
                                             vmem_limit_bytes=VMEM_LIMIT),
        name="mix_sample",
    )(*args)


def _rope_tables(base_pos, off_pos):
    half = HEAD_DIM // 2
    lane = np.arange(LANES)
    inv = ROPE_THETA ** (-2.0 * (lane % half) / HEAD_DIM)
    first = (lane % HEAD_DIM) < half
    ab = np.asarray(base_pos, np.float64)[:, None] * inv[None, :]
    ao = np.asarray(off_pos, np.float64)[:, None] * inv[None, :]
    base = np.stack([np.cos(ab), np.sin(ab)], axis=1)
    co, so = np.cos(ao), np.sin(ao)
    lo = np.where(first, -1.0, 0.0)
    hi = np.where(first, 0.0, 1.0)
    off = np.stack([co, so, co * lo, so * lo, co * hi, so * hi], axis=0)
    return jnp.asarray(base, F32), jnp.asarray(off, F32)


def _block_diag_halves(m):
    g, r, c = m.shape
    hg = g // 2
    eye = jnp.asarray(np.eye(hg), m.dtype)
    mh = m.reshape(2, hg, r, c)
    return (mh[:, :, :, None, :] * eye[None, :, None, :, None]).reshape(2, hg * r, hg * c)


def _layout_params(ffn1_norm, ffn1_w1, ffn1_w3, ffn1_w2, mix_norm, w_in, q_norm, k_norm, attn_sinks,
                   w_attn_out, ssm_a_re, ssm_a_im, ssm_log_dt, ssm_b_re, ssm_b_im, ssm_c_re, ssm_c_im,
                   ssm_d, w_glu, b_glu, w_ssm_out, w_out, ffn2_norm, ffn2_w1, ffn2_w3, ffn2_w2):
    l = 0
    row = lambda a: a.reshape(1, -1).astype(F32)
    head_ones = jnp.asarray(np.kron(np.eye(LANES // HEAD_DIM), np.ones((HEAD_DIM, HEAD_DIM))), F32)
    r = np.arange(MIX_TM)
    tok = (r % SUBLANES) * SCAN_SEG + r // SUBLANES
    perm = jnp.asarray(tok[:, None] == np.arange(MIX_TM)[None, :], BF16)
    c_cat = jnp.concatenate([_block_diag_halves(jnp.swapaxes(ssm_c_re[l], 1, 2)),
                             -_block_diag_halves(jnp.swapaxes(ssm_c_im[l], 1, 2))], axis=1)
    return dict(
        n1=row(ffn1_norm[l]), w1a=ffn1_w1[l].astype(BF16), w3a=ffn1_w3[l].astype(BF16), w2a=ffn1_w2[l].astype(BF16),
        nm=row(mix_norm[l]), win=w_in[l].astype(BF16),
        qn=row(jnp.tile(q_norm[l], LANES // HEAD_DIM)), kn=row(jnp.tile(k_norm[l], LANES // HEAD_DIM)),
        ones=head_ones.astype(BF16), sinks=attn_sinks[l].astype(F32),
        wao=w_attn_out[l].astype(BF16), wglu=w_glu[l].astype(BF16), bglu=row(b_glu[l]),
        wso=w_ssm_out[l].astype(BF16), wout=w_out[l].astype(BF16),
        are=row(ssm_a_re[l]), aim=row(ssm_a_im[l]), ldt=row(jnp.repeat(ssm_log_dt[l], SSM_N)),
        bbre=_block_diag_halves(jnp.swapaxes(ssm_b_re[l], 1, 2).astype(F32)),
        bbim=_block_diag_halves(jnp.swapaxes(ssm_b_im[l], 1, 2).astype(F32)),
        ccat=c_cat.astype(BF16), dskip=row(ssm_d[l]), perm=perm, permt=perm.T,
        n2=row(ffn2_norm[l]), w1b=ffn2_w1[l].astype(BF16), w3b=ffn2_w3[l].astype(BF16), w2b=ffn2_w2[l].astype(BF16),
    )


def kernel(x_prompt, x_sample, cache_k, cache_v, state_ssm_re, state_ssm_im, ffn1_norm, ffn1_w1, ffn1_w3, ffn1_w2, mix_norm, w_in, q_norm, k_norm, attn_sinks, w_attn_out, ssm_a_re, ssm_a_im, ssm_log_dt, ssm_b_re, ssm_b_im, ssm_c_re, ssm_c_im, ssm_d, w_glu, b_glu, w_ssm_out, w_out, ffn2_norm, ffn2_w1, ffn2_w3, ffn2_w2):
    assert ffn1_norm.shape[0] == 1, "single trunk layer"
    batch, seq, _ = x_prompt.shape
    nseq, steps, _ = x_sample.shape
    p = _layout_params(ffn1_norm, ffn1_w1, ffn1_w3, ffn1_w2, mix_norm, w_in, q_norm, k_norm, attn_sinks,
                       w_attn_out, ssm_a_re, ssm_a_im, ssm_log_dt, ssm_b_re, ssm_b_im, ssm_c_re, ssm_c_im,
                       ssm_d, w_glu, b_glu, w_ssm_out, w_out, ffn2_norm, ffn2_w1, ffn2_w3, ffn2_w2)

    tile0 = np.arange(batch * seq // FRONT_TM) * FRONT_TM
    tabs_p = _rope_tables(tile0 % seq, np.arange(FRONT_TM))
    fr = _front(x_prompt.reshape(batch * seq, D_MODEL), tabs_p, p, BF16)
    x2p, re_p, im_p = _mix_prompt(fr, p, batch, seq)
    y_prompt = _ffn(x2p, p).reshape(batch, seq, D_MODEL)
    tail_kv = lambda a: a.reshape(batch, seq, KV_W)[:, seq - WINDOW:].reshape(batch, WINDOW, N_KV, HEAD_DIM)
    k_p, v_p = tail_kv(fr[2]), tail_kv(fr[3])

    xs = jnp.swapaxes(x_sample, 0, 1).reshape(steps * nseq, D_MODEL)
    tile0 = np.arange(steps * nseq // FRONT_TM) * FRONT_TM
    tabs_s = _rope_tables(PAST_LEN + tile0 // nseq, np.arange(FRONT_TM) // nseq)
    x1s, qs, ks, vs, us, sgas, sgss = _front(xs, tabs_s, p, F32)
    to_lanes = lambda c: jnp.transpose(c[0], (0, 2, 3, 1))
    attn_s, nkt, nvt = _attn_sample(qs, ks, vs, to_lanes(cache_k), to_lanes(cache_v), p['sinks'], steps, nseq)
    x2s, re_s, im_s = _mix_sample(x1s, attn_s.reshape(steps * nseq, ATTN_W), us, sgas, sgss,
                                  state_ssm_re[0].reshape(nseq, SSM_S), state_ssm_im[0].reshape(nseq, SSM_S), p)
    y_sample = jnp.swapaxes(_ffn(x2s, p).reshape(steps, nseq, D_MODEL), 0, 1)
    k_s, v_s = jnp.transpose(nkt, (0, 3, 1, 2)), jnp.transpose(nvt, (0, 3, 1, 2))

    st = lambda a, n: a.reshape(1, n, SSM_G, SSM_N)
    return (y_prompt, y_sample, k_p[None], v_p[None], st(re_p, batch), st(im_p, batch),
            k_s[None], v_s[None], st(re_s, nseq), st(im_s, nseq))
```

```python
import jax
import jax.numpy as jnp
import numpy as np
from jax import lax
from jax.experimental import pallas as pl
from jax.experimental.pallas import tpu as pltpu

F32 = jnp.float32
BF16 = jnp.bfloat16

D_MODEL = 1024
HEAD_DIM = 64
N_HEADS = 8
N_KV = 2
WINDOW = 128
ATTN_W = N_HEADS * HEAD_DIM
KV_W = N_KV * HEAD_DIM
SSM_W = 512
SSM_G = 32
SSM_CH = 16
SSM_N = 64
SSM_S = SSM_G * SSM_N
D_FF = 2816
IN_COLS = ATTN_W + 2 * KV_W + SSM_W + 2 * D_MODEL
ROPE_THETA = 10000.0
RMS_EPS = 1e-6
MASK_VALUE = -1e30
PAST_LEN = 16384

LANES = 128
SUBLANES = 8
MXU_DIM = 256

FRONT_TM = 512
MIX_TM = 256
SCAN_SEG = MIX_TM // SUBLANES
FF_CHUNK = 256
SCAN_LC = 512
SAMPLE_BS = 8
VMEM_LIMIT = 56 * 1024 * 1024


def _const_spec(shape):
    nd = len(shape)
    return pl.BlockSpec(shape, lambda *_: (0,) * nd, pipeline_mode=pl.Buffered(1))


def _rms(x, w):
    ms = jnp.mean(x * x, axis=-1, keepdims=True)
    return x * lax.rsqrt(ms + RMS_EPS) * w


class _SwigluStream:
    def __init__(self, x, nw_ref, w1_ref, w3_ref, w2_ref):
        self.x, self.w1_ref, self.w3_ref, self.w2_ref = x, w1_ref, w3_ref, w2_ref
        self.h = _rms(x, nw_ref[...]).astype(BF16)
        self.acc = None
        self.c0 = 0

    def step(self, n=1):
        t = None
        for _ in range(n):
            if self.c0 >= D_FF:
                break
            cs = slice(self.c0, self.c0 + FF_CHUNK)
            a = jnp.dot(self.h, self.w1_ref[:, cs], preferred_element_type=F32)
            b = jnp.dot(self.h, self.w3_ref[:, cs], preferred_element_type=F32)
            g = (a * jax.nn.sigmoid(a) * b).astype(BF16)
            t = jnp.dot(g, self.w2_ref[cs, :], preferred_element_type=F32)
            self.acc = t if self.acc is None else self.acc + t
            self.c0 += FF_CHUNK
        if t is None:
            return jnp.zeros((SUBLANES, SCAN_LC), F32)
        bits = lax.bitcast_convert_type(t[0:SUBLANES, 0:SCAN_LC], jnp.uint32)
        sixteen = jnp.uint32(16)
        return lax.bitcast_convert_type(
            lax.shift_right_logical(lax.shift_right_logical(bits, sixteen), sixteen), F32)

    def result(self):
        self.step(D_FF // FF_CHUNK)
        return self.x + 0.5 * self.acc


def _swiglu_residual(x, nw_ref, w1_ref, w3_ref, w2_ref):
    return _SwigluStream(x, nw_ref, w1_ref, w3_ref, w2_ref).result()


def _head_norm_rope(t, ones_ref, g_ref, cos, sinlo, sinhi, scale):
    ssum = jnp.dot((t * t).astype(BF16), ones_ref[...], preferred_element_type=F32)
    tn = t * lax.rsqrt(ssum * (1.0 / HEAD_DIM) + RMS_EPS) * g_ref[...]
    up = pltpu.roll(tn, HEAD_DIM // 2, axis=1)
    dn = pltpu.roll(tn, LANES - HEAD_DIM // 2, axis=1)
    r = tn * cos + dn * sinlo + up * sinhi
    return r * scale if scale != 1.0 else r


def _front_kernel(x_ref, base_ref, off_ref,
                  n1_ref, w1_ref, w3_ref, w2_ref, nm_ref, win_ref, qn_ref, kn_ref, ones_ref,
                  x1_ref, q_ref, k_ref, v_ref, u_ref, sga_ref, sgs_ref):
    x1 = _swiglu_residual(x_ref[...], n1_ref, w1_ref, w3_ref, w2_ref)
    x1_ref[...] = x1
    h = _rms(x1, nm_ref[...]).astype(BF16)
    cb, sb = base_ref[0:1, :], base_ref[1:2, :]
    cos = cb * off_ref[0] - sb * off_ref[1]
    sinlo = sb * off_ref[2] + cb * off_ref[3]
    sinhi = sb * off_ref[4] + cb * off_ref[5]
    o = ATTN_W + 2 * KV_W + SSM_W
    qkvu = jnp.dot(h, win_ref[:, 0:o], preferred_element_type=F32)
    for j in range(ATTN_W // LANES):
        q_ref[:, j * LANES:(j + 1) * LANES] = _head_norm_rope(
            qkvu[:, j * LANES:(j + 1) * LANES], ones_ref, qn_ref, cos, sinlo, sinhi,
            HEAD_DIM ** -0.5).astype(q_ref.dtype)
    k_ref[...] = _head_norm_rope(qkvu[:, ATTN_W:ATTN_W + KV_W], ones_ref, kn_ref, cos, sinlo, sinhi, 1.0)
    v_ref[...] = qkvu[:, ATTN_W + KV_W:ATTN_W + 2 * KV_W]
    u_ref[...] = qkvu[:, ATTN_W + 2 * KV_W:o]
    ga = jnp.dot(h, win_ref[:, o:o + D_MODEL], preferred_element_type=F32)
    sga_ref[...] = jax.nn.sigmoid(ga).astype(BF16)
    o += D_MODEL
    gs = jnp.dot(h, win_ref[:, o:o + D_MODEL], preferred_element_type=F32)
    sgs_ref[...] = jax.nn.sigmoid(gs).astype(BF16)


def _front(x, tabs, p, q_dtype):
    n = x.shape[0]
    tm = FRONT_TM
    base_tab, off_tab = tabs
    row = lambda w: pl.BlockSpec((tm, w), lambda i: (i, 0))
    consts = [p['n1'], p['w1a'], p['w3a'], p['w2a'], p['nm'], p['win'], p['qn'], p['kn'], p['ones']]
    outs = [(D_MODEL, F32), (ATTN_W, q_dtype), (KV_W, F32), (KV_W, F32), (SSM_W, F32),
            (D_MODEL, BF16), (D_MODEL, BF16)]
    return pl.pallas_call(
        _front_kernel,
        grid=(n // tm,),
        in_specs=[row(D_MODEL), pl.BlockSpec((None, 2, LANES), lambda i: (i, 0, 0)), _const_spec(off_tab.shape)]
        + [_const_spec(c.shape) for c in consts],
        out_specs=[row(w) for w, _ in outs],
        out_shape=[jax.ShapeDtypeStruct((n, w), dt) for w, dt in outs],
        compiler_params=pltpu.CompilerParams(dimension_semantics=("arbitrary",),
                                             vmem_limit_bytes=VMEM_LIMIT),
        name="front",
    )(x, base_tab, off_tab, *consts)


def _ffn_kernel(x_ref, n_ref, w1_ref, w3_ref, w2_ref, y_ref):
    y_ref[...] = _swiglu_residual(x_ref[...], n_ref, w1_ref, w3_ref, w2_ref)


def _ffn(x, p):
    n = x.shape[0]
    tm = FRONT_TM
    consts = [p['n2'], p['w1b'], p['w3b'], p['w2b']]
    return pl.pallas_call(
        _ffn_kernel,
        grid=(n // tm,),
        in_specs=[pl.BlockSpec((tm, D_MODEL), lambda i: (i, 0))] + [_const_spec(c.shape) for c in consts],
        out_specs=pl.BlockSpec((tm, D_MODEL), lambda i: (i, 0)),
        out_shape=jax.ShapeDtypeStruct((n, D_MODEL), F32),
        compiler_params=pltpu.CompilerParams(dimension_semantics=("arbitrary",),
                                             vmem_limit_bytes=VMEM_LIMIT),
        name="ffn",
    )(x, *consts)


def _cmul(ar, ai, br, bi):
    return ar * br - ai * bi, ar * bi + ai * br


def _ssm_discretize(are_ref, aim_ref, ldt_ref):
    a_re, a_im = are_ref[...], aim_ref[...]
    dt = jnp.exp(ldt_ref[...])
    mag = jnp.exp(dt * a_re)
    ab_re = mag * jnp.cos(dt * a_im)
    ab_im = mag * jnp.sin(dt * a_im)
    den = a_re * a_re + a_im * a_im
    nr, ni = ab_re - 1.0, ab_im
    f_re = (nr * a_re + ni * a_im) / den
    f_im = (ni * a_re - nr * a_im) / den
    return ab_re, ab_im, f_re, f_im


def _build_wb(bbre_ref, bbim_ref, f_re, f_im, wbre_s, wbim_s):
    half = SSM_S // 2
    for hh in range(2):
        fr, fi = f_re[:, hh * half:(hh + 1) * half], f_im[:, hh * half:(hh + 1) * half]
        br, bi = bbre_ref[hh], bbim_ref[hh]
        wbre_s[hh] = (br * fr - bi * fi).astype(BF16)
        wbim_s[hh] = (br * fi + bi * fr).astype(BF16)


def _expand_state(ub, wbre_s, wbim_s, bur_s, bui_s):
    half_in, half_st = SSM_W // 2, SSM_S // 2
    for hh in range(2):
        uh = ub[:, hh * half_in:(hh + 1) * half_in]
        bur_s[:, hh * half_st:(hh + 1) * half_st] = jnp.dot(uh, wbre_s[hh], preferred_element_type=F32)
        bui_s[:, hh * half_st:(hh + 1) * half_st] = jnp.dot(uh, wbim_s[hh], preferred_element_type=F32)


def _project_state(xr_s, xi_s, ccat_ref):
    half_st = SSM_S // 2
    outs = []
    for hh in range(2):
        sl = slice(hh * half_st, (hh + 1) * half_st)
        xcat = jnp.concatenate([xr_s[:, sl].astype(BF16), xi_s[:, sl].astype(BF16)], axis=1)
        outs.append(jnp.dot(xcat, ccat_ref[hh], preferred_element_type=F32))
    return jnp.concatenate(outs, axis=1)


def _gated_attn(attn_b, sga, wao_ref):
    return sga.astype(F32) * jnp.dot(attn_b, wao_ref[...], preferred_element_type=F32)


def _gated_merge(x1, attn_gated, y_ssm, sgs, wglu_ref, bglu_ref, wso_ref, wout_ref):
    z = jax.nn.gelu(y_ssm, approximate=True)
    gl = jnp.dot(z.astype(BF16), wglu_ref[...], preferred_element_type=F32) + bglu_ref[...]
    ssm = (z * jax.nn.sigmoid(gl)).astype(BF16)
    merged = attn_gated + sgs.astype(F32) * jnp.dot(ssm, wso_ref[...], preferred_element_type=F32)
    return x1 + jnp.dot(merged.astype(BF16), wout_ref[...], preferred_element_type=F32)


def _replicate_heads(kv):
    sw = pltpu.roll(kv, HEAD_DIM, axis=1)
    lo = lax.broadcasted_iota(jnp.int32, kv.shape, 1) < HEAD_DIM
    return jnp.where(lo, kv, sw).astype(BF16), jnp.where(lo, sw, kv).astype(BF16)


def _attn_scores(qs, kreps, keys_on_lanes=False):
    qb = qs[0].shape[0]
    lo = lax.broadcasted_iota(jnp.int32, (qb, LANES), 1) < HEAD_DIM
    dims = (((1,), (0,)), ((), ())) if keys_on_lanes else (((1,), (1,)), ((), ()))
    scores = []
    for q, krep in zip(qs, kreps):
        if q.dtype != BF16 or qb % (2 * SUBLANES):
            q = q.astype(F32)
        zero = jnp.zeros((), q.dtype)
        for kv in range(N_KV):
            parts = []
            for jj in range(2):
                t = q[:, (2 * kv + jj) * LANES:(2 * kv + jj + 1) * LANES]
                parts += [jnp.where(lo, t, zero), jnp.where(lo, zero, t)]
            stacked = jnp.concatenate(parts, axis=0).astype(BF16)
            scores.append(lax.dot_general(stacked, krep[kv], dims, preferred_element_type=F32))
    return jnp.concatenate(scores, axis=0)


def _attn_softmax(s, sinks_ref, qb, firsts):
    per_block = s.shape[0] // len(firsts)
    sink_parts = [jnp.full((qb, 1), sinks_ref[h], F32) for h in range(N_HEADS)]
    sink = jnp.concatenate(sink_parts * len(firsts), axis=0)
    qi = lax.broadcasted_iota(jnp.int32, (per_block, 2 * WINDOW), 0) & (qb - 1)
    sj = lax.broadcasted_iota(jnp.int32, (per_block, 2 * WINDOW), 1)
    band = (sj <= WINDOW + qi) & (sj > qi)
    valid = jnp.concatenate(
        [band if f is False else band & ((sj >= WINDOW) | jnp.logical_not(f)) for f in firsts], axis=0)
    s = jnp.where(valid, s, MASK_VALUE)
    m = jnp.maximum(jnp.max(s, axis=-1, keepdims=True), sink)
    e = jnp.exp(s - m)
    denom = jnp.sum(e, axis=-1, keepdims=True) + jnp.exp(sink - m)
    return (e * (1.0 / denom)).astype(BF16)


def _attn_values(probs, vreps, qb, keys_on_lanes=False):
    lo = lax.broadcasted_iota(jnp.int32, (qb, LANES), 1) < HEAD_DIM
    rows = 2 * (N_HEADS // N_KV // 2) * qb
    dims = (((1,), (1,)), ((), ())) if keys_on_lanes else (((1,), (0,)), ((), ()))
    outs = []
    for b, vrep in enumerate(vreps):
        tiles = []
        for kv in range(N_KV):
            r0 = (b * N_KV + kv) * rows
            o = lax.dot_general(probs[r0:r0 + rows], vrep[kv], dims, preferred_element_type=F32)
            for jj in range(2):
                oa = o[(2 * jj) * qb:(2 * jj + 1) * qb]
                ob = o[(2 * jj + 1) * qb:(2 * jj + 2) * qb]
                tiles.append(jnp.where(lo, oa, ob))
        outs.append(jnp.concatenate(tiles, axis=1))
    return outs


def _mix_prompt_kernel(x1_ref, q_ref, k_ref, v_ref, u_ref, sga_ref, sgs_ref, sinks_ref,
                       wao_ref, wglu_ref, bglu_ref, wso_ref, wout_ref,
                       are_ref, aim_ref, ldt_ref, bbre_ref, bbim_ref, ccat_ref, d_ref, perm_ref, permt_ref,
                       n2_ref, w1b_ref, w3b_ref, w2b_ref,
                       y_ref, sre_ref, sim_ref,
                       wbre_s, wbim_s, apr_s, api_s, a1_s, aseg_s, carry_s, cin_s,
                       bur_s, bui_s, xb_s, kprev_s, vprev_s, x2prev_s):
    b, i = pl.program_id(0), pl.program_id(1)
    nt = pl.num_programs(1) - 1

    @pl.when((b == 0) & (i == 0))
    def _():
        ab_re, ab_im, f_re, f_im = _ssm_discretize(are_ref, aim_ref, ldt_ref)
        _build_wb(bbre_ref, bbim_ref, f_re, f_im, wbre_s, wbim_s)
        a1_s[0:1, :] = ab_re
        a1_s[1:2, :] = ab_im
        pr, pi = ab_re, ab_im
        for j in range(SCAN_SEG):
            apr_s[j * SUBLANES:(j + 1) * SUBLANES, :] = jnp.broadcast_to(pr, (SUBLANES, SSM_S))
            api_s[j * SUBLANES:(j + 1) * SUBLANES, :] = jnp.broadcast_to(pi, (SUBLANES, SSM_S))
            if j == SCAN_SEG - 1:
                aseg_s[0:1, :] = pr
                aseg_s[1:2, :] = pi
            pr, pi = _cmul(pr, pi, ab_re, ab_im)

    @pl.when(i == 0)
    def _():
        carry_s[...] = jnp.zeros_like(carry_s)
        kprev_s[...] = jnp.zeros_like(kprev_s)
        vprev_s[...] = jnp.zeros_like(vprev_s)
        x2prev_s[...] = jnp.zeros_like(x2prev_s)

    ffn = _SwigluStream(x2prev_s[...], n2_ref, w1b_ref, w3b_ref, w2b_ref)

    k0, k1 = _replicate_heads(k_ref[...])
    v0, v1 = _replicate_heads(v_ref[...])
    qs, kreps, vreps, firsts = [], [], [], []
    for blk in range(MIX_TM // WINDOW):
        cur = slice(blk * WINDOW, (blk + 1) * WINDOW)
        if blk == 0:
            prev = [kprev_s[0], kprev_s[1], vprev_s[0], vprev_s[1]]
            firsts.append(i == 0)
        else:
            ps = slice((blk - 1) * WINDOW, blk * WINDOW)
            prev = [k0[ps], k1[ps], v0[ps], v1[ps]]
            firsts.append(False)
        qs.append(q_ref[cur, :])
        kreps.append([jnp.concatenate([prev[0], k0[cur]], axis=0), jnp.concatenate([prev[1], k1[cur]], axis=0)])
        vreps.append([jnp.concatenate([prev[2], v0[cur]], axis=0), jnp.concatenate([prev[3], v1[cur]], axis=0)])
    tail = slice(MIX_TM - WINDOW, MIX_TM)
    kprev_s[0], kprev_s[1] = k0[tail], k1[tail]
    vprev_s[0], vprev_s[1] = v0[tail], v1[tail]
    scores = _attn_scores(qs, kreps)

    half_in, half_st = SSM_W // 2, SSM_S // 2
    chunks = list(range(0, SSM_S, SCAN_LC))
    blk_rows = scores.shape[0] // len(qs)
    u = u_ref[...]
    up = jnp.dot(perm_ref[...], u.astype(BF16), preferred_element_type=F32).astype(BF16)

    def expand_half(hh):
        uh = up[:, hh * half_in:(hh + 1) * half_in]
        bur_s[:, hh * half_st:(hh + 1) * half_st] = jnp.dot(uh, wbre_s[hh], preferred_element_type=F32)
        bui_s[:, hh * half_st:(hh + 1) * half_st] = jnp.dot(uh, wbim_s[hh], preferred_element_type=F32)

    def softmax_block(bk):
        return _attn_softmax(scores[bk * blk_rows:(bk + 1) * blk_rows], sinks_ref, WINDOW, [firsts[bk]])

    def scan_chunk(c0, pin):
        cs = slice(c0, c0 + SCAN_LC)
        ar = jnp.broadcast_to(a1_s[0:1, cs], (SUBLANES, SCAN_LC))
        ai = jnp.broadcast_to(a1_s[1:2, cs], (SUBLANES, SCAN_LC))
        xr, xi = bur_s[0:SUBLANES, cs] + pin, bui_s[0:SUBLANES, cs] + pin
        for j in range(1, SCAN_SEG):
            rs = slice(j * SUBLANES, (j + 1) * SUBLANES)
            xr, xi = (ar * xr - ai * xi + bur_s[rs, cs], ar * xi + ai * xr + bui_s[rs, cs])
            bur_s[rs, cs] = xr
            bui_s[rs, cs] = xi

    ffn.step()
    expand_half(0)
    probs = [softmax_block(0)]
    expand_half(1)
    probs.append(softmax_block(1))
    scan_chunk(chunks[0], ffn.step())
    attn = _attn_values(probs[0], vreps[0:1], WINDOW)
    scan_chunk(chunks[1], ffn.step())
    attn += _attn_values(probs[1], vreps[1:2], WINDOW)
    scan_chunk(chunks[2], ffn.step())
    attn_gated = _gated_attn(jnp.concatenate(attn, axis=0).astype(BF16), sga_ref[...], wao_ref)
    scan_chunk(chunks[3], ffn.step())

    last = MIX_TM - SUBLANES
    sr, si = aseg_s[0:1, :], aseg_s[1:2, :]
    cr, ci = carry_s[0:1, :], carry_s[1:2, :]
    for s in range(SUBLANES):
        cin_s[0, s:s + 1, :] = cr
        cin_s[1, s:s + 1, :] = ci
        pr, pi = _cmul(sr, si, cr, ci)
        cr = pr + bur_s[last + s:last + s + 1, :]
        ci = pi + bui_s[last + s:last + s + 1, :]
    commit = i < nt
    carry_s[0:1, :] = jnp.where(commit, cr, carry_s[0:1, :])
    carry_s[1:2, :] = jnp.where(commit, ci, carry_s[1:2, :])
    pack = 2 * SUBLANES

    def fix_chunk(c0, pin):
        cs = slice(c0, c0 + SCAN_LC)
        cinr = jnp.concatenate([cin_s[0, :, cs] + pin] * 2, axis=0)
        cini = jnp.concatenate([cin_s[1, :, cs] + pin] * 2, axis=0)
        col = (c0 // half_st) * SSM_S + c0 % half_st
        for r0 in range(0, MIX_TM, pack):
            rs = slice(r0, r0 + pack)
            pr, pi = apr_s[rs, cs], api_s[rs, cs]
            xb_s[rs, col:col + SCAN_LC] = (bur_s[rs, cs] + pr * cinr - pi * cini).astype(BF16)
            xb_s[rs, col + half_st:col + half_st + SCAN_LC] = (bui_s[rs, cs] + pr * cini + pi * cinr).astype(BF16)

    def project_half(hh):
        return jnp.dot(xb_s[:, hh * SSM_S:(hh + 1) * SSM_S], ccat_ref[hh], preferred_element_type=F32)

    per_half = len(chunks) // 2
    for c0 in chunks[:per_half]:
        fix_chunk(c0, ffn.step())
    yp0 = project_half(0)
    for c0 in chunks[per_half:]:
        fix_chunk(c0, ffn.step())
    yp = jnp.concatenate([yp0, project_half(1)], axis=1)
    y_hi = yp.astype(BF16)
    y_lo = (yp - y_hi.astype(F32)).astype(BF16)
    skip = (d_ref[...] + ffn.step()[0:1, :]) * u
    y_ssm = (jnp.dot(permt_ref[...], y_hi, preferred_element_type=F32)
             + jnp.dot(permt_ref[...], y_lo, preferred_element_type=F32) + skip)

    x2 = _gated_merge(x1_ref[...], attn_gated, y_ssm, sgs_ref[...], wglu_ref, bglu_ref, wso_ref, wout_ref)
    y_ref[...] = ffn.result()
    x2prev_s[...] = x2

    @pl.when(i == nt)
    def _():
        sre_ref[...] = carry_s[0:1, :]
        sim_ref[...] = carry_s[1:2, :]


def _mix_prompt(fr, p, batch, seq):
    x1, q, k, v, u, sga, sgs = fr
    tm = MIX_TM
    nt = seq // tm
    row = lambda w: pl.BlockSpec((tm, w), lambda b, i: (b * nt + jnp.minimum(i, nt - 1), 0))
    lag = pl.BlockSpec((tm, D_MODEL), lambda b, i: (b * nt + jnp.maximum(i - 1, 0), 0))
    consts = [p['wao'], p['wglu'], p['bglu'], p['wso'], p['wout'],
              p['are'], p['aim'], p['ldt'], p['bbre'], p['bbim'], p['ccat'], p['dskip'], p['perm'], p['permt'],
              p['n2'], p['w1b'], p['w3b'], p['w2b']]
    st_spec = pl.BlockSpec((None, 1, SSM_S), lambda b, i: (b, 0, 0))
    scratch = [pltpu.VMEM((2, MXU_DIM, SSM_S // 2), BF16), pltpu.VMEM((2, MXU_DIM, SSM_S // 2), BF16),
               pltpu.VMEM((tm, SSM_S), F32), pltpu.VMEM((tm, SSM_S), F32),
               pltpu.VMEM((SUBLANES, SSM_S), F32), pltpu.VMEM((SUBLANES, SSM_S), F32),
               pltpu.VMEM((SUBLANES, SSM_S), F32), pltpu.VMEM((2, SUBLANES, SSM_S), F32),
               pltpu.VMEM((tm, SSM_S), F32), pltpu.VMEM((tm, SSM_S), F32), pltpu.VMEM((tm, 2 * SSM_S), BF16),
               pltpu.VMEM((N_KV, WINDOW, LANES), BF16), pltpu.VMEM((N_KV, WINDOW, LANES), BF16),
               pltpu.VMEM((tm, D_MODEL), F32)]
    return pl.pallas_call(
        _mix_prompt_kernel,
        grid=(batch, nt + 1),
        in_specs=[row(D_MODEL), row(ATTN_W), row(KV_W), row(KV_W), row(SSM_W), row(D_MODEL), row(D_MODEL),
                  pl.BlockSpec(memory_space=pltpu.SMEM)] + [_const_spec(c.shape) for c in consts],
        out_specs=[lag, st_spec, st_spec],
        out_shape=[jax.ShapeDtypeStruct((batch * seq, D_MODEL), F32),
                   jax.ShapeDtypeStruct((batch, 1, SSM_S), F32),
                   jax.ShapeDtypeStruct((batch, 1, SSM_S), F32)],
        scratch_shapes=scratch,
        compiler_params=pltpu.CompilerParams(dimension_semantics=("arbitrary", "arbitrary"),
                                             vmem_limit_bytes=VMEM_LIMIT),
        name="mix_prompt",
    )(x1, q, k, v, u, sga, sgs, p['sinks'], *consts)


def _attn_sample_kernel(q_ref, kn_ref, vn_ref, ckt_ref, cvt_ref, sinks_ref, o_ref, nkt_ref, nvt_ref,
                        q8_s, knp_s, vnp_s):
    steps = q_ref.shape[0]
    shift = WINDOW - steps
    keep = lax.broadcasted_iota(jnp.int32, (HEAD_DIM, WINDOW), 1) < shift
    q8_s[...] = jnp.zeros_like(q8_s)
    knp_s[...] = jnp.zeros_like(knp_s)
    vnp_s[...] = jnp.zeros_like(vnp_s)
    qs, kreps, vreps = [], [], []
    for s in range(SAMPLE_BS):
        for t in range(steps):
            q8_s[s, t:t + 1, :] = q_ref[t, s:s + 1, :]
            knp_s[s, t:t + 1, :] = kn_ref[t, s:s + 1, :]
            vnp_s[s, t:t + 1, :] = vn_ref[t, s:s + 1, :]
        news = [knp_s[s].T, vnp_s[s].T]
        ends = [pltpu.roll(n, shift, axis=1) for n in news]
        krep, vrep = [], []
        for kv in range(N_KV):
            hs = slice(kv * HEAD_DIM, (kv + 1) * HEAD_DIM)
            for cache_ref, out_ref, new, end, reps in ((ckt_ref, nkt_ref, news[0], ends[0], krep),
                                                       (cvt_ref, nvt_ref, news[1], ends[1], vrep)):
                old = cache_ref[s, kv]
                out_ref[s, kv] = jnp.where(keep, pltpu.roll(old, shift, axis=1), end[hs])
                both = jnp.concatenate([old, new[hs]], axis=1).astype(BF16)
                reps.append(jnp.concatenate([both, both], axis=0))
        qs.append(q8_s[s])
        kreps.append(krep)
        vreps.append(vrep)
    probs = _attn_softmax(_attn_scores(qs, kreps, keys_on_lanes=True), sinks_ref, SUBLANES, [False] * SAMPLE_BS)
    outs = _attn_values(probs, vreps, SUBLANES, keys_on_lanes=True)
    for s in range(SAMPLE_BS):
        for t in range(steps):
            o_ref[t, s:s + 1, :] = outs[s][t:t + 1, :]


def _attn_sample(q, kn, vn, ckt, cvt, sinks, steps, nseq):
    bs = SAMPLE_BS
    blk3 = lambda w: pl.BlockSpec((steps, bs, w), lambda i: (0, i, 0))
    cache = pl.BlockSpec((bs, N_KV, HEAD_DIM, WINDOW), lambda i: (i, 0, 0, 0))
    cache_shape = jax.ShapeDtypeStruct((nseq, N_KV, HEAD_DIM, WINDOW), F32)
    return pl.pallas_call(
        _attn_sample_kernel,
        grid=(nseq // bs,),
        in_specs=[blk3(ATTN_W), blk3(KV_W), blk3(KV_W), cache, cache, pl.BlockSpec(memory_space=pltpu.SMEM)],
        out_specs=[blk3(ATTN_W), cache, cache],
        out_shape=[jax.ShapeDtypeStruct((steps, nseq, ATTN_W), F32), cache_shape, cache_shape],
        scratch_shapes=[pltpu.VMEM((bs, SUBLANES, ATTN_W), F32), pltpu.VMEM((bs, LANES, KV_W), F32),
                        pltpu.VMEM((bs, LANES, KV_W), F32)],
        compiler_params=pltpu.CompilerParams(dimension_semantics=("arbitrary",)),
        name="attn_sample",
    )(q.reshape(steps, nseq, ATTN_W), kn.reshape(steps, nseq, KV_W), vn.reshape(steps, nseq, KV_W),
      ckt, cvt, sinks)


def _mix_sample_kernel(x1_ref, attn_ref, u_ref, sga_ref, sgs_ref, x0r_ref, x0i_ref,
                       wao_ref, wglu_ref, bglu_ref, wso_ref, wout_ref,
                       are_ref, aim_ref, ldt_ref, bbre_ref, bbim_ref, ccat_ref, d_ref,
                       x2_ref, sre_ref, sim_ref,
                       wbre_s, wbim_s, bur_s, bui_s):
    nseq = x0r_ref.shape[0]
    steps = x1_ref.shape[0] // nseq
    ab_re, ab_im, f_re, f_im = _ssm_discretize(are_ref, aim_ref, ldt_ref)
    _build_wb(bbre_ref, bbim_ref, f_re, f_im, wbre_s, wbim_s)
    u = u_ref[...]
    _expand_state(u.astype(BF16), wbre_s, wbim_s, bur_s, bui_s)
    for c0 in range(0, SSM_S, LANES):
        cs = slice(c0, c0 + LANES)
        ar, ai = ab_re[:, cs], ab_im[:, cs]
        xr, xi = x0r_ref[:, cs], x0i_ref[:, cs]
        for t in range(steps):
            rs = slice(t * nseq, (t + 1) * nseq)
            nr = ar * xr - ai * xi + bur_s[rs, cs]
            ni = ar * xi + ai * xr + bui_s[rs, cs]
            bur_s[rs, cs] = nr
            bui_s[rs, cs] = ni
            xr, xi = nr, ni
        sre_ref[:, cs] = xr
        sim_ref[:, cs] = xi
    y_ssm = _project_state(bur_s, bui_s, ccat_ref) + d_ref[...] * u
    attn_gated = _gated_attn(attn_ref[...].astype(BF16), sga_ref[...], wao_ref)
    x2_ref[...] = _gated_merge(x1_ref[...], attn_gated, y_ssm, sgs_ref[...],
                               wglu_ref, bglu_ref, wso_ref, wout_ref)


def _mix_sample(x1, attn, u, sga, sgs, x0r, x0i, p):
    n, nseq = x1.shape[0], x0r.shape[0]
    args = [x1, attn, u, sga, sgs, x0r, x0i, p['wao'], p['wglu'], p['bglu'], p['wso'], p['wout'],
            p['are'], p['aim'], p['ldt'], p['bbre'], p['bbim'], p['ccat'], p['dskip']]
    return pl.pallas_call(
        _mix_sample_kernel,
        grid=(1,),
        in_specs=[_const_spec(a.shape) for a in args],
        out_specs=[pl.BlockSpec((n, D_MODEL), lambda i: (0, 0)), pl.BlockSpec((nseq, SSM_S), lambda i: (0, 0)),
                   pl.BlockSpec((nseq, SSM_S), lambda i: (0, 0))],
        out_shape=[jax.ShapeDtypeStruct((n, D_MODEL), F32),
                   jax.ShapeDtypeStruct((nseq, SSM_S), F32),
                   jax.ShapeDtypeStruct((nseq, SSM_S), F32)],
        scratch_shapes=[pltpu.VMEM((2, MXU_DIM, SSM_S // 2), BF16), pltpu.VMEM((2, MXU_DIM, SSM_S // 2), BF16),
                        pltpu.VMEM((n, SSM_S), F32), pltpu.VMEM((n, SSM_S), F32)],
        compiler_params=pltpu.CompilerParams(dimension_semantics=("arbitrary",),
                                             vmem_limit_bytes=VMEM_LIMIT),
        name="mix_sample",
    )(*args)


def _rope_tables(base_pos, off_pos):
    half = HEAD_DIM // 2
    lane = np.arange(LANES)
    inv = ROPE_THETA ** (-2.0 * (lane % half) / HEAD_DIM)
    first = (lane % HEAD_DIM) < half
    ab = np.asarray(base_pos, np.float64)[:, None] * inv[None, :]
    ao = np.asarray(off_pos, np.float64)[:, None] * inv[None, :]
    base = np.stack([np.cos(ab), np.sin(ab)], axis=1)
    co, so = np.cos(ao), np.sin(ao)
    lo = np.where(first, -1.0, 0.0)
    hi = np.where(first, 0.0, 1.0)
    off = np.stack([co, so, co * lo, so * lo, co * hi, so * hi], axis=0)
    return jnp.asarray(base, F32), jnp.asarray(off, F32)


def _block_diag_halves(m):
    g, r, c = m.shape
    hg = g // 2
    eye = jnp.asarray(np.eye(hg), m.dtype)
    mh = m.reshape(2, hg, r, c)
    return (mh[:, :, :, None, :] * eye[None, :, None, :, None]).reshape(2, hg * r, hg * c)


def _layout_params(ffn1_norm, ffn1_w1, ffn1_w3, ffn1_w2, mix_norm, w_in, q_norm, k_norm, attn_sinks,
                   w_attn_out, ssm_a_re, ssm_a_im, ssm_log_dt, ssm_b_re, ssm_b_im, ssm_c_re, ssm_c_im,
                   ssm_d, w_glu, b_glu, w_ssm_out, w_out, ffn2_norm, ffn2_w1, ffn2_w3, ffn2_w2):
    l = 0
    row = lambda a: a.reshape(1, -1).astype(F32)
    head_ones = jnp.asarray(np.kron(np.eye(LANES // HEAD_DIM), np.ones((HEAD_DIM, HEAD_DIM))), F32)
    r = np.arange(MIX_TM)
    tok = (r % SUBLANES) * SCAN_SEG + r // SUBLANES
    perm = jnp.asarray(tok[:, None] == np.arange(MIX_TM)[None, :], BF16)
    c_cat = jnp.concatenate([_block_diag_halves(jnp.swapaxes(ssm_c_re[l], 1, 2)),
                             -_block_diag_halves(jnp.swapaxes(ssm_c_im[l], 1, 2))], axis=1)
    return dict(
        n1=row(ffn1_norm[l]), w1a=ffn1_w1[l].astype(BF16), w3a=ffn1_w3[l].astype(BF16), w2a=ffn1_w2[l].astype(BF16),
        nm=row(mix_norm[l]), win=w_in[l].astype(BF16),
        qn=row(jnp.tile(q_norm[l], LANES // HEAD_DIM)), kn=row(jnp.tile(k_norm[l], LANES // HEAD_DIM)),
        ones=head_ones.astype(BF16), sinks=attn_sinks[l].astype(F32),
        wao=w_attn_out[l].astype(BF16), wglu=w_glu[l].astype(BF16), bglu=row(b_glu[l]),
        wso=w_ssm_out[l].astype(BF16), wout=w_out[l].astype(BF16),
        are=row(ssm_a_re[l]), aim=row(ssm_a_im[l]), ldt=row(jnp.repeat(ssm_log_dt[l], SSM_N)),
        bbre=_block_diag_halves(jnp.swapaxes(ssm_b_re[l], 1, 2).astype(F32)),
        bbim=_block_diag_halves(jnp.swapaxes(ssm_b_im[l], 1, 2).astype(F32)),
        ccat=c_cat.astype(BF16), dskip=row(ssm_d[l]), perm=perm, permt=perm.T,
        n2=row(ffn2_norm[l]), w1b=ffn2_w1[l].astype(BF16), w3b=ffn2_w3[l].astype(BF16), w2b=ffn2_w2[l].astype(BF16),
    )


def kernel(x_prompt, x_sample, cache_k, cache_v, state_ssm_re, state_ssm_im, ffn1_norm, ffn1_w1, ffn1_w3, ffn1_w2, mix_norm, w_in, q_norm, k_norm, attn_sinks, w_attn_out, ssm_a_re, ssm_a_im, ssm_log_dt, ssm_b_re, ssm_b_im, ssm_c_re, ssm_c_im, ssm_d, w_glu, b_glu, w_ssm_out, w_out, ffn2_norm, ffn2_w1, ffn2_w3, ffn2_w2):
    assert ffn1_norm.shape[0] == 1, "single trunk layer"
    batch, seq, _ = x_prompt.shape
    nseq, steps, _ = x_sample.shape
    p = _layout_params(ffn1_norm, ffn1_w1, ffn1_w3, ffn1_w2, mix_norm, w_in, q_norm, k_norm, attn_sinks,
                       w_attn_out, ssm_a_re, ssm_a_im, ssm_log_dt, ssm_b_re, ssm_b_im, ssm_c_re, ssm_c_im,
                       ssm_d, w_glu, b_glu, w_ssm_out, w_out, ffn2_norm, ffn2_w1, ffn2_w3, ffn2_w2)

    tile0 = np.arange(batch * seq // FRONT_TM) * FRONT_TM
    tabs_p = _rope_tables(tile0 % seq, np.arange(FRONT_TM))
    fr = _front(x_prompt.reshape(batch * seq, D_MODEL), tabs_p, p, BF16)
    yp, re_p, im_p = _mix_prompt(fr, p, batch, seq)
    y_prompt = yp.reshape(batch, seq, D_MODEL)
    tail_kv = lambda a: a.reshape(batch, seq, KV_W)[:, seq - WINDOW:].reshape(batch, WINDOW, N_KV, HEAD_DIM)
    k_p, v_p = tail_kv(fr[2]), tail_kv(fr[3])

    xs = jnp.swapaxes(x_sample, 0, 1).reshape(steps * nseq, D_MODEL)
    tile0 = np.arange(steps * nseq // FRONT_TM) * FRONT_TM
    tabs_s = _rope_tables(PAST_LEN + tile0 // nseq, np.arange(FRONT_TM) // nseq)
    x1s, qs, ks, vs, us, sgas, sgss = _front(xs, tabs_s, p, F32)
    to_lanes = lambda c: jnp.transpose(c[0], (0, 2, 3, 1))
    attn_s, nkt, nvt = _attn_sample(qs, ks, vs, to_lanes(cache_k), to_lanes(cache_v), p['sinks'], steps, nseq)
    x2s, re_s, im_s = _mix_sample(x1s, attn_s.reshape(steps * nseq, ATTN_W), us, sgas, sgss,
                                  state_ssm_re[0].reshape(nseq, SSM_S), state_ssm_im[0].reshape(nseq, SSM_S), p)
    y_sample = jnp.swapaxes(_ffn(x2s, p).reshape(steps, nseq, D_MODEL), 0, 1)
    k_s, v_s = jnp.transpose(nkt, (0, 3, 1, 2)), jnp.transpose(nvt, (0, 3, 1, 2))

    st = lambda a, n: a.reshape(1, n, SSM_G, SSM_N)
    return (y_prompt, y_sample, k_p[None], v_p[None], st(re_p, batch), st(im_p, batch),
            k_s[None], v_s[None], st(re_s, nseq), st(im_s, nseq))
```

```python
import jax
import jax.numpy as jnp
import numpy as np
from jax import lax
from jax.experimental import pallas as pl
from jax.experimental.pallas import tpu as pltpu

F32 = jnp.float32
BF16 = jnp.bfloat16

D_MODEL = 1024
HEAD_DIM = 64
N_HEADS = 8
N_KV = 2
WINDOW = 128
ATTN_W = N_HEADS * HEAD_DIM
KV_W = N_KV * HEAD_DIM
SSM_W = 512
SSM_G = 32
SSM_CH = 16
SSM_N = 64
SSM_S = SSM_G * SSM_N
D_FF = 2816
IN_COLS = ATTN_W + 2 * KV_W + SSM_W + 2 * D_MODEL
ROPE_THETA = 10000.0
RMS_EPS = 1e-6
MASK_VALUE = -1e30
PAST_LEN = 16384

LANES = 128
SUBLANES = 8
MXU_DIM = 256

FRONT_TM = 512
MIX_TM = 512
SCAN_SEG = MIX_TM // SUBLANES
FF_CHUNK = 256
SCAN_LC = 512
SAMPLE_BS = 8
WEIGHT_STAGE_ROWS = 64
WEIGHT_STAGE_SLOTS = 2
VMEM_LIMIT = 58 * 1024 * 1024


def _const_spec(shape):
    nd = len(shape)
    return pl.BlockSpec(shape, lambda *_: (0,) * nd, pipeline_mode=pl.Buffered(1))


def _rms(x, w):
    ms = jnp.mean(x * x, axis=-1, keepdims=True)
    return x * lax.rsqrt(ms + RMS_EPS) * w


def _load_weights_bf16(pairs, stage_s, sem):
    slots, rows = stage_s.shape[0], stage_s.shape[1]
    jobs = [(src, dst, r0) for src, dst in pairs for r0 in range(0, dst.shape[0], rows)]

    def copy(j):
        src, dst, r0 = jobs[j]
        return pltpu.make_async_copy(src.at[0, pl.ds(r0, rows), :], stage_s.at[j % slots, :, 0:dst.shape[1]],
                                     sem.at[j % slots])

    for j in range(min(slots, len(jobs))):
        copy(j).start()
    for j, (_, dst, r0) in enumerate(jobs):
        copy(j).wait()
        dst[r0:r0 + rows, :] = stage_s[j % slots, :, 0:dst.shape[1]].astype(BF16)
        if j + slots < len(jobs):
            copy(j + slots).start()


def _swiglu_residual(x, nw_ref, w1_ref, w3_ref, w2_ref):
    h = _rms(x, nw_ref[...]).astype(BF16)
    acc = None
    for c0 in range(0, D_FF, FF_CHUNK):
        a = jnp.dot(h, w1_ref[:, c0:c0 + FF_CHUNK], preferred_element_type=F32)
        b = jnp.dot(h, w3_ref[:, c0:c0 + FF_CHUNK], preferred_element_type=F32)
        g = (a * jax.nn.sigmoid(a) * b).astype(BF16)
        t = jnp.dot(g, w2_ref[c0:c0 + FF_CHUNK, :], preferred_element_type=F32)
        acc = t if acc is None else acc + t
    return x + 0.5 * acc


def _head_norm_rope(t, ones_ref, g_ref, cos, sinlo, sinhi, scale):
    ssum = jnp.dot((t * t).astype(BF16), ones_ref[...], preferred_element_type=F32)
    tn = t * lax.rsqrt(ssum * (1.0 / HEAD_DIM) + RMS_EPS) * g_ref[...]
    up = pltpu.roll(tn, HEAD_DIM // 2, axis=1)
    dn = pltpu.roll(tn, LANES - HEAD_DIM // 2, axis=1)
    r = tn * cos + dn * sinlo + up * sinhi
    return r * scale if scale != 1.0 else r


def _front_kernel(x_ref, base_ref, off_ref,
                  n1_ref, w1_hbm, w3_hbm, w2_hbm, nm_ref, win_hbm, qn_ref, kn_ref, ones_ref,
                  x1_ref, q_ref, k_ref, v_ref, u_ref, sga_ref, sgs_ref,
                  w1_ref, w3_ref, w2_ref, win_ref, stage_s, sem):
    @pl.when(pl.program_id(0) == 0)
    def _():
        _load_weights_bf16([(w1_hbm, w1_ref), (w3_hbm, w3_ref), (w2_hbm, w2_ref), (win_hbm, win_ref)],
                           stage_s, sem)

    x1 = _swiglu_residual(x_ref[...], n1_ref, w1_ref, w3_ref, w2_ref)
    x1_ref[...] = x1
    h = _rms(x1, nm_ref[...]).astype(BF16)
    cb, sb = base_ref[0:1, :], base_ref[1:2, :]
    cos = cb * off_ref[0] - sb * off_ref[1]
    sinlo = sb * off_ref[2] + cb * off_ref[3]
    sinhi = sb * off_ref[4] + cb * off_ref[5]
    o = ATTN_W + 2 * KV_W + SSM_W
    qkvu = jnp.dot(h, win_ref[:, 0:o], preferred_element_type=F32)
    for j in range(ATTN_W // LANES):
        q_ref[:, j * LANES:(j + 1) * LANES] = _head_norm_rope(
            qkvu[:, j * LANES:(j + 1) * LANES], ones_ref, qn_ref, cos, sinlo, sinhi,
            HEAD_DIM ** -0.5).astype(q_ref.dtype)
    k_ref[...] = _head_norm_rope(qkvu[:, ATTN_W:ATTN_W + KV_W], ones_ref, kn_ref, cos, sinlo, sinhi, 1.0)
    v_ref[...] = qkvu[:, ATTN_W + KV_W:ATTN_W + 2 * KV_W]
    u_ref[...] = qkvu[:, ATTN_W + 2 * KV_W:o]
    ga = jnp.dot(h, win_ref[:, o:o + D_MODEL], preferred_element_type=F32)
    sga_ref[...] = jax.nn.sigmoid(ga).astype(BF16)
    o += D_MODEL
    gs = jnp.dot(h, win_ref[:, o:o + D_MODEL], preferred_element_type=F32)
    sgs_ref[...] = jax.nn.sigmoid(gs).astype(BF16)


def _front(x, tabs, p, q_dtype):
    n = x.shape[0]
    tm = FRONT_TM
    base_tab, off_tab = tabs
    row = lambda w: pl.BlockSpec((tm, w), lambda i: (i, 0))
    consts = [p['n1'], p['w1a'], p['w3a'], p['w2a'], p['nm'], p['win'], p['qn'], p['kn'], p['ones']]
    in_hbm = {1, 2, 3, 5}
    outs = [(D_MODEL, F32), (ATTN_W, q_dtype), (KV_W, F32), (KV_W, F32), (SSM_W, F32),
            (D_MODEL, BF16), (D_MODEL, BF16)]
    return pl.pallas_call(
        _front_kernel,
        grid=(n // tm,),
        in_specs=[row(D_MODEL), pl.BlockSpec((None, 2, LANES), lambda i: (i, 0, 0)), _const_spec(off_tab.shape)]
        + [pl.BlockSpec(memory_space=pl.ANY) if j in in_hbm else _const_spec(c.shape)
           for j, c in enumerate(consts)],
        out_specs=[row(w) for w, _ in outs],
        out_shape=[jax.ShapeDtypeStruct((n, w), dt) for w, dt in outs],
        scratch_shapes=[pltpu.VMEM(consts[j].shape[1:], BF16) for j in sorted(in_hbm)]
        + [pltpu.VMEM((WEIGHT_STAGE_SLOTS, WEIGHT_STAGE_ROWS, IN_COLS), F32),
           pltpu.SemaphoreType.DMA((WEIGHT_STAGE_SLOTS,))],
        compiler_params=pltpu.CompilerParams(dimension_semantics=("arbitrary",),
                                             vmem_limit_bytes=VMEM_LIMIT),
        name="front",
    )(x, base_tab, off_tab, *consts)


def _ffn_kernel(x_ref, n_ref, w1_hbm, w3_hbm, w2_hbm, y_ref, w1_ref, w3_ref, w2_ref, stage_s, sem):
    @pl.when(pl.program_id(0) == 0)
    def _():
        _load_weights_bf16([(w1_hbm, w1_ref), (w3_hbm, w3_ref), (w2_hbm, w2_ref)], stage_s, sem)

    y_ref[...] = _swiglu_residual(x_ref[...], n_ref, w1_ref, w3_ref, w2_ref)


def _ffn(x, p):
    n = x.shape[0]
    tm = FRONT_TM
    weights = [p['w1b'], p['w3b'], p['w2b']]
    return pl.pallas_call(
        _ffn_kernel,
        grid=(n // tm,),
        in_specs=[pl.BlockSpec((tm, D_MODEL), lambda i: (i, 0)), _const_spec(p['n2'].shape)]
        + [pl.BlockSpec(memory_space=pl.ANY)] * len(weights),
        out_specs=pl.BlockSpec((tm, D_MODEL), lambda i: (i, 0)),
        out_shape=jax.ShapeDtypeStruct((n, D_MODEL), F32),
        scratch_shapes=[pltpu.VMEM(w.shape[1:], BF16) for w in weights]
        + [pltpu.VMEM((WEIGHT_STAGE_SLOTS, WEIGHT_STAGE_ROWS, D_FF), F32),
           pltpu.SemaphoreType.DMA((WEIGHT_STAGE_SLOTS,))],
        compiler_params=pltpu.CompilerParams(dimension_semantics=("arbitrary",),
                                             vmem_limit_bytes=VMEM_LIMIT),
        name="ffn",
    )(x, p['n2'], *weights)


def _cmul(ar, ai, br, bi):
    return ar * br - ai * bi, ar * bi + ai * br


def _ssm_discretize(are_ref, aim_ref, ldt_ref):
    a_re, a_im = are_ref[...], aim_ref[...]
    dt = jnp.exp(ldt_ref[...])
    mag = jnp.exp(dt * a_re)
    ab_re = mag * jnp.cos(dt * a_im)
    ab_im = mag * jnp.sin(dt * a_im)
    den = a_re * a_re + a_im * a_im
    nr, ni = ab_re - 1.0, ab_im
    f_re = (nr * a_re + ni * a_im) / den
    f_im = (ni * a_re - nr * a_im) / den
    return ab_re, ab_im, f_re, f_im


def _build_wb(bbre_ref, bbim_ref, f_re, f_im, wbre_s, wbim_s):
    half = SSM_S // 2
    for hh in range(2):
        fr, fi = f_re[:, hh * half:(hh + 1) * half], f_im[:, hh * half:(hh + 1) * half]
        br, bi = bbre_ref[hh], bbim_ref[hh]
        wbre_s[hh] = (br * fr - bi * fi).astype(BF16)
        wbim_s[hh] = (br * fi + bi * fr).astype(BF16)


def _expand_state(ub, wbre_s, wbim_s, bur_s, bui_s):
    half_in, half_st = SSM_W // 2, SSM_S // 2
    for hh in range(2):
        uh = ub[:, hh * half_in:(hh + 1) * half_in]
        bur_s[:, hh * half_st:(hh + 1) * half_st] = jnp.dot(uh, wbre_s[hh], preferred_element_type=F32)
        bui_s[:, hh * half_st:(hh + 1) * half_st] = jnp.dot(uh, wbim_s[hh], preferred_element_type=F32)


def _project_state(xr_s, xi_s, ccat_ref):
    half_st = SSM_S // 2
    outs = []
    for hh in range(2):
        sl = slice(hh * half_st, (hh + 1) * half_st)
        xcat = jnp.concatenate([xr_s[:, sl].astype(BF16), xi_s[:, sl].astype(BF16)], axis=1)
        outs.append(jnp.dot(xcat, ccat_ref[hh], preferred_element_type=F32))
    return jnp.concatenate(outs, axis=1)


def _gated_attn(attn_b, sga, wao_ref):
    return sga.astype(F32) * jnp.dot(attn_b, wao_ref[...], preferred_element_type=F32)


def _gated_merge(x1, attn_gated, y_ssm, sgs, wglu_ref, bglu_ref, wso_ref, wout_ref):
    z = jax.nn.gelu(y_ssm, approximate=True)
    gl = jnp.dot(z.astype(BF16), wglu_ref[...], preferred_element_type=F32) + bglu_ref[...]
    ssm = (z * jax.nn.sigmoid(gl)).astype(BF16)
    merged = attn_gated + sgs.astype(F32) * jnp.dot(ssm, wso_ref[...], preferred_element_type=F32)
    return x1 + jnp.dot(merged.astype(BF16), wout_ref[...], preferred_element_type=F32)


def _replicate_heads(kv):
    sw = pltpu.roll(kv, HEAD_DIM, axis=1)
    lo = lax.broadcasted_iota(jnp.int32, kv.shape, 1) < HEAD_DIM
    return jnp.where(lo, kv, sw).astype(BF16), jnp.where(lo, sw, kv).astype(BF16)


def _attn_scores(qs, kreps, keys_on_lanes=False):
    qb = qs[0].shape[0]
    lo = lax.broadcasted_iota(jnp.int32, (qb, LANES), 1) < HEAD_DIM
    dims = (((1,), (0,)), ((), ())) if keys_on_lanes else (((1,), (1,)), ((), ()))
    scores = []
    for q, krep in zip(qs, kreps):
        if q.dtype != BF16 or qb % (2 * SUBLANES):
            q = q.astype(F32)
        zero = jnp.zeros((), q.dtype)
        for kv in range(N_KV):
            parts = []
            for jj in range(2):
                t = q[:, (2 * kv + jj) * LANES:(2 * kv + jj + 1) * LANES]
                parts += [jnp.where(lo, t, zero), jnp.where(lo, zero, t)]
            stacked = jnp.concatenate(parts, axis=0).astype(BF16)
            scores.append(lax.dot_general(stacked, krep[kv], dims, preferred_element_type=F32))
    return jnp.concatenate(scores, axis=0)


def _attn_softmax(s, sinks_ref, qb, firsts):
    per_block = s.shape[0] // len(firsts)
    sink_parts = [jnp.full((qb, 1), sinks_ref[h], F32) for h in range(N_HEADS)]
    sink = jnp.concatenate(sink_parts * len(firsts), axis=0)
    qi = lax.broadcasted_iota(jnp.int32, (per_block, 2 * WINDOW), 0) & (qb - 1)
    sj = lax.broadcasted_iota(jnp.int32, (per_block, 2 * WINDOW), 1)
    band = (sj <= WINDOW + qi) & (sj > qi)
    valid = jnp.concatenate(
        [band if f is False else band & ((sj >= WINDOW) | jnp.logical_not(f)) for f in firsts], axis=0)
    s = jnp.where(valid, s, MASK_VALUE)
    m = jnp.maximum(jnp.max(s, axis=-1, keepdims=True), sink)
    e = jnp.exp(s - m)
    denom = jnp.sum(e, axis=-1, keepdims=True) + jnp.exp(sink - m)
    return (e * (1.0 / denom)).astype(BF16)


def _attn_values(probs, vreps, qb, keys_on_lanes=False):
    lo = lax.broadcasted_iota(jnp.int32, (qb, LANES), 1) < HEAD_DIM
    rows = 2 * (N_HEADS // N_KV // 2) * qb
    dims = (((1,), (1,)), ((), ())) if keys_on_lanes else (((1,), (0,)), ((), ()))
    outs = []
    for b, vrep in enumerate(vreps):
        tiles = []
        for kv in range(N_KV):
            r0 = (b * N_KV + kv) * rows
            o = lax.dot_general(probs[r0:r0 + rows], vrep[kv], dims, preferred_element_type=F32)
            for jj in range(2):
                oa = o[(2 * jj) * qb:(2 * jj + 1) * qb]
                ob = o[(2 * jj + 1) * qb:(2 * jj + 2) * qb]
                tiles.append(jnp.where(lo, oa, ob))
        outs.append(jnp.concatenate(tiles, axis=1))
    return outs


def _mix_prompt_kernel(x1_ref, q_ref, k_ref, v_ref, u_ref, sga_ref, sgs_ref, sinks_ref,
                       wao_ref, wglu_ref, bglu_ref, wso_ref, wout_ref,
                       are_ref, aim_ref, ldt_ref, bbre_ref, bbim_ref, ccat_ref, d_ref, perm_ref, permt_ref,
                       x2_ref, sre_ref, sim_ref,
                       wbre_s, wbim_s, apr_s, api_s, a1_s, aseg_s, carry_s, cin_s,
                       bur_s, bui_s, xb_s, kprev_s, vprev_s):
    b, i = pl.program_id(0), pl.program_id(1)
    nt = pl.num_programs(1)

    @pl.when((b == 0) & (i == 0))
    def _():
        ab_re, ab_im, f_re, f_im = _ssm_discretize(are_ref, aim_ref, ldt_ref)
        _build_wb(bbre_ref, bbim_ref, f_re, f_im, wbre_s, wbim_s)
        a1_s[0:1, :] = ab_re
        a1_s[1:2, :] = ab_im
        pr, pi = ab_re, ab_im
        for j in range(SCAN_SEG):
            apr_s[j * SUBLANES:(j + 1) * SUBLANES, :] = jnp.broadcast_to(pr, (SUBLANES, SSM_S))
            api_s[j * SUBLANES:(j + 1) * SUBLANES, :] = jnp.broadcast_to(pi, (SUBLANES, SSM_S))
            if j == SCAN_SEG - 1:
                aseg_s[0:1, :] = pr
                aseg_s[1:2, :] = pi
            pr, pi = _cmul(pr, pi, ab_re, ab_im)

    @pl.when(i == 0)
    def _():
        carry_s[...] = jnp.zeros_like(carry_s)
        kprev_s[...] = jnp.zeros_like(kprev_s)
        vprev_s[...] = jnp.zeros_like(vprev_s)

    k0, k1 = _replicate_heads(k_ref[...])
    v0, v1 = _replicate_heads(v_ref[...])
    qs, kreps, vreps, firsts = [], [], [], []
    for blk in range(MIX_TM // WINDOW):
        cur = slice(blk * WINDOW, (blk + 1) * WINDOW)
        if blk == 0:
            prev = [kprev_s[0], kprev_s[1], vprev_s[0], vprev_s[1]]
            firsts.append(i == 0)
        else:
            ps = slice((blk - 1) * WINDOW, blk * WINDOW)
            prev = [k0[ps], k1[ps], v0[ps], v1[ps]]
            firsts.append(False)
        qs.append(q_ref[cur, :])
        kreps.append([jnp.concatenate([prev[0], k0[cur]], axis=0), jnp.concatenate([prev[1], k1[cur]], axis=0)])
        vreps.append([jnp.concatenate([prev[2], v0[cur]], axis=0), jnp.concatenate([prev[3], v1[cur]], axis=0)])
    tail = slice(MIX_TM - WINDOW, MIX_TM)
    kprev_s[0], kprev_s[1] = k0[tail], k1[tail]
    vprev_s[0], vprev_s[1] = v0[tail], v1[tail]
    scores = _attn_scores(qs, kreps)

    half_in, half_st = SSM_W // 2, SSM_S // 2
    chunks = list(range(0, SSM_S, SCAN_LC))
    nblk = len(qs)
    blk_rows = scores.shape[0] // nblk
    u = u_ref[...]
    up = jnp.dot(perm_ref[...], u.astype(BF16), preferred_element_type=F32).astype(BF16)

    def expand_half(hh):
        uh = up[:, hh * half_in:(hh + 1) * half_in]
        bur_s[:, hh * half_st:(hh + 1) * half_st] = jnp.dot(uh, wbre_s[hh], preferred_element_type=F32)
        bui_s[:, hh * half_st:(hh + 1) * half_st] = jnp.dot(uh, wbim_s[hh], preferred_element_type=F32)

    def softmax_blocks(lo, hi):
        return _attn_softmax(scores[lo * blk_rows:hi * blk_rows], sinks_ref, WINDOW, firsts[lo:hi])

    def scan_chunk(c0):
        cs = slice(c0, c0 + SCAN_LC)
        ar = jnp.broadcast_to(a1_s[0:1, cs], (SUBLANES, SCAN_LC))
        ai = jnp.broadcast_to(a1_s[1:2, cs], (SUBLANES, SCAN_LC))
        xr, xi = bur_s[0:SUBLANES, cs], bui_s[0:SUBLANES, cs]
        for j in range(1, SCAN_SEG):
            rs = slice(j * SUBLANES, (j + 1) * SUBLANES)
            xr, xi = (ar * xr - ai * xi + bur_s[rs, cs], ar * xi + ai * xr + bui_s[rs, cs])
            bur_s[rs, cs] = xr
            bui_s[rs, cs] = xi

    mid = nblk // 2
    expand_half(0)
    probs_a = softmax_blocks(0, mid)
    expand_half(1)
    probs_b = softmax_blocks(mid, nblk)
    scan_chunk(chunks[0])
    attn = _attn_values(probs_a, vreps[0:mid], WINDOW)
    scan_chunk(chunks[1])
    attn += _attn_values(probs_b, vreps[mid:nblk], WINDOW)
    scan_chunk(chunks[2])
    attn_gated = _gated_attn(jnp.concatenate(attn, axis=0).astype(BF16), sga_ref[...], wao_ref)
    scan_chunk(chunks[3])

    last = MIX_TM - SUBLANES
    sr, si = aseg_s[0:1, :], aseg_s[1:2, :]
    cr, ci = carry_s[0:1, :], carry_s[1:2, :]
    for s in range(SUBLANES):
        cin_s[0, s:s + 1, :] = cr
        cin_s[1, s:s + 1, :] = ci
        pr, pi = _cmul(sr, si, cr, ci)
        cr = pr + bur_s[last + s:last + s + 1, :]
        ci = pi + bui_s[last + s:last + s + 1, :]
    carry_s[0:1, :] = cr
    carry_s[1:2, :] = ci
    pack = 2 * SUBLANES

    def fix_chunk(c0):
        cs = slice(c0, c0 + SCAN_LC)
        cinr = jnp.concatenate([cin_s[0, :, cs]] * 2, axis=0)
        cini = jnp.concatenate([cin_s[1, :, cs]] * 2, axis=0)
        col = (c0 // half_st) * SSM_S + c0 % half_st
        for r0 in range(0, MIX_TM, pack):
            rs = slice(r0, r0 + pack)
            pr, pi = apr_s[rs, cs], api_s[rs, cs]
            xb_s[rs, col:col + SCAN_LC] = (bur_s[rs, cs] + pr * cinr - pi * cini).astype(BF16)
            xb_s[rs, col + half_st:col + half_st + SCAN_LC] = (bui_s[rs, cs] + pr * cini + pi * cinr).astype(BF16)

    def project_half(hh):
        return jnp.dot(xb_s[:, hh * SSM_S:(hh + 1) * SSM_S], ccat_ref[hh], preferred_element_type=F32)

    per_half = len(chunks) // 2
    for c0 in chunks[:per_half]:
        fix_chunk(c0)
    yp0 = project_half(0)
    for c0 in chunks[per_half:]:
        fix_chunk(c0)
    yp = jnp.concatenate([yp0, project_half(1)], axis=1)
    y_hi = yp.astype(BF16)
    y_lo = (yp - y_hi.astype(F32)).astype(BF16)
    y_ssm = (jnp.dot(permt_ref[...], y_hi, preferred_element_type=F32)
             + jnp.dot(permt_ref[...], y_lo, preferred_element_type=F32) + d_ref[...] * u)

    x2_ref[...] = _gated_merge(x1_ref[...], attn_gated, y_ssm, sgs_ref[...],
                               wglu_ref, bglu_ref, wso_ref, wout_ref)

    @pl.when(i == nt - 1)
    def _():
        sre_ref[...] = carry_s[0:1, :]
        sim_ref[...] = carry_s[1:2, :]


def _mix_prompt(fr, p, batch, seq):
    x1, q, k, v, u, sga, sgs = fr
    tm = MIX_TM
    nt = seq // tm
    row = lambda w: pl.BlockSpec((tm, w), lambda b, i: (b * nt + i, 0))
    consts = [p['wao'], p['wglu'], p['bglu'], p['wso'], p['wout'],
              p['are'], p['aim'], p['ldt'], p['bbre'], p['bbim'], p['ccat'], p['dskip'], p['perm'], p['permt']]
    st_spec = pl.BlockSpec((None, 1, SSM_S), lambda b, i: (b, 0, 0))
    scratch = [pltpu.VMEM((2, MXU_DIM, SSM_S // 2), BF16), pltpu.VMEM((2, MXU_DIM, SSM_S // 2), BF16),
               pltpu.VMEM((tm, SSM_S), F32), pltpu.VMEM((tm, SSM_S), F32),
               pltpu.VMEM((SUBLANES, SSM_S), F32), pltpu.VMEM((SUBLANES, SSM_S), F32),
               pltpu.VMEM((SUBLANES, SSM_S), F32), pltpu.VMEM((2, SUBLANES, SSM_S), F32),
               pltpu.VMEM((tm, SSM_S), F32), pltpu.VMEM((tm, SSM_S), F32), pltpu.VMEM((tm, 2 * SSM_S), BF16),
               pltpu.VMEM((N_KV, WINDOW, LANES), BF16), pltpu.VMEM((N_KV, WINDOW, LANES), BF16)]
    return pl.pallas_call(
        _mix_prompt_kernel,
        grid=(batch, nt),
        in_specs=[row(D_MODEL), row(ATTN_W), row(KV_W), row(KV_W), row(SSM_W), row(D_MODEL), row(D_MODEL),
                  pl.BlockSpec(memory_space=pltpu.SMEM)] + [_const_spec(c.shape) for c in consts],
        out_specs=[row(D_MODEL), st_spec, st_spec],
        out_shape=[jax.ShapeDtypeStruct((batch * seq, D_MODEL), F32),
                   jax.ShapeDtypeStruct((batch, 1, SSM_S), F32),
                   jax.ShapeDtypeStruct((batch, 1, SSM_S), F32)],
        scratch_shapes=scratch,
        compiler_params=pltpu.CompilerParams(dimension_semantics=("arbitrary", "arbitrary"),
                                             vmem_limit_bytes=VMEM_LIMIT),
        name="mix_prompt",
    )(x1, q, k, v, u, sga, sgs, p['sinks'], *consts)


def _attn_sample_kernel(q_ref, kn_ref, vn_ref, ckt_ref, cvt_ref, sinks_ref, o_ref, nkt_ref, nvt_ref,
                        q8_s, knp_s, vnp_s):
    steps = q_ref.shape[0]
    shift = WINDOW - steps
    keep = lax.broadcasted_iota(jnp.int32, (HEAD_DIM, WINDOW), 1) < shift
    q8_s[...] = jnp.zeros_like(q8_s)
    knp_s[...] = jnp.zeros_like(knp_s)
    vnp_s[...] = jnp.zeros_like(vnp_s)
    qs, kreps, vreps = [], [], []
    for s in range(SAMPLE_BS):
        for t in range(steps):
            q8_s[s, t:t + 1, :] = q_ref[t, s:s + 1, :]
            knp_s[s, t:t + 1, :] = kn_ref[t, s:s + 1, :]
            vnp_s[s, t:t + 1, :] = vn_ref[t, s:s + 1, :]
        news = [knp_s[s].T, vnp_s[s].T]
        ends = [pltpu.roll(n, shift, axis=1) for n in news]
        krep, vrep = [], []
        for kv in range(N_KV):
            hs = slice(kv * HEAD_DIM, (kv + 1) * HEAD_DIM)
            for cache_ref, out_ref, new, end, reps in ((ckt_ref, nkt_ref, news[0], ends[0], krep),
                                                       (cvt_ref, nvt_ref, news[1], ends[1], vrep)):
                old = cache_ref[s, kv]
                out_ref[s, kv] = jnp.where(keep, pltpu.roll(old, shift, axis=1), end[hs])
                both = jnp.concatenate([old, new[hs]], axis=1).astype(BF16)
                reps.append(jnp.concatenate([both, both], axis=0))
        qs.append(q8_s[s])
        kreps.append(krep)
        vreps.append(vrep)
    probs = _attn_softmax(_attn_scores(qs, kreps, keys_on_lanes=True), sinks_ref, SUBLANES, [False] * SAMPLE_BS)
    outs = _attn_values(probs, vreps, SUBLANES, keys_on_lanes=True)
    for s in range(SAMPLE_BS):
        for t in range(steps):
            o_ref[t, s:s + 1, :] = outs[s][t:t + 1, :]


def _attn_sample(q, kn, vn, ckt, cvt, sinks, steps, nseq):
    bs = SAMPLE_BS
    blk3 = lambda w: pl.BlockSpec((steps, bs, w), lambda i: (0, i, 0))
    cache = pl.BlockSpec((bs, N_KV, HEAD_DIM, WINDOW), lambda i: (i, 0, 0, 0))
    cache_shape = jax.ShapeDtypeStruct((nseq, N_KV, HEAD_DIM, WINDOW), F32)
    return pl.pallas_call(
        _attn_sample_kernel,
        grid=(nseq // bs,),
        in_specs=[blk3(ATTN_W), blk3(KV_W), blk3(KV_W), cache, cache, pl.BlockSpec(memory_space=pltpu.SMEM)],
        out_specs=[blk3(ATTN_W), cache, cache],
        out_shape=[jax.ShapeDtypeStruct((steps, nseq, ATTN_W), F32), cache_shape, cache_shape],
        scratch_shapes=[pltpu.VMEM((bs, SUBLANES, ATTN_W), F32), pltpu.VMEM((bs, LANES, KV_W), F32),
                        pltpu.VMEM((bs, LANES, KV_W), F32)],
        compiler_params=pltpu.CompilerParams(dimension_semantics=("arbitrary",)),
        name="attn_sample",
    )(q.reshape(steps, nseq, ATTN_W), kn.reshape(steps, nseq, KV_W), vn.reshape(steps, nseq, KV_W),
      ckt, cvt, sinks)


def _mix_sample_kernel(x1_ref, attn_ref, u_ref, sga_ref, sgs_ref, x0r_ref, x0i_ref,
                       wao_ref, wglu_ref, bglu_ref, wso_ref, wout_ref,
                       are_ref, aim_ref, ldt_ref, bbre_ref, bbim_ref, ccat_ref, d_ref,
                       x2_ref, sre_ref, sim_ref,
                       wbre_s, wbim_s, bur_s, bui_s):
    nseq = x0r_ref.shape[0]
    steps = x1_ref.shape[0] // nseq
    ab_re, ab_im, f_re, f_im = _ssm_discretize(are_ref, aim_ref, ldt_ref)
    _build_wb(bbre_ref, bbim_ref, f_re, f_im, wbre_s, wbim_s)
    u = u_ref[...]
    _expand_state(u.astype(BF16), wbre_s, wbim_s, bur_s, bui_s)
    for c0 in range(0, SSM_S, LANES):
        cs = slice(c0, c0 + LANES)
        ar, ai = ab_re[:, cs], ab_im[:, cs]
        xr, xi = x0r_ref[:, cs], x0i_ref[:, cs]
        for t in range(steps):
            rs = slice(t * nseq, (t + 1) * nseq)
            nr = ar * xr - ai * xi + bur_s[rs, cs]
            ni = ar * xi + ai * xr + bui_s[rs, cs]
            bur_s[rs, cs] = nr
            bui_s[rs, cs] = ni
            xr, xi = nr, ni
        sre_ref[:, cs] = xr
        sim_ref[:, cs] = xi
    y_ssm = _project_state(bur_s, bui_s, ccat_ref) + d_ref[...] * u
    attn_gated = _gated_attn(attn_ref[...].astype(BF16), sga_ref[...], wao_ref)
    x2_ref[...] = _gated_merge(x1_ref[...], attn_gated, y_ssm, sgs_ref[...],
                               wglu_ref, bglu_ref, wso_ref, wout_ref)


def _mix_sample(x1, attn, u, sga, sgs, x0r, x0i, p):
    n, nseq = x1.shape[0], x0r.shape[0]
    args = [x1, attn, u, sga, sgs, x0r, x0i, p['wao'], p['wglu'], p['bglu'], p['wso'], p['wout'],
            p['are'], p['aim'], p['ldt'], p['bbre'], p['bbim'], p['ccat'], p['dskip']]
    return pl.pallas_call(
        _mix_sample_kernel,
        grid=(1,),
        in_specs=[_const_spec(a.shape) for a in args],
        out_specs=[pl.BlockSpec((n, D_MODEL), lambda i: (0, 0)), pl.BlockSpec((nseq, SSM_S), lambda i: (0, 0)),
                   pl.BlockSpec((nseq, SSM_S), lambda i: (0, 0))],
        out_shape=[jax.ShapeDtypeStruct((n, D_MODEL), F32),
                   jax.ShapeDtypeStruct((nseq, SSM_S), F32),
                   jax.ShapeDtypeStruct((nseq, SSM_S), F32)],
        scratch_shapes=[pltpu.VMEM((2, MXU_DIM, SSM_S // 2), BF16), pltpu.VMEM((2, MXU_DIM, SSM_S // 2), BF16),
                        pltpu.VMEM((n, SSM_S), F32), pltpu.VMEM((n, SSM_S), F32)],
        compiler_params=pltpu.CompilerParams(dimension_semantics=("arbitrary",),
                                             vmem_limit_bytes=VMEM_LIMIT),
        name="mix_sample",
    )(*args)


def _rope_tables(base_pos, off_pos):
    half = HEAD_DIM // 2
    lane = np.arange(LANES)
    inv = ROPE_THETA ** (-2.0 * (lane % half) / HEAD_DIM)
    first = (lane % HEAD_DIM) < half
    ab = np.asarray(base_pos, np.float64)[:, None] * inv[None, :]
    ao = np.asarray(off_pos, np.float64)[:, None] * inv[None, :]
    base = np.stack([np.cos(ab), np.sin(ab)], axis=1)
    co, so = np.cos(ao), np.sin(ao)
    lo = np.where(first, -1.0, 0.0)
    hi = np.where(first, 0.0, 1.0)
    off = np.stack([co, so, co * lo, so * lo, co * hi, so * hi], axis=0)
    return jnp.asarray(base, F32), jnp.asarray(off, F32)


def _block_diag_halves(m):
    g, r, c = m.shape
    hg = g // 2
    eye = jnp.asarray(np.eye(hg), m.dtype)
    mh = m.reshape(2, hg, r, c)
    return (mh[:, :, :, None, :] * eye[None, :, None, :, None]).reshape(2, hg * r, hg * c)


def _layout_params(ffn1_norm, ffn1_w1, ffn1_w3, ffn1_w2, mix_norm, w_in, q_norm, k_norm, attn_sinks,
                   w_attn_out, ssm_a_re, ssm_a_im, ssm_log_dt, ssm_b_re, ssm_b_im, ssm_c_re, ssm_c_im,
                   ssm_d, w_glu, b_glu, w_ssm_out, w_out, ffn2_norm, ffn2_w1, ffn2_w3, ffn2_w2):
    l = 0
    row = lambda a: a.reshape(1, -1).astype(F32)
    head_ones = jnp.asarray(np.kron(np.eye(LANES // HEAD_DIM), np.ones((HEAD_DIM, HEAD_DIM))), F32)
    r = np.arange(MIX_TM)
    tok = (r % SUBLANES) * SCAN_SEG + r // SUBLANES
    perm = jnp.asarray(tok[:, None] == np.arange(MIX_TM)[None, :], BF16)
    c_cat = jnp.concatenate([_block_diag_halves(jnp.swapaxes(ssm_c_re[l], 1, 2)),
                             -_block_diag_halves(jnp.swapaxes(ssm_c_im[l], 1, 2))], axis=1)
    return dict(
        n1=row(ffn1_norm[l]), w1a=ffn1_w1, w3a=ffn1_w3, w2a=ffn1_w2,
        nm=row(mix_norm[l]), win=w_in,
        qn=row(jnp.tile(q_norm[l], LANES // HEAD_DIM)), kn=row(jnp.tile(k_norm[l], LANES // HEAD_DIM)),
        ones=head_ones.astype(BF16), sinks=attn_sinks[l].astype(F32),
        wao=w_attn_out[l].astype(BF16), wglu=w_glu[l].astype(BF16), bglu=row(b_glu[l]),
        wso=w_ssm_out[l].astype(BF16), wout=w_out[l].astype(BF16),
        are=row(ssm_a_re[l]), aim=row(ssm_a_im[l]), ldt=row(jnp.repeat(ssm_log_dt[l], SSM_N)),
        bbre=_block_diag_halves(jnp.swapaxes(ssm_b_re[l], 1, 2).astype(F32)),
        bbim=_block_diag_halves(jnp.swapaxes(ssm_b_im[l], 1, 2).astype(F32)),
        ccat=c_cat.astype(BF16), dskip=row(ssm_d[l]), perm=perm, permt=perm.T,
        n2=row(ffn2_norm[l]), w1b=ffn2_w1, w3b=ffn2_w3, w2b=ffn2_w2,
    )


def kernel(x_prompt, x_sample, cache_k, cache_v, state_ssm_re, state_ssm_im, ffn1_norm, ffn1_w1, ffn1_w3, ffn1_w2, mix_norm, w_in, q_norm, k_norm, attn_sinks, w_attn_out, ssm_a_re, ssm_a_im, ssm_log_dt, ssm_b_re, ssm_b_im, ssm_c_re, ssm_c_im, ssm_d, w_glu, b_glu, w_ssm_out, w_out, ffn2_norm, ffn2_w1, ffn2_w3, ffn2_w2):
    assert ffn1_norm.shape[0] == 1, "single trunk layer"
    batch, seq, _ = x_prompt.shape
    nseq, steps, _ = x_sample.shape
    p = _layout_params(ffn1_norm, ffn1_w1, ffn1_w3, ffn1_w2, mix_norm, w_in, q_norm, k_norm, attn_sinks,
                       w_attn_out, ssm_a_re, ssm_a_im, ssm_log_dt, ssm_b_re, ssm_b_im, ssm_c_re, ssm_c_im,
                       ssm_d, w_glu, b_glu, w_ssm_out, w_out, ffn2_norm, ffn2_w1, ffn2_w3, ffn2_w2)

    tile0 = np.arange(batch * seq // FRONT_TM) * FRONT_TM
    tabs_p = _rope_tables(tile0 % seq, np.arange(FRONT_TM))
    fr = _front(x_prompt.reshape(batch * seq, D_MODEL), tabs_p, p, BF16)
    x2p, re_p, im_p = _mix_prompt(fr, p, batch, seq)
    y_prompt = _ffn(x2p, p).reshape(batch, seq, D_MODEL)
    tail_kv = lambda a: a.reshape(batch, seq, KV_W)[:, seq - WINDOW:].reshape(batch, WINDOW, N_KV, HEAD_DIM)
    k_p, v_p = tail_kv(fr[2]), tail_kv(fr[3])

    xs = jnp.swapaxes(x_sample, 0, 1).reshape(steps * nseq, D_MODEL)
    tile0 = np.arange(steps * nseq // FRONT_TM) * FRONT_TM
    tabs_s = _rope_tables(PAST_LEN + tile0 // nseq, np.arange(FRONT_TM) // nseq)
    x1s, qs, ks, vs, us, sgas, sgss = _front(xs, tabs_s, p, F32)
    to_lanes = lambda c: jnp.transpose(c[0], (0, 2, 3, 1))
    attn_s, nkt, nvt = _attn_sample(qs, ks, vs, to_lanes(cache_k), to_lanes(cache_v), p['sinks'], steps, nseq)
    x2s, re_s, im_s = _mix_sample(x1s, attn_s.reshape(steps * nseq, ATTN_W), us, sgas, sgss,
                                  state_ssm_re[0].reshape(nseq, SSM_S), state_ssm_im[0].reshape(nseq, SSM_S), p)
    y_sample = jnp.swapaxes(_ffn(x2s, p).reshape(steps, nseq, D_MODEL), 0, 1)
    k_s, v_s = jnp.transpose(nkt, (0, 3, 1, 2)), jnp.transpose(nvt, (0, 3, 1, 2))

    st = lambda a, n: a.reshape(1, n, SSM_G, SSM_N)
    return (y_prompt, y_sample, k_p[None], v_p[None], st(re_p, batch), st(im_p, batch),
            k_s[None], v_s[None], st(re_s, nseq), st(im_s, nseq))
```

```python
import jax
import jax.numpy as jnp
import numpy as np
from jax import lax
from jax.experimental import pallas as pl
from jax.experimental.pallas import tpu as pltpu

F32 = jnp.float32
BF16 = jnp.bfloat16

D_MODEL = 1024
HEAD_DIM = 64
N_HEADS = 8
N_KV = 2
WINDOW = 128
ATTN_W = N_HEADS * HEAD_DIM
KV_W = N_KV * HEAD_DIM
SSM_W = 512
SSM_G = 32
SSM_CH = 16
SSM_N = 64
SSM_S = SSM_G * SSM_N
D_FF = 2816
IN_COLS = ATTN_W + 2 * KV_W + SSM_W + 2 * D_MODEL
ROPE_THETA = 10000.0
RMS_EPS = 1e-6
MASK_VALUE = -1e30
PAST_LEN = 16384

LANES = 128
SUBLANES = 8
MXU_DIM = 256

FRONT_TM = 512
MIX_TM = 512
SCAN_SEG = MIX_TM // SUBLANES
FF_CHUNK = 256
SCAN_LC = 512
SAMPLE_BS = 8
VMEM_LIMIT = 58 * 1024 * 1024


def _const_spec(shape):
    nd = len(shape)
    return pl.BlockSpec(shape, lambda *_: (0,) * nd, pipeline_mode=pl.Buffered(1))


def _rms(x, w):
    ms = jnp.mean(x * x, axis=-1, keepdims=True)
    return x * lax.rsqrt(ms + RMS_EPS) * w


def _swiglu_residual(x, nw_ref, w1_ref, w3_ref, w2_ref):
    h = _rms(x, nw_ref[...]).astype(BF16)
    acc = None
    for c0 in range(0, D_FF, FF_CHUNK):
        a = jnp.dot(h, w1_ref[:, c0:c0 + FF_CHUNK], preferred_element_type=F32)
        b = jnp.dot(h, w3_ref[:, c0:c0 + FF_CHUNK], preferred_element_type=F32)
        g = (a * jax.nn.sigmoid(a) * b).astype(BF16)
        t = jnp.dot(g, w2_ref[c0:c0 + FF_CHUNK, :], preferred_element_type=F32)
        acc = t if acc is None else acc + t
    return x + 0.5 * acc


def _head_norm_rope(t, ones_ref, g_ref, cos, sinlo, sinhi, scale):
    ssum = jnp.dot((t * t).astype(BF16), ones_ref[...], preferred_element_type=F32)
    tn = t * lax.rsqrt(ssum * (1.0 / HEAD_DIM) + RMS_EPS) * g_ref[...]
    up = pltpu.roll(tn, HEAD_DIM // 2, axis=1)
    dn = pltpu.roll(tn, LANES - HEAD_DIM // 2, axis=1)
    r = tn * cos + dn * sinlo + up * sinhi
    return r * scale if scale != 1.0 else r


def _front_kernel(x_ref, base_ref, off_ref,
                  n1_ref, w1_ref, w3_ref, w2_ref, nm_ref, win_ref, qn_ref, kn_ref, ones_ref,
                  x1_ref, q_ref, k_ref, v_ref, u_ref, sga_ref, sgs_ref):
    x1 = _swiglu_residual(x_ref[...], n1_ref, w1_ref, w3_ref, w2_ref)
    x1_ref[...] = x1
    h = _rms(x1, nm_ref[...]).astype(BF16)
    cb, sb = base_ref[0:1, :], base_ref[1:2, :]
    cos = cb * off_ref[0] - sb * off_ref[1]
    sinlo = sb * off_ref[2] + cb * off_ref[3]
    sinhi = sb * off_ref[4] + cb * off_ref[5]
    o = ATTN_W + 2 * KV_W + SSM_W
    qkvu = jnp.dot(h, win_ref[:, 0:o], preferred_element_type=F32)
    for j in range(ATTN_W // LANES):
        q_ref[:, j * LANES:(j + 1) * LANES] = _head_norm_rope(
            qkvu[:, j * LANES:(j + 1) * LANES], ones_ref, qn_ref, cos, sinlo, sinhi,
            HEAD_DIM ** -0.5).astype(q_ref.dtype)
    k_ref[...] = _head_norm_rope(qkvu[:, ATTN_W:ATTN_W + KV_W], ones_ref, kn_ref, cos, sinlo, sinhi, 1.0)
    v_ref[...] = qkvu[:, ATTN_W + KV_W:ATTN_W + 2 * KV_W]
    u_ref[...] = qkvu[:, ATTN_W + 2 * KV_W:o]
    ga = jnp.dot(h, win_ref[:, o:o + D_MODEL], preferred_element_type=F32)
    sga_ref[...] = jax.nn.sigmoid(ga).astype(BF16)
    o += D_MODEL
    gs = jnp.dot(h, win_ref[:, o:o + D_MODEL], preferred_element_type=F32)
    sgs_ref[...] = jax.nn.sigmoid(gs).astype(BF16)


def _front(x, tabs, p, q_dtype):
    n = x.shape[0]
    tm = FRONT_TM
    base_tab, off_tab = tabs
    row = lambda w: pl.BlockSpec((tm, w), lambda i: (i, 0))
    consts = [p['n1'], p['w1a'], p['w3a'], p['w2a'], p['nm'], p['win'], p['qn'], p['kn'], p['ones']]
    outs = [(D_MODEL, F32), (ATTN_W, q_dtype), (KV_W, F32), (KV_W, F32), (SSM_W, F32),
            (D_MODEL, BF16), (D_MODEL, BF16)]
    return pl.pallas_call(
        _front_kernel,
        grid=(n // tm,),
        in_specs=[row(D_MODEL), pl.BlockSpec((None, 2, LANES), lambda i: (i, 0, 0)), _const_spec(off_tab.shape)]
        + [_const_spec(c.shape) for c in consts],
        out_specs=[row(w) for w, _ in outs],
        out_shape=[jax.ShapeDtypeStruct((n, w), dt) for w, dt in outs],
        compiler_params=pltpu.CompilerParams(dimension_semantics=("arbitrary",),
                                             vmem_limit_bytes=VMEM_LIMIT),
        name="front",
    )(x, base_tab, off_tab, *consts)


def _ffn_kernel(x_ref, n_ref, w1_ref, w3_ref, w2_ref, y_ref):
    y_ref[...] = _swiglu_residual(x_ref[...], n_ref, w1_ref, w3_ref, w2_ref)


def _ffn(x, p):
    n = x.shape[0]
    tm = FRONT_TM
    weights = [p['w1b'], p['w3b'], p['w2b']]
    return pl.pallas_call(
        _ffn_kernel,
        grid=(n // tm,),
        in_specs=[pl.BlockSpec((tm, D_MODEL), lambda i: (i, 0)), _const_spec(p['n2'].shape)]
        + [_const_spec(w.shape) for w in weights],
        out_specs=pl.BlockSpec((tm, D_MODEL), lambda i: (i, 0)),
        out_shape=jax.ShapeDtypeStruct((n, D_MODEL), F32),
        compiler_params=pltpu.CompilerParams(dimension_semantics=("arbitrary",),
                                             vmem_limit_bytes=VMEM_LIMIT),
        name="ffn",
    )(x, p['n2'], *weights)


def _cmul(ar, ai, br, bi):
    return ar * br - ai * bi, ar * bi + ai * br


def _ssm_discretize(are_ref, aim_ref, ldt_ref):
    a_re, a_im = are_ref[...], aim_ref[...]
    dt = jnp.exp(ldt_ref[...])
    mag = jnp.exp(dt * a_re)
    ab_re = mag * jnp.cos(dt * a_im)
    ab_im = mag * jnp.sin(dt * a_im)
    den = a_re * a_re + a_im * a_im
    nr, ni = ab_re - 1.0, ab_im
    f_re = (nr * a_re + ni * a_im) / den
    f_im = (ni * a_re - nr * a_im) / den
    return ab_re, ab_im, f_re, f_im


def _build_wb(bbre_ref, bbim_ref, f_re, f_im, wbre_s, wbim_s):
    half = SSM_S // 2
    for hh in range(2):
        fr, fi = f_re[:, hh * half:(hh + 1) * half], f_im[:, hh * half:(hh + 1) * half]
        br, bi = bbre_ref[hh], bbim_ref[hh]
        wbre_s[hh] = (br * fr - bi * fi).astype(BF16)
        wbim_s[hh] = (br * fi + bi * fr).astype(BF16)


def _expand_state(ub, wbre_s, wbim_s, bur_s, bui_s):
    half_in, half_st = SSM_W // 2, SSM_S // 2
    for hh in range(2):
        uh = ub[:, hh * half_in:(hh + 1) * half_in]
        bur_s[:, hh * half_st:(hh + 1) * half_st] = jnp.dot(uh, wbre_s[hh], preferred_element_type=F32)
        bui_s[:, hh * half_st:(hh + 1) * half_st] = jnp.dot(uh, wbim_s[hh], preferred_element_type=F32)


def _project_state(xr_s, xi_s, ccat_ref):
    half_st = SSM_S // 2
    outs = []
    for hh in range(2):
        sl = slice(hh * half_st, (hh + 1) * half_st)
        xcat = jnp.concatenate([xr_s[:, sl].astype(BF16), xi_s[:, sl].astype(BF16)], axis=1)
        outs.append(jnp.dot(xcat, ccat_ref[hh], preferred_element_type=F32))
    return jnp.concatenate(outs, axis=1)


def _gated_attn(attn_b, sga, wao_ref):
    return sga.astype(F32) * jnp.dot(attn_b, wao_ref[...], preferred_element_type=F32)


def _gated_merge(x1, attn_gated, y_ssm, sgs, wglu_ref, bglu_ref, wso_ref, wout_ref):
    z = jax.nn.gelu(y_ssm, approximate=True)
    gl = jnp.dot(z.astype(BF16), wglu_ref[...], preferred_element_type=F32) + bglu_ref[...]
    ssm = (z * jax.nn.sigmoid(gl)).astype(BF16)
    merged = attn_gated + sgs.astype(F32) * jnp.dot(ssm, wso_ref[...], preferred_element_type=F32)
    return x1 + jnp.dot(merged.astype(BF16), wout_ref[...], preferred_element_type=F32)


def _replicate_heads(kv):
    sw = pltpu.roll(kv, HEAD_DIM, axis=1)
    lo = lax.broadcasted_iota(jnp.int32, kv.shape, 1) < HEAD_DIM
    return jnp.where(lo, kv, sw).astype(BF16), jnp.where(lo, sw, kv).astype(BF16)


def _attn_scores(qs, kreps, keys_on_lanes=False):
    qb = qs[0].shape[0]
    lo = lax.broadcasted_iota(jnp.int32, (qb, LANES), 1) < HEAD_DIM
    dims = (((1,), (0,)), ((), ())) if keys_on_lanes else (((1,), (1,)), ((), ()))
    scores = []
    for q, krep in zip(qs, kreps):
        if q.dtype != BF16 or qb % (2 * SUBLANES):
            q = q.astype(F32)
        zero = jnp.zeros((), q.dtype)
        for kv in range(N_KV):
            parts = []
            for jj in range(2):
                t = q[:, (2 * kv + jj) * LANES:(2 * kv + jj + 1) * LANES]
                parts += [jnp.where(lo, t, zero), jnp.where(lo, zero, t)]
            stacked = jnp.concatenate(parts, axis=0).astype(BF16)
            scores.append(lax.dot_general(stacked, krep[kv], dims, preferred_element_type=F32))
    return jnp.concatenate(scores, axis=0)


def _attn_softmax(s, sinks_ref, qb, firsts):
    per_block = s.shape[0] // len(firsts)
    sink_parts = [jnp.full((qb, 1), sinks_ref[h], F32) for h in range(N_HEADS)]
    sink = jnp.concatenate(sink_parts * len(firsts), axis=0)
    qi = lax.broadcasted_iota(jnp.int32, (per_block, 2 * WINDOW), 0) & (qb - 1)
    sj = lax.broadcasted_iota(jnp.int32, (per_block, 2 * WINDOW), 1)
    band = (sj <= WINDOW + qi) & (sj > qi)
    valid = jnp.concatenate(
        [band if f is False else band & ((sj >= WINDOW) | jnp.logical_not(f)) for f in firsts], axis=0)
    s = jnp.where(valid, s, MASK_VALUE)
    m = jnp.maximum(jnp.max(s, axis=-1, keepdims=True), sink)
    e = jnp.exp(s - m)
    denom = jnp.sum(e, axis=-1, keepdims=True) + jnp.exp(sink - m)
    return (e * (1.0 / denom)).astype(BF16)


def _attn_values(probs, vreps, qb, keys_on_lanes=False):
    lo = lax.broadcasted_iota(jnp.int32, (qb, LANES), 1) < HEAD_DIM
    rows = 2 * (N_HEADS // N_KV // 2) * qb
    dims = (((1,), (1,)), ((), ())) if keys_on_lanes else (((1,), (0,)), ((), ()))
    outs = []
    for b, vrep in enumerate(vreps):
        tiles = []
        for kv in range(N_KV):
            r0 = (b * N_KV + kv) * rows
            o = lax.dot_general(probs[r0:r0 + rows], vrep[kv], dims, preferred_element_type=F32)
            for jj in range(2):
                oa = o[(2 * jj) * qb:(2 * jj + 1) * qb]
                ob = o[(2 * jj + 1) * qb:(2 * jj + 2) * qb]
                tiles.append(jnp.where(lo, oa, ob))
        outs.append(jnp.concatenate(tiles, axis=1))
    return outs


def _mix_prompt_kernel(x1_ref, q_ref, k_ref, v_ref, u_ref, sga_ref, sgs_ref, sinks_ref,
                       wao_ref, wglu_ref, bglu_ref, wso_ref, wout_ref,
                       are_ref, aim_ref, ldt_ref, bbre_ref, bbim_ref, ccat_ref, d_ref, perm_ref, permt_ref,
                       x2_ref, sre_ref, sim_ref,
                       wbre_s, wbim_s, apr_s, api_s, a1_s, aseg_s, carry_s, cin_s,
                       bur_s, bui_s, xb_s, kprev_s, vprev_s):
    b, i = pl.program_id(0), pl.program_id(1)
    nt = pl.num_programs(1)

    @pl.when((b == 0) & (i == 0))
    def _():
        ab_re, ab_im, f_re, f_im = _ssm_discretize(are_ref, aim_ref, ldt_ref)
        _build_wb(bbre_ref, bbim_ref, f_re, f_im, wbre_s, wbim_s)
        a1_s[0:1, :] = ab_re
        a1_s[1:2, :] = ab_im
        pr, pi = ab_re, ab_im
        for j in range(SCAN_SEG):
            apr_s[j * SUBLANES:(j + 1) * SUBLANES, :] = jnp.broadcast_to(pr, (SUBLANES, SSM_S))
            api_s[j * SUBLANES:(j + 1) * SUBLANES, :] = jnp.broadcast_to(pi, (SUBLANES, SSM_S))
            if j == SCAN_SEG - 1:
                aseg_s[0:1, :] = pr
                aseg_s[1:2, :] = pi
            pr, pi = _cmul(pr, pi, ab_re, ab_im)

    @pl.when(i == 0)
    def _():
        carry_s[...] = jnp.zeros_like(carry_s)
        kprev_s[...] = jnp.zeros_like(kprev_s)
        vprev_s[...] = jnp.zeros_like(vprev_s)

    k0, k1 = _replicate_heads(k_ref[...])
    v0, v1 = _replicate_heads(v_ref[...])
    qs, kreps, vreps, firsts = [], [], [], []
    for blk in range(MIX_TM // WINDOW):
        cur = slice(blk * WINDOW, (blk + 1) * WINDOW)
        if blk == 0:
            prev = [kprev_s[0], kprev_s[1], vprev_s[0], vprev_s[1]]
            firsts.append(i == 0)
        else:
            ps = slice((blk - 1) * WINDOW, blk * WINDOW)
            prev = [k0[ps], k1[ps], v0[ps], v1[ps]]
            firsts.append(False)
        qs.append(q_ref[cur, :])
        kreps.append([jnp.concatenate([prev[0], k0[cur]], axis=0), jnp.concatenate([prev[1], k1[cur]], axis=0)])
        vreps.append([jnp.concatenate([prev[2], v0[cur]], axis=0), jnp.concatenate([prev[3], v1[cur]], axis=0)])
    tail = slice(MIX_TM - WINDOW, MIX_TM)
    kprev_s[0], kprev_s[1] = k0[tail], k1[tail]
    vprev_s[0], vprev_s[1] = v0[tail], v1[tail]
    scores = _attn_scores(qs, kreps)

    half_in, half_st = SSM_W // 2, SSM_S // 2
    chunks = list(range(0, SSM_S, SCAN_LC))
    nblk = len(qs)
    blk_rows = scores.shape[0] // nblk
    u = u_ref[...]
    up = jnp.dot(perm_ref[...], u.astype(BF16), preferred_element_type=F32).astype(BF16)

    def expand_half(hh):
        uh = up[:, hh * half_in:(hh + 1) * half_in]
        bur_s[:, hh * half_st:(hh + 1) * half_st] = jnp.dot(uh, wbre_s[hh], preferred_element_type=F32)
        bui_s[:, hh * half_st:(hh + 1) * half_st] = jnp.dot(uh, wbim_s[hh], preferred_element_type=F32)

    def softmax_blocks(lo, hi):
        return _attn_softmax(scores[lo * blk_rows:hi * blk_rows], sinks_ref, WINDOW, firsts[lo:hi])

    def scan_chunk(c0):
        cs = slice(c0, c0 + SCAN_LC)
        ar = jnp.broadcast_to(a1_s[0:1, cs], (SUBLANES, SCAN_LC))
        ai = jnp.broadcast_to(a1_s[1:2, cs], (SUBLANES, SCAN_LC))
        xr, xi = bur_s[0:SUBLANES, cs], bui_s[0:SUBLANES, cs]
        for j in range(1, SCAN_SEG):
            rs = slice(j * SUBLANES, (j + 1) * SUBLANES)
            xr, xi = (ar * xr - ai * xi + bur_s[rs, cs], ar * xi + ai * xr + bui_s[rs, cs])
            bur_s[rs, cs] = xr
            bui_s[rs, cs] = xi

    mid = nblk // 2
    expand_half(0)
    probs_a = softmax_blocks(0, mid)
    expand_half(1)
    probs_b = softmax_blocks(mid, nblk)
    scan_chunk(chunks[0])
    attn = _attn_values(probs_a, vreps[0:mid], WINDOW)
    scan_chunk(chunks[1])
    attn += _attn_values(probs_b, vreps[mid:nblk], WINDOW)
    scan_chunk(chunks[2])
    attn_gated = _gated_attn(jnp.concatenate(attn, axis=0).astype(BF16), sga_ref[...], wao_ref)
    scan_chunk(chunks[3])

    last = MIX_TM - SUBLANES
    sr, si = aseg_s[0:1, :], aseg_s[1:2, :]
    cr, ci = carry_s[0:1, :], carry_s[1:2, :]
    for s in range(SUBLANES):
        cin_s[0, s:s + 1, :] = cr
        cin_s[1, s:s + 1, :] = ci
        pr, pi = _cmul(sr, si, cr, ci)
        cr = pr + bur_s[last + s:last + s + 1, :]
        ci = pi + bui_s[last + s:last + s + 1, :]
    carry_s[0:1, :] = cr
    carry_s[1:2, :] = ci
    pack = 2 * SUBLANES

    def fix_chunk(c0):
        cs = slice(c0, c0 + SCAN_LC)
        cinr = jnp.concatenate([cin_s[0, :, cs]] * 2, axis=0)
        cini = jnp.concatenate([cin_s[1, :, cs]] * 2, axis=0)
        col = (c0 // half_st) * SSM_S + c0 % half_st
        for r0 in range(0, MIX_TM, pack):
            rs = slice(r0, r0 + pack)
            pr, pi = apr_s[rs, cs], api_s[rs, cs]
            xb_s[rs, col:col + SCAN_LC] = (bur_s[rs, cs] + pr * cinr - pi * cini).astype(BF16)
            xb_s[rs, col + half_st:col + half_st + SCAN_LC] = (bui_s[rs, cs] + pr * cini + pi * cinr).astype(BF16)

    def project_half(hh):
        return jnp.dot(xb_s[:, hh * SSM_S:(hh + 1) * SSM_S], ccat_ref[hh], preferred_element_type=F32)

    per_half = len(chunks) // 2
    for c0 in chunks[:per_half]:
        fix_chunk(c0)
    yp0 = project_half(0)
    for c0 in chunks[per_half:]:
        fix_chunk(c0)
    yp = jnp.concatenate([yp0, project_half(1)], axis=1)
    y_hi = yp.astype(BF16)
    y_lo = (yp - y_hi.astype(F32)).astype(BF16)
    y_ssm = (jnp.dot(permt_ref[...], y_hi, preferred_element_type=F32)
             + jnp.dot(permt_ref[...], y_lo, preferred_element_type=F32) + d_ref[...] * u)

    x2_ref[...] = _gated_merge(x1_ref[...], attn_gated, y_ssm, sgs_ref[...],
                               wglu_ref, bglu_ref, wso_ref, wout_ref)

    @pl.when(i == nt - 1)
    def _():
        sre_ref[...] = carry_s[0:1, :]
        sim_ref[...] = carry_s[1:2, :]


def _mix_prompt(fr, p, batch, seq):
    x1, q, k, v, u, sga, sgs = fr
    tm = MIX_TM
    nt = seq // tm
    row = lambda w: pl.BlockSpec((tm, w), lambda b, i: (b * nt + i, 0))
    consts = [p['wao'], p['wglu'], p['bglu'], p['wso'], p['wout'],
              p['are'], p['aim'], p['ldt'], p['bbre'], p['bbim'], p['ccat'], p['dskip'], p['perm'], p['permt']]
    st_spec = pl.BlockSpec((None, 1, SSM_S), lambda b, i: (b, 0, 0))
    scratch = [pltpu.VMEM((2, MXU_DIM, SSM_S // 2), BF16), pltpu.VMEM((2, MXU_DIM, SSM_S // 2), BF16),
               pltpu.VMEM((tm, SSM_S), F32), pltpu.VMEM((tm, SSM_S), F32),
               pltpu.VMEM((SUBLANES, SSM_S), F32), pltpu.VMEM((SUBLANES, SSM_S), F32),
               pltpu.VMEM((SUBLANES, SSM_S), F32), pltpu.VMEM((2, SUBLANES, SSM_S), F32),
               pltpu.VMEM((tm, SSM_S), F32), pltpu.VMEM((tm, SSM_S), F32), pltpu.VMEM((tm, 2 * SSM_S), BF16),
               pltpu.VMEM((N_KV, WINDOW, LANES), BF16), pltpu.VMEM((N_KV, WINDOW, LANES), BF16)]
    return pl.pallas_call(
        _mix_prompt_kernel,
        grid=(batch, nt),
        in_specs=[row(D_MODEL), row(ATTN_W), row(KV_W), row(KV_W), row(SSM_W), row(D_MODEL), row(D_MODEL),
                  pl.BlockSpec(memory_space=pltpu.SMEM)] + [_const_spec(c.shape) for c in consts],
        out_specs=[row(D_MODEL), st_spec, st_spec],
        out_shape=[jax.ShapeDtypeStruct((batch * seq, D_MODEL), F32),
                   jax.ShapeDtypeStruct((batch, 1, SSM_S), F32),
                   jax.ShapeDtypeStruct((batch, 1, SSM_S), F32)],
        scratch_shapes=scratch,
        compiler_params=pltpu.CompilerParams(dimension_semantics=("arbitrary", "arbitrary"),
                                             vmem_limit_bytes=VMEM_LIMIT),
        name="mix_prompt",
    )(x1, q, k, v, u, sga, sgs, p['sinks'], *consts)


def _attn_sample_kernel(q_ref, kn_ref, vn_ref, ckt_ref, cvt_ref, sinks_ref, o_ref, nkt_ref, nvt_ref,
                        q8_s, knp_s, vnp_s):
    steps = q_ref.shape[0]
    shift = WINDOW - steps
    keep = lax.broadcasted_iota(jnp.int32, (HEAD_DIM, WINDOW), 1) < shift
    q8_s[...] = jnp.zeros_like(q8_s)
    knp_s[...] = jnp.zeros_like(knp_s)
    vnp_s[...] = jnp.zeros_like(vnp_s)
    qs, kreps, vreps = [], [], []
    for s in range(SAMPLE_BS):
        for t in range(steps):
            q8_s[s, t:t + 1, :] = q_ref[t, s:s + 1, :]
            knp_s[s, t:t + 1, :] = kn_ref[t, s:s + 1, :]
            vnp_s[s, t:t + 1, :] = vn_ref[t, s:s + 1, :]
        news = [knp_s[s].T, vnp_s[s].T]
        ends = [pltpu.roll(n, shift, axis=1) for n in news]
        krep, vrep = [], []
        for kv in range(N_KV):
            hs = slice(kv * HEAD_DIM, (kv + 1) * HEAD_DIM)
            for cache_ref, out_ref, new, end, reps in ((ckt_ref, nkt_ref, news[0], ends[0], krep),
                                                       (cvt_ref, nvt_ref, news[1], ends[1], vrep)):
                old = cache_ref[s, kv]
                out_ref[s, kv] = jnp.where(keep, pltpu.roll(old, shift, axis=1), end[hs])
                both = jnp.concatenate([old, new[hs]], axis=1).astype(BF16)
                reps.append(jnp.concatenate([both, both], axis=0))
        qs.append(q8_s[s])
        kreps.append(krep)
        vreps.append(vrep)
    probs = _attn_softmax(_attn_scores(qs, kreps, keys_on_lanes=True), sinks_ref, SUBLANES, [False] * SAMPLE_BS)
    outs = _attn_values(probs, vreps, SUBLANES, keys_on_lanes=True)
    for s in range(SAMPLE_BS):
        for t in range(steps):
            o_ref[t, s:s + 1, :] = outs[s][t:t + 1, :]


def _attn_sample(q, kn, vn, ckt, cvt, sinks, steps, nseq):
    bs = SAMPLE_BS
    blk3 = lambda w: pl.BlockSpec((steps, bs, w), lambda i: (0, i, 0))
    cache = pl.BlockSpec((bs, N_KV, HEAD_DIM, WINDOW), lambda i: (i, 0, 0, 0))
    cache_shape = jax.ShapeDtypeStruct((nseq, N_KV, HEAD_DIM, WINDOW), F32)
    return pl.pallas_call(
        _attn_sample_kernel,
        grid=(nseq // bs,),
        in_specs=[blk3(ATTN_W), blk3(KV_W), blk3(KV_W), cache, cache, pl.BlockSpec(memory_space=pltpu.SMEM)],
        out_specs=[blk3(ATTN_W), cache, cache],
        out_shape=[jax.ShapeDtypeStruct((steps, nseq, ATTN_W), F32), cache_shape, cache_shape],
        scratch_shapes=[pltpu.VMEM((bs, SUBLANES, ATTN_W), F32), pltpu.VMEM((bs, LANES, KV_W), F32),
                        pltpu.VMEM((bs, LANES, KV_W), F32)],
        compiler_params=pltpu.CompilerParams(dimension_semantics=("arbitrary",)),
        name="attn_sample",
    )(q.reshape(steps, nseq, ATTN_W), kn.reshape(steps, nseq, KV_W), vn.reshape(steps, nseq, KV_W),
      ckt, cvt, sinks)


def _mix_sample_kernel(x1_ref, attn_ref, u_ref, sga_ref, sgs_ref, x0r_ref, x0i_ref,
                       wao_ref, wglu_ref, bglu_ref, wso_ref, wout_ref,
                       are_ref, aim_ref, ldt_ref, bbre_ref, bbim_ref, ccat_ref, d_ref,
                       x2_ref, sre_ref, sim_ref,
                       wbre_s, wbim_s, bur_s, bui_s):
    nseq = x0r_ref.shape[0]
    steps = x1_ref.shape[0] // nseq
    ab_re, ab_im, f_re, f_im = _ssm_discretize(are_ref, aim_ref, ldt_ref)
    _build_wb(bbre_ref, bbim_ref, f_re, f_im, wbre_s, wbim_s)
    u = u_ref[...]
    _expand_state(u.astype(BF16), wbre_s, wbim_s, bur_s, bui_s)
    for c0 in range(0, SSM_S, LANES):
        cs = slice(c0, c0 + LANES)
        ar, ai = ab_re[:, cs], ab_im[:, cs]
        xr, xi = x0r_ref[:, cs], x0i_ref[:, cs]
        for t in range(steps):
            rs = slice(t * nseq, (t + 1) * nseq)
            nr = ar * xr - ai * xi + bur_s[rs, cs]
            ni = ar * xi + ai * xr + bui_s[rs, cs]
            bur_s[rs, cs] = nr
            bui_s[rs, cs] = ni
            xr, xi = nr, ni
        sre_ref[:, cs] = xr
        sim_ref[:, cs] = xi
    y_ssm = _project_state(bur_s, bui_s, ccat_ref) + d_ref[...] * u
    attn_gated = _gated_attn(attn_ref[...].astype(BF16), sga_ref[...], wao_ref)
    x2_ref[...] = _gated_merge(x1_ref[...], attn_gated, y_ssm, sgs_ref[...],
                               wglu_ref, bglu_ref, wso_ref, wout_ref)


def _mix_sample(x1, attn, u, sga, sgs, x0r, x0i, p):
    n, nseq = x1.shape[0], x0r.shape[0]
    args = [x1, attn, u, sga, sgs, x0r, x0i, p['wao'], p['wglu'], p['bglu'], p['wso'], p['wout'],
            p['are'], p['aim'], p['ldt'], p['bbre'], p['bbim'], p['ccat'], p['dskip']]
    return pl.pallas_call(
        _mix_sample_kernel,
        grid=(1,),
        in_specs=[_const_spec(a.shape) for a in args],
        out_specs=[pl.BlockSpec((n, D_MODEL), lambda i: (0, 0)), pl.BlockSpec((nseq, SSM_S), lambda i: (0, 0)),
                   pl.BlockSpec((nseq, SSM_S), lambda i: (0, 0))],
        out_shape=[jax.ShapeDtypeStruct((n, D_MODEL), F32),
                   jax.ShapeDtypeStruct((nseq, SSM_S), F32),
                   jax.ShapeDtypeStruct((nseq, SSM_S), F32)],
        scratch_shapes=[pltpu.VMEM((2, MXU_DIM, SSM_S // 2), BF16), pltpu.VMEM((2, MXU_DIM, SSM_S // 2), BF16),
                        pltpu.VMEM((n, SSM_S), F32), pltpu.VMEM((n, SSM_S), F32)],
        compiler_params=pltpu.CompilerParams(dimension_semantics=("arbitrary",),
                                             vmem_limit_bytes=VMEM_LIMIT),
        name="mix_sample",
    )(*args)


def _rope_tables(base_pos, off_pos):
    half = HEAD_DIM // 2
    lane = np.arange(LANES)
    inv = ROPE_THETA ** (-2.0 * (lane % half) / HEAD_DIM)
    first = (lane % HEAD_DIM) < half
    ab = np.asarray(base_pos, np.float64)[:, None] * inv[None, :]
    ao = np.asarray(off_pos, np.float64)[:, None] * inv[None, :]
    base = np.stack([np.cos(ab), np.sin(ab)], axis=1)
    co, so = np.cos(ao), np.sin(ao)
    lo = np.where(first, -1.0, 0.0)
    hi = np.where(first, 0.0, 1.0)
    off = np.stack([co, so, co * lo, so * lo, co * hi, so * hi], axis=0)
    return jnp.asarray(base, F32), jnp.asarray(off, F32)


def _block_diag_halves(m):
    g, r, c = m.shape
    hg = g // 2
    strip = jnp.transpose(m.reshape(2, hg, r, c), (0, 2, 1, 3)).reshape(2, 1, r, hg * c)
    full = jnp.broadcast_to(strip, (2, hg, r, hg * c)).reshape(2, hg * r, hg * c)
    mask = jnp.asarray(np.kron(np.eye(hg), np.ones((r, c))), m.dtype)
    return full * mask


def _layout_params(ffn1_norm, ffn1_w1, ffn1_w3, ffn1_w2, mix_norm, w_in, q_norm, k_norm, attn_sinks,
                   w_attn_out, ssm_a_re, ssm_a_im, ssm_log_dt, ssm_b_re, ssm_b_im, ssm_c_re, ssm_c_im,
                   ssm_d, w_glu, b_glu, w_ssm_out, w_out, ffn2_norm, ffn2_w1, ffn2_w3, ffn2_w2):
    l = 0
    row = lambda a: a.reshape(1, -1).astype(F32)
    head_ones = jnp.asarray(np.kron(np.eye(LANES // HEAD_DIM), np.ones((HEAD_DIM, HEAD_DIM))), F32)
    r = np.arange(MIX_TM)
    tok = (r % SUBLANES) * SCAN_SEG + r // SUBLANES
    perm = jnp.asarray(tok[:, None] == np.arange(MIX_TM)[None, :], BF16)
    c_cat = jnp.concatenate([_block_diag_halves(jnp.swapaxes(ssm_c_re[l], 1, 2)),
                             -_block_diag_halves(jnp.swapaxes(ssm_c_im[l], 1, 2))], axis=1)
    return dict(
        n1=row(ffn1_norm[l]), w1a=ffn1_w1[l].astype(BF16), w3a=ffn1_w3[l].astype(BF16), w2a=ffn1_w2[l].astype(BF16),
        nm=row(mix_norm[l]), win=w_in[l].astype(BF16),
        qn=row(jnp.tile(q_norm[l], LANES // HEAD_DIM)), kn=row(jnp.tile(k_norm[l], LANES // HEAD_DIM)),
        ones=head_ones.astype(BF16), sinks=attn_sinks[l].astype(F32),
        wao=w_attn_out[l].astype(BF16), wglu=w_glu[l].astype(BF16), bglu=row(b_glu[l]),
        wso=w_ssm_out[l].astype(BF16), wout=w_out[l].astype(BF16),
        are=row(ssm_a_re[l]), aim=row(ssm_a_im[l]), ldt=row(jnp.repeat(ssm_log_dt[l], SSM_N)),
        bbre=_block_diag_halves(jnp.swapaxes(ssm_b_re[l], 1, 2).astype(F32)),
        bbim=_block_diag_halves(jnp.swapaxes(ssm_b_im[l], 1, 2).astype(F32)),
        ccat=c_cat.astype(BF16), dskip=row(ssm_d[l]), perm=perm, permt=perm.T,
        n2=row(ffn2_norm[l]), w1b=ffn2_w1[l].astype(BF16), w3b=ffn2_w3[l].astype(BF16), w2b=ffn2_w2[l].astype(BF16),
    )


def kernel(x_prompt, x_sample, cache_k, cache_v, state_ssm_re, state_ssm_im, ffn1_norm, ffn1_w1, ffn1_w3, ffn1_w2, mix_norm, w_in, q_norm, k_norm, attn_sinks, w_attn_out, ssm_a_re, ssm_a_im, ssm_log_dt, ssm_b_re, ssm_b_im, ssm_c_re, ssm_c_im, ssm_d, w_glu, b_glu, w_ssm_out, w_out, ffn2_norm, ffn2_w1, ffn2_w3, ffn2_w2):
    assert ffn1_norm.shape[0] == 1, "single trunk layer"
    batch, seq, _ = x_prompt.shape
    nseq, steps, _ = x_sample.shape
    p = _layout_params(ffn1_norm, ffn1_w1, ffn1_w3, ffn1_w2, mix_norm, w_in, q_norm, k_norm, attn_sinks,
                       w_attn_out, ssm_a_re, ssm_a_im, ssm_log_dt, ssm_b_re, ssm_b_im, ssm_c_re, ssm_c_im,
                       ssm_d, w_glu, b_glu, w_ssm_out, w_out, ffn2_norm, ffn2_w1, ffn2_w3, ffn2_w2)

    tile0 = np.arange(batch * seq // FRONT_TM) * FRONT_TM
    tabs_p = _rope_tables(tile0 % seq, np.arange(FRONT_TM))
    fr = _front(x_prompt.reshape(batch * seq, D_MODEL), tabs_p, p, BF16)
    x2p, re_p, im_p = _mix_prompt(fr, p, batch, seq)
    y_prompt = _ffn(x2p, p).reshape(batch, seq, D_MODEL)
    tail_kv = lambda a: a.reshape(batch, seq, KV_W)[:, seq - WINDOW:].reshape(batch, WINDOW, N_KV, HEAD_DIM)
    k_p, v_p = tail_kv(fr[2]), tail_kv(fr[3])

    xs = jnp.swapaxes(x_sample, 0, 1).reshape(steps * nseq, D_MODEL)
    tile0 = np.arange(steps * nseq // FRONT_TM) * FRONT_TM
    tabs_s = _rope_tables(PAST_LEN + tile0 // nseq, np.arange(FRONT_TM) // nseq)
    x1s, qs, ks, vs, us, sgas, sgss = _front(xs, tabs_s, p, F32)
    to_lanes = lambda c: jnp.transpose(c[0], (0, 2, 3, 1))
    attn_s, nkt, nvt = _attn_sample(qs, ks, vs, to_lanes(cache_k), to_lanes(cache_v), p['sinks'], steps, nseq)
    x2s, re_s, im_s = _mix_sample(x1s, attn_s.reshape(steps * nseq, ATTN_W), us, sgas, sgss,
                                  state_ssm_re[0].reshape(nseq, SSM_S), state_ssm_im[0].reshape(nseq, SSM_S), p)
    y_sample = jnp.swapaxes(_ffn(x2s, p).reshape(steps, nseq, D_MODEL), 0, 1)
    k_s, v_s = jnp.transpose(nkt, (0, 3, 1, 2)), jnp.transpose(nvt, (0, 3, 1, 2))

    st = lambda a, n: a.reshape(1, n, SSM_G, SSM_N)
    return (y_prompt, y_sample, k_p[None], v_p[None], st(re_p, batch), st(im_p, batch),
            k_s[None], v_s[None], st(re_s, nseq), st(im_s, nseq))
```

```python
import jax
import jax.numpy as jnp
import numpy as np
from jax import lax
from jax.experimental import pallas as pl
from jax.experimental.pallas import tpu as pltpu

F32 = jnp.float32
BF16 = jnp.bfloat16

D_MODEL = 1024
HEAD_DIM = 64
N_HEADS = 8
N_KV = 2
WINDOW = 128
ATTN_W = N_HEADS * HEAD_DIM
KV_W = N_KV * HEAD_DIM
SSM_W = 512
SSM_G = 32
SSM_CH = 16
SSM_N = 64
SSM_S = SSM_G * SSM_N
D_FF = 2816
IN_COLS = ATTN_W + 2 * KV_W + SSM_W + 2 * D_MODEL
ROPE_THETA = 10000.0
RMS_EPS = 1e-6
MASK_VALUE = -1e30
PAST_LEN = 16384

LANES = 128
SUBLANES = 8
MXU_DIM = 256

FRONT_TM = 512
MIX_TM = 512
SCAN_SEG = MIX_TM // SUBLANES
FF_CHUNK = 256
SCAN_LC = 512
SAMPLE_BS = 16
VMEM_LIMIT = 58 * 1024 * 1024


def _const_spec(shape):
    nd = len(shape)
    return pl.BlockSpec(shape, lambda *_: (0,) * nd, pipeline_mode=pl.Buffered(1))


def _rms(x, w):
    ms = jnp.mean(x * x, axis=-1, keepdims=True)
    return x * lax.rsqrt(ms + RMS_EPS) * w


def _swiglu_residual(x, nw_ref, w1_ref, w3_ref, w2_ref):
    h = _rms(x, nw_ref[...]).astype(BF16)
    acc = None
    for c0 in range(0, D_FF, FF_CHUNK):
        cs = slice(c0, min(c0 + FF_CHUNK, D_FF))
        a = jnp.dot(h, w1_ref[:, cs], preferred_element_type=F32)
        b = jnp.dot(h, w3_ref[:, cs], preferred_element_type=F32)
        g = (a * jax.nn.sigmoid(a) * b).astype(BF16)
        t = jnp.dot(g, w2_ref[cs, :], preferred_element_type=F32)
        acc = t if acc is None else acc + t
    return x + 0.5 * acc


def _head_norm_rope(t, ones_ref, g_ref, cos, sinlo, sinhi, scale):
    ssum = jnp.dot((t * t).astype(BF16), ones_ref[...], preferred_element_type=F32)
    tn = t * lax.rsqrt(ssum * (1.0 / HEAD_DIM) + RMS_EPS) * g_ref[...]
    up = pltpu.roll(tn, HEAD_DIM // 2, axis=1)
    dn = pltpu.roll(tn, LANES - HEAD_DIM // 2, axis=1)
    r = tn * cos + dn * sinlo + up * sinhi
    return r * scale if scale != 1.0 else r


def _front_kernel(x_ref, base_ref, off_ref,
                  n1_ref, w1_ref, w3_ref, w2_ref, nm_ref, win_ref, qn_ref, kn_ref, ones_ref,
                  x1_ref, q_ref, k_ref, v_ref, u_ref, sga_ref, sgs_ref):
    x1 = _swiglu_residual(x_ref[...], n1_ref, w1_ref, w3_ref, w2_ref)
    x1_ref[...] = x1
    h = _rms(x1, nm_ref[...]).astype(BF16)
    cb, sb = base_ref[0:1, :], base_ref[1:2, :]
    cos = cb * off_ref[0] - sb * off_ref[1]
    sinlo = sb * off_ref[2] + cb * off_ref[3]
    sinhi = sb * off_ref[4] + cb * off_ref[5]
    o = ATTN_W + 2 * KV_W + SSM_W
    qkvu = jnp.dot(h, win_ref[:, 0:o], preferred_element_type=F32)
    for j in range(ATTN_W // LANES):
        q_ref[:, j * LANES:(j + 1) * LANES] = _head_norm_rope(
            qkvu[:, j * LANES:(j + 1) * LANES], ones_ref, qn_ref, cos, sinlo, sinhi,
            HEAD_DIM ** -0.5).astype(q_ref.dtype)
    k_ref[...] = _head_norm_rope(qkvu[:, ATTN_W:ATTN_W + KV_W], ones_ref, kn_ref, cos, sinlo, sinhi, 1.0)
    v_ref[...] = qkvu[:, ATTN_W + KV_W:ATTN_W + 2 * KV_W]
    u_ref[...] = qkvu[:, ATTN_W + 2 * KV_W:o]
    ga = jnp.dot(h, win_ref[:, o:o + D_MODEL], preferred_element_type=F32)
    sga_ref[...] = jax.nn.sigmoid(ga).astype(BF16)
    o += D_MODEL
    gs = jnp.dot(h, win_ref[:, o:o + D_MODEL], preferred_element_type=F32)
    sgs_ref[...] = jax.nn.sigmoid(gs).astype(BF16)


def _front(x, tabs, p, q_dtype):
    n = x.shape[0]
    tm = FRONT_TM
    base_tab, off_tab = tabs
    row = lambda w: pl.BlockSpec((tm, w), lambda i: (i, 0))
    consts = [p['n1'], p['w1a'], p['w3a'], p['w2a'], p['nm'], p['win'], p['qn'], p['kn'], p['ones']]
    outs = [(D_MODEL, F32), (ATTN_W, q_dtype), (KV_W, F32), (KV_W, F32), (SSM_W, F32),
            (D_MODEL, BF16), (D_MODEL, BF16)]
    return pl.pallas_call(
        _front_kernel,
        grid=(n // tm,),
        in_specs=[row(D_MODEL), pl.BlockSpec((None, 2, LANES), lambda i: (i, 0, 0)), _const_spec(off_tab.shape)]
        + [_const_spec(c.shape) for c in consts],
        out_specs=[row(w) for w, _ in outs],
        out_shape=[jax.ShapeDtypeStruct((n, w), dt) for w, dt in outs],
        compiler_params=pltpu.CompilerParams(dimension_semantics=("arbitrary",),
                                             vmem_limit_bytes=VMEM_LIMIT),
        name="front",
    )(x, base_tab, off_tab, *consts)


def _ffn_kernel(x_ref, n_ref, w1_ref, w3_ref, w2_ref, y_ref):
    y_ref[...] = _swiglu_residual(x_ref[...], n_ref, w1_ref, w3_ref, w2_ref)


def _ffn(x, p):
    n = x.shape[0]
    tm = FRONT_TM
    weights = [p['w1b'], p['w3b'], p['w2b']]
    return pl.pallas_call(
        _ffn_kernel,
        grid=(n // tm,),
        in_specs=[pl.BlockSpec((tm, D_MODEL), lambda i: (i, 0)), _const_spec(p['n2'].shape)]
        + [_const_spec(w.shape) for w in weights],
        out_specs=pl.BlockSpec((tm, D_MODEL), lambda i: (i, 0)),
        out_shape=jax.ShapeDtypeStruct((n, D_MODEL), F32),
        compiler_params=pltpu.CompilerParams(dimension_semantics=("arbitrary",),
                                             vmem_limit_bytes=VMEM_LIMIT),
        name="ffn",
    )(x, p['n2'], *weights)


def _cmul(ar, ai, br, bi):
    return ar * br - ai * bi, ar * bi + ai * br


def _ssm_discretize(are_ref, aim_ref, ldt_ref):
    a_re, a_im = are_ref[...], aim_ref[...]
    dt = jnp.exp(ldt_ref[...])
    mag = jnp.exp(dt * a_re)
    ab_re = mag * jnp.cos(dt * a_im)
    ab_im = mag * jnp.sin(dt * a_im)
    den = a_re * a_re + a_im * a_im
    nr, ni = ab_re - 1.0, ab_im
    f_re = (nr * a_re + ni * a_im) / den
    f_im = (ni * a_re - nr * a_im) / den
    return ab_re, ab_im, f_re, f_im


def _build_wb(bbre_ref, bbim_ref, f_re, f_im, wbre_s, wbim_s):
    half = SSM_S // 2
    for hh in range(2):
        fr, fi = f_re[:, hh * half:(hh + 1) * half], f_im[:, hh * half:(hh + 1) * half]
        br, bi = bbre_ref[hh], bbim_ref[hh]
        wbre_s[hh] = (br * fr - bi * fi).astype(BF16)
        wbim_s[hh] = (br * fi + bi * fr).astype(BF16)


def _expand_state(ub, wbre_s, wbim_s, bur_s, bui_s):
    half_in, half_st = SSM_W // 2, SSM_S // 2
    for hh in range(2):
        uh = ub[:, hh * half_in:(hh + 1) * half_in]
        bur_s[:, hh * half_st:(hh + 1) * half_st] = jnp.dot(uh, wbre_s[hh], preferred_element_type=F32)
        bui_s[:, hh * half_st:(hh + 1) * half_st] = jnp.dot(uh, wbim_s[hh], preferred_element_type=F32)


def _project_state(xr_s, xi_s, ccat_ref):
    half_st = SSM_S // 2
    outs = []
    for hh in range(2):
        sl = slice(hh * half_st, (hh + 1) * half_st)
        xcat = jnp.concatenate([xr_s[:, sl].astype(BF16), xi_s[:, sl].astype(BF16)], axis=1)
        outs.append(jnp.dot(xcat, ccat_ref[hh], preferred_element_type=F32))
    return jnp.concatenate(outs, axis=1)


def _gated_attn(attn_b, sga, wao_ref):
    return sga.astype(F32) * jnp.dot(attn_b, wao_ref[...], preferred_element_type=F32)


def _gated_merge(x1, attn_gated, y_ssm, sgs, wglu_ref, bglu_ref, wso_ref, wout_ref):
    z = jax.nn.gelu(y_ssm, approximate=True)
    gl = jnp.dot(z.astype(BF16), wglu_ref[...], preferred_element_type=F32) + bglu_ref[...]
    ssm = (z * jax.nn.sigmoid(gl)).astype(BF16)
    merged = attn_gated + sgs.astype(F32) * jnp.dot(ssm, wso_ref[...], preferred_element_type=F32)
    return x1 + jnp.dot(merged.astype(BF16), wout_ref[...], preferred_element_type=F32)


def _replicate_heads(kv):
    sw = pltpu.roll(kv, HEAD_DIM, axis=1)
    lo = lax.broadcasted_iota(jnp.int32, kv.shape, 1) < HEAD_DIM
    return jnp.where(lo, kv, sw).astype(BF16), jnp.where(lo, sw, kv).astype(BF16)


def _attn_scores(qs, kreps, keys_on_lanes=False):
    qb = qs[0].shape[0]
    lo = lax.broadcasted_iota(jnp.int32, (qb, LANES), 1) < HEAD_DIM
    dims = (((1,), (0,)), ((), ())) if keys_on_lanes else (((1,), (1,)), ((), ()))
    scores = []
    for q, krep in zip(qs, kreps):
        if q.dtype != BF16 or qb % (2 * SUBLANES):
            q = q.astype(F32)
        zero = jnp.zeros((), q.dtype)
        for kv in range(N_KV):
            parts = []
            for jj in range(2):
                t = q[:, (2 * kv + jj) * LANES:(2 * kv + jj + 1) * LANES]
                parts += [jnp.where(lo, t, zero), jnp.where(lo, zero, t)]
            stacked = jnp.concatenate(parts, axis=0).astype(BF16)
            scores.append(lax.dot_general(stacked, krep[kv], dims, preferred_element_type=F32))
    return jnp.concatenate(scores, axis=0)


def _window_valid(rows, qb, first):
    qi = lax.broadcasted_iota(jnp.int32, (rows, 2 * WINDOW), 0) & (qb - 1)
    sj = lax.broadcasted_iota(jnp.int32, (rows, 2 * WINDOW), 1)
    band = (sj <= WINDOW + qi) & (sj > qi)
    return band if first is False else band & ((sj >= WINDOW) | jnp.logical_not(first))


def _attn_softmax(s, sinks_ref, qb, firsts, bias=None):
    per_block = s.shape[0] // len(firsts)
    sink_parts = [jnp.full((qb, 1), sinks_ref[h], F32) for h in range(N_HEADS)]
    sink = jnp.concatenate(sink_parts * len(firsts), axis=0)
    if bias is None:
        valid = jnp.concatenate([_window_valid(per_block, qb, f) for f in firsts], axis=0)
        s = jnp.where(valid, s, MASK_VALUE)
    else:
        s = s + bias
    m = jnp.maximum(jnp.max(s, axis=-1, keepdims=True), sink)
    e = jnp.exp(s - m)
    denom = jnp.sum(e, axis=-1, keepdims=True) + jnp.exp(sink - m)
    return (e * (1.0 / denom)).astype(BF16)


def _attn_values(probs, vreps, qb, keys_on_lanes=False):
    lo = lax.broadcasted_iota(jnp.int32, (qb, LANES), 1) < HEAD_DIM
    rows = 2 * (N_HEADS // N_KV // 2) * qb
    dims = (((1,), (1,)), ((), ())) if keys_on_lanes else (((1,), (0,)), ((), ()))
    outs = []
    for b, vrep in enumerate(vreps):
        tiles = []
        for kv in range(N_KV):
            r0 = (b * N_KV + kv) * rows
            o = lax.dot_general(probs[r0:r0 + rows], vrep[kv], dims, preferred_element_type=F32)
            for jj in range(2):
                oa = o[(2 * jj) * qb:(2 * jj + 1) * qb]
                ob = o[(2 * jj + 1) * qb:(2 * jj + 2) * qb]
                tiles.append(jnp.where(lo, oa, ob))
        outs.append(jnp.concatenate(tiles, axis=1))
    return outs


def _mix_prompt_kernel(x1_ref, q_ref, k_ref, v_ref, u_ref, sga_ref, sgs_ref, sinks_ref,
                       wao_ref, wglu_ref, bglu_ref, wso_ref, wout_ref,
                       are_ref, aim_ref, ldt_ref, bbre_ref, bbim_ref, ccat_ref, d_ref, perm_ref, permt_ref,
                       x2_ref, sre_ref, sim_ref,
                       wbre_s, wbim_s, apr_s, api_s, a1_s, aseg_s, carry_s, cin_s,
                       bur_s, bui_s, xb_s, kprev_s, vprev_s, bias_s):
    b, i = pl.program_id(0), pl.program_id(1)
    nt = pl.num_programs(1)

    @pl.when((b == 0) & (i == 0))
    def _():
        ab_re, ab_im, f_re, f_im = _ssm_discretize(are_ref, aim_ref, ldt_ref)
        _build_wb(bbre_ref, bbim_ref, f_re, f_im, wbre_s, wbim_s)
        a1_s[0:1, :] = ab_re
        a1_s[1:2, :] = ab_im
        pr, pi = ab_re, ab_im
        held = None
        for j in range(SCAN_SEG):
            rows = [jnp.broadcast_to(pr, (SUBLANES, SSM_S)), jnp.broadcast_to(pi, (SUBLANES, SSM_S))]
            if j % 2 == 0:
                held = rows
            else:
                rs = slice((j - 1) * SUBLANES, (j + 1) * SUBLANES)
                apr_s[rs, :] = jnp.concatenate([held[0], rows[0]], axis=0).astype(BF16)
                api_s[rs, :] = jnp.concatenate([held[1], rows[1]], axis=0).astype(BF16)
            if j == SCAN_SEG - 1:
                aseg_s[0:1, :] = pr
                aseg_s[1:2, :] = pi
            pr, pi = _cmul(pr, pi, ab_re, ab_im)
        for t, first in enumerate((False, True)):
            bias_s[t] = jnp.where(_window_valid(bias_s.shape[1], WINDOW, first), 0.0, MASK_VALUE)

    @pl.when(i == 0)
    def _():
        carry_s[...] = jnp.zeros_like(carry_s)
        kprev_s[...] = jnp.zeros_like(kprev_s)
        vprev_s[...] = jnp.zeros_like(vprev_s)

    k0, k1 = _replicate_heads(k_ref[...])
    v0, v1 = _replicate_heads(v_ref[...])
    qs, kreps, vreps, firsts = [], [], [], []
    for blk in range(MIX_TM // WINDOW):
        cur = slice(blk * WINDOW, (blk + 1) * WINDOW)
        if blk == 0:
            prev = [kprev_s[0], kprev_s[1], vprev_s[0], vprev_s[1]]
            firsts.append(i == 0)
        else:
            ps = slice((blk - 1) * WINDOW, blk * WINDOW)
            prev = [k0[ps], k1[ps], v0[ps], v1[ps]]
            firsts.append(False)
        qs.append(q_ref[cur, :])
        kreps.append([jnp.concatenate([prev[0], k0[cur]], axis=0), jnp.concatenate([prev[1], k1[cur]], axis=0)])
        vreps.append([jnp.concatenate([prev[2], v0[cur]], axis=0), jnp.concatenate([prev[3], v1[cur]], axis=0)])
    tail = slice(MIX_TM - WINDOW, MIX_TM)
    kprev_s[0], kprev_s[1] = k0[tail], k1[tail]
    vprev_s[0], vprev_s[1] = v0[tail], v1[tail]
    scores = _attn_scores(qs, kreps)

    half_in, half_st = SSM_W // 2, SSM_S // 2
    chunks = list(range(0, SSM_S, SCAN_LC))
    nblk = len(qs)
    blk_rows = scores.shape[0] // nblk
    u = u_ref[...]
    up = jnp.dot(perm_ref[...], u.astype(BF16), preferred_element_type=F32).astype(BF16)

    def expand_half(hh):
        uh = up[:, hh * half_in:(hh + 1) * half_in]
        bur_s[:, hh * half_st:(hh + 1) * half_st] = jnp.dot(uh, wbre_s[hh], preferred_element_type=F32)
        bui_s[:, hh * half_st:(hh + 1) * half_st] = jnp.dot(uh, wbim_s[hh], preferred_element_type=F32)

    def softmax_blocks(lo, hi):
        no_prev = (i == 0).astype(jnp.int32)
        bias = jnp.concatenate([bias_s[no_prev] if blk == 0 else bias_s[0] for blk in range(lo, hi)], axis=0)
        return _attn_softmax(scores[lo * blk_rows:hi * blk_rows], sinks_ref, WINDOW, firsts[lo:hi], bias)

    def scan_chunk(c0):
        cs = slice(c0, c0 + SCAN_LC)
        ar = jnp.broadcast_to(a1_s[0:1, cs], (SUBLANES, SCAN_LC))
        ai = jnp.broadcast_to(a1_s[1:2, cs], (SUBLANES, SCAN_LC))
        xr, xi = bur_s[0:SUBLANES, cs], bui_s[0:SUBLANES, cs]
        for j in range(1, SCAN_SEG):
            rs = slice(j * SUBLANES, (j + 1) * SUBLANES)
            xr, xi = (ar * xr - ai * xi + bur_s[rs, cs], ar * xi + ai * xr + bui_s[rs, cs])
            bur_s[rs, cs] = xr
            bui_s[rs, cs] = xi

    mid = nblk // 2
    expand_half(0)
    probs_a = softmax_blocks(0, mid)
    expand_half(1)
    probs_b = softmax_blocks(mid, nblk)
    quarter = len(chunks) // 4
    scan_group = lambda g: [scan_chunk(c0) for c0 in chunks[g * quarter:(g + 1) * quarter]]
    scan_group(0)
    attn = _attn_values(probs_a, vreps[0:mid], WINDOW)
    scan_group(1)
    attn += _attn_values(probs_b, vreps[mid:nblk], WINDOW)
    scan_group(2)
    attn_gated = _gated_attn(jnp.concatenate(attn, axis=0).astype(BF16), sga_ref[...], wao_ref)
    scan_group(3)

    last = MIX_TM - SUBLANES
    sr, si = aseg_s[0:1, :], aseg_s[1:2, :]
    cr, ci = carry_s[0:1, :], carry_s[1:2, :]
    for s in range(SUBLANES):
        cin_s[0, s:s + 1, :] = cr
        cin_s[1, s:s + 1, :] = ci
        pr, pi = _cmul(sr, si, cr, ci)
        cr = pr + bur_s[last + s:last + s + 1, :]
        ci = pi + bui_s[last + s:last + s + 1, :]
    carry_s[0:1, :] = cr
    carry_s[1:2, :] = ci
    pack = 2 * SUBLANES

    def fix_chunk(c0):
        cs = slice(c0, c0 + SCAN_LC)
        cinr = jnp.concatenate([cin_s[0, :, cs]] * 2, axis=0).astype(BF16)
        cini = jnp.concatenate([cin_s[1, :, cs]] * 2, axis=0).astype(BF16)
        col = (c0 // half_st) * SSM_S + c0 % half_st
        for r0 in range(0, MIX_TM, pack):
            rs = slice(r0, r0 + pack)
            pr, pi = apr_s[rs, cs], api_s[rs, cs]
            xb_s[rs, col:col + SCAN_LC] = bur_s[rs, cs].astype(BF16) + pr * cinr - pi * cini
            xb_s[rs, col + half_st:col + half_st + SCAN_LC] = bui_s[rs, cs].astype(BF16) + pr * cini + pi * cinr

    def project_half(hh):
        return jnp.dot(xb_s[:, hh * SSM_S:(hh + 1) * SSM_S], ccat_ref[hh], preferred_element_type=F32)

    per_half = len(chunks) // 2
    for c0 in chunks[:per_half]:
        fix_chunk(c0)
    yp0 = project_half(0)
    for c0 in chunks[per_half:]:
        fix_chunk(c0)
    yp = jnp.concatenate([yp0, project_half(1)], axis=1)
    y_ssm = jnp.dot(permt_ref[...], yp.astype(BF16), preferred_element_type=F32) + d_ref[...] * u

    x2_ref[...] = _gated_merge(x1_ref[...], attn_gated, y_ssm, sgs_ref[...],
                               wglu_ref, bglu_ref, wso_ref, wout_ref)

    @pl.when(i == nt - 1)
    def _():
        sre_ref[...] = carry_s[0:1, :]
        sim_ref[...] = carry_s[1:2, :]


def _mix_prompt(fr, p, batch, seq):
    x1, q, k, v, u, sga, sgs = fr
    tm = MIX_TM
    nt = seq // tm
    row = lambda w: pl.BlockSpec((tm, w), lambda b, i: (b * nt + i, 0))
    consts = [p['wao'], p['wglu'], p['bglu'], p['wso'], p['wout'],
              p['are'], p['aim'], p['ldt'], p['bbre'], p['bbim'], p['ccat'], p['dskip'], p['perm'], p['permt']]
    st_spec = pl.BlockSpec((None, 1, SSM_S), lambda b, i: (b, 0, 0))
    scratch = [pltpu.VMEM((2, MXU_DIM, SSM_S // 2), BF16), pltpu.VMEM((2, MXU_DIM, SSM_S // 2), BF16),
               pltpu.VMEM((tm, SSM_S), BF16), pltpu.VMEM((tm, SSM_S), BF16),
               pltpu.VMEM((SUBLANES, SSM_S), F32), pltpu.VMEM((SUBLANES, SSM_S), F32),
               pltpu.VMEM((SUBLANES, SSM_S), F32), pltpu.VMEM((2, SUBLANES, SSM_S), F32),
               pltpu.VMEM((tm, SSM_S), F32), pltpu.VMEM((tm, SSM_S), F32), pltpu.VMEM((tm, 2 * SSM_S), BF16),
               pltpu.VMEM((N_KV, WINDOW, LANES), BF16), pltpu.VMEM((N_KV, WINDOW, LANES), BF16),
               pltpu.VMEM((2, N_HEADS * WINDOW, 2 * WINDOW), F32)]
    return pl.pallas_call(
        _mix_prompt_kernel,
        grid=(batch, nt),
        in_specs=[row(D_MODEL), row(ATTN_W), row(KV_W), row(KV_W), row(SSM_W), row(D_MODEL), row(D_MODEL),
                  pl.BlockSpec(memory_space=pltpu.SMEM)] + [_const_spec(c.shape) for c in consts],
        out_specs=[row(D_MODEL), st_spec, st_spec],
        out_shape=[jax.ShapeDtypeStruct((batch * seq, D_MODEL), F32),
                   jax.ShapeDtypeStruct((batch, 1, SSM_S), F32),
                   jax.ShapeDtypeStruct((batch, 1, SSM_S), F32)],
        scratch_shapes=scratch,
        compiler_params=pltpu.CompilerParams(dimension_semantics=("arbitrary", "arbitrary"),
                                             vmem_limit_bytes=VMEM_LIMIT),
        name="mix_prompt",
    )(x1, q, k, v, u, sga, sgs, p['sinks'], *consts)


def _attn_sample_kernel(q_ref, kn_ref, vn_ref, ckt_ref, cvt_ref, sinks_ref, o_ref, nkt_ref, nvt_ref,
                        q8_s, knp_s, vnp_s):
    steps = q_ref.shape[0]
    shift = WINDOW - steps
    keep = lax.broadcasted_iota(jnp.int32, (HEAD_DIM, WINDOW), 1) < shift
    q8_s[...] = jnp.zeros_like(q8_s)
    knp_s[...] = jnp.zeros_like(knp_s)
    vnp_s[...] = jnp.zeros_like(vnp_s)
    qs, kreps, vreps = [], [], []
    for s in range(SAMPLE_BS):
        for t in range(steps):
            q8_s[s, t:t + 1, :] = q_ref[t, s:s + 1, :]
            knp_s[s, t:t + 1, :] = kn_ref[t, s:s + 1, :]
            vnp_s[s, t:t + 1, :] = vn_ref[t, s:s + 1, :]
        news = [knp_s[s].T, vnp_s[s].T]
        ends = [pltpu.roll(n, shift, axis=1) for n in news]
        krep, vrep = [], []
        for kv in range(N_KV):
            hs = slice(kv * HEAD_DIM, (kv + 1) * HEAD_DIM)
            for cache_ref, out_ref, new, end, reps in ((ckt_ref, nkt_ref, news[0], ends[0], krep),
                                                       (cvt_ref, nvt_ref, news[1], ends[1], vrep)):
                old = cache_ref[s, kv]
                out_ref[s, kv] = jnp.where(keep, pltpu.roll(old, shift, axis=1), end[hs])
                both = jnp.concatenate([old, new[hs]], axis=1).astype(BF16)
                reps.append(jnp.concatenate([both, both], axis=0))
        qs.append(q8_s[s])
        kreps.append(krep)
        vreps.append(vrep)
    probs = _attn_softmax(_attn_scores(qs, kreps, keys_on_lanes=True), sinks_ref, SUBLANES, [False] * SAMPLE_BS)
    outs = _attn_values(probs, vreps, SUBLANES, keys_on_lanes=True)
    for s in range(SAMPLE_BS):
        for t in range(steps):
            o_ref[t, s:s + 1, :] = outs[s][t:t + 1, :]


def _attn_sample(q, kn, vn, ckt, cvt, sinks, steps, nseq):
    bs = SAMPLE_BS
    blk3 = lambda w: pl.BlockSpec((steps, bs, w), lambda i: (0, i, 0))
    cache = pl.BlockSpec((bs, N_KV, HEAD_DIM, WINDOW), lambda i: (i, 0, 0, 0))
    cache_shape = jax.ShapeDtypeStruct((nseq, N_KV, HEAD_DIM, WINDOW), F32)
    return pl.pallas_call(
        _attn_sample_kernel,
        grid=(nseq // bs,),
        in_specs=[blk3(ATTN_W), blk3(KV_W), blk3(KV_W), cache, cache, pl.BlockSpec(memory_space=pltpu.SMEM)],
        out_specs=[blk3(ATTN_W), cache, cache],
        out_shape=[jax.ShapeDtypeStruct((steps, nseq, ATTN_W), F32), cache_shape, cache_shape],
        scratch_shapes=[pltpu.VMEM((bs, SUBLANES, ATTN_W), F32), pltpu.VMEM((bs, LANES, KV_W), F32),
                        pltpu.VMEM((bs, LANES, KV_W), F32)],
        compiler_params=pltpu.CompilerParams(dimension_semantics=("arbitrary",)),
        name="attn_sample",
    )(q.reshape(steps, nseq, ATTN_W), kn.reshape(steps, nseq, KV_W), vn.reshape(steps, nseq, KV_W),
      ckt, cvt, sinks)


def _mix_sample_kernel(x1_ref, attn_ref, u_ref, sga_ref, sgs_ref, x0r_ref, x0i_ref,
                       wao_ref, wglu_ref, bglu_ref, wso_ref, wout_ref,
                       are_ref, aim_ref, ldt_ref, bbre_ref, bbim_ref, ccat_ref, d_ref,
                       x2_ref, sre_ref, sim_ref,
                       wbre_s, wbim_s, bur_s, bui_s):
    nseq = x0r_ref.shape[0]
    steps = x1_ref.shape[0] // nseq
    ab_re, ab_im, f_re, f_im = _ssm_discretize(are_ref, aim_ref, ldt_ref)
    _build_wb(bbre_ref, bbim_ref, f_re, f_im, wbre_s, wbim_s)
    u = u_ref[...]
    _expand_state(u.astype(BF16), wbre_s, wbim_s, bur_s, bui_s)
    for c0 in range(0, SSM_S, LANES):
        cs = slice(c0, c0 + LANES)
        ar, ai = ab_re[:, cs], ab_im[:, cs]
        xr, xi = x0r_ref[:, cs], x0i_ref[:, cs]
        for t in range(steps):
            rs = slice(t * nseq, (t + 1) * nseq)
            nr = ar * xr - ai * xi + bur_s[rs, cs]
            ni = ar * xi + ai * xr + bui_s[rs, cs]
            bur_s[rs, cs] = nr
            bui_s[rs, cs] = ni
            xr, xi = nr, ni
        sre_ref[:, cs] = xr
        sim_ref[:, cs] = xi
    y_ssm = _project_state(bur_s, bui_s, ccat_ref) + d_ref[...] * u
    attn_gated = _gated_attn(attn_ref[...].astype(BF16), sga_ref[...], wao_ref)
    x2_ref[...] = _gated_merge(x1_ref[...], attn_gated, y_ssm, sgs_ref[...],
                               wglu_ref, bglu_ref, wso_ref, wout_ref)


def _mix_sample(x1, attn, u, sga, sgs, x0r, x0i, p):
    n, nseq = x1.shape[0], x0r.shape[0]
    args = [x1, attn, u, sga, sgs, x0r, x0i, p['wao'], p['wglu'], p['bglu'], p['wso'], p['wout'],
            p['are'], p['aim'], p['ldt'], p['bbre'], p['bbim'], p['ccat'], p['dskip']]
    return pl.pallas_call(
        _mix_sample_kernel,
        grid=(1,),
        in_specs=[_const_spec(a.shape) for a in args],
        out_specs=[pl.BlockSpec((n, D_MODEL), lambda i: (0, 0)), pl.BlockSpec((nseq, SSM_S), lambda i: (0, 0)),
                   pl.BlockSpec((nseq, SSM_S), lambda i: (0, 0))],
        out_shape=[jax.ShapeDtypeStruct((n, D_MODEL), F32),
                   jax.ShapeDtypeStruct((nseq, SSM_S), F32),
                   jax.ShapeDtypeStruct((nseq, SSM_S), F32)],
        scratch_shapes=[pltpu.VMEM((2, MXU_DIM, SSM_S // 2), BF16), pltpu.VMEM((2, MXU_DIM, SSM_S // 2), BF16),
                        pltpu.VMEM((n, SSM_S), F32), pltpu.VMEM((n, SSM_S), F32)],
        compiler_params=pltpu.CompilerParams(dimension_semantics=("arbitrary",),
                                             vmem_limit_bytes=VMEM_LIMIT),
        name="mix_sample",
    )(*args)


def _rope_tables(base_pos, off_pos):
    half = HEAD_DIM // 2
    lane = np.arange(LANES)
    inv = ROPE_THETA ** (-2.0 * (lane % half) / HEAD_DIM)
    first = (lane % HEAD_DIM) < half
    ab = np.asarray(base_pos, np.float64)[:, None] * inv[None, :]
    ao = np.asarray(off_pos, np.float64)[:, None] * inv[None, :]
    base = np.stack([np.cos(ab), np.sin(ab)], axis=1)
    co, so = np.cos(ao), np.sin(ao)
    lo = np.where(first, -1.0, 0.0)
    hi = np.where(first, 0.0, 1.0)
    off = np.stack([co, so, co * lo, so * lo, co * hi, so * hi], axis=0)
    return jnp.asarray(base, F32), jnp.asarray(off, F32)


def _block_diag_halves(m):
    g, r, c = m.shape
    hg = g // 2
    strip = jnp.transpose(m.reshape(2, hg, r, c), (0, 2, 1, 3)).reshape(2, 1, r, hg * c)
    full = jnp.broadcast_to(strip, (2, hg, r, hg * c)).reshape(2, hg * r, hg * c)
    mask = jnp.asarray(np.kron(np.eye(hg), np.ones((r, c))), m.dtype)
    return full * mask


def _layout_params(ffn1_norm, ffn1_w1, ffn1_w3, ffn1_w2, mix_norm, w_in, q_norm, k_norm, attn_sinks,
                   w_attn_out, ssm_a_re, ssm_a_im, ssm_log_dt, ssm_b_re, ssm_b_im, ssm_c_re, ssm_c_im,
                   ssm_d, w_glu, b_glu, w_ssm_out, w_out, ffn2_norm, ffn2_w1, ffn2_w3, ffn2_w2):
    l = 0
    row = lambda a: a.reshape(1, -1).astype(F32)
    head_ones = jnp.asarray(np.kron(np.eye(LANES // HEAD_DIM), np.ones((HEAD_DIM, HEAD_DIM))), F32)
    r = np.arange(MIX_TM)
    tok = (r % SUBLANES) * SCAN_SEG + r // SUBLANES
    perm = jnp.asarray(tok[:, None] == np.arange(MIX_TM)[None, :], BF16)
    c_cat = jnp.concatenate([_block_diag_halves(jnp.swapaxes(ssm_c_re[l], 1, 2)),
                             -_block_diag_halves(jnp.swapaxes(ssm_c_im[l], 1, 2))], axis=1)
    return dict(
        n1=row(ffn1_norm[l]), w1a=ffn1_w1[l].astype(BF16), w3a=ffn1_w3[l].astype(BF16), w2a=ffn1_w2[l].astype(BF16),
        nm=row(mix_norm[l]), win=w_in[l].astype(BF16),
        qn=row(jnp.tile(q_norm[l], LANES // HEAD_DIM)), kn=row(jnp.tile(k_norm[l], LANES // HEAD_DIM)),
        ones=head_ones.astype(BF16), sinks=attn_sinks[l].astype(F32),
        wao=w_attn_out[l].astype(BF16), wglu=w_glu[l].astype(BF16), bglu=row(b_glu[l]),
        wso=w_ssm_out[l].astype(BF16), wout=w_out[l].astype(BF16),
        are=row(ssm_a_re[l]), aim=row(ssm_a_im[l]), ldt=row(jnp.repeat(ssm_log_dt[l], SSM_N)),
        bbre=_block_diag_halves(jnp.swapaxes(ssm_b_re[l], 1, 2).astype(F32)),
        bbim=_block_diag_halves(jnp.swapaxes(ssm_b_im[l], 1, 2).astype(F32)),
        ccat=c_cat.astype(BF16), dskip=row(ssm_d[l]), perm=perm, permt=perm.T,
        n2=row(ffn2_norm[l]), w1b=ffn2_w1[l].astype(BF16), w3b=ffn2_w3[l].astype(BF16), w2b=ffn2_w2[l].astype(BF16),
    )


def kernel(x_prompt, x_sample, cache_k, cache_v, state_ssm_re, state_ssm_im, ffn1_norm, ffn1_w1, ffn1_w3, ffn1_w2, mix_norm, w_in, q_norm, k_norm, attn_sinks, w_attn_out, ssm_a_re, ssm_a_im, ssm_log_dt, ssm_b_re, ssm_b_im, ssm_c_re, ssm_c_im, ssm_d, w_glu, b_glu, w_ssm_out, w_out, ffn2_norm, ffn2_w1, ffn2_w3, ffn2_w2):
    assert ffn1_norm.shape[0] == 1, "single trunk layer"
    batch, seq, _ = x_prompt.shape
    nseq, steps, _ = x_sample.shape
    p = _layout_params(ffn1_norm, ffn1_w1, ffn1_w3, ffn1_w2, mix_norm, w_in, q_norm, k_norm, attn_sinks,
                       w_attn_out, ssm_a_re, ssm_a_im, ssm_log_dt, ssm_b_re, ssm_b_im, ssm_c_re, ssm_c_im,
                       ssm_d, w_glu, b_glu, w_ssm_out, w_out, ffn2_norm, ffn2_w1, ffn2_w3, ffn2_w2)

    tile0 = np.arange(batch * seq // FRONT_TM) * FRONT_TM
    tabs_p = _rope_tables(tile0 % seq, np.arange(FRONT_TM))
    fr = _front(x_prompt.reshape(batch * seq, D_MODEL), tabs_p, p, BF16)
    x2p, re_p, im_p = _mix_prompt(fr, p, batch, seq)
    y_prompt = _ffn(x2p, p).reshape(batch, seq, D_MODEL)
    tail_kv = lambda a: a.reshape(batch, seq, KV_W)[:, seq - WINDOW:].reshape(batch, WINDOW, N_KV, HEAD_DIM)
    k_p, v_p = tail_kv(fr[2]), tail_kv(fr[3])

    xs = jnp.swapaxes(x_sample, 0, 1).reshape(steps * nseq, D_MODEL)
    tile0 = np.arange(steps * nseq // FRONT_TM) * FRONT_TM
    tabs_s = _rope_tables(PAST_LEN + tile0 // nseq, np.arange(FRONT_TM) // nseq)
    x1s, qs, ks, vs, us, sgas, sgss = _front(xs, tabs_s, p, F32)
    to_lanes = lambda c: jnp.transpose(c[0], (0, 2, 3, 1))
    attn_s, nkt, nvt = _attn_sample(qs, ks, vs, to_lanes(cache_k), to_lanes(cache_v), p['sinks'], steps, nseq)
    x2s, re_s, im_s = _mix_sample(x1s, attn_s.reshape(steps * nseq, ATTN_W), us, sgas, sgss,
                                  state_ssm_re[0].reshape(nseq, SSM_S), state_ssm_im[0].reshape(nseq, SSM_S), p)
    y_sample = jnp.swapaxes(_ffn(x2s, p).reshape(steps, nseq, D_MODEL), 0, 1)
    k_s, v_s = jnp.transpose(nkt, (0, 3, 1, 2)), jnp.transpose(nvt, (0, 3, 1, 2))

    st = lambda a, n: a.reshape(1, n, SSM_G, SSM_N)
    return (y_prompt, y_sample, k_p[None], v_p[None], st(re_p, batch), st(im_p, batch),
            k_s[None], v_s[None], st(re_s, nseq), st(im_s, nseq))
```

```python
import functools

import jax
import jax.numpy as jnp
import numpy as np
from jax import lax
from jax.experimental import pallas as pl
from jax.experimental.pallas import tpu as pltpu

F32 = jnp.float32
BF16 = jnp.bfloat16

D_MODEL = 1024
HEAD_DIM = 64
N_HEADS = 8
N_KV = 2
WINDOW = 128
ATTN_W = N_HEADS * HEAD_DIM
KV_W = N_KV * HEAD_DIM
SSM_W = 512
SSM_G = 32
SSM_CH = 16
SSM_N = 64
SSM_S = SSM_G * SSM_N
D_FF = 2816
IN_COLS = ATTN_W + 2 * KV_W + SSM_W + 2 * D_MODEL
ROPE_THETA = 10000.0
RMS_EPS = 1e-6
MASK_VALUE = -1e30
PAST_LEN = 16384

LANES = 128
SUBLANES = 8
MXU_DIM = 256

FRONT_TM = 512
MIX_TM = 512
SCAN_SEG = MIX_TM // SUBLANES
FF_CHUNK = 256
SCAN_LC = 512
SAMPLE_BS = 16
VMEM_LIMIT = 58 * 1024 * 1024


def _const_spec(shape):
    nd = len(shape)
    return pl.BlockSpec(shape, lambda *_: (0,) * nd, pipeline_mode=pl.Buffered(1))


def _rms(x, w):
    ms = jnp.mean(x * x, axis=-1, keepdims=True)
    return x * lax.rsqrt(ms + RMS_EPS) * w


def _swiglu_residual(x, nw_ref, w1_ref, w3_ref, w2_ref):
    h = _rms(x, nw_ref[...]).astype(BF16)
    acc = None
    for c0 in range(0, D_FF, FF_CHUNK):
        cs = slice(c0, min(c0 + FF_CHUNK, D_FF))
        a = jnp.dot(h, w1_ref[:, cs], preferred_element_type=F32)
        b = jnp.dot(h, w3_ref[:, cs], preferred_element_type=F32)
        g = (a * jax.nn.sigmoid(a) * b).astype(BF16)
        t = jnp.dot(g, w2_ref[cs, :], preferred_element_type=F32)
        acc = t if acc is None else acc + t
    return x + 0.5 * acc


def _head_norm_rope(t, ones_ref, g_ref, cos, sinlo, sinhi, scale):
    ssum = jnp.dot((t * t).astype(BF16), ones_ref[...], preferred_element_type=F32)
    tn = t * lax.rsqrt(ssum * (1.0 / HEAD_DIM) + RMS_EPS) * g_ref[...]
    up = pltpu.roll(tn, HEAD_DIM // 2, axis=1)
    dn = pltpu.roll(tn, LANES - HEAD_DIM // 2, axis=1)
    r = tn * cos + dn * sinlo + up * sinhi
    return r * scale if scale != 1.0 else r


def _front_kernel(n_cast, x_ref, base_ref, off_ref,
                  n1_ref, w1_ref, w3_ref, w2_ref, nm_ref, win_ref, qn_ref, kn_ref, ones_ref, *refs):
    cast_in, refs = refs[:n_cast], refs[n_cast:]
    x1_ref, q_ref, k_ref, v_ref, u_ref, sga_ref, sgs_ref = refs[:7]
    for src, dst in zip(cast_in, refs[7:]):
        dst[...] = src[...].astype(BF16)
    x1 = _swiglu_residual(x_ref[...], n1_ref, w1_ref, w3_ref, w2_ref)
    x1_ref[...] = x1
    h = _rms(x1, nm_ref[...]).astype(BF16)
    cb, sb = base_ref[0:1, :], base_ref[1:2, :]
    cos = cb * off_ref[0] - sb * off_ref[1]
    sinlo = sb * off_ref[2] + cb * off_ref[3]
    sinhi = sb * off_ref[4] + cb * off_ref[5]
    o = ATTN_W + 2 * KV_W + SSM_W
    qkvu = jnp.dot(h, win_ref[:, 0:o], preferred_element_type=F32)
    for j in range(ATTN_W // LANES):
        q_ref[:, j * LANES:(j + 1) * LANES] = _head_norm_rope(
            qkvu[:, j * LANES:(j + 1) * LANES], ones_ref, qn_ref, cos, sinlo, sinhi,
            HEAD_DIM ** -0.5).astype(q_ref.dtype)
    k_ref[...] = _head_norm_rope(qkvu[:, ATTN_W:ATTN_W + KV_W], ones_ref, kn_ref, cos, sinlo, sinhi, 1.0)
    v_ref[...] = qkvu[:, ATTN_W + KV_W:ATTN_W + 2 * KV_W]
    u_ref[...] = qkvu[:, ATTN_W + 2 * KV_W:o]
    ga = jnp.dot(h, win_ref[:, o:o + D_MODEL], preferred_element_type=F32)
    sga_ref[...] = jax.nn.sigmoid(ga).astype(BF16)
    o += D_MODEL
    gs = jnp.dot(h, win_ref[:, o:o + D_MODEL], preferred_element_type=F32)
    sgs_ref[...] = jax.nn.sigmoid(gs).astype(BF16)


def _cast_block_rows(rows, steps):
    for blocks in range(steps, 0, -1):
        if rows % blocks == 0 and (rows // blocks) % (2 * SUBLANES) == 0:
            return rows // blocks
    raise ValueError(rows)


def _front(x, tabs, p, q_dtype, cast=()):
    n = x.shape[0]
    tm = FRONT_TM
    steps = n // tm
    base_tab, off_tab = tabs
    row = lambda w: pl.BlockSpec((tm, w), lambda i: (i, 0))
    consts = [p['n1'], p['w1a'], p['w3a'], p['w2a'], p['nm'], p['win'], p['qn'], p['kn'], p['ones']]
    outs = [(D_MODEL, F32), (ATTN_W, q_dtype), (KV_W, F32), (KV_W, F32), (SSM_W, F32),
            (D_MODEL, BF16), (D_MODEL, BF16)]
    cast_in, cast_out = [], []
    for w in cast:
        _, rows, cols = w.shape
        rb = _cast_block_rows(rows, steps)
        last = rows // rb - 1
        cast_in.append(pl.BlockSpec((None, rb, cols), lambda i, last=last: (0, jnp.minimum(i, last), 0)))
        cast_out.append(pl.BlockSpec((rb, cols), lambda i, last=last: (jnp.minimum(i, last), 0)))
    return pl.pallas_call(
        functools.partial(_front_kernel, len(cast)),
        grid=(steps,),
        in_specs=[row(D_MODEL), pl.BlockSpec((None, 2, LANES), lambda i: (i, 0, 0)), _const_spec(off_tab.shape)]
        + [_const_spec(c.shape) for c in consts] + cast_in,
        out_specs=[row(w) for w, _ in outs] + cast_out,
        out_shape=[jax.ShapeDtypeStruct((n, w), dt) for w, dt in outs]
        + [jax.ShapeDtypeStruct(w.shape[1:], BF16) for w in cast],
        compiler_params=pltpu.CompilerParams(dimension_semantics=("arbitrary",),
                                             vmem_limit_bytes=VMEM_LIMIT),
        name="front",
    )(x, base_tab, off_tab, *consts, *cast)


def _ffn_kernel(x_ref, n_ref, w1_ref, w3_ref, w2_ref, y_ref):
    y_ref[...] = _swiglu_residual(x_ref[...], n_ref, w1_ref, w3_ref, w2_ref)


def _ffn(x, p):
    n = x.shape[0]
    tm = FRONT_TM
    weights = [p['w1b'], p['w3b'], p['w2b']]
    return pl.pallas_call(
        _ffn_kernel,
        grid=(n // tm,),
        in_specs=[pl.BlockSpec((tm, D_MODEL), lambda i: (i, 0)), _const_spec(p['n2'].shape)]
        + [_const_spec(w.shape) for w in weights],
        out_specs=pl.BlockSpec((tm, D_MODEL), lambda i: (i, 0)),
        out_shape=jax.ShapeDtypeStruct((n, D_MODEL), F32),
        compiler_params=pltpu.CompilerParams(dimension_semantics=("arbitrary",),
                                             vmem_limit_bytes=VMEM_LIMIT),
        name="ffn",
    )(x, p['n2'], *weights)


def _cmul(ar, ai, br, bi):
    return ar * br - ai * bi, ar * bi + ai * br


def _ssm_discretize(are_ref, aim_ref, ldt_ref):
    a_re, a_im = are_ref[...], aim_ref[...]
    dt = jnp.exp(ldt_ref[...])
    mag = jnp.exp(dt * a_re)
    ab_re = mag * jnp.cos(dt * a_im)
    ab_im = mag * jnp.sin(dt * a_im)
    den = a_re * a_re + a_im * a_im
    nr, ni = ab_re - 1.0, ab_im
    f_re = (nr * a_re + ni * a_im) / den
    f_im = (ni * a_re - nr * a_im) / den
    return ab_re, ab_im, f_re, f_im


def _build_wb(bbre_ref, bbim_ref, f_re, f_im, wbre_s, wbim_s):
    half = SSM_S // 2
    for hh in range(2):
        fr, fi = f_re[:, hh * half:(hh + 1) * half], f_im[:, hh * half:(hh + 1) * half]
        br, bi = bbre_ref[hh], bbim_ref[hh]
        wbre_s[hh] = (br * fr - bi * fi).astype(BF16)
        wbim_s[hh] = (br * fi + bi * fr).astype(BF16)


def _expand_state(ub, wbre_s, wbim_s, bur_s, bui_s):
    half_in, half_st = SSM_W // 2, SSM_S // 2
    for hh in range(2):
        uh = ub[:, hh * half_in:(hh + 1) * half_in]
        bur_s[:, hh * half_st:(hh + 1) * half_st] = jnp.dot(uh, wbre_s[hh], preferred_element_type=F32)
        bui_s[:, hh * half_st:(hh + 1) * half_st] = jnp.dot(uh, wbim_s[hh], preferred_element_type=F32)


def _project_state(xr_s, xi_s, ccat_ref):
    half_st = SSM_S // 2
    outs = []
    for hh in range(2):
        sl = slice(hh * half_st, (hh + 1) * half_st)
        xcat = jnp.concatenate([xr_s[:, sl].astype(BF16), xi_s[:, sl].astype(BF16)], axis=1)
        outs.append(jnp.dot(xcat, ccat_ref[hh], preferred_element_type=F32))
    return jnp.concatenate(outs, axis=1)


def _gated_attn(attn_b, sga, wao_ref):
    return sga.astype(F32) * jnp.dot(attn_b, wao_ref[...], preferred_element_type=F32)


def _gated_merge(x1, attn_gated, y_ssm, sgs, wglu_ref, bglu_ref, wso_ref, wout_ref):
    z = jax.nn.gelu(y_ssm, approximate=True)
    gl = jnp.dot(z.astype(BF16), wglu_ref[...], preferred_element_type=F32) + bglu_ref[...]
    ssm = (z * jax.nn.sigmoid(gl)).astype(BF16)
    merged = attn_gated + sgs.astype(F32) * jnp.dot(ssm, wso_ref[...], preferred_element_type=F32)
    return x1 + jnp.dot(merged.astype(BF16), wout_ref[...], preferred_element_type=F32)


def _replicate_heads(kv):
    sw = pltpu.roll(kv, HEAD_DIM, axis=1)
    lo = lax.broadcasted_iota(jnp.int32, kv.shape, 1) < HEAD_DIM
    return jnp.where(lo, kv, sw).astype(BF16), jnp.where(lo, sw, kv).astype(BF16)


def _attn_scores(qs, kreps, keys_on_lanes=False):
    qb = qs[0].shape[0]
    lo = lax.broadcasted_iota(jnp.int32, (qb, LANES), 1) < HEAD_DIM
    dims = (((1,), (0,)), ((), ())) if keys_on_lanes else (((1,), (1,)), ((), ()))
    scores = []
    for q, krep in zip(qs, kreps):
        if q.dtype != BF16 or qb % (2 * SUBLANES):
            q = q.astype(F32)
        zero = jnp.zeros((), q.dtype)
        for kv in range(N_KV):
            parts = []
            for jj in range(2):
                t = q[:, (2 * kv + jj) * LANES:(2 * kv + jj + 1) * LANES]
                parts += [jnp.where(lo, t, zero), jnp.where(lo, zero, t)]
            stacked = jnp.concatenate(parts, axis=0).astype(BF16)
            scores.append(lax.dot_general(stacked, krep[kv], dims, preferred_element_type=F32))
    return jnp.concatenate(scores, axis=0)


def _window_valid(rows, qb, first):
    qi = lax.broadcasted_iota(jnp.int32, (rows, 2 * WINDOW), 0) & (qb - 1)
    sj = lax.broadcasted_iota(jnp.int32, (rows, 2 * WINDOW), 1)
    band = (sj <= WINDOW + qi) & (sj > qi)
    return band if first is False else band & ((sj >= WINDOW) | jnp.logical_not(first))


def _attn_softmax(s, sinks_ref, qb, firsts, bias=None):
    per_block = s.shape[0] // len(firsts)
    sink_parts = [jnp.full((qb, 1), sinks_ref[h], F32) for h in range(N_HEADS)]
    sink = jnp.concatenate(sink_parts * len(firsts), axis=0)
    if bias is None:
        valid = jnp.concatenate([_window_valid(per_block, qb, f) for f in firsts], axis=0)
        s = jnp.where(valid, s, MASK_VALUE)
    else:
        s = s + bias
    m = jnp.maximum(jnp.max(s, axis=-1, keepdims=True), sink)
    e = jnp.exp(s - m)
    denom = jnp.sum(e, axis=-1, keepdims=True) + jnp.exp(sink - m)
    return (e * (1.0 / denom)).astype(BF16)


def _attn_values(probs, vreps, qb, keys_on_lanes=False):
    lo = lax.broadcasted_iota(jnp.int32, (qb, LANES), 1) < HEAD_DIM
    rows = 2 * (N_HEADS // N_KV // 2) * qb
    dims = (((1,), (1,)), ((), ())) if keys_on_lanes else (((1,), (0,)), ((), ()))
    outs = []
    for b, vrep in enumerate(vreps):
        tiles = []
        for kv in range(N_KV):
            r0 = (b * N_KV + kv) * rows
            o = lax.dot_general(probs[r0:r0 + rows], vrep[kv], dims, preferred_element_type=F32)
            for jj in range(2):
                oa = o[(2 * jj) * qb:(2 * jj + 1) * qb]
                ob = o[(2 * jj + 1) * qb:(2 * jj + 2) * qb]
                tiles.append(jnp.where(lo, oa, ob))
        outs.append(jnp.concatenate(tiles, axis=1))
    return outs


def _mix_prompt_kernel(x1_ref, q_ref, k_ref, v_ref, u_ref, sga_ref, sgs_ref, sinks_ref,
                       wao_ref, wglu_ref, bglu_ref, wso_ref, wout_ref,
                       are_ref, aim_ref, ldt_ref, bbre_ref, bbim_ref, ccat_ref, d_ref, perm_ref, permt_ref,
                       x2_ref, sre_ref, sim_ref,
                       wbre_s, wbim_s, apr_s, api_s, a1_s, aseg_s, carry_s, cin_s,
                       bur_s, bui_s, xb_s, kprev_s, vprev_s, bias_s):
    b, i = pl.program_id(0), pl.program_id(1)
    nt = pl.num_programs(1)

    @pl.when((b == 0) & (i == 0))
    def _():
        ab_re, ab_im, f_re, f_im = _ssm_discretize(are_ref, aim_ref, ldt_ref)
        _build_wb(bbre_ref, bbim_ref, f_re, f_im, wbre_s, wbim_s)
        a1_s[0:1, :] = ab_re
        a1_s[1:2, :] = ab_im
        pr, pi = ab_re, ab_im
        held = None
        for j in range(SCAN_SEG):
            rows = [jnp.broadcast_to(pr, (SUBLANES, SSM_S)), jnp.broadcast_to(pi, (SUBLANES, SSM_S))]
            if j % 2 == 0:
                held = rows
            else:
                rs = slice((j - 1) * SUBLANES, (j + 1) * SUBLANES)
                apr_s[rs, :] = jnp.concatenate([held[0], rows[0]], axis=0).astype(BF16)
                api_s[rs, :] = jnp.concatenate([held[1], rows[1]], axis=0).astype(BF16)
            if j == SCAN_SEG - 1:
                aseg_s[0:1, :] = pr
                aseg_s[1:2, :] = pi
            pr, pi = _cmul(pr, pi, ab_re, ab_im)
        for t, first in enumerate((False, True)):
            bias_s[t] = jnp.where(_window_valid(bias_s.shape[1], WINDOW, first), 0.0, MASK_VALUE)

    @pl.when(i == 0)
    def _():
        carry_s[...] = jnp.zeros_like(carry_s)
        kprev_s[...] = jnp.zeros_like(kprev_s)
        vprev_s[...] = jnp.zeros_like(vprev_s)

    k0, k1 = _replicate_heads(k_ref[...])
    v0, v1 = _replicate_heads(v_ref[...])
    qs, kreps, vreps, firsts = [], [], [], []
    for blk in range(MIX_TM // WINDOW):
        cur = slice(blk * WINDOW, (blk + 1) * WINDOW)
        if blk == 0:
            prev = [kprev_s[0], kprev_s[1], vprev_s[0], vprev_s[1]]
            firsts.append(i == 0)
        else:
            ps = slice((blk - 1) * WINDOW, blk * WINDOW)
            prev = [k0[ps], k1[ps], v0[ps], v1[ps]]
            firsts.append(False)
        qs.append(q_ref[cur, :])
        kreps.append([jnp.concatenate([prev[0], k0[cur]], axis=0), jnp.concatenate([prev[1], k1[cur]], axis=0)])
        vreps.append([jnp.concatenate([prev[2], v0[cur]], axis=0), jnp.concatenate([prev[3], v1[cur]], axis=0)])
    tail = slice(MIX_TM - WINDOW, MIX_TM)
    kprev_s[0], kprev_s[1] = k0[tail], k1[tail]
    vprev_s[0], vprev_s[1] = v0[tail], v1[tail]
    scores = _attn_scores(qs, kreps)

    half_in, half_st = SSM_W // 2, SSM_S // 2
    chunks = list(range(0, SSM_S, SCAN_LC))
    nblk = len(qs)
    blk_rows = scores.shape[0] // nblk
    u = u_ref[...]
    up = jnp.dot(perm_ref[...], u.astype(BF16), preferred_element_type=F32).astype(BF16)

    def expand_half(hh):
        uh = up[:, hh * half_in:(hh + 1) * half_in]
        bur_s[:, hh * half_st:(hh + 1) * half_st] = jnp.dot(uh, wbre_s[hh], preferred_element_type=F32)
        bui_s[:, hh * half_st:(hh + 1) * half_st] = jnp.dot(uh, wbim_s[hh], preferred_element_type=F32)

    def softmax_blocks(lo, hi):
        no_prev = (i == 0).astype(jnp.int32)
        bias = jnp.concatenate([bias_s[no_prev] if blk == 0 else bias_s[0] for blk in range(lo, hi)], axis=0)
        return _attn_softmax(scores[lo * blk_rows:hi * blk_rows], sinks_ref, WINDOW, firsts[lo:hi], bias)

    def scan_chunk(c0):
        cs = slice(c0, c0 + SCAN_LC)
        ar = jnp.broadcast_to(a1_s[0:1, cs], (SUBLANES, SCAN_LC))
        ai = jnp.broadcast_to(a1_s[1:2, cs], (SUBLANES, SCAN_LC))
        xr, xi = bur_s[0:SUBLANES, cs], bui_s[0:SUBLANES, cs]
        for j in range(1, SCAN_SEG):
            rs = slice(j * SUBLANES, (j + 1) * SUBLANES)
            xr, xi = (ar * xr - ai * xi + bur_s[rs, cs], ar * xi + ai * xr + bui_s[rs, cs])
            bur_s[rs, cs] = xr
            bui_s[rs, cs] = xi

    mid = nblk // 2
    expand_half(0)
    probs_a = softmax_blocks(0, mid)
    expand_half(1)
    probs_b = softmax_blocks(mid, nblk)
    quarter = len(chunks) // 4
    scan_group = lambda g: [scan_chunk(c0) for c0 in chunks[g * quarter:(g + 1) * quarter]]
    scan_group(0)
    attn = _attn_values(probs_a, vreps[0:mid], WINDOW)
    scan_group(1)
    attn += _attn_values(probs_b, vreps[mid:nblk], WINDOW)
    scan_group(2)
    attn_gated = _gated_attn(jnp.concatenate(attn, axis=0).astype(BF16), sga_ref[...], wao_ref)
    scan_group(3)

    last = MIX_TM - SUBLANES
    sr, si = aseg_s[0:1, :], aseg_s[1:2, :]
    cr, ci = carry_s[0:1, :], carry_s[1:2, :]
    for s in range(SUBLANES):
        cin_s[0, s:s + 1, :] = cr
        cin_s[1, s:s + 1, :] = ci
        pr, pi = _cmul(sr, si, cr, ci)
        cr = pr + bur_s[last + s:last + s + 1, :]
        ci = pi + bui_s[last + s:last + s + 1, :]
    carry_s[0:1, :] = cr
    carry_s[1:2, :] = ci
    pack = 2 * SUBLANES

    def fix_chunk(c0):
        cs = slice(c0, c0 + SCAN_LC)
        cinr = jnp.concatenate([cin_s[0, :, cs]] * 2, axis=0).astype(BF16)
        cini = jnp.concatenate([cin_s[1, :, cs]] * 2, axis=0).astype(BF16)
        col = (c0 // half_st) * SSM_S + c0 % half_st
        for r0 in range(0, MIX_TM, pack):
            rs = slice(r0, r0 + pack)
            pr, pi = apr_s[rs, cs], api_s[rs, cs]
            xb_s[rs, col:col + SCAN_LC] = bur_s[rs, cs].astype(BF16) + pr * cinr - pi * cini
            xb_s[rs, col + half_st:col + half_st + SCAN_LC] = bui_s[rs, cs].astype(BF16) + pr * cini + pi * cinr

    def project_half(hh):
        return jnp.dot(xb_s[:, hh * SSM_S:(hh + 1) * SSM_S], ccat_ref[hh], preferred_element_type=F32)

    per_half = len(chunks) // 2
    for c0 in chunks[:per_half]:
        fix_chunk(c0)
    yp0 = project_half(0)
    for c0 in chunks[per_half:]:
        fix_chunk(c0)
    yp = jnp.concatenate([yp0, project_half(1)], axis=1)
    y_ssm = jnp.dot(permt_ref[...], yp.astype(BF16), preferred_element_type=F32) + d_ref[...] * u

    x2_ref[...] = _gated_merge(x1_ref[...], attn_gated, y_ssm, sgs_ref[...],
                               wglu_ref, bglu_ref, wso_ref, wout_ref)

    @pl.when(i == nt - 1)
    def _():
        sre_ref[...] = carry_s[0:1, :]
        sim_ref[...] = carry_s[1:2, :]


def _mix_prompt(fr, p, batch, seq):
    x1, q, k, v, u, sga, sgs = fr
    tm = MIX_TM
    nt = seq // tm
    row = lambda w: pl.BlockSpec((tm, w), lambda b, i: (b * nt + i, 0))
    consts = [p['wao'], p['wglu'], p['bglu'], p['wso'], p['wout'],
              p['are'], p['aim'], p['ldt'], p['bbre'], p['bbim'], p['ccat'], p['dskip'], p['perm'], p['permt']]
    st_spec = pl.BlockSpec((None, 1, SSM_S), lambda b, i: (b, 0, 0))
    scratch = [pltpu.VMEM((2, MXU_DIM, SSM_S // 2), BF16), pltpu.VMEM((2, MXU_DIM, SSM_S // 2), BF16),
               pltpu.VMEM((tm, SSM_S), BF16), pltpu.VMEM((tm, SSM_S), BF16),
               pltpu.VMEM((SUBLANES, SSM_S), F32), pltpu.VMEM((SUBLANES, SSM_S), F32),
               pltpu.VMEM((SUBLANES, SSM_S), F32), pltpu.VMEM((2, SUBLANES, SSM_S), F32),
               pltpu.VMEM((tm, SSM_S), F32), pltpu.VMEM((tm, SSM_S), F32), pltpu.VMEM((tm, 2 * SSM_S), BF16),
               pltpu.VMEM((N_KV, WINDOW, LANES), BF16), pltpu.VMEM((N_KV, WINDOW, LANES), BF16),
               pltpu.VMEM((2, N_HEADS * WINDOW, 2 * WINDOW), F32)]
    return pl.pallas_call(
        _mix_prompt_kernel,
        grid=(batch, nt),
        in_specs=[row(D_MODEL), row(ATTN_W), row(KV_W), row(KV_W), row(SSM_W), row(D_MODEL), row(D_MODEL),
                  pl.BlockSpec(memory_space=pltpu.SMEM)] + [_const_spec(c.shape) for c in consts],
        out_specs=[row(D_MODEL), st_spec, st_spec],
        out_shape=[jax.ShapeDtypeStruct((batch * seq, D_MODEL), F32),
                   jax.ShapeDtypeStruct((batch, 1, SSM_S), F32),
                   jax.ShapeDtypeStruct((batch, 1, SSM_S), F32)],
        scratch_shapes=scratch,
        compiler_params=pltpu.CompilerParams(dimension_semantics=("arbitrary", "arbitrary"),
                                             vmem_limit_bytes=VMEM_LIMIT),
        name="mix_prompt",
    )(x1, q, k, v, u, sga, sgs, p['sinks'], *consts)


def _attn_sample_kernel(q_ref, kn_ref, vn_ref, ckt_ref, cvt_ref, sinks_ref, o_ref, nkt_ref, nvt_ref,
                        q8_s, knp_s, vnp_s):
    steps = q_ref.shape[0]
    shift = WINDOW - steps
    keep = lax.broadcasted_iota(jnp.int32, (HEAD_DIM, WINDOW), 1) < shift
    q8_s[...] = jnp.zeros_like(q8_s)
    knp_s[...] = jnp.zeros_like(knp_s)
    vnp_s[...] = jnp.zeros_like(vnp_s)
    qs, kreps, vreps = [], [], []
    for s in range(SAMPLE_BS):
        for t in range(steps):
            q8_s[s, t:t + 1, :] = q_ref[t, s:s + 1, :]
            knp_s[s, t:t + 1, :] = kn_ref[t, s:s + 1, :]
            vnp_s[s, t:t + 1, :] = vn_ref[t, s:s + 1, :]
        news = [knp_s[s].T, vnp_s[s].T]
        ends = [pltpu.roll(n, shift, axis=1) for n in news]
        krep, vrep = [], []
        for kv in range(N_KV):
            hs = slice(kv * HEAD_DIM, (kv + 1) * HEAD_DIM)
            for cache_ref, out_ref, new, end, reps in ((ckt_ref, nkt_ref, news[0], ends[0], krep),
                                                       (cvt_ref, nvt_ref, news[1], ends[1], vrep)):
                old = cache_ref[s, kv]
                out_ref[s, kv] = jnp.where(keep, pltpu.roll(old, shift, axis=1), end[hs])
                both = jnp.concatenate([old, new[hs]], axis=1).astype(BF16)
                reps.append(jnp.concatenate([both, both], axis=0))
        qs.append(q8_s[s])
        kreps.append(krep)
        vreps.append(vrep)
    probs = _attn_softmax(_attn_scores(qs, kreps, keys_on_lanes=True), sinks_ref, SUBLANES, [False] * SAMPLE_BS)
    outs = _attn_values(probs, vreps, SUBLANES, keys_on_lanes=True)
    for s in range(SAMPLE_BS):
        for t in range(steps):
            o_ref[t, s:s + 1, :] = outs[s][t:t + 1, :]


def _attn_sample(q, kn, vn, ckt, cvt, sinks, steps, nseq):
    bs = SAMPLE_BS
    blk3 = lambda w: pl.BlockSpec((steps, bs, w), lambda i: (0, i, 0))
    cache = pl.BlockSpec((bs, N_KV, HEAD_DIM, WINDOW), lambda i: (i, 0, 0, 0))
    cache_shape = jax.ShapeDtypeStruct((nseq, N_KV, HEAD_DIM, WINDOW), F32)
    return pl.pallas_call(
        _attn_sample_kernel,
        grid=(nseq // bs,),
        in_specs=[blk3(ATTN_W), blk3(KV_W), blk3(KV_W), cache, cache, pl.BlockSpec(memory_space=pltpu.SMEM)],
        out_specs=[blk3(ATTN_W), cache, cache],
        out_shape=[jax.ShapeDtypeStruct((steps, nseq, ATTN_W), F32), cache_shape, cache_shape],
        scratch_shapes=[pltpu.VMEM((bs, SUBLANES, ATTN_W), F32), pltpu.VMEM((bs, LANES, KV_W), F32),
                        pltpu.VMEM((bs, LANES, KV_W), F32)],
        compiler_params=pltpu.CompilerParams(dimension_semantics=("arbitrary",)),
        name="attn_sample",
    )(q.reshape(steps, nseq, ATTN_W), kn.reshape(steps, nseq, KV_W), vn.reshape(steps, nseq, KV_W),
      ckt, cvt, sinks)


def _mix_sample_kernel(x1_ref, attn_ref, u_ref, sga_ref, sgs_ref, x0r_ref, x0i_ref,
                       wao_ref, wglu_ref, bglu_ref, wso_ref, wout_ref,
                       are_ref, aim_ref, ldt_ref, bbre_ref, bbim_ref, ccat_ref, d_ref,
                       x2_ref, sre_ref, sim_ref,
                       wbre_s, wbim_s, bur_s, bui_s):
    nseq = x0r_ref.shape[0]
    steps = x1_ref.shape[0] // nseq
    ab_re, ab_im, f_re, f_im = _ssm_discretize(are_ref, aim_ref, ldt_ref)
    _build_wb(bbre_ref, bbim_ref, f_re, f_im, wbre_s, wbim_s)
    u = u_ref[...]
    _expand_state(u.astype(BF16), wbre_s, wbim_s, bur_s, bui_s)
    for c0 in range(0, SSM_S, LANES):
        cs = slice(c0, c0 + LANES)
        ar, ai = ab_re[:, cs], ab_im[:, cs]
        xr, xi = x0r_ref[:, cs], x0i_ref[:, cs]
        for t in range(steps):
            rs = slice(t * nseq, (t + 1) * nseq)
            nr = ar * xr - ai * xi + bur_s[rs, cs]
            ni = ar * xi + ai * xr + bui_s[rs, cs]
            bur_s[rs, cs] = nr
            bui_s[rs, cs] = ni
            xr, xi = nr, ni
        sre_ref[:, cs] = xr
        sim_ref[:, cs] = xi
    y_ssm = _project_state(bur_s, bui_s, ccat_ref) + d_ref[...] * u
    attn_gated = _gated_attn(attn_ref[...].astype(BF16), sga_ref[...], wao_ref)
    x2_ref[...] = _gated_merge(x1_ref[...], attn_gated, y_ssm, sgs_ref[...],
                               wglu_ref, bglu_ref, wso_ref, wout_ref)


def _mix_sample(x1, attn, u, sga, sgs, x0r, x0i, p):
    n, nseq = x1.shape[0], x0r.shape[0]
    args = [x1, attn, u, sga, sgs, x0r, x0i, p['wao'], p['wglu'], p['bglu'], p['wso'], p['wout'],
            p['are'], p['aim'], p['ldt'], p['bbre'], p['bbim'], p['ccat'], p['dskip']]
    return pl.pallas_call(
        _mix_sample_kernel,
        grid=(1,),
        in_specs=[_const_spec(a.shape) for a in args],
        out_specs=[pl.BlockSpec((n, D_MODEL), lambda i: (0, 0)), pl.BlockSpec((nseq, SSM_S), lambda i: (0, 0)),
                   pl.BlockSpec((nseq, SSM_S), lambda i: (0, 0))],
        out_shape=[jax.ShapeDtypeStruct((n, D_MODEL), F32),
                   jax.ShapeDtypeStruct((nseq, SSM_S), F32),
                   jax.ShapeDtypeStruct((nseq, SSM_S), F32)],
        scratch_shapes=[pltpu.VMEM((2, MXU_DIM, SSM_S // 2), BF16), pltpu.VMEM((2, MXU_DIM, SSM_S // 2), BF16),
                        pltpu.VMEM((n, SSM_S), F32), pltpu.VMEM((n, SSM_S), F32)],
        compiler_params=pltpu.CompilerParams(dimension_semantics=("arbitrary",),
                                             vmem_limit_bytes=VMEM_LIMIT),
        name="mix_sample",
    )(*args)


def _rope_tables(base_pos, off_pos):
    half = HEAD_DIM // 2
    lane = np.arange(LANES)
    inv = ROPE_THETA ** (-2.0 * (lane % half) / HEAD_DIM)
    first = (lane % HEAD_DIM) < half
    ab = np.asarray(base_pos, np.float64)[:, None] * inv[None, :]
    ao = np.asarray(off_pos, np.float64)[:, None] * inv[None, :]
    base = np.stack([np.cos(ab), np.sin(ab)], axis=1)
    co, so = np.cos(ao), np.sin(ao)
    lo = np.where(first, -1.0, 0.0)
    hi = np.where(first, 0.0, 1.0)
    off = np.stack([co, so, co * lo, so * lo, co * hi, so * hi], axis=0)
    return jnp.asarray(base, F32), jnp.asarray(off, F32)


def _block_diag_halves(m):
    g, r, c = m.shape
    hg = g // 2
    strip = jnp.transpose(m.reshape(2, hg, r, c), (0, 2, 1, 3)).reshape(2, 1, r, hg * c)
    full = jnp.broadcast_to(strip, (2, hg, r, hg * c)).reshape(2, hg * r, hg * c)
    mask = jnp.asarray(np.kron(np.eye(hg), np.ones((r, c))), m.dtype)
    return full * mask


def _layout_params(ffn1_norm, ffn1_w1, ffn1_w3, ffn1_w2, mix_norm, w_in, q_norm, k_norm, attn_sinks,
                   w_attn_out, ssm_a_re, ssm_a_im, ssm_log_dt, ssm_b_re, ssm_b_im, ssm_c_re, ssm_c_im,
                   ssm_d, w_glu, b_glu, w_ssm_out, w_out, ffn2_norm, ffn2_w1, ffn2_w3, ffn2_w2):
    l = 0
    row = lambda a: a.reshape(1, -1).astype(F32)
    head_ones = jnp.asarray(np.kron(np.eye(LANES // HEAD_DIM), np.ones((HEAD_DIM, HEAD_DIM))), F32)
    r = np.arange(MIX_TM)
    tok = (r % SUBLANES) * SCAN_SEG + r // SUBLANES
    perm = jnp.asarray(tok[:, None] == np.arange(MIX_TM)[None, :], BF16)
    c_cat = jnp.concatenate([_block_diag_halves(jnp.swapaxes(ssm_c_re[l], 1, 2)),
                             -_block_diag_halves(jnp.swapaxes(ssm_c_im[l], 1, 2))], axis=1)
    return dict(
        n1=row(ffn1_norm[l]), w1a=ffn1_w1[l].astype(BF16), w3a=ffn1_w3[l].astype(BF16), w2a=ffn1_w2[l].astype(BF16),
        nm=row(mix_norm[l]), win=w_in[l].astype(BF16),
        qn=row(jnp.tile(q_norm[l], LANES // HEAD_DIM)), kn=row(jnp.tile(k_norm[l], LANES // HEAD_DIM)),
        ones=head_ones.astype(BF16), sinks=attn_sinks[l].astype(F32),
        bglu=row(b_glu[l]),
        are=row(ssm_a_re[l]), aim=row(ssm_a_im[l]), ldt=row(jnp.repeat(ssm_log_dt[l], SSM_N)),
        bbre=_block_diag_halves(jnp.swapaxes(ssm_b_re[l], 1, 2).astype(F32)),
        bbim=_block_diag_halves(jnp.swapaxes(ssm_b_im[l], 1, 2).astype(F32)),
        ccat=c_cat.astype(BF16), dskip=row(ssm_d[l]), perm=perm, permt=perm.T,
        n2=row(ffn2_norm[l]),
    )


def kernel(x_prompt, x_sample, cache_k, cache_v, state_ssm_re, state_ssm_im, ffn1_norm, ffn1_w1, ffn1_w3, ffn1_w2, mix_norm, w_in, q_norm, k_norm, attn_sinks, w_attn_out, ssm_a_re, ssm_a_im, ssm_log_dt, ssm_b_re, ssm_b_im, ssm_c_re, ssm_c_im, ssm_d, w_glu, b_glu, w_ssm_out, w_out, ffn2_norm, ffn2_w1, ffn2_w3, ffn2_w2):
    assert ffn1_norm.shape[0] == 1, "single trunk layer"
    batch, seq, _ = x_prompt.shape
    nseq, steps, _ = x_sample.shape
    p = _layout_params(ffn1_norm, ffn1_w1, ffn1_w3, ffn1_w2, mix_norm, w_in, q_norm, k_norm, attn_sinks,
                       w_attn_out, ssm_a_re, ssm_a_im, ssm_log_dt, ssm_b_re, ssm_b_im, ssm_c_re, ssm_c_im,
                       ssm_d, w_glu, b_glu, w_ssm_out, w_out, ffn2_norm, ffn2_w1, ffn2_w3, ffn2_w2)

    tile0 = np.arange(batch * seq // FRONT_TM) * FRONT_TM
    tabs_p = _rope_tables(tile0 % seq, np.arange(FRONT_TM))
    later = dict(wao=w_attn_out, wglu=w_glu, wso=w_ssm_out, wout=w_out, w1b=ffn2_w1, w3b=ffn2_w3, w2b=ffn2_w2)
    fr = _front(x_prompt.reshape(batch * seq, D_MODEL), tabs_p, p, BF16, cast=list(later.values()))
    p.update(zip(later, fr[7:]))
    fr = fr[:7]
    x2p, re_p, im_p = _mix_prompt(fr, p, batch, seq)
    y_prompt = _ffn(x2p, p).reshape(batch, seq, D_MODEL)
    tail_kv = lambda a: a.reshape(batch, seq, KV_W)[:, seq - WINDOW:].reshape(batch, WINDOW, N_KV, HEAD_DIM)
    k_p, v_p = tail_kv(fr[2]), tail_kv(fr[3])

    xs = jnp.swapaxes(x_sample, 0, 1).reshape(steps * nseq, D_MODEL)
    tile0 = np.arange(steps * nseq // FRONT_TM) * FRONT_TM
    tabs_s = _rope_tables(PAST_LEN + tile0 // nseq, np.arange(FRONT_TM) // nseq)
    x1s, qs, ks, vs, us, sgas, sgss = _front(xs, tabs_s, p, F32)
    to_lanes = lambda c: jnp.transpose(c[0], (0, 2, 3, 1))
    attn_s, nkt, nvt = _attn_sample(qs, ks, vs, to_lanes(cache_k), to_lanes(cache_v), p['sinks'], steps, nseq)
    x2s, re_s, im_s = _mix_sample(x1s, attn_s.reshape(steps * nseq, ATTN_W), us, sgas, sgss,
                                  state_ssm_re[0].reshape(nseq, SSM_S), state_ssm_im[0].reshape(nseq, SSM_S), p)
    y_sample = jnp.swapaxes(_ffn(x2s, p).reshape(steps, nseq, D_MODEL), 0, 1)
    k_s, v_s = jnp.transpose(nkt, (0, 3, 1, 2)), jnp.transpose(nvt, (0, 3, 1, 2))

    st = lambda a, n: a.reshape(1, n, SSM_G, SSM_N)
    return (y_prompt, y_sample, k_p[None], v_p[None], st(re_p, batch), st(im_p, batch),
            k_s[None], v_s[None], st(re_s, nseq), st(im_s, nseq))
```

```python
import functools

import jax
import jax.numpy as jnp
import numpy as np
from jax import lax
from jax.experimental import pallas as pl
from jax.experimental.pallas import tpu as pltpu

F32 = jnp.float32
BF16 = jnp.bfloat16

D_MODEL = 1024
HEAD_DIM = 64
N_HEADS = 8
N_KV = 2
WINDOW = 128
ATTN_W = N_HEADS * HEAD_DIM
KV_W = N_KV * HEAD_DIM
SSM_W = 512
SSM_G = 32
SSM_CH = 16
SSM_N = 64
SSM_S = SSM_G * SSM_N
D_FF = 2816
IN_COLS = ATTN_W + 2 * KV_W + SSM_W + 2 * D_MODEL
ROPE_THETA = 10000.0
RMS_EPS = 1e-6
MASK_VALUE = -1e30
PAST_LEN = 16384

LANES = 128
SUBLANES = 8
MXU_DIM = 256

FRONT_TM = 512
MIX_TM = 512
SCAN_SEG = MIX_TM // SUBLANES
FF_CHUNK = 256
SCAN_LC = 512
SAMPLE_BS = 16
VMEM_LIMIT = 58 * 1024 * 1024


def _const_spec(shape):
    nd = len(shape)
    return pl.BlockSpec(shape, lambda *_: (0,) * nd, pipeline_mode=pl.Buffered(1))


def _rms(x, w):
    ms = jnp.mean(x * x, axis=-1, keepdims=True)
    return x * lax.rsqrt(ms + RMS_EPS) * w


def _swiglu_residual(x, nw_ref, w1_ref, w3_ref, w2_ref):
    h = _rms(x, nw_ref[...]).astype(BF16)
    acc = None
    for c0 in range(0, D_FF, FF_CHUNK):
        cs = slice(c0, min(c0 + FF_CHUNK, D_FF))
        a = jnp.dot(h, w1_ref[:, cs], preferred_element_type=F32)
        b = jnp.dot(h, w3_ref[:, cs], preferred_element_type=F32)
        g = (a * jax.nn.sigmoid(a) * b).astype(BF16)
        t = jnp.dot(g, w2_ref[cs, :], preferred_element_type=F32)
        acc = t if acc is None else acc + t
    return x + 0.5 * acc


def _head_norm_rope(t, ones_ref, g_ref, cos, sinlo, sinhi, scale):
    ssum = jnp.dot((t * t).astype(BF16), ones_ref[...], preferred_element_type=F32)
    tn = t * lax.rsqrt(ssum * (1.0 / HEAD_DIM) + RMS_EPS) * g_ref[...]
    up = pltpu.roll(tn, HEAD_DIM // 2, axis=1)
    dn = pltpu.roll(tn, LANES - HEAD_DIM // 2, axis=1)
    r = tn * cos + dn * sinlo + up * sinhi
    return r * scale if scale != 1.0 else r


def _front_kernel(n_cast, x_ref, base_ref, off_ref,
                  n1_ref, w1_ref, w3_ref, w2_ref, nm_ref, win_ref, qn_ref, kn_ref, ones_ref, *refs):
    cast_in, refs = refs[:n_cast], refs[n_cast:]
    x1_ref, q_ref, k_ref, v_ref, u_ref, sga_ref, sgs_ref = refs[:7]
    for src, dst in zip(cast_in, refs[7:]):
        dst[...] = src[...].astype(BF16)
    x1 = _swiglu_residual(x_ref[...], n1_ref, w1_ref, w3_ref, w2_ref)
    x1_ref[...] = x1
    h = _rms(x1, nm_ref[...]).astype(BF16)
    cb, sb = base_ref[0:1, :], base_ref[1:2, :]
    cos = cb * off_ref[0] - sb * off_ref[1]
    sinlo = sb * off_ref[2] + cb * off_ref[3]
    sinhi = sb * off_ref[4] + cb * off_ref[5]
    o = ATTN_W + 2 * KV_W + SSM_W
    qkvu = jnp.dot(h, win_ref[:, 0:o], preferred_element_type=F32)
    for j in range(ATTN_W // LANES):
        q_ref[:, j * LANES:(j + 1) * LANES] = _head_norm_rope(
            qkvu[:, j * LANES:(j + 1) * LANES], ones_ref, qn_ref, cos, sinlo, sinhi,
            HEAD_DIM ** -0.5).astype(q_ref.dtype)
    k_ref[...] = _head_norm_rope(qkvu[:, ATTN_W:ATTN_W + KV_W], ones_ref, kn_ref, cos, sinlo, sinhi, 1.0)
    v_ref[...] = qkvu[:, ATTN_W + KV_W:ATTN_W + 2 * KV_W]
    u_ref[...] = qkvu[:, ATTN_W + 2 * KV_W:o]
    ga = jnp.dot(h, win_ref[:, o:o + D_MODEL], preferred_element_type=F32)
    sga_ref[...] = jax.nn.sigmoid(ga).astype(BF16)
    o += D_MODEL
    gs = jnp.dot(h, win_ref[:, o:o + D_MODEL], preferred_element_type=F32)
    sgs_ref[...] = jax.nn.sigmoid(gs).astype(BF16)


def _cast_block_rows(rows, steps):
    for blocks in range(steps, 0, -1):
        if rows % blocks == 0 and (rows // blocks) % (2 * SUBLANES) == 0:
            return rows // blocks
    raise ValueError(rows)


def _front(x, tabs, p, q_dtype, cast=()):
    n = x.shape[0]
    tm = FRONT_TM
    steps = n // tm
    base_tab, off_tab = tabs
    row = lambda w: pl.BlockSpec((tm, w), lambda i: (i, 0))
    consts = [p['n1'], p['w1a'], p['w3a'], p['w2a'], p['nm'], p['win'], p['qn'], p['kn'], p['ones']]
    outs = [(D_MODEL, F32), (ATTN_W, q_dtype), (KV_W, F32), (KV_W, F32), (SSM_W, F32),
            (D_MODEL, BF16), (D_MODEL, BF16)]
    cast_in, cast_out = [], []
    for w in cast:
        _, rows, cols = w.shape
        rb = _cast_block_rows(rows, steps)
        last = rows // rb - 1
        cast_in.append(pl.BlockSpec((None, rb, cols), lambda i, last=last: (0, jnp.minimum(i, last), 0)))
        cast_out.append(pl.BlockSpec((rb, cols), lambda i, last=last: (jnp.minimum(i, last), 0)))
    return pl.pallas_call(
        functools.partial(_front_kernel, len(cast)),
        grid=(steps,),
        in_specs=[row(D_MODEL), pl.BlockSpec((None, 2, LANES), lambda i: (i, 0, 0)), _const_spec(off_tab.shape)]
        + [_const_spec(c.shape) for c in consts] + cast_in,
        out_specs=[row(w) for w, _ in outs] + cast_out,
        out_shape=[jax.ShapeDtypeStruct((n, w), dt) for w, dt in outs]
        + [jax.ShapeDtypeStruct(w.shape[1:], BF16) for w in cast],
        compiler_params=pltpu.CompilerParams(dimension_semantics=("arbitrary",),
                                             vmem_limit_bytes=VMEM_LIMIT),
        name="front",
    )(x, base_tab, off_tab, *consts, *cast)


def _ffn_kernel(x_ref, n_ref, w1_ref, w3_ref, w2_ref, y_ref):
    y_ref[...] = _swiglu_residual(x_ref[...], n_ref, w1_ref, w3_ref, w2_ref)


def _ffn(x, p):
    n = x.shape[0]
    tm = FRONT_TM
    weights = [p['w1b'], p['w3b'], p['w2b']]
    return pl.pallas_call(
        _ffn_kernel,
        grid=(n // tm,),
        in_specs=[pl.BlockSpec((tm, D_MODEL), lambda i: (i, 0)), _const_spec(p['n2'].shape)]
        + [_const_spec(w.shape) for w in weights],
        out_specs=pl.BlockSpec((tm, D_MODEL), lambda i: (i, 0)),
        out_shape=jax.ShapeDtypeStruct((n, D_MODEL), F32),
        compiler_params=pltpu.CompilerParams(dimension_semantics=("arbitrary",),
                                             vmem_limit_bytes=VMEM_LIMIT),
        name="ffn",
    )(x, p['n2'], *weights)


def _cmul(ar, ai, br, bi):
    return ar * br - ai * bi, ar * bi + ai * br


def _ssm_discretize(are_ref, aim_ref, ldt_ref):
    a_re, a_im = are_ref[...], aim_ref[...]
    dt = jnp.exp(ldt_ref[...])
    mag = jnp.exp(dt * a_re)
    ab_re = mag * jnp.cos(dt * a_im)
    ab_im = mag * jnp.sin(dt * a_im)
    den = a_re * a_re + a_im * a_im
    nr, ni = ab_re - 1.0, ab_im
    f_re = (nr * a_re + ni * a_im) / den
    f_im = (ni * a_re - nr * a_im) / den
    return ab_re, ab_im, f_re, f_im


def _build_wb(bbre_ref, bbim_ref, f_re, f_im, wbre_s, wbim_s):
    half = SSM_S // 2
    for hh in range(2):
        fr, fi = f_re[:, hh * half:(hh + 1) * half], f_im[:, hh * half:(hh + 1) * half]
        br, bi = bbre_ref[hh], bbim_ref[hh]
        wbre_s[hh] = (br * fr - bi * fi).astype(BF16)
        wbim_s[hh] = (br * fi + bi * fr).astype(BF16)


def _expand_state(ub, wbre_s, wbim_s, bur_s, bui_s):
    half_in, half_st = SSM_W // 2, SSM_S // 2
    for hh in range(2):
        uh = ub[:, hh * half_in:(hh + 1) * half_in]
        bur_s[:, hh * half_st:(hh + 1) * half_st] = jnp.dot(uh, wbre_s[hh], preferred_element_type=F32)
        bui_s[:, hh * half_st:(hh + 1) * half_st] = jnp.dot(uh, wbim_s[hh], preferred_element_type=F32)


def _project_state(xr_s, xi_s, ccat_ref):
    half_st = SSM_S // 2
    outs = []
    for hh in range(2):
        sl = slice(hh * half_st, (hh + 1) * half_st)
        xcat = jnp.concatenate([xr_s[:, sl].astype(BF16), xi_s[:, sl].astype(BF16)], axis=1)
        outs.append(jnp.dot(xcat, ccat_ref[hh], preferred_element_type=F32))
    return jnp.concatenate(outs, axis=1)


def _gated_attn(attn_b, sga, wao_ref):
    return sga.astype(F32) * jnp.dot(attn_b, wao_ref[...], preferred_element_type=F32)


def _gated_sum(attn_gated, y_ssm, sgs, wglu_ref, bglu_ref, wso_ref):
    z = jax.nn.gelu(y_ssm, approximate=True)
    gl = jnp.dot(z.astype(BF16), wglu_ref[...], preferred_element_type=F32) + bglu_ref[...]
    ssm = (z * jax.nn.sigmoid(gl)).astype(BF16)
    merged = attn_gated + sgs.astype(F32) * jnp.dot(ssm, wso_ref[...], preferred_element_type=F32)
    return merged.astype(BF16)


def _gated_merge(x1, attn_gated, y_ssm, sgs, wglu_ref, bglu_ref, wso_ref, wout_ref):
    merged = _gated_sum(attn_gated, y_ssm, sgs, wglu_ref, bglu_ref, wso_ref)
    return x1 + jnp.dot(merged, wout_ref[...], preferred_element_type=F32)


def _replicate_heads(kv):
    sw = pltpu.roll(kv, HEAD_DIM, axis=1)
    lo = lax.broadcasted_iota(jnp.int32, kv.shape, 1) < HEAD_DIM
    return jnp.where(lo, kv, sw).astype(BF16), jnp.where(lo, sw, kv).astype(BF16)


def _attn_scores(qs, kreps, keys_on_lanes=False):
    qb = qs[0].shape[0]
    lo = lax.broadcasted_iota(jnp.int32, (qb, LANES), 1) < HEAD_DIM
    dims = (((1,), (0,)), ((), ())) if keys_on_lanes else (((1,), (1,)), ((), ()))
    scores = []
    for q, krep in zip(qs, kreps):
        if q.dtype != BF16 or qb % (2 * SUBLANES):
            q = q.astype(F32)
        zero = jnp.zeros((), q.dtype)
        for kv in range(N_KV):
            parts = []
            for jj in range(2):
                t = q[:, (2 * kv + jj) * LANES:(2 * kv + jj + 1) * LANES]
                parts += [jnp.where(lo, t, zero), jnp.where(lo, zero, t)]
            stacked = jnp.concatenate(parts, axis=0).astype(BF16)
            scores.append(lax.dot_general(stacked, krep[kv], dims, preferred_element_type=F32))
    return jnp.concatenate(scores, axis=0)


def _window_valid(rows, qb, first):
    qi = lax.broadcasted_iota(jnp.int32, (rows, 2 * WINDOW), 0) & (qb - 1)
    sj = lax.broadcasted_iota(jnp.int32, (rows, 2 * WINDOW), 1)
    band = (sj <= WINDOW + qi) & (sj > qi)
    return band if first is False else band & ((sj >= WINDOW) | jnp.logical_not(first))


def _attn_softmax(s, sinks_ref, qb, firsts, bias=None):
    per_block = s.shape[0] // len(firsts)
    sink_parts = [jnp.full((qb, 1), sinks_ref[h], F32) for h in range(N_HEADS)]
    sink = jnp.concatenate(sink_parts * len(firsts), axis=0)
    if bias is None:
        valid = jnp.concatenate([_window_valid(per_block, qb, f) for f in firsts], axis=0)
        s = jnp.where(valid, s, MASK_VALUE)
    else:
        s = s + bias
    m = jnp.maximum(jnp.max(s, axis=-1, keepdims=True), sink)
    e = jnp.exp(s - m)
    denom = jnp.sum(e, axis=-1, keepdims=True) + jnp.exp(sink - m)
    return (e * (1.0 / denom)).astype(BF16)


def _attn_values(probs, vreps, qb, keys_on_lanes=False):
    lo = lax.broadcasted_iota(jnp.int32, (qb, LANES), 1) < HEAD_DIM
    rows = 2 * (N_HEADS // N_KV // 2) * qb
    dims = (((1,), (1,)), ((), ())) if keys_on_lanes else (((1,), (0,)), ((), ()))
    outs = []
    for b, vrep in enumerate(vreps):
        tiles = []
        for kv in range(N_KV):
            r0 = (b * N_KV + kv) * rows
            o = lax.dot_general(probs[r0:r0 + rows], vrep[kv], dims, preferred_element_type=F32)
            for jj in range(2):
                oa = o[(2 * jj) * qb:(2 * jj + 1) * qb]
                ob = o[(2 * jj + 1) * qb:(2 * jj + 2) * qb]
                tiles.append(jnp.where(lo, oa, ob))
        outs.append(jnp.concatenate(tiles, axis=1))
    return outs


def _mix_prompt_kernel(x1_ref, q_ref, k_ref, v_ref, u_ref, sga_ref, sgs_ref, sinks_ref,
                       wao_ref, wglu_ref, bglu_ref, wso_ref, wout_ref,
                       are_ref, aim_ref, ldt_ref, bbre_ref, bbim_ref, ccat_ref, d_ref, perm_ref, permt_ref,
                       x2_ref, sre_ref, sim_ref,
                       wbre_s, wbim_s, apr_s, api_s, a1_s, aseg_s, carry_s, cin_s,
                       bur_s, bui_s, xb_s, kprev_s, vprev_s, bias_s, merged_s, x1p_s):
    b, i = pl.program_id(0), pl.program_id(1)
    nt = pl.num_programs(1) - 1

    @pl.when((b == 0) & (i == 0))
    def _():
        ab_re, ab_im, f_re, f_im = _ssm_discretize(are_ref, aim_ref, ldt_ref)
        _build_wb(bbre_ref, bbim_ref, f_re, f_im, wbre_s, wbim_s)
        a1_s[0:1, :] = ab_re
        a1_s[1:2, :] = ab_im
        pr, pi = ab_re, ab_im
        held = None
        for j in range(SCAN_SEG):
            rows = [jnp.broadcast_to(pr, (SUBLANES, SSM_S)), jnp.broadcast_to(pi, (SUBLANES, SSM_S))]
            if j % 2 == 0:
                held = rows
            else:
                rs = slice((j - 1) * SUBLANES, (j + 1) * SUBLANES)
                apr_s[rs, :] = jnp.concatenate([held[0], rows[0]], axis=0).astype(BF16)
                api_s[rs, :] = jnp.concatenate([held[1], rows[1]], axis=0).astype(BF16)
            if j == SCAN_SEG - 1:
                aseg_s[0:1, :] = pr
                aseg_s[1:2, :] = pi
            pr, pi = _cmul(pr, pi, ab_re, ab_im)
        for t, first in enumerate((False, True)):
            bias_s[t] = jnp.where(_window_valid(bias_s.shape[1], WINDOW, first), 0.0, MASK_VALUE)

    @pl.when(i == 0)
    def _():
        carry_s[...] = jnp.zeros_like(carry_s)
        kprev_s[...] = jnp.zeros_like(kprev_s)
        vprev_s[...] = jnp.zeros_like(vprev_s)
        merged_s[...] = jnp.zeros_like(merged_s)
        x1p_s[...] = jnp.zeros_like(x1p_s)

    merged_prev, x1_prev = merged_s[...], x1p_s[...]

    k0, k1 = _replicate_heads(k_ref[...])
    v0, v1 = _replicate_heads(v_ref[...])
    qs, kreps, vreps, firsts = [], [], [], []
    for blk in range(MIX_TM // WINDOW):
        cur = slice(blk * WINDOW, (blk + 1) * WINDOW)
        if blk == 0:
            prev = [kprev_s[0], kprev_s[1], vprev_s[0], vprev_s[1]]
            firsts.append(i == 0)
        else:
            ps = slice((blk - 1) * WINDOW, blk * WINDOW)
            prev = [k0[ps], k1[ps], v0[ps], v1[ps]]
            firsts.append(False)
        qs.append(q_ref[cur, :])
        kreps.append([jnp.concatenate([prev[0], k0[cur]], axis=0), jnp.concatenate([prev[1], k1[cur]], axis=0)])
        vreps.append([jnp.concatenate([prev[2], v0[cur]], axis=0), jnp.concatenate([prev[3], v1[cur]], axis=0)])
    tail = slice(MIX_TM - WINDOW, MIX_TM)
    kprev_s[0], kprev_s[1] = k0[tail], k1[tail]
    vprev_s[0], vprev_s[1] = v0[tail], v1[tail]
    scores = _attn_scores(qs, kreps)

    half_in, half_st = SSM_W // 2, SSM_S // 2
    chunks = list(range(0, SSM_S, SCAN_LC))
    nblk = len(qs)
    blk_rows = scores.shape[0] // nblk
    u = u_ref[...]
    up = jnp.dot(perm_ref[...], u.astype(BF16), preferred_element_type=F32).astype(BF16)

    def expand_half(hh):
        uh = up[:, hh * half_in:(hh + 1) * half_in]
        bur_s[:, hh * half_st:(hh + 1) * half_st] = jnp.dot(uh, wbre_s[hh], preferred_element_type=F32)
        bui_s[:, hh * half_st:(hh + 1) * half_st] = jnp.dot(uh, wbim_s[hh], preferred_element_type=F32)

    def softmax_blocks(lo, hi):
        no_prev = (i == 0).astype(jnp.int32)
        bias = jnp.concatenate([bias_s[no_prev] if blk == 0 else bias_s[0] for blk in range(lo, hi)], axis=0)
        return _attn_softmax(scores[lo * blk_rows:hi * blk_rows], sinks_ref, WINDOW, firsts[lo:hi], bias)

    def scan_chunk(c0):
        cs = slice(c0, c0 + SCAN_LC)
        ar = jnp.broadcast_to(a1_s[0:1, cs], (SUBLANES, SCAN_LC))
        ai = jnp.broadcast_to(a1_s[1:2, cs], (SUBLANES, SCAN_LC))
        xr, xi = bur_s[0:SUBLANES, cs], bui_s[0:SUBLANES, cs]
        for j in range(1, SCAN_SEG):
            rs = slice(j * SUBLANES, (j + 1) * SUBLANES)
            xr, xi = (ar * xr - ai * xi + bur_s[rs, cs], ar * xi + ai * xr + bui_s[rs, cs])
            bur_s[rs, cs] = xr
            bui_s[rs, cs] = xi
        return xr

    mid = nblk // 2
    expand_half(0)
    probs_a = softmax_blocks(0, mid)
    expand_half(1)
    probs_b = softmax_blocks(mid, nblk)
    quarter = len(chunks) // 4
    scan_group = lambda g: [scan_chunk(c0) for c0 in chunks[g * quarter:(g + 1) * quarter]]
    started = scan_group(0)[-1]
    bits = lax.bitcast_convert_type(started[0:1, 0:1], jnp.uint32)
    zero = lax.bitcast_convert_type(lax.shift_right_logical(lax.shift_right_logical(bits, jnp.uint32(16)),
                                                            jnp.uint32(16)), F32).astype(BF16)
    x2_ref[...] = x1_prev + jnp.dot(merged_prev + zero, wout_ref[...], preferred_element_type=F32)
    attn = _attn_values(probs_a, vreps[0:mid], WINDOW)
    scan_group(1)
    attn += _attn_values(probs_b, vreps[mid:nblk], WINDOW)
    scan_group(2)
    attn_gated = _gated_attn(jnp.concatenate(attn, axis=0).astype(BF16), sga_ref[...], wao_ref)
    scan_group(3)

    last = MIX_TM - SUBLANES
    sr, si = aseg_s[0:1, :], aseg_s[1:2, :]
    cr, ci = carry_s[0:1, :], carry_s[1:2, :]
    for s in range(SUBLANES):
        cin_s[0, s:s + 1, :] = cr
        cin_s[1, s:s + 1, :] = ci
        pr, pi = _cmul(sr, si, cr, ci)
        cr = pr + bur_s[last + s:last + s + 1, :]
        ci = pi + bui_s[last + s:last + s + 1, :]
    carry_s[0:1, :] = cr
    carry_s[1:2, :] = ci
    pack = 2 * SUBLANES

    def fix_chunk(c0):
        cs = slice(c0, c0 + SCAN_LC)
        cinr = jnp.concatenate([cin_s[0, :, cs]] * 2, axis=0).astype(BF16)
        cini = jnp.concatenate([cin_s[1, :, cs]] * 2, axis=0).astype(BF16)
        col = (c0 // half_st) * SSM_S + c0 % half_st
        for r0 in range(0, MIX_TM, pack):
            rs = slice(r0, r0 + pack)
            pr, pi = apr_s[rs, cs], api_s[rs, cs]
            xb_s[rs, col:col + SCAN_LC] = bur_s[rs, cs].astype(BF16) + pr * cinr - pi * cini
            xb_s[rs, col + half_st:col + half_st + SCAN_LC] = bui_s[rs, cs].astype(BF16) + pr * cini + pi * cinr

    def project_half(hh):
        return jnp.dot(xb_s[:, hh * SSM_S:(hh + 1) * SSM_S], ccat_ref[hh], preferred_element_type=F32)

    per_half = len(chunks) // 2
    for c0 in chunks[:per_half]:
        fix_chunk(c0)
    yp0 = project_half(0)
    for c0 in chunks[per_half:]:
        fix_chunk(c0)
    yp = jnp.concatenate([yp0, project_half(1)], axis=1)
    y_ssm = jnp.dot(permt_ref[...], yp.astype(BF16), preferred_element_type=F32) + d_ref[...] * u

    merged_s[...] = _gated_sum(attn_gated, y_ssm, sgs_ref[...], wglu_ref, bglu_ref, wso_ref)
    x1p_s[...] = x1_ref[...]

    @pl.when(i == nt - 1)
    def _():
        sre_ref[...] = carry_s[0:1, :]
        sim_ref[...] = carry_s[1:2, :]


def _mix_prompt(fr, p, batch, seq):
    x1, q, k, v, u, sga, sgs = fr
    tm = MIX_TM
    nt = seq // tm
    row = lambda w: pl.BlockSpec((tm, w), lambda b, i: (b * nt + jnp.minimum(i, nt - 1), 0))
    lag = pl.BlockSpec((tm, D_MODEL), lambda b, i: (b * nt + jnp.maximum(i - 1, 0), 0))
    consts = [p['wao'], p['wglu'], p['bglu'], p['wso'], p['wout'],
              p['are'], p['aim'], p['ldt'], p['bbre'], p['bbim'], p['ccat'], p['dskip'], p['perm'], p['permt']]
    st_spec = pl.BlockSpec((None, 1, SSM_S), lambda b, i: (b, 0, 0))
    scratch = [pltpu.VMEM((2, MXU_DIM, SSM_S // 2), BF16), pltpu.VMEM((2, MXU_DIM, SSM_S // 2), BF16),
               pltpu.VMEM((tm, SSM_S), BF16), pltpu.VMEM((tm, SSM_S), BF16),
               pltpu.VMEM((SUBLANES, SSM_S), F32), pltpu.VMEM((SUBLANES, SSM_S), F32),
               pltpu.VMEM((SUBLANES, SSM_S), F32), pltpu.VMEM((2, SUBLANES, SSM_S), F32),
               pltpu.VMEM((tm, SSM_S), F32), pltpu.VMEM((tm, SSM_S), F32), pltpu.VMEM((tm, 2 * SSM_S), BF16),
               pltpu.VMEM((N_KV, WINDOW, LANES), BF16), pltpu.VMEM((N_KV, WINDOW, LANES), BF16),
               pltpu.VMEM((2, N_HEADS * WINDOW, 2 * WINDOW), F32),
               pltpu.VMEM((tm, D_MODEL), BF16), pltpu.VMEM((tm, D_MODEL), F32)]
    return pl.pallas_call(
        _mix_prompt_kernel,
        grid=(batch, nt + 1),
        in_specs=[row(D_MODEL), row(ATTN_W), row(KV_W), row(KV_W), row(SSM_W), row(D_MODEL), row(D_MODEL),
                  pl.BlockSpec(memory_space=pltpu.SMEM)] + [_const_spec(c.shape) for c in consts],
        out_specs=[lag, st_spec, st_spec],
        out_shape=[jax.ShapeDtypeStruct((batch * seq, D_MODEL), F32),
                   jax.ShapeDtypeStruct((batch, 1, SSM_S), F32),
                   jax.ShapeDtypeStruct((batch, 1, SSM_S), F32)],
        scratch_shapes=scratch,
        compiler_params=pltpu.CompilerParams(dimension_semantics=("arbitrary", "arbitrary"),
                                             vmem_limit_bytes=VMEM_LIMIT),
        name="mix_prompt",
    )(x1, q, k, v, u, sga, sgs, p['sinks'], *consts)


def _attn_sample_kernel(q_ref, kn_ref, vn_ref, ckt_ref, cvt_ref, sinks_ref, o_ref, nkt_ref, nvt_ref,
                        q8_s, knp_s, vnp_s):
    steps = q_ref.shape[0]
    shift = WINDOW - steps
    keep = lax.broadcasted_iota(jnp.int32, (HEAD_DIM, WINDOW), 1) < shift
    q8_s[...] = jnp.zeros_like(q8_s)
    knp_s[...] = jnp.zeros_like(knp_s)
    vnp_s[...] = jnp.zeros_like(vnp_s)
    qs, kreps, vreps = [], [], []
    for s in range(SAMPLE_BS):
        for t in range(steps):
            q8_s[s, t:t + 1, :] = q_ref[t, s:s + 1, :]
            knp_s[s, t:t + 1, :] = kn_ref[t, s:s + 1, :]
            vnp_s[s, t:t + 1, :] = vn_ref[t, s:s + 1, :]
        news = [knp_s[s].T, vnp_s[s].T]
        ends = [pltpu.roll(n, shift, axis=1) for n in news]
        krep, vrep = [], []
        for kv in range(N_KV):
            hs = slice(kv * HEAD_DIM, (kv + 1) * HEAD_DIM)
            for cache_ref, out_ref, new, end, reps in ((ckt_ref, nkt_ref, news[0], ends[0], krep),
                                                       (cvt_ref, nvt_ref, news[1], ends[1], vrep)):
                old = cache_ref[s, kv]
                out_ref[s, kv] = jnp.where(keep, pltpu.roll(old, shift, axis=1), end[hs])
                both = jnp.concatenate([old, new[hs]], axis=1).astype(BF16)
                reps.append(jnp.concatenate([both, both], axis=0))
        qs.append(q8_s[s])
        kreps.append(krep)
        vreps.append(vrep)
    probs = _attn_softmax(_attn_scores(qs, kreps, keys_on_lanes=True), sinks_ref, SUBLANES, [False] * SAMPLE_BS)
    outs = _attn_values(probs, vreps, SUBLANES, keys_on_lanes=True)
    for s in range(SAMPLE_BS):
        for t in range(steps):
            o_ref[t, s:s + 1, :] = outs[s][t:t + 1, :]


def _attn_sample(q, kn, vn, ckt, cvt, sinks, steps, nseq):
    bs = SAMPLE_BS
    blk3 = lambda w: pl.BlockSpec((steps, bs, w), lambda i: (0, i, 0))
    cache = pl.BlockSpec((bs, N_KV, HEAD_DIM, WINDOW), lambda i: (i, 0, 0, 0))
    cache_shape = jax.ShapeDtypeStruct((nseq, N_KV, HEAD_DIM, WINDOW), F32)
    return pl.pallas_call(
        _attn_sample_kernel,
        grid=(nseq // bs,),
        in_specs=[blk3(ATTN_W), blk3(KV_W), blk3(KV_W), cache, cache, pl.BlockSpec(memory_space=pltpu.SMEM)],
        out_specs=[blk3(ATTN_W), cache, cache],
        out_shape=[jax.ShapeDtypeStruct((steps, nseq, ATTN_W), F32), cache_shape, cache_shape],
        scratch_shapes=[pltpu.VMEM((bs, SUBLANES, ATTN_W), F32), pltpu.VMEM((bs, LANES, KV_W), F32),
                        pltpu.VMEM((bs, LANES, KV_W), F32)],
        compiler_params=pltpu.CompilerParams(dimension_semantics=("arbitrary",)),
        name="attn_sample",
    )(q.reshape(steps, nseq, ATTN_W), kn.reshape(steps, nseq, KV_W), vn.reshape(steps, nseq, KV_W),
      ckt, cvt, sinks)


def _mix_sample_kernel(x1_ref, attn_ref, u_ref, sga_ref, sgs_ref, x0r_ref, x0i_ref,
                       wao_ref, wglu_ref, bglu_ref, wso_ref, wout_ref,
                       are_ref, aim_ref, ldt_ref, bbre_ref, bbim_ref, ccat_ref, d_ref,
                       x2_ref, sre_ref, sim_ref,
                       wbre_s, wbim_s, bur_s, bui_s):
    nseq = x0r_ref.shape[0]
    steps = x1_ref.shape[0] // nseq
    ab_re, ab_im, f_re, f_im = _ssm_discretize(are_ref, aim_ref, ldt_ref)
    _build_wb(bbre_ref, bbim_ref, f_re, f_im, wbre_s, wbim_s)
    u = u_ref[...]
    _expand_state(u.astype(BF16), wbre_s, wbim_s, bur_s, bui_s)
    for c0 in range(0, SSM_S, LANES):
        cs = slice(c0, c0 + LANES)
        ar, ai = ab_re[:, cs], ab_im[:, cs]
        xr, xi = x0r_ref[:, cs], x0i_ref[:, cs]
        for t in range(steps):
            rs = slice(t * nseq, (t + 1) * nseq)
            nr = ar * xr - ai * xi + bur_s[rs, cs]
            ni = ar * xi + ai * xr + bui_s[rs, cs]
            bur_s[rs, cs] = nr
            bui_s[rs, cs] = ni
            xr, xi = nr, ni
        sre_ref[:, cs] = xr
        sim_ref[:, cs] = xi
    y_ssm = _project_state(bur_s, bui_s, ccat_ref) + d_ref[...] * u
    attn_gated = _gated_attn(attn_ref[...].astype(BF16), sga_ref[...], wao_ref)
    x2_ref[...] = _gated_merge(x1_ref[...], attn_gated, y_ssm, sgs_ref[...],
                               wglu_ref, bglu_ref, wso_ref, wout_ref)


def _mix_sample(x1, attn, u, sga, sgs, x0r, x0i, p):
    n, nseq = x1.shape[0], x0r.shape[0]
    args = [x1, attn, u, sga, sgs, x0r, x0i, p['wao'], p['wglu'], p['bglu'], p['wso'], p['wout'],
            p['are'], p['aim'], p['ldt'], p['bbre'], p['bbim'], p['ccat'], p['dskip']]
    return pl.pallas_call(
        _mix_sample_kernel,
        grid=(1,),
        in_specs=[_const_spec(a.shape) for a in args],
        out_specs=[pl.BlockSpec((n, D_MODEL), lambda i: (0, 0)), pl.BlockSpec((nseq, SSM_S), lambda i: (0, 0)),
                   pl.BlockSpec((nseq, SSM_S), lambda i: (0, 0))],
        out_shape=[jax.ShapeDtypeStruct((n, D_MODEL), F32),
                   jax.ShapeDtypeStruct((nseq, SSM_S), F32),
                   jax.ShapeDtypeStruct((nseq, SSM_S), F32)],
        scratch_shapes=[pltpu.VMEM((2, MXU_DIM, SSM_S // 2), BF16), pltpu.VMEM((2, MXU_DIM, SSM_S // 2), BF16),
                        pltpu.VMEM((n, SSM_S), F32), pltpu.VMEM((n, SSM_S), F32)],
        compiler_params=pltpu.CompilerParams(dimension_semantics=("arbitrary",),
                                             vmem_limit_bytes=VMEM_LIMIT),
        name="mix_sample",
    )(*args)


def _rope_tables(base_pos, off_pos):
    half = HEAD_DIM // 2
    lane = np.arange(LANES)
    inv = ROPE_THETA ** (-2.0 * (lane % half) / HEAD_DIM)
    first = (lane % HEAD_DIM) < half
    ab = np.asarray(base_pos, np.float64)[:, None] * inv[None, :]
    ao = np.asarray(off_pos, np.float64)[:, None] * inv[None, :]
    base = np.stack([np.cos(ab), np.sin(ab)], axis=1)
    co, so = np.cos(ao), np.sin(ao)
    lo = np.where(first, -1.0, 0.0)
    hi = np.where(first, 0.0, 1.0)
    off = np.stack([co, so, co * lo, so * lo, co * hi, so * hi], axis=0)
    return jnp.asarray(base, F32), jnp.asarray(off, F32)


def _block_diag_halves(m):
    g, r, c = m.shape
    hg = g // 2
    strip = jnp.transpose(m.reshape(2, hg, r, c), (0, 2, 1, 3)).reshape(2, 1, r, hg * c)
    full = jnp.broadcast_to(strip, (2, hg, r, hg * c)).reshape(2, hg * r, hg * c)
    mask = jnp.asarray(np.kron(np.eye(hg), np.ones((r, c))), m.dtype)
    return full * mask


def _layout_params(ffn1_norm, ffn1_w1, ffn1_w3, ffn1_w2, mix_norm, w_in, q_norm, k_norm, attn_sinks,
                   w_attn_out, ssm_a_re, ssm_a_im, ssm_log_dt, ssm_b_re, ssm_b_im, ssm_c_re, ssm_c_im,
                   ssm_d, w_glu, b_glu, w_ssm_out, w_out, ffn2_norm, ffn2_w1, ffn2_w3, ffn2_w2):
    l = 0
    row = lambda a: a.reshape(1, -1).astype(F32)
    head_ones = jnp.asarray(np.kron(np.eye(LANES // HEAD_DIM), np.ones((HEAD_DIM, HEAD_DIM))), F32)
    r = np.arange(MIX_TM)
    tok = (r % SUBLANES) * SCAN_SEG + r // SUBLANES
    perm = jnp.asarray(tok[:, None] == np.arange(MIX_TM)[None, :], BF16)
    c_cat = jnp.concatenate([_block_diag_halves(jnp.swapaxes(ssm_c_re[l], 1, 2)),
                             -_block_diag_halves(jnp.swapaxes(ssm_c_im[l], 1, 2))], axis=1)
    return dict(
        n1=row(ffn1_norm[l]), w1a=ffn1_w1[l].astype(BF16), w3a=ffn1_w3[l].astype(BF16), w2a=ffn1_w2[l].astype(BF16),
        nm=row(mix_norm[l]), win=w_in[l].astype(BF16),
        qn=row(jnp.tile(q_norm[l], LANES // HEAD_DIM)), kn=row(jnp.tile(k_norm[l], LANES // HEAD_DIM)),
        ones=head_ones.astype(BF16), sinks=attn_sinks[l].astype(F32),
        bglu=row(b_glu[l]),
        are=row(ssm_a_re[l]), aim=row(ssm_a_im[l]), ldt=row(jnp.repeat(ssm_log_dt[l], SSM_N)),
        bbre=_block_diag_halves(jnp.swapaxes(ssm_b_re[l], 1, 2).astype(F32)),
        bbim=_block_diag_halves(jnp.swapaxes(ssm_b_im[l], 1, 2).astype(F32)),
        ccat=c_cat.astype(BF16), dskip=row(ssm_d[l]), perm=perm, permt=perm.T,
        n2=row(ffn2_norm[l]),
    )


def kernel(x_prompt, x_sample, cache_k, cache_v, state_ssm_re, state_ssm_im, ffn1_norm, ffn1_w1, ffn1_w3, ffn1_w2, mix_norm, w_in, q_norm, k_norm, attn_sinks, w_attn_out, ssm_a_re, ssm_a_im, ssm_log_dt, ssm_b_re, ssm_b_im, ssm_c_re, ssm_c_im, ssm_d, w_glu, b_glu, w_ssm_out, w_out, ffn2_norm, ffn2_w1, ffn2_w3, ffn2_w2):
    assert ffn1_norm.shape[0] == 1, "single trunk layer"
    batch, seq, _ = x_prompt.shape
    nseq, steps, _ = x_sample.shape
    p = _layout_params(ffn1_norm, ffn1_w1, ffn1_w3, ffn1_w2, mix_norm, w_in, q_norm, k_norm, attn_sinks,
                       w_attn_out, ssm_a_re, ssm_a_im, ssm_log_dt, ssm_b_re, ssm_b_im, ssm_c_re, ssm_c_im,
                       ssm_d, w_glu, b_glu, w_ssm_out, w_out, ffn2_norm, ffn2_w1, ffn2_w3, ffn2_w2)

    tile0 = np.arange(batch * seq // FRONT_TM) * FRONT_TM
    tabs_p = _rope_tables(tile0 % seq, np.arange(FRONT_TM))
    later = dict(wao=w_attn_out, wglu=w_glu, wso=w_ssm_out, wout=w_out, w1b=ffn2_w1, w3b=ffn2_w3, w2b=ffn2_w2)
    fr = _front(x_prompt.reshape(batch * seq, D_MODEL), tabs_p, p, BF16, cast=list(later.values()))
    p.update(zip(later, fr[7:]))
    fr = fr[:7]
    x2p, re_p, im_p = _mix_prompt(fr, p, batch, seq)
    y_prompt = _ffn(x2p, p).reshape(batch, seq, D_MODEL)
    tail_kv = lambda a: a.reshape(batch, seq, KV_W)[:, seq - WINDOW:].reshape(batch, WINDOW, N_KV, HEAD_DIM)
    k_p, v_p = tail_kv(fr[2]), tail_kv(fr[3])

    xs = jnp.swapaxes(x_sample, 0, 1).reshape(steps * nseq, D_MODEL)
    tile0 = np.arange(steps * nseq // FRONT_TM) * FRONT_TM
    tabs_s = _rope_tables(PAST_LEN + tile0 // nseq, np.arange(FRONT_TM) // nseq)
    x1s, qs, ks, vs, us, sgas, sgss = _front(xs, tabs_s, p, F32)
    to_lanes = lambda c: jnp.transpose(c[0], (0, 2, 3, 1))
    attn_s, nkt, nvt = _attn_sample(qs, ks, vs, to_lanes(cache_k), to_lanes(cache_v), p['sinks'], steps, nseq)
    x2s, re_s, im_s = _mix_sample(x1s, attn_s.reshape(steps * nseq, ATTN_W), us, sgas, sgss,
                                  state_ssm_re[0].reshape(nseq, SSM_S), state_ssm_im[0].reshape(nseq, SSM_S), p)
    y_sample = jnp.swapaxes(_ffn(x2s, p).reshape(steps, nseq, D_MODEL), 0, 1)
    k_s, v_s = jnp.transpose(nkt, (0, 3, 1, 2)), jnp.transpose(nvt, (0, 3, 1, 2))

    st = lambda a, n: a.reshape(1, n, SSM_G, SSM_N)
    return (y_prompt, y_sample, k_p[None], v_p[None], st(re_p, batch), st(im_p, batch),
            k_s[None], v_s[None], st(re_s, nseq), st(im_s, nseq))
```

```python
import functools

import jax
import jax.numpy as jnp
import numpy as np
from jax import lax
from jax.experimental import pallas as pl
from jax.experimental.pallas import tpu as pltpu

F32 = jnp.float32
BF16 = jnp.bfloat16

D_MODEL = 1024
HEAD_DIM = 64
N_HEADS = 8
N_KV = 2
WINDOW = 128
ATTN_W = N_HEADS * HEAD_DIM
KV_W = N_KV * HEAD_DIM
SSM_W = 512
SSM_G = 32
SSM_CH = 16
SSM_N = 64
SSM_S = SSM_G * SSM_N
D_FF = 2816
IN_COLS = ATTN_W + 2 * KV_W + SSM_W + 2 * D_MODEL
ROPE_THETA = 10000.0
RMS_EPS = 1e-6
MASK_VALUE = -1e30
PAST_LEN = 16384

LANES = 128
SUBLANES = 8
MXU_DIM = 256

FRONT_TM = 512
FFN_TM = 1024
MIX_TM = 512
SCAN_SEG = MIX_TM // SUBLANES
FF_CHUNK = 256
SCAN_LC = 512
SAMPLE_BS = 16
VMEM_LIMIT = 58 * 1024 * 1024


def _const_spec(shape):
    nd = len(shape)
    return pl.BlockSpec(shape, lambda *_: (0,) * nd, pipeline_mode=pl.Buffered(1))


def _rms(x, w):
    ms = jnp.mean(x * x, axis=-1, keepdims=True)
    return x * lax.rsqrt(ms + RMS_EPS) * w


def _swiglu_residual(x, nw_ref, w1_ref, w3_ref, w2_ref):
    h = _rms(x, nw_ref[...]).astype(BF16)
    acc = None
    for c0 in range(0, D_FF, FF_CHUNK):
        cs = slice(c0, min(c0 + FF_CHUNK, D_FF))
        a = jnp.dot(h, w1_ref[:, cs], preferred_element_type=F32)
        b = jnp.dot(h, w3_ref[:, cs], preferred_element_type=F32)
        g = (a * jax.nn.sigmoid(a) * b).astype(BF16)
        t = jnp.dot(g, w2_ref[cs, :], preferred_element_type=F32)
        acc = t if acc is None else acc + t
    return x + 0.5 * acc


def _head_norm_rope(t, ones_ref, g_ref, cos, sinlo, sinhi, scale):
    ssum = jnp.dot((t * t).astype(BF16), ones_ref[...], preferred_element_type=F32)
    tn = t * lax.rsqrt(ssum * (1.0 / HEAD_DIM) + RMS_EPS) * g_ref[...]
    up = pltpu.roll(tn, HEAD_DIM // 2, axis=1)
    dn = pltpu.roll(tn, LANES - HEAD_DIM // 2, axis=1)
    r = tn * cos + dn * sinlo + up * sinhi
    return r * scale if scale != 1.0 else r


def _front_kernel(n_cast, x_ref, base_ref, off_ref,
                  n1_ref, w1_ref, w3_ref, w2_ref, nm_ref, win_ref, qn_ref, kn_ref, ones_ref, *refs):
    cast_in, refs = refs[:n_cast], refs[n_cast:]
    x1_ref, q_ref, k_ref, v_ref, u_ref, sga_ref, sgs_ref = refs[:7]
    for src, dst in zip(cast_in, refs[7:]):
        dst[...] = src[...].astype(BF16)
    x1 = _swiglu_residual(x_ref[...], n1_ref, w1_ref, w3_ref, w2_ref)
    x1_ref[...] = x1
    h = _rms(x1, nm_ref[...]).astype(BF16)
    cb, sb = base_ref[0:1, :], base_ref[1:2, :]
    cos = cb * off_ref[0] - sb * off_ref[1]
    sinlo = sb * off_ref[2] + cb * off_ref[3]
    sinhi = sb * off_ref[4] + cb * off_ref[5]
    o = ATTN_W + 2 * KV_W + SSM_W
    qkvu = jnp.dot(h, win_ref[:, 0:o], preferred_element_type=F32)
    for j in range(ATTN_W // LANES):
        q_ref[:, j * LANES:(j + 1) * LANES] = _head_norm_rope(
            qkvu[:, j * LANES:(j + 1) * LANES], ones_ref, qn_ref, cos, sinlo, sinhi,
            HEAD_DIM ** -0.5).astype(q_ref.dtype)
    k_ref[...] = _head_norm_rope(qkvu[:, ATTN_W:ATTN_W + KV_W], ones_ref, kn_ref, cos, sinlo, sinhi, 1.0)
    v_ref[...] = qkvu[:, ATTN_W + KV_W:ATTN_W + 2 * KV_W]
    u_ref[...] = qkvu[:, ATTN_W + 2 * KV_W:o]
    ga = jnp.dot(h, win_ref[:, o:o + D_MODEL], preferred_element_type=F32)
    sga_ref[...] = jax.nn.sigmoid(ga).astype(BF16)
    o += D_MODEL
    gs = jnp.dot(h, win_ref[:, o:o + D_MODEL], preferred_element_type=F32)
    sgs_ref[...] = jax.nn.sigmoid(gs).astype(BF16)


def _cast_block_rows(rows, steps):
    for blocks in range(steps, 0, -1):
        if rows % blocks == 0 and (rows // blocks) % (2 * SUBLANES) == 0:
            return rows // blocks
    raise ValueError(rows)


def _front(x, tabs, p, q_dtype, cast=()):
    n = x.shape[0]
    tm = FRONT_TM
    steps = n // tm
    base_tab, off_tab = tabs
    row = lambda w: pl.BlockSpec((tm, w), lambda i: (i, 0))
    consts = [p['n1'], p['w1a'], p['w3a'], p['w2a'], p['nm'], p['win'], p['qn'], p['kn'], p['ones']]
    outs = [(D_MODEL, F32), (ATTN_W, q_dtype), (KV_W, F32), (KV_W, F32), (SSM_W, F32),
            (D_MODEL, BF16), (D_MODEL, BF16)]
    cast_in, cast_out = [], []
    for w in cast:
        _, rows, cols = w.shape
        rb = _cast_block_rows(rows, steps)
        last = rows // rb - 1
        cast_in.append(pl.BlockSpec((None, rb, cols), lambda i, last=last: (0, jnp.minimum(i, last), 0)))
        cast_out.append(pl.BlockSpec((rb, cols), lambda i, last=last: (jnp.minimum(i, last), 0)))
    return pl.pallas_call(
        functools.partial(_front_kernel, len(cast)),
        grid=(steps,),
        in_specs=[row(D_MODEL), pl.BlockSpec((None, 2, LANES), lambda i: (i, 0, 0)), _const_spec(off_tab.shape)]
        + [_const_spec(c.shape) for c in consts] + cast_in,
        out_specs=[row(w) for w, _ in outs] + cast_out,
        out_shape=[jax.ShapeDtypeStruct((n, w), dt) for w, dt in outs]
        + [jax.ShapeDtypeStruct(w.shape[1:], BF16) for w in cast],
        compiler_params=pltpu.CompilerParams(dimension_semantics=("arbitrary",),
                                             vmem_limit_bytes=VMEM_LIMIT),
        name="front",
    )(x, base_tab, off_tab, *consts, *cast)


def _ffn_kernel(x_ref, n_ref, w1_ref, w3_ref, w2_ref, y_ref):
    y_ref[...] = _swiglu_residual(x_ref[...], n_ref, w1_ref, w3_ref, w2_ref)


def _ffn(x, p):
    n = x.shape[0]
    tm = min(FFN_TM, n)
    weights = [p['w1b'], p['w3b'], p['w2b']]
    return pl.pallas_call(
        _ffn_kernel,
        grid=(n // tm,),
        in_specs=[pl.BlockSpec((tm, D_MODEL), lambda i: (i, 0)), _const_spec(p['n2'].shape)]
        + [_const_spec(w.shape) for w in weights],
        out_specs=pl.BlockSpec((tm, D_MODEL), lambda i: (i, 0)),
        out_shape=jax.ShapeDtypeStruct((n, D_MODEL), F32),
        compiler_params=pltpu.CompilerParams(dimension_semantics=("arbitrary",),
                                             vmem_limit_bytes=VMEM_LIMIT),
        name="ffn",
    )(x, p['n2'], *weights)


def _cmul(ar, ai, br, bi):
    return ar * br - ai * bi, ar * bi + ai * br


def _ssm_discretize(are_ref, aim_ref, ldt_ref):
    a_re, a_im = are_ref[...], aim_ref[...]
    dt = jnp.exp(ldt_ref[...])
    mag = jnp.exp(dt * a_re)
    ab_re = mag * jnp.cos(dt * a_im)
    ab_im = mag * jnp.sin(dt * a_im)
    den = a_re * a_re + a_im * a_im
    nr, ni = ab_re - 1.0, ab_im
    f_re = (nr * a_re + ni * a_im) / den
    f_im = (ni * a_re - nr * a_im) / den
    return ab_re, ab_im, f_re, f_im


def _build_wb(bbre_ref, bbim_ref, f_re, f_im, wbre_s, wbim_s):
    half = SSM_S // 2
    for hh in range(2):
        fr, fi = f_re[:, hh * half:(hh + 1) * half], f_im[:, hh * half:(hh + 1) * half]
        br, bi = bbre_ref[hh], bbim_ref[hh]
        wbre_s[hh] = (br * fr - bi * fi).astype(BF16)
        wbim_s[hh] = (br * fi + bi * fr).astype(BF16)


def _expand_state(ub, wbre_s, wbim_s, bur_s, bui_s):
    half_in, half_st = SSM_W // 2, SSM_S // 2
    for hh in range(2):
        uh = ub[:, hh * half_in:(hh + 1) * half_in]
        bur_s[:, hh * half_st:(hh + 1) * half_st] = jnp.dot(uh, wbre_s[hh], preferred_element_type=F32)
        bui_s[:, hh * half_st:(hh + 1) * half_st] = jnp.dot(uh, wbim_s[hh], preferred_element_type=F32)


def _project_state(xr_s, xi_s, ccat_ref):
    half_st = SSM_S // 2
    outs = []
    for hh in range(2):
        sl = slice(hh * half_st, (hh + 1) * half_st)
        xcat = jnp.concatenate([xr_s[:, sl].astype(BF16), xi_s[:, sl].astype(BF16)], axis=1)
        outs.append(jnp.dot(xcat, ccat_ref[hh], preferred_element_type=F32))
    return jnp.concatenate(outs, axis=1)


def _gated_attn(attn_b, sga, wao_ref):
    return sga.astype(F32) * jnp.dot(attn_b, wao_ref[...], preferred_element_type=F32)


def _gated_sum(attn_gated, y_ssm, sgs, wglu_ref, bglu_ref, wso_ref):
    z = jax.nn.gelu(y_ssm, approximate=True)
    gl = jnp.dot(z.astype(BF16), wglu_ref[...], preferred_element_type=F32) + bglu_ref[...]
    ssm = (z * jax.nn.sigmoid(gl)).astype(BF16)
    merged = attn_gated + sgs.astype(F32) * jnp.dot(ssm, wso_ref[...], preferred_element_type=F32)
    return merged.astype(BF16)


def _gated_merge(x1, attn_gated, y_ssm, sgs, wglu_ref, bglu_ref, wso_ref, wout_ref):
    merged = _gated_sum(attn_gated, y_ssm, sgs, wglu_ref, bglu_ref, wso_ref)
    return x1 + jnp.dot(merged, wout_ref[...], preferred_element_type=F32)


def _replicate_heads(kv):
    sw = pltpu.roll(kv, HEAD_DIM, axis=1)
    lo = lax.broadcasted_iota(jnp.int32, kv.shape, 1) < HEAD_DIM
    return jnp.where(lo, kv, sw).astype(BF16), jnp.where(lo, sw, kv).astype(BF16)


def _attn_scores(qs, kreps, keys_on_lanes=False):
    qb = qs[0].shape[0]
    lo = lax.broadcasted_iota(jnp.int32, (qb, LANES), 1) < HEAD_DIM
    dims = (((1,), (0,)), ((), ())) if keys_on_lanes else (((1,), (1,)), ((), ()))
    scores = []
    for q, krep in zip(qs, kreps):
        if q.dtype != BF16 or qb % (2 * SUBLANES):
            q = q.astype(F32)
        zero = jnp.zeros((), q.dtype)
        for kv in range(N_KV):
            parts = []
            for jj in range(2):
                t = q[:, (2 * kv + jj) * LANES:(2 * kv + jj + 1) * LANES]
                parts += [jnp.where(lo, t, zero), jnp.where(lo, zero, t)]
            stacked = jnp.concatenate(parts, axis=0).astype(BF16)
            scores.append(lax.dot_general(stacked, krep[kv], dims, preferred_element_type=F32))
    return jnp.concatenate(scores, axis=0)


def _window_valid(rows, qb, first):
    qi = lax.broadcasted_iota(jnp.int32, (rows, 2 * WINDOW), 0) & (qb - 1)
    sj = lax.broadcasted_iota(jnp.int32, (rows, 2 * WINDOW), 1)
    band = (sj <= WINDOW + qi) & (sj > qi)
    return band if first is False else band & ((sj >= WINDOW) | jnp.logical_not(first))


def _attn_softmax(s, sinks_ref, qb, firsts, bias=None):
    per_block = s.shape[0] // len(firsts)
    sink_parts = [jnp.full((qb, 1), sinks_ref[h], F32) for h in range(N_HEADS)]
    sink = jnp.concatenate(sink_parts * len(firsts), axis=0)
    if bias is None:
        valid = jnp.concatenate([_window_valid(per_block, qb, f) for f in firsts], axis=0)
        s = jnp.where(valid, s, MASK_VALUE)
    else:
        s = s + bias
    m = jnp.maximum(jnp.max(s, axis=-1, keepdims=True), sink)
    e = jnp.exp(s - m)
    denom = jnp.sum(e, axis=-1, keepdims=True) + jnp.exp(sink - m)
    return (e * (1.0 / denom)).astype(BF16)


def _attn_values(probs, vreps, qb, keys_on_lanes=False):
    lo = lax.broadcasted_iota(jnp.int32, (qb, LANES), 1) < HEAD_DIM
    rows = 2 * (N_HEADS // N_KV // 2) * qb
    dims = (((1,), (1,)), ((), ())) if keys_on_lanes else (((1,), (0,)), ((), ()))
    outs = []
    for b, vrep in enumerate(vreps):
        tiles = []
        for kv in range(N_KV):
            r0 = (b * N_KV + kv) * rows
            o = lax.dot_general(probs[r0:r0 + rows], vrep[kv], dims, preferred_element_type=F32)
            for jj in range(2):
                oa = o[(2 * jj) * qb:(2 * jj + 1) * qb]
                ob = o[(2 * jj + 1) * qb:(2 * jj + 2) * qb]
                tiles.append(jnp.where(lo, oa, ob))
        outs.append(jnp.concatenate(tiles, axis=1))
    return outs


def _mix_prompt_kernel(x1_ref, q_ref, k_ref, v_ref, u_ref, sga_ref, sgs_ref, sinks_ref,
                       wao_ref, wglu_ref, bglu_ref, wso_ref, wout_ref,
                       are_ref, aim_ref, ldt_ref, bbre_ref, bbim_ref, ccat_ref, d_ref, perm_ref, permt_ref,
                       x2_ref, sre_ref, sim_ref,
                       wbre_s, wbim_s, apr_s, api_s, a1_s, aseg_s, carry_s, cin_s,
                       bur_s, bui_s, xb_s, kprev_s, vprev_s, bias_s, merged_s, x1p_s):
    b, i = pl.program_id(0), pl.program_id(1)
    nt = pl.num_programs(1) - 1

    @pl.when((b == 0) & (i == 0))
    def _():
        ab_re, ab_im, f_re, f_im = _ssm_discretize(are_ref, aim_ref, ldt_ref)
        _build_wb(bbre_ref, bbim_ref, f_re, f_im, wbre_s, wbim_s)
        a1_s[0:1, :] = ab_re
        a1_s[1:2, :] = ab_im
        pr, pi = ab_re, ab_im
        held = None
        for j in range(SCAN_SEG):
            rows = [jnp.broadcast_to(pr, (SUBLANES, SSM_S)), jnp.broadcast_to(pi, (SUBLANES, SSM_S))]
            if j % 2 == 0:
                held = rows
            else:
                rs = slice((j - 1) * SUBLANES, (j + 1) * SUBLANES)
                apr_s[rs, :] = jnp.concatenate([held[0], rows[0]], axis=0).astype(BF16)
                api_s[rs, :] = jnp.concatenate([held[1], rows[1]], axis=0).astype(BF16)
            if j == SCAN_SEG - 1:
                aseg_s[0:1, :] = pr
                aseg_s[1:2, :] = pi
            pr, pi = _cmul(pr, pi, ab_re, ab_im)
        for t, first in enumerate((False, True)):
            bias_s[t] = jnp.where(_window_valid(bias_s.shape[1], WINDOW, first), 0.0, MASK_VALUE)

    @pl.when(i == 0)
    def _():
        carry_s[...] = jnp.zeros_like(carry_s)
        kprev_s[...] = jnp.zeros_like(kprev_s)
        vprev_s[...] = jnp.zeros_like(vprev_s)
        merged_s[...] = jnp.zeros_like(merged_s)
        x1p_s[...] = jnp.zeros_like(x1p_s)

    merged_prev, x1_prev = merged_s[...], x1p_s[...]

    k0, k1 = _replicate_heads(k_ref[...])
    v0, v1 = _replicate_heads(v_ref[...])
    qs, kreps, vreps, firsts = [], [], [], []
    for blk in range(MIX_TM // WINDOW):
        cur = slice(blk * WINDOW, (blk + 1) * WINDOW)
        if blk == 0:
            prev = [kprev_s[0], kprev_s[1], vprev_s[0], vprev_s[1]]
            firsts.append(i == 0)
        else:
            ps = slice((blk - 1) * WINDOW, blk * WINDOW)
            prev = [k0[ps], k1[ps], v0[ps], v1[ps]]
            firsts.append(False)
        qs.append(q_ref[cur, :])
        kreps.append([jnp.concatenate([prev[0], k0[cur]], axis=0), jnp.concatenate([prev[1], k1[cur]], axis=0)])
        vreps.append([jnp.concatenate([prev[2], v0[cur]], axis=0), jnp.concatenate([prev[3], v1[cur]], axis=0)])
    tail = slice(MIX_TM - WINDOW, MIX_TM)
    kprev_s[0], kprev_s[1] = k0[tail], k1[tail]
    vprev_s[0], vprev_s[1] = v0[tail], v1[tail]
    scores = _attn_scores(qs, kreps)

    half_in, half_st = SSM_W // 2, SSM_S // 2
    chunks = list(range(0, SSM_S, SCAN_LC))
    nblk = len(qs)
    blk_rows = scores.shape[0] // nblk
    u = u_ref[...]
    up = jnp.dot(perm_ref[...], u.astype(BF16), preferred_element_type=F32).astype(BF16)

    def expand_half(hh):
        uh = up[:, hh * half_in:(hh + 1) * half_in]
        bur_s[:, hh * half_st:(hh + 1) * half_st] = jnp.dot(uh, wbre_s[hh], preferred_element_type=F32)
        bui_s[:, hh * half_st:(hh + 1) * half_st] = jnp.dot(uh, wbim_s[hh], preferred_element_type=F32)

    def softmax_blocks(lo, hi):
        no_prev = (i == 0).astype(jnp.int32)
        bias = jnp.concatenate([bias_s[no_prev] if blk == 0 else bias_s[0] for blk in range(lo, hi)], axis=0)
        return _attn_softmax(scores[lo * blk_rows:hi * blk_rows], sinks_ref, WINDOW, firsts[lo:hi], bias)

    def scan_chunk(c0):
        cs = slice(c0, c0 + SCAN_LC)
        ar = jnp.broadcast_to(a1_s[0:1, cs], (SUBLANES, SCAN_LC))
        ai = jnp.broadcast_to(a1_s[1:2, cs], (SUBLANES, SCAN_LC))
        xr, xi = bur_s[0:SUBLANES, cs], bui_s[0:SUBLANES, cs]
        for j in range(1, SCAN_SEG):
            rs = slice(j * SUBLANES, (j + 1) * SUBLANES)
            xr, xi = (ar * xr - ai * xi + bur_s[rs, cs], ar * xi + ai * xr + bui_s[rs, cs])
            bur_s[rs, cs] = xr
            bui_s[rs, cs] = xi
        return xr

    mid = nblk // 2
    expand_half(0)
    probs_a = softmax_blocks(0, mid)
    expand_half(1)
    probs_b = softmax_blocks(mid, nblk)
    quarter = len(chunks) // 4
    scan_group = lambda g: [scan_chunk(c0) for c0 in chunks[g * quarter:(g + 1) * quarter]]
    started = scan_group(0)[-1]
    bits = lax.bitcast_convert_type(started[0:1, 0:1], jnp.uint32)
    zero = lax.bitcast_convert_type(lax.shift_right_logical(lax.shift_right_logical(bits, jnp.uint32(16)),
                                                            jnp.uint32(16)), F32).astype(BF16)
    x2_ref[...] = x1_prev + jnp.dot(merged_prev + zero, wout_ref[...], preferred_element_type=F32)
    attn = _attn_values(probs_a, vreps[0:mid], WINDOW)
    scan_group(1)
    attn += _attn_values(probs_b, vreps[mid:nblk], WINDOW)
    scan_group(2)
    attn_gated = _gated_attn(jnp.concatenate(attn, axis=0).astype(BF16), sga_ref[...], wao_ref)
    scan_group(3)

    last = MIX_TM - SUBLANES
    sr, si = aseg_s[0:1, :], aseg_s[1:2, :]
    cr, ci = carry_s[0:1, :], carry_s[1:2, :]
    for s in range(SUBLANES):
        cin_s[0, s:s + 1, :] = cr
        cin_s[1, s:s + 1, :] = ci
        pr, pi = _cmul(sr, si, cr, ci)
        cr = pr + bur_s[last + s:last + s + 1, :]
        ci = pi + bui_s[last + s:last + s + 1, :]
    carry_s[0:1, :] = cr
    carry_s[1:2, :] = ci
    pack = 2 * SUBLANES

    def fix_chunk(c0):
        cs = slice(c0, c0 + SCAN_LC)
        cinr = jnp.concatenate([cin_s[0, :, cs]] * 2, axis=0).astype(BF16)
        cini = jnp.concatenate([cin_s[1, :, cs]] * 2, axis=0).astype(BF16)
        col = (c0 // half_st) * SSM_S + c0 % half_st
        for r0 in range(0, MIX_TM, pack):
            rs = slice(r0, r0 + pack)
            pr, pi = apr_s[rs, cs], api_s[rs, cs]
            xb_s[rs, col:col + SCAN_LC] = bur_s[rs, cs].astype(BF16) + pr * cinr - pi * cini
            xb_s[rs, col + half_st:col + half_st + SCAN_LC] = bui_s[rs, cs].astype(BF16) + pr * cini + pi * cinr

    def project_half(hh):
        return jnp.dot(xb_s[:, hh * SSM_S:(hh + 1) * SSM_S], ccat_ref[hh], preferred_element_type=F32)

    per_half = len(chunks) // 2
    for c0 in chunks[:per_half]:
        fix_chunk(c0)
    yp0 = project_half(0)
    for c0 in chunks[per_half:]:
        fix_chunk(c0)
    yp = jnp.concatenate([yp0, project_half(1)], axis=1)
    y_ssm = jnp.dot(permt_ref[...], yp.astype(BF16), preferred_element_type=F32) + d_ref[...] * u

    merged_s[...] = _gated_sum(attn_gated, y_ssm, sgs_ref[...], wglu_ref, bglu_ref, wso_ref)
    x1p_s[...] = x1_ref[...]

    @pl.when(i == nt - 1)
    def _():
        sre_ref[...] = carry_s[0:1, :]
        sim_ref[...] = carry_s[1:2, :]


def _mix_prompt(fr, p, batch, seq):
    x1, q, k, v, u, sga, sgs = fr
    tm = MIX_TM
    nt = seq // tm
    row = lambda w: pl.BlockSpec((tm, w), lambda b, i: (b * nt + jnp.minimum(i, nt - 1), 0))
    lag = pl.BlockSpec((tm, D_MODEL), lambda b, i: (b * nt + jnp.maximum(i - 1, 0), 0))
    consts = [p['wao'], p['wglu'], p['bglu'], p['wso'], p['wout'],
              p['are'], p['aim'], p['ldt'], p['bbre'], p['bbim'], p['ccat'], p['dskip'], p['perm'], p['permt']]
    st_spec = pl.BlockSpec((None, 1, SSM_S), lambda b, i: (b, 0, 0))
    scratch = [pltpu.VMEM((2, MXU_DIM, SSM_S // 2), BF16), pltpu.VMEM((2, MXU_DIM, SSM_S // 2), BF16),
               pltpu.VMEM((tm, SSM_S), BF16), pltpu.VMEM((tm, SSM_S), BF16),
               pltpu.VMEM((SUBLANES, SSM_S), F32), pltpu.VMEM((SUBLANES, SSM_S), F32),
               pltpu.VMEM((SUBLANES, SSM_S), F32), pltpu.VMEM((2, SUBLANES, SSM_S), F32),
               pltpu.VMEM((tm, SSM_S), F32), pltpu.VMEM((tm, SSM_S), F32), pltpu.VMEM((tm, 2 * SSM_S), BF16),
               pltpu.VMEM((N_KV, WINDOW, LANES), BF16), pltpu.VMEM((N_KV, WINDOW, LANES), BF16),
               pltpu.VMEM((2, N_HEADS * WINDOW, 2 * WINDOW), F32),
               pltpu.VMEM((tm, D_MODEL), BF16), pltpu.VMEM((tm, D_MODEL), F32)]
    return pl.pallas_call(
        _mix_prompt_kernel,
        grid=(batch, nt + 1),
        in_specs=[row(D_MODEL), row(ATTN_W), row(KV_W), row(KV_W), row(SSM_W), row(D_MODEL), row(D_MODEL),
                  pl.BlockSpec(memory_space=pltpu.SMEM)] + [_const_spec(c.shape) for c in consts],
        out_specs=[lag, st_spec, st_spec],
        out_shape=[jax.ShapeDtypeStruct((batch * seq, D_MODEL), F32),
                   jax.ShapeDtypeStruct((batch, 1, SSM_S), F32),
                   jax.ShapeDtypeStruct((batch, 1, SSM_S), F32)],
        scratch_shapes=scratch,
        compiler_params=pltpu.CompilerParams(dimension_semantics=("arbitrary", "arbitrary"),
                                             vmem_limit_bytes=VMEM_LIMIT),
        name="mix_prompt",
    )(x1, q, k, v, u, sga, sgs, p['sinks'], *consts)


def _attn_sample_kernel(q_ref, kn_ref, vn_ref, ckt_ref, cvt_ref, sinks_ref, o_ref, nkt_ref, nvt_ref,
                        q8_s, knp_s, vnp_s):
    steps = q_ref.shape[0]
    shift = WINDOW - steps
    keep = lax.broadcasted_iota(jnp.int32, (HEAD_DIM, WINDOW), 1) < shift
    q8_s[...] = jnp.zeros_like(q8_s)
    knp_s[...] = jnp.zeros_like(knp_s)
    vnp_s[...] = jnp.zeros_like(vnp_s)
    qs, kreps, vreps = [], [], []
    for s in range(SAMPLE_BS):
        for t in range(steps):
            q8_s[s, t:t + 1, :] = q_ref[t, s:s + 1, :]
            knp_s[s, t:t + 1, :] = kn_ref[t, s:s + 1, :]
            vnp_s[s, t:t + 1, :] = vn_ref[t, s:s + 1, :]
        news = [knp_s[s].T, vnp_s[s].T]
        ends = [pltpu.roll(n, shift, axis=1) for n in news]
        krep, vrep = [], []
        for kv in range(N_KV):
            hs = slice(kv * HEAD_DIM, (kv + 1) * HEAD_DIM)
            for cache_ref, out_ref, new, end, reps in ((ckt_ref, nkt_ref, news[0], ends[0], krep),
                                                       (cvt_ref, nvt_ref, news[1], ends[1], vrep)):
                old = cache_ref[s, kv]
                out_ref[s, kv] = jnp.where(keep, pltpu.roll(old, shift, axis=1), end[hs])
                both = jnp.concatenate([old, new[hs]], axis=1).astype(BF16)
                reps.append(jnp.concatenate([both, both], axis=0))
        qs.append(q8_s[s])
        kreps.append(krep)
        vreps.append(vrep)
    probs = _attn_softmax(_attn_scores(qs, kreps, keys_on_lanes=True), sinks_ref, SUBLANES, [False] * SAMPLE_BS)
    outs = _attn_values(probs, vreps, SUBLANES, keys_on_lanes=True)
    for s in range(SAMPLE_BS):
        for t in range(steps):
            o_ref[t, s:s + 1, :] = outs[s][t:t + 1, :]


def _attn_sample(q, kn, vn, ckt, cvt, sinks, steps, nseq):
    bs = SAMPLE_BS
    blk3 = lambda w: pl.BlockSpec((steps, bs, w), lambda i: (0, i, 0))
    cache = pl.BlockSpec((bs, N_KV, HEAD_DIM, WINDOW), lambda i: (i, 0, 0, 0))
    cache_shape = jax.ShapeDtypeStruct((nseq, N_KV, HEAD_DIM, WINDOW), F32)
    return pl.pallas_call(
        _attn_sample_kernel,
        grid=(nseq // bs,),
        in_specs=[blk3(ATTN_W), blk3(KV_W), blk3(KV_W), cache, cache, pl.BlockSpec(memory_space=pltpu.SMEM)],
        out_specs=[blk3(ATTN_W), cache, cache],
        out_shape=[jax.ShapeDtypeStruct((steps, nseq, ATTN_W), F32), cache_shape, cache_shape],
        scratch_shapes=[pltpu.VMEM((bs, SUBLANES, ATTN_W), F32), pltpu.VMEM((bs, LANES, KV_W), F32),
                        pltpu.VMEM((bs, LANES, KV_W), F32)],
        compiler_params=pltpu.CompilerParams(dimension_semantics=("arbitrary",)),
        name="attn_sample",
    )(q.reshape(steps, nseq, ATTN_W), kn.reshape(steps, nseq, KV_W), vn.reshape(steps, nseq, KV_W),
      ckt, cvt, sinks)


def _mix_sample_kernel(x1_ref, attn_ref, u_ref, sga_ref, sgs_ref, x0r_ref, x0i_ref,
                       wao_ref, wglu_ref, bglu_ref, wso_ref, wout_ref,
                       are_ref, aim_ref, ldt_ref, bbre_ref, bbim_ref, ccat_ref, d_ref,
                       x2_ref, sre_ref, sim_ref,
                       wbre_s, wbim_s, bur_s, bui_s):
    nseq = x0r_ref.shape[0]
    steps = x1_ref.shape[0] // nseq
    ab_re, ab_im, f_re, f_im = _ssm_discretize(are_ref, aim_ref, ldt_ref)
    _build_wb(bbre_ref, bbim_ref, f_re, f_im, wbre_s, wbim_s)
    u = u_ref[...]
    _expand_state(u.astype(BF16), wbre_s, wbim_s, bur_s, bui_s)
    for c0 in range(0, SSM_S, LANES):
        cs = slice(c0, c0 + LANES)
        ar, ai = ab_re[:, cs], ab_im[:, cs]
        xr, xi = x0r_ref[:, cs], x0i_ref[:, cs]
        for t in range(steps):
            rs = slice(t * nseq, (t + 1) * nseq)
            nr = ar * xr - ai * xi + bur_s[rs, cs]
            ni = ar * xi + ai * xr + bui_s[rs, cs]
            bur_s[rs, cs] = nr
            bui_s[rs, cs] = ni
            xr, xi = nr, ni
        sre_ref[:, cs] = xr
        sim_ref[:, cs] = xi
    y_ssm = _project_state(bur_s, bui_s, ccat_ref) + d_ref[...] * u
    attn_gated = _gated_attn(attn_ref[...].astype(BF16), sga_ref[...], wao_ref)
    x2_ref[...] = _gated_merge(x1_ref[...], attn_gated, y_ssm, sgs_ref[...],
                               wglu_ref, bglu_ref, wso_ref, wout_ref)


def _mix_sample(x1, attn, u, sga, sgs, x0r, x0i, p):
    n, nseq = x1.shape[0], x0r.shape[0]
    args = [x1, attn, u, sga, sgs, x0r, x0i, p['wao'], p['wglu'], p['bglu'], p['wso'], p['wout'],
            p['are'], p['aim'], p['ldt'], p['bbre'], p['bbim'], p['ccat'], p['dskip']]
    return pl.pallas_call(
        _mix_sample_kernel,
        grid=(1,),
        in_specs=[_const_spec(a.shape) for a in args],
        out_specs=[pl.BlockSpec((n, D_MODEL), lambda i: (0, 0)), pl.BlockSpec((nseq, SSM_S), lambda i: (0, 0)),
                   pl.BlockSpec((nseq, SSM_S), lambda i: (0, 0))],
        out_shape=[jax.ShapeDtypeStruct((n, D_MODEL), F32),
                   jax.ShapeDtypeStruct((nseq, SSM_S), F32),
                   jax.ShapeDtypeStruct((nseq, SSM_S), F32)],
        scratch_shapes=[pltpu.VMEM((2, MXU_DIM, SSM_S // 2), BF16), pltpu.VMEM((2, MXU_DIM, SSM_S // 2), BF16),
                        pltpu.VMEM((n, SSM_S), F32), pltpu.VMEM((n, SSM_S), F32)],
        compiler_params=pltpu.CompilerParams(dimension_semantics=("arbitrary",),
                                             vmem_limit_bytes=VMEM_LIMIT),
        name="mix_sample",
    )(*args)


def _rope_tables(base_pos, off_pos):
    half = HEAD_DIM // 2
    lane = np.arange(LANES)
    inv = ROPE_THETA ** (-2.0 * (lane % half) / HEAD_DIM)
    first = (lane % HEAD_DIM) < half
    ab = np.asarray(base_pos, np.float64)[:, None] * inv[None, :]
    ao = np.asarray(off_pos, np.float64)[:, None] * inv[None, :]
    base = np.stack([np.cos(ab), np.sin(ab)], axis=1)
    co, so = np.cos(ao), np.sin(ao)
    lo = np.where(first, -1.0, 0.0)
    hi = np.where(first, 0.0, 1.0)
    off = np.stack([co, so, co * lo, so * lo, co * hi, so * hi], axis=0)
    return jnp.asarray(base, F32), jnp.asarray(off, F32)


def _block_diag_halves(m):
    g, r, c = m.shape
    hg = g // 2
    strip = jnp.transpose(m.reshape(2, hg, r, c), (0, 2, 1, 3)).reshape(2, 1, r, hg * c)
    full = jnp.broadcast_to(strip, (2, hg, r, hg * c)).reshape(2, hg * r, hg * c)
    mask = jnp.asarray(np.kron(np.eye(hg), np.ones((r, c))), m.dtype)
    return full * mask


def _layout_params(ffn1_norm, ffn1_w1, ffn1_w3, ffn1_w2, mix_norm, w_in, q_norm, k_norm, attn_sinks,
                   w_attn_out, ssm_a_re, ssm_a_im, ssm_log_dt, ssm_b_re, ssm_b_im, ssm_c_re, ssm_c_im,
                   ssm_d, w_glu, b_glu, w_ssm_out, w_out, ffn2_norm, ffn2_w1, ffn2_w3, ffn2_w2):
    l = 0
    row = lambda a: a.reshape(1, -1).astype(F32)
    head_ones = jnp.asarray(np.kron(np.eye(LANES // HEAD_DIM), np.ones((HEAD_DIM, HEAD_DIM))), F32)
    r = np.arange(MIX_TM)
    tok = (r % SUBLANES) * SCAN_SEG + r // SUBLANES
    perm = jnp.asarray(tok[:, None] == np.arange(MIX_TM)[None, :], BF16)
    c_cat = jnp.concatenate([_block_diag_halves(jnp.swapaxes(ssm_c_re[l], 1, 2)),
                             -_block_diag_halves(jnp.swapaxes(ssm_c_im[l], 1, 2))], axis=1)
    return dict(
        n1=row(ffn1_norm[l]), w1a=ffn1_w1[l].astype(BF16), w3a=ffn1_w3[l].astype(BF16), w2a=ffn1_w2[l].astype(BF16),
        nm=row(mix_norm[l]), win=w_in[l].astype(BF16),
        qn=row(jnp.tile(q_norm[l], LANES // HEAD_DIM)), kn=row(jnp.tile(k_norm[l], LANES // HEAD_DIM)),
        ones=head_ones.astype(BF16), sinks=attn_sinks[l].astype(F32),
        bglu=row(b_glu[l]),
        are=row(ssm_a_re[l]), aim=row(ssm_a_im[l]), ldt=row(jnp.repeat(ssm_log_dt[l], SSM_N)),
        bbre=_block_diag_halves(jnp.swapaxes(ssm_b_re[l], 1, 2).astype(F32)),
        bbim=_block_diag_halves(jnp.swapaxes(ssm_b_im[l], 1, 2).astype(F32)),
        ccat=c_cat.astype(BF16), dskip=row(ssm_d[l]), perm=perm, permt=perm.T,
        n2=row(ffn2_norm[l]),
    )


def kernel(x_prompt, x_sample, cache_k, cache_v, state_ssm_re, state_ssm_im, ffn1_norm, ffn1_w1, ffn1_w3, ffn1_w2, mix_norm, w_in, q_norm, k_norm, attn_sinks, w_attn_out, ssm_a_re, ssm_a_im, ssm_log_dt, ssm_b_re, ssm_b_im, ssm_c_re, ssm_c_im, ssm_d, w_glu, b_glu, w_ssm_out, w_out, ffn2_norm, ffn2_w1, ffn2_w3, ffn2_w2):
    assert ffn1_norm.shape[0] == 1, "single trunk layer"
    batch, seq, _ = x_prompt.shape
    nseq, steps, _ = x_sample.shape
    p = _layout_params(ffn1_norm, ffn1_w1, ffn1_w3, ffn1_w2, mix_norm, w_in, q_norm, k_norm, attn_sinks,
                       w_attn_out, ssm_a_re, ssm_a_im, ssm_log_dt, ssm_b_re, ssm_b_im, ssm_c_re, ssm_c_im,
                       ssm_d, w_glu, b_glu, w_ssm_out, w_out, ffn2_norm, ffn2_w1, ffn2_w3, ffn2_w2)

    tile0 = np.arange(batch * seq // FRONT_TM) * FRONT_TM
    tabs_p = _rope_tables(tile0 % seq, np.arange(FRONT_TM))
    later = dict(wao=w_attn_out, wglu=w_glu, wso=w_ssm_out, wout=w_out, w1b=ffn2_w1, w3b=ffn2_w3, w2b=ffn2_w2)
    fr = _front(x_prompt.reshape(batch * seq, D_MODEL), tabs_p, p, BF16, cast=list(later.values()))
    p.update(zip(later, fr[7:]))
    fr = fr[:7]
    x2p, re_p, im_p = _mix_prompt(fr, p, batch, seq)
    y_prompt = _ffn(x2p, p).reshape(batch, seq, D_MODEL)
    tail_kv = lambda a: a.reshape(batch, seq, KV_W)[:, seq - WINDOW:].reshape(batch, WINDOW, N_KV, HEAD_DIM)
    k_p, v_p = tail_kv(fr[2]), tail_kv(fr[3])

    xs = jnp.swapaxes(x_sample, 0, 1).reshape(steps * nseq, D_MODEL)
    tile0 = np.arange(steps * nseq // FRONT_TM) * FRONT_TM
    tabs_s = _rope_tables(PAST_LEN + tile0 // nseq, np.arange(FRONT_TM) // nseq)
    x1s, qs, ks, vs, us, sgas, sgss = _front(xs, tabs_s, p, F32)
    to_lanes = lambda c: jnp.transpose(c[0], (0, 2, 3, 1))
    attn_s, nkt, nvt = _attn_sample(qs, ks, vs, to_lanes(cache_k), to_lanes(cache_v), p['sinks'], steps, nseq)
    x2s, re_s, im_s = _mix_sample(x1s, attn_s.reshape(steps * nseq, ATTN_W), us, sgas, sgss,
                                  state_ssm_re[0].reshape(nseq, SSM_S), state_ssm_im[0].reshape(nseq, SSM_S), p)
    y_sample = jnp.swapaxes(_ffn(x2s, p).reshape(steps, nseq, D_MODEL), 0, 1)
    k_s, v_s = jnp.transpose(nkt, (0, 3, 1, 2)), jnp.transpose(nvt, (0, 3, 1, 2))

    st = lambda a, n: a.reshape(1, n, SSM_G, SSM_N)
    return (y_prompt, y_sample, k_p[None], v_p[None], st(re_p, batch), st(im_p, batch),
            k_s[None], v_s[None], st(re_s, nseq), st(im_s, nseq))
```

```python
import functools

import jax
import jax.numpy as jnp
import numpy as np
from jax import lax
from jax.experimental import pallas as pl
from jax.experimental.pallas import tpu as pltpu

F32 = jnp.float32
BF16 = jnp.bfloat16

D_MODEL = 1024
HEAD_DIM = 64
N_HEADS = 8
N_KV = 2
WINDOW = 128
ATTN_W = N_HEADS * HEAD_DIM
KV_W = N_KV * HEAD_DIM
SSM_W = 512
SSM_G = 32
SSM_CH = 16
SSM_N = 64
SSM_S = SSM_G * SSM_N
D_FF = 2816
IN_COLS = ATTN_W + 2 * KV_W + SSM_W + 2 * D_MODEL
ROPE_THETA = 10000.0
RMS_EPS = 1e-6
MASK_VALUE = -1e30
PAST_LEN = 16384

LANES = 128
SUBLANES = 8
MXU_DIM = 256

FRONT_TM = 512
FFN_TM = 1024
MIX_TM = 512
SCAN_SEG = MIX_TM // SUBLANES
FF_CHUNK = 256
SCAN_LC = 512
SAMPLE_BS = 16
VMEM_LIMIT = 58 * 1024 * 1024


def _const_spec(shape):
    nd = len(shape)
    return pl.BlockSpec(shape, lambda *_: (0,) * nd, pipeline_mode=pl.Buffered(1))


def _rms(x, w):
    ms = jnp.mean(x * x, axis=-1, keepdims=True)
    return x * lax.rsqrt(ms + RMS_EPS) * w


def _swiglu_residual(x, nw_ref, w1_ref, w3_ref, w2_ref):
    h = _rms(x, nw_ref[...]).astype(BF16)
    acc = None
    for c0 in range(0, D_FF, FF_CHUNK):
        cs = slice(c0, min(c0 + FF_CHUNK, D_FF))
        a = jnp.dot(h, w1_ref[:, cs], preferred_element_type=F32)
        b = jnp.dot(h, w3_ref[:, cs], preferred_element_type=F32)
        g = (a * jax.nn.sigmoid(a) * b).astype(BF16)
        t = jnp.dot(g, w2_ref[cs, :], preferred_element_type=F32)
        acc = t if acc is None else acc + t
    return x + 0.5 * acc


def _head_norm_rope(t, ones_ref, g_ref, cos, sinlo, sinhi, scale):
    ssum = jnp.dot((t * t).astype(BF16), ones_ref[...], preferred_element_type=F32)
    tn = t * lax.rsqrt(ssum * (1.0 / HEAD_DIM) + RMS_EPS) * g_ref[...]
    up = pltpu.roll(tn, HEAD_DIM // 2, axis=1)
    dn = pltpu.roll(tn, LANES - HEAD_DIM // 2, axis=1)
    r = tn * cos + dn * sinlo + up * sinhi
    return r * scale if scale != 1.0 else r


def _front_kernel(n_cast, x_ref, base_ref, off_ref,
                  n1_ref, w1_ref, w3_ref, w2_ref, nm_ref, win_ref, qn_ref, kn_ref, ones_ref, *refs):
    cast_in, refs = refs[:n_cast], refs[n_cast:]
    x1_ref, q_ref, k_ref, v_ref, u_ref, sga_ref, sgs_ref = refs[:7]
    for src, dst in zip(cast_in, refs[7:]):
        dst[...] = src[...].astype(BF16)
    x1 = _swiglu_residual(x_ref[...], n1_ref, w1_ref, w3_ref, w2_ref)
    x1_ref[...] = x1
    h = _rms(x1, nm_ref[...]).astype(BF16)
    cb, sb = base_ref[0:1, :], base_ref[1:2, :]
    cos = cb * off_ref[0] - sb * off_ref[1]
    sinlo = sb * off_ref[2] + cb * off_ref[3]
    sinhi = sb * off_ref[4] + cb * off_ref[5]
    o = ATTN_W + 2 * KV_W + SSM_W
    qkvu = jnp.dot(h, win_ref[:, 0:o], preferred_element_type=F32)
    for j in range(ATTN_W // LANES):
        q_ref[:, j * LANES:(j + 1) * LANES] = _head_norm_rope(
            qkvu[:, j * LANES:(j + 1) * LANES], ones_ref, qn_ref, cos, sinlo, sinhi,
            HEAD_DIM ** -0.5).astype(q_ref.dtype)
    k_ref[...] = _head_norm_rope(qkvu[:, ATTN_W:ATTN_W + KV_W], ones_ref, kn_ref, cos, sinlo, sinhi, 1.0)
    v_ref[...] = qkvu[:, ATTN_W + KV_W:ATTN_W + 2 * KV_W]
    u_ref[...] = qkvu[:, ATTN_W + 2 * KV_W:o]
    ga = jnp.dot(h, win_ref[:, o:o + D_MODEL], preferred_element_type=F32)
    sga_ref[...] = jax.nn.sigmoid(ga).astype(BF16)
    o += D_MODEL
    gs = jnp.dot(h, win_ref[:, o:o + D_MODEL], preferred_element_type=F32)
    sgs_ref[...] = jax.nn.sigmoid(gs).astype(BF16)


def _cast_block_rows(rows, steps):
    for blocks in range(steps, 0, -1):
        if rows % blocks == 0 and (rows // blocks) % (2 * SUBLANES) == 0:
            return rows // blocks
    raise ValueError(rows)


def _front(x, tabs, p, q_dtype, cast=()):
    n = x.shape[0]
    tm = FRONT_TM
    steps = n // tm
    base_tab, off_tab = tabs
    row = lambda w: pl.BlockSpec((tm, w), lambda i: (i, 0))
    consts = [p['n1'], p['w1a'], p['w3a'], p['w2a'], p['nm'], p['win'], p['qn'], p['kn'], p['ones']]
    outs = [(D_MODEL, F32), (ATTN_W, q_dtype), (KV_W, F32), (KV_W, F32), (SSM_W, F32),
            (D_MODEL, BF16), (D_MODEL, BF16)]
    cast_in, cast_out = [], []
    for w in cast:
        _, rows, cols = w.shape
        rb = _cast_block_rows(rows, steps)
        last = rows // rb - 1
        cast_in.append(pl.BlockSpec((None, rb, cols), lambda i, last=last: (0, jnp.minimum(i, last), 0)))
        cast_out.append(pl.BlockSpec((rb, cols), lambda i, last=last: (jnp.minimum(i, last), 0)))
    return pl.pallas_call(
        functools.partial(_front_kernel, len(cast)),
        grid=(steps,),
        in_specs=[row(D_MODEL), pl.BlockSpec((None, 2, LANES), lambda i: (i, 0, 0)), _const_spec(off_tab.shape)]
        + [_const_spec(c.shape) for c in consts] + cast_in,
        out_specs=[row(w) for w, _ in outs] + cast_out,
        out_shape=[jax.ShapeDtypeStruct((n, w), dt) for w, dt in outs]
        + [jax.ShapeDtypeStruct(w.shape[1:], BF16) for w in cast],
        compiler_params=pltpu.CompilerParams(dimension_semantics=("arbitrary",),
                                             vmem_limit_bytes=VMEM_LIMIT),
        name="front",
    )(x, base_tab, off_tab, *consts, *cast)


def _ffn_kernel(tiles_per_seq, x_ref, xl_ref, n_ref, w1_ref, w3_ref, w2_ref, y_ref):
    x = x_ref[...]
    if tiles_per_seq:
        keep = x.shape[0] - xl_ref.shape[0]
        is_last = pl.program_id(0) % tiles_per_seq == tiles_per_seq - 1
        x = jnp.concatenate([x[:keep], jnp.where(is_last, xl_ref[...], x[keep:])], axis=0)
    y_ref[...] = _swiglu_residual(x, n_ref, w1_ref, w3_ref, w2_ref)


def _ffn(x, p, x_last=None, seq=None):
    n = x.shape[0]
    tm = min(FFN_TM, n)
    weights = [p['w1b'], p['w3b'], p['w2b']]
    if x_last is None:
        tiles_per_seq, x_last = 0, x
        last_spec = pl.BlockSpec((SUBLANES, D_MODEL), lambda i: (0, 0))
    else:
        tiles_per_seq = seq // tm
        last_spec = pl.BlockSpec((MIX_TM, D_MODEL), lambda i: (i // tiles_per_seq, 0))
    return pl.pallas_call(
        functools.partial(_ffn_kernel, tiles_per_seq),
        grid=(n // tm,),
        in_specs=[pl.BlockSpec((tm, D_MODEL), lambda i: (i, 0)), last_spec, _const_spec(p['n2'].shape)]
        + [_const_spec(w.shape) for w in weights],
        out_specs=pl.BlockSpec((tm, D_MODEL), lambda i: (i, 0)),
        out_shape=jax.ShapeDtypeStruct((n, D_MODEL), F32),
        compiler_params=pltpu.CompilerParams(dimension_semantics=("arbitrary",),
                                             vmem_limit_bytes=VMEM_LIMIT),
        name="ffn",
    )(x, x_last, p['n2'], *weights)


def _cmul(ar, ai, br, bi):
    return ar * br - ai * bi, ar * bi + ai * br


def _ssm_discretize(are_ref, aim_ref, ldt_ref):
    a_re, a_im = are_ref[...], aim_ref[...]
    dt = jnp.exp(ldt_ref[...])
    mag = jnp.exp(dt * a_re)
    ab_re = mag * jnp.cos(dt * a_im)
    ab_im = mag * jnp.sin(dt * a_im)
    den = a_re * a_re + a_im * a_im
    nr, ni = ab_re - 1.0, ab_im
    f_re = (nr * a_re + ni * a_im) / den
    f_im = (ni * a_re - nr * a_im) / den
    return ab_re, ab_im, f_re, f_im


def _build_wb(bbre_ref, bbim_ref, f_re, f_im, wbre_s, wbim_s):
    half = SSM_S // 2
    for hh in range(2):
        fr, fi = f_re[:, hh * half:(hh + 1) * half], f_im[:, hh * half:(hh + 1) * half]
        br, bi = bbre_ref[hh], bbim_ref[hh]
        wbre_s[hh] = (br * fr - bi * fi).astype(BF16)
        wbim_s[hh] = (br * fi + bi * fr).astype(BF16)


def _ssm_weights_kernel(are_ref, aim_ref, ldt_ref, bbre_ref, bbim_ref, wbre_ref, wbim_ref):
    _, _, f_re, f_im = _ssm_discretize(are_ref, aim_ref, ldt_ref)
    _build_wb(bbre_ref, bbim_ref, f_re, f_im, wbre_ref, wbim_ref)


def _ssm_weights(p):
    shape = jax.ShapeDtypeStruct(p['bbre'].shape, BF16)
    return pl.pallas_call(_ssm_weights_kernel, out_shape=[shape, shape], name="ssm_weights")(
        p['are'], p['aim'], p['ldt'], p['bbre'], p['bbim'])


def _expand_state(ub, wbre_s, wbim_s, bur_s, bui_s):
    half_in, half_st = SSM_W // 2, SSM_S // 2
    for hh in range(2):
        uh = ub[:, hh * half_in:(hh + 1) * half_in]
        bur_s[:, hh * half_st:(hh + 1) * half_st] = jnp.dot(uh, wbre_s[hh], preferred_element_type=F32)
        bui_s[:, hh * half_st:(hh + 1) * half_st] = jnp.dot(uh, wbim_s[hh], preferred_element_type=F32)


def _project_state(xr_s, xi_s, ccat_ref):
    half_st = SSM_S // 2
    outs = []
    for hh in range(2):
        sl = slice(hh * half_st, (hh + 1) * half_st)
        xcat = jnp.concatenate([xr_s[:, sl].astype(BF16), xi_s[:, sl].astype(BF16)], axis=1)
        outs.append(jnp.dot(xcat, ccat_ref[hh], preferred_element_type=F32))
    return jnp.concatenate(outs, axis=1)


def _gated_attn(attn_b, sga, wao_ref):
    return sga.astype(F32) * jnp.dot(attn_b, wao_ref[...], preferred_element_type=F32)


def _gated_sum(attn_gated, y_ssm, sgs, wglu_ref, bglu_ref, wso_ref):
    z = jax.nn.gelu(y_ssm, approximate=True)
    gl = jnp.dot(z.astype(BF16), wglu_ref[...], preferred_element_type=F32) + bglu_ref[...]
    ssm = (z * jax.nn.sigmoid(gl)).astype(BF16)
    merged = attn_gated + sgs.astype(F32) * jnp.dot(ssm, wso_ref[...], preferred_element_type=F32)
    return merged.astype(BF16)


def _gated_merge(x1, attn_gated, y_ssm, sgs, wglu_ref, bglu_ref, wso_ref, wout_ref):
    merged = _gated_sum(attn_gated, y_ssm, sgs, wglu_ref, bglu_ref, wso_ref)
    return x1 + jnp.dot(merged, wout_ref[...], preferred_element_type=F32)


def _replicate_heads(kv):
    sw = pltpu.roll(kv, HEAD_DIM, axis=1)
    lo = lax.broadcasted_iota(jnp.int32, kv.shape, 1) < HEAD_DIM
    return jnp.where(lo, kv, sw).astype(BF16), jnp.where(lo, sw, kv).astype(BF16)


def _attn_scores(qs, kreps, keys_on_lanes=False):
    qb = qs[0].shape[0]
    lo = lax.broadcasted_iota(jnp.int32, (qb, LANES), 1) < HEAD_DIM
    dims = (((1,), (0,)), ((), ())) if keys_on_lanes else (((1,), (1,)), ((), ()))
    scores = []
    for q, krep in zip(qs, kreps):
        if q.dtype != BF16 or qb % (2 * SUBLANES):
            q = q.astype(F32)
        zero = jnp.zeros((), q.dtype)
        for kv in range(N_KV):
            parts = []
            for jj in range(2):
                t = q[:, (2 * kv + jj) * LANES:(2 * kv + jj + 1) * LANES]
                parts += [jnp.where(lo, t, zero), jnp.where(lo, zero, t)]
            stacked = jnp.concatenate(parts, axis=0).astype(BF16)
            scores.append(lax.dot_general(stacked, krep[kv], dims, preferred_element_type=F32))
    return jnp.concatenate(scores, axis=0)


def _window_valid(rows, qb, first):
    qi = lax.broadcasted_iota(jnp.int32, (rows, 2 * WINDOW), 0) & (qb - 1)
    sj = lax.broadcasted_iota(jnp.int32, (rows, 2 * WINDOW), 1)
    band = (sj <= WINDOW + qi) & (sj > qi)
    return band if first is False else band & ((sj >= WINDOW) | jnp.logical_not(first))


def _attn_softmax(s, sinks_ref, qb, firsts, bias=None):
    per_block = s.shape[0] // len(firsts)
    sink_parts = [jnp.full((qb, 1), sinks_ref[h], F32) for h in range(N_HEADS)]
    sink = jnp.concatenate(sink_parts * len(firsts), axis=0)
    if bias is None:
        valid = jnp.concatenate([_window_valid(per_block, qb, f) for f in firsts], axis=0)
        s = jnp.where(valid, s, MASK_VALUE)
    else:
        s = s + bias
    m = jnp.maximum(jnp.max(s, axis=-1, keepdims=True), sink)
    e = jnp.exp(s - m)
    denom = jnp.sum(e, axis=-1, keepdims=True) + jnp.exp(sink - m)
    return (e * (1.0 / denom)).astype(BF16)


def _attn_values(probs, vreps, qb, keys_on_lanes=False):
    lo = lax.broadcasted_iota(jnp.int32, (qb, LANES), 1) < HEAD_DIM
    rows = 2 * (N_HEADS // N_KV // 2) * qb
    dims = (((1,), (1,)), ((), ())) if keys_on_lanes else (((1,), (0,)), ((), ()))
    outs = []
    for b, vrep in enumerate(vreps):
        tiles = []
        for kv in range(N_KV):
            r0 = (b * N_KV + kv) * rows
            o = lax.dot_general(probs[r0:r0 + rows], vrep[kv], dims, preferred_element_type=F32)
            for jj in range(2):
                oa = o[(2 * jj) * qb:(2 * jj + 1) * qb]
                ob = o[(2 * jj + 1) * qb:(2 * jj + 2) * qb]
                tiles.append(jnp.where(lo, oa, ob))
        outs.append(jnp.concatenate(tiles, axis=1))
    return outs


def _mix_prompt_kernel(x1_ref, q_ref, k_ref, v_ref, u_ref, sga_ref, sgs_ref, sinks_ref,
                       wao_ref, wglu_ref, bglu_ref, wso_ref, wout_ref,
                       are_ref, aim_ref, ldt_ref, wbre_s, wbim_s, ccat_ref, d_ref, perm_ref, permt_ref,
                       x2_ref, x2_last_ref, sre_ref, sim_ref,
                       apr_s, api_s, a1_s, aseg_s, carry_s, cin_s,
                       bur_s, bui_s, xb_s, kprev_s, vprev_s, bias_s, merged_s, x1p_s):
    b, i = pl.program_id(0), pl.program_id(1)
    nt = pl.num_programs(1)

    @pl.when((b == 0) & (i == 0))
    def _():
        ab_re, ab_im, _, _ = _ssm_discretize(are_ref, aim_ref, ldt_ref)
        a1_s[0:1, :] = ab_re
        a1_s[1:2, :] = ab_im
        pr, pi = ab_re, ab_im
        held = None
        for j in range(SCAN_SEG):
            rows = [jnp.broadcast_to(pr, (SUBLANES, SSM_S)), jnp.broadcast_to(pi, (SUBLANES, SSM_S))]
            if j % 2 == 0:
                held = rows
            else:
                rs = slice((j - 1) * SUBLANES, (j + 1) * SUBLANES)
                apr_s[rs, :] = jnp.concatenate([held[0], rows[0]], axis=0).astype(BF16)
                api_s[rs, :] = jnp.concatenate([held[1], rows[1]], axis=0).astype(BF16)
            if j == SCAN_SEG - 1:
                aseg_s[0:1, :] = pr
                aseg_s[1:2, :] = pi
            pr, pi = _cmul(pr, pi, ab_re, ab_im)
        for t, first in enumerate((False, True)):
            bias_s[t] = jnp.where(_window_valid(bias_s.shape[1], WINDOW, first), 0.0, MASK_VALUE)

    @pl.when(i == 0)
    def _():
        carry_s[...] = jnp.zeros_like(carry_s)
        kprev_s[...] = jnp.zeros_like(kprev_s)
        vprev_s[...] = jnp.zeros_like(vprev_s)
        merged_s[...] = jnp.zeros_like(merged_s)
        x1p_s[...] = jnp.zeros_like(x1p_s)

    merged_prev, x1_prev = merged_s[...], x1p_s[...]

    k0, k1 = _replicate_heads(k_ref[...])
    v0, v1 = _replicate_heads(v_ref[...])
    qs, kreps, vreps, firsts = [], [], [], []
    for blk in range(MIX_TM // WINDOW):
        cur = slice(blk * WINDOW, (blk + 1) * WINDOW)
        if blk == 0:
            prev = [kprev_s[0], kprev_s[1], vprev_s[0], vprev_s[1]]
            firsts.append(i == 0)
        else:
            ps = slice((blk - 1) * WINDOW, blk * WINDOW)
            prev = [k0[ps], k1[ps], v0[ps], v1[ps]]
            firsts.append(False)
        qs.append(q_ref[cur, :])
        kreps.append([jnp.concatenate([prev[0], k0[cur]], axis=0), jnp.concatenate([prev[1], k1[cur]], axis=0)])
        vreps.append([jnp.concatenate([prev[2], v0[cur]], axis=0), jnp.concatenate([prev[3], v1[cur]], axis=0)])
    tail = slice(MIX_TM - WINDOW, MIX_TM)
    kprev_s[0], kprev_s[1] = k0[tail], k1[tail]
    vprev_s[0], vprev_s[1] = v0[tail], v1[tail]
    scores = _attn_scores(qs, kreps)

    half_in, half_st = SSM_W // 2, SSM_S // 2
    chunks = list(range(0, SSM_S, SCAN_LC))
    nblk = len(qs)
    blk_rows = scores.shape[0] // nblk
    u = u_ref[...]
    up = jnp.dot(perm_ref[...], u.astype(BF16), preferred_element_type=F32).astype(BF16)

    def expand_half(hh):
        uh = up[:, hh * half_in:(hh + 1) * half_in]
        bur_s[:, hh * half_st:(hh + 1) * half_st] = jnp.dot(uh, wbre_s[hh], preferred_element_type=F32)
        bui_s[:, hh * half_st:(hh + 1) * half_st] = jnp.dot(uh, wbim_s[hh], preferred_element_type=F32)

    def softmax_blocks(lo, hi):
        no_prev = (i == 0).astype(jnp.int32)
        bias = jnp.concatenate([bias_s[no_prev] if blk == 0 else bias_s[0] for blk in range(lo, hi)], axis=0)
        return _attn_softmax(scores[lo * blk_rows:hi * blk_rows], sinks_ref, WINDOW, firsts[lo:hi], bias)

    def scan_chunk(c0):
        cs = slice(c0, c0 + SCAN_LC)
        ar = jnp.broadcast_to(a1_s[0:1, cs], (SUBLANES, SCAN_LC))
        ai = jnp.broadcast_to(a1_s[1:2, cs], (SUBLANES, SCAN_LC))
        xr, xi = bur_s[0:SUBLANES, cs], bui_s[0:SUBLANES, cs]
        for j in range(1, SCAN_SEG):
            rs = slice(j * SUBLANES, (j + 1) * SUBLANES)
            xr, xi = (ar * xr - ai * xi + bur_s[rs, cs], ar * xi + ai * xr + bui_s[rs, cs])
            bur_s[rs, cs] = xr
            bui_s[rs, cs] = xi
        return xr

    mid = nblk // 2
    expand_half(0)
    probs_a = softmax_blocks(0, mid)
    expand_half(1)
    probs_b = softmax_blocks(mid, nblk)
    quarter = len(chunks) // 4
    scan_group = lambda g: [scan_chunk(c0) for c0 in chunks[g * quarter:(g + 1) * quarter]]
    started = scan_group(0)[-1]
    bits = lax.bitcast_convert_type(started[0:1, 0:1], jnp.uint32)
    zero = lax.bitcast_convert_type(lax.shift_right_logical(lax.shift_right_logical(bits, jnp.uint32(16)),
                                                            jnp.uint32(16)), F32).astype(BF16)
    x2_ref[...] = x1_prev + jnp.dot(merged_prev + zero, wout_ref[...], preferred_element_type=F32)
    attn = _attn_values(probs_a, vreps[0:mid], WINDOW)
    scan_group(1)
    attn += _attn_values(probs_b, vreps[mid:nblk], WINDOW)
    scan_group(2)
    attn_gated = _gated_attn(jnp.concatenate(attn, axis=0).astype(BF16), sga_ref[...], wao_ref)
    scan_group(3)

    last = MIX_TM - SUBLANES
    sr, si = aseg_s[0:1, :], aseg_s[1:2, :]
    cr, ci = carry_s[0:1, :], carry_s[1:2, :]
    for s in range(SUBLANES):
        cin_s[0, s:s + 1, :] = cr
        cin_s[1, s:s + 1, :] = ci
        pr, pi = _cmul(sr, si, cr, ci)
        cr = pr + bur_s[last + s:last + s + 1, :]
        ci = pi + bui_s[last + s:last + s + 1, :]
    carry_s[0:1, :] = cr
    carry_s[1:2, :] = ci
    pack = 2 * SUBLANES

    def fix_chunk(c0):
        cs = slice(c0, c0 + SCAN_LC)
        cinr = jnp.concatenate([cin_s[0, :, cs]] * 2, axis=0).astype(BF16)
        cini = jnp.concatenate([cin_s[1, :, cs]] * 2, axis=0).astype(BF16)
        col = (c0 // half_st) * SSM_S + c0 % half_st
        for r0 in range(0, MIX_TM, pack):
            rs = slice(r0, r0 + pack)
            pr, pi = apr_s[rs, cs], api_s[rs, cs]
            xb_s[rs, col:col + SCAN_LC] = bur_s[rs, cs].astype(BF16) + pr * cinr - pi * cini
            xb_s[rs, col + half_st:col + half_st + SCAN_LC] = bui_s[rs, cs].astype(BF16) + pr * cini + pi * cinr

    def project_half(hh):
        return jnp.dot(xb_s[:, hh * SSM_S:(hh + 1) * SSM_S], ccat_ref[hh], preferred_element_type=F32)

    per_half = len(chunks) // 2
    for c0 in chunks[:per_half]:
        fix_chunk(c0)
    yp0 = project_half(0)
    for c0 in chunks[per_half:]:
        fix_chunk(c0)
    yp = jnp.concatenate([yp0, project_half(1)], axis=1)
    y_ssm = jnp.dot(permt_ref[...], yp.astype(BF16), preferred_element_type=F32) + d_ref[...] * u

    merged = _gated_sum(attn_gated, y_ssm, sgs_ref[...], wglu_ref, bglu_ref, wso_ref)
    merged_s[...] = merged
    x1p_s[...] = x1_ref[...]

    @pl.when(i == nt - 1)
    def _():
        x2_last_ref[...] = x1_ref[...] + jnp.dot(merged, wout_ref[...], preferred_element_type=F32)
        sre_ref[...] = carry_s[0:1, :]
        sim_ref[...] = carry_s[1:2, :]


def _mix_prompt(fr, p, batch, seq):
    x1, q, k, v, u, sga, sgs = fr
    tm = MIX_TM
    nt = seq // tm
    row = lambda w: pl.BlockSpec((tm, w), lambda b, i: (b * nt + i, 0))
    lag = pl.BlockSpec((tm, D_MODEL), lambda b, i: (b * nt + (i + nt - 1) % nt, 0))
    last = pl.BlockSpec((tm, D_MODEL), lambda b, i: (b, 0))
    consts = [p['wao'], p['wglu'], p['bglu'], p['wso'], p['wout'],
              p['are'], p['aim'], p['ldt'], p['wbre'], p['wbim'], p['ccat'], p['dskip'], p['perm'], p['permt']]
    st_spec = pl.BlockSpec((None, 1, SSM_S), lambda b, i: (b, 0, 0))
    scratch = [pltpu.VMEM((tm, SSM_S), BF16), pltpu.VMEM((tm, SSM_S), BF16),
               pltpu.VMEM((SUBLANES, SSM_S), F32), pltpu.VMEM((SUBLANES, SSM_S), F32),
               pltpu.VMEM((SUBLANES, SSM_S), F32), pltpu.VMEM((2, SUBLANES, SSM_S), F32),
               pltpu.VMEM((tm, SSM_S), F32), pltpu.VMEM((tm, SSM_S), F32), pltpu.VMEM((tm, 2 * SSM_S), BF16),
               pltpu.VMEM((N_KV, WINDOW, LANES), BF16), pltpu.VMEM((N_KV, WINDOW, LANES), BF16),
               pltpu.VMEM((2, N_HEADS * WINDOW, 2 * WINDOW), F32),
               pltpu.VMEM((tm, D_MODEL), BF16), pltpu.VMEM((tm, D_MODEL), F32)]
    return pl.pallas_call(
        _mix_prompt_kernel,
        grid=(batch, nt),
        in_specs=[row(D_MODEL), row(ATTN_W), row(KV_W), row(KV_W), row(SSM_W), row(D_MODEL), row(D_MODEL),
                  pl.BlockSpec(memory_space=pltpu.SMEM)] + [_const_spec(c.shape) for c in consts],
        out_specs=[lag, last, st_spec, st_spec],
        out_shape=[jax.ShapeDtypeStruct((batch * seq, D_MODEL), F32),
                   jax.ShapeDtypeStruct((batch * tm, D_MODEL), F32),
                   jax.ShapeDtypeStruct((batch, 1, SSM_S), F32),
                   jax.ShapeDtypeStruct((batch, 1, SSM_S), F32)],
        scratch_shapes=scratch,
        compiler_params=pltpu.CompilerParams(dimension_semantics=("arbitrary", "arbitrary"),
                                             vmem_limit_bytes=VMEM_LIMIT),
        name="mix_prompt",
    )(x1, q, k, v, u, sga, sgs, p['sinks'], *consts)


def _attn_sample_kernel(q_ref, kn_ref, vn_ref, ckt_ref, cvt_ref, sinks_ref, o_ref, nkt_ref, nvt_ref,
                        q8_s, knp_s, vnp_s):
    steps = q_ref.shape[0]
    shift = WINDOW - steps
    keep = lax.broadcasted_iota(jnp.int32, (HEAD_DIM, WINDOW), 1) < shift
    q8_s[...] = jnp.zeros_like(q8_s)
    knp_s[...] = jnp.zeros_like(knp_s)
    vnp_s[...] = jnp.zeros_like(vnp_s)
    qs, kreps, vreps = [], [], []
    for s in range(SAMPLE_BS):
        for t in range(steps):
            q8_s[s, t:t + 1, :] = q_ref[t, s:s + 1, :]
            knp_s[s, t:t + 1, :] = kn_ref[t, s:s + 1, :]
            vnp_s[s, t:t + 1, :] = vn_ref[t, s:s + 1, :]
        news = [knp_s[s].T, vnp_s[s].T]
        ends = [pltpu.roll(n, shift, axis=1) for n in news]
        krep, vrep = [], []
        for kv in range(N_KV):
            hs = slice(kv * HEAD_DIM, (kv + 1) * HEAD_DIM)
            for cache_ref, out_ref, new, end, reps in ((ckt_ref, nkt_ref, news[0], ends[0], krep),
                                                       (cvt_ref, nvt_ref, news[1], ends[1], vrep)):
                old = cache_ref[s, kv]
                out_ref[s, kv] = jnp.where(keep, pltpu.roll(old, shift, axis=1), end[hs])
                both = jnp.concatenate([old, new[hs]], axis=1).astype(BF16)
                reps.append(jnp.concatenate([both, both], axis=0))
        qs.append(q8_s[s])
        kreps.append(krep)
        vreps.append(vrep)
    probs = _attn_softmax(_attn_scores(qs, kreps, keys_on_lanes=True), sinks_ref, SUBLANES, [False] * SAMPLE_BS)
    outs = _attn_values(probs, vreps, SUBLANES, keys_on_lanes=True)
    for s in range(SAMPLE_BS):
        for t in range(steps):
            o_ref[t, s:s + 1, :] = outs[s][t:t + 1, :]


def _attn_sample(q, kn, vn, ckt, cvt, sinks, steps, nseq):
    bs = SAMPLE_BS
    blk3 = lambda w: pl.BlockSpec((steps, bs, w), lambda i: (0, i, 0))
    cache = pl.BlockSpec((bs, N_KV, HEAD_DIM, WINDOW), lambda i: (i, 0, 0, 0))
    cache_shape = jax.ShapeDtypeStruct((nseq, N_KV, HEAD_DIM, WINDOW), F32)
    return pl.pallas_call(
        _attn_sample_kernel,
        grid=(nseq // bs,),
        in_specs=[blk3(ATTN_W), blk3(KV_W), blk3(KV_W), cache, cache, pl.BlockSpec(memory_space=pltpu.SMEM)],
        out_specs=[blk3(ATTN_W), cache, cache],
        out_shape=[jax.ShapeDtypeStruct((steps, nseq, ATTN_W), F32), cache_shape, cache_shape],
        scratch_shapes=[pltpu.VMEM((bs, SUBLANES, ATTN_W), F32), pltpu.VMEM((bs, LANES, KV_W), F32),
                        pltpu.VMEM((bs, LANES, KV_W), F32)],
        compiler_params=pltpu.CompilerParams(dimension_semantics=("arbitrary",)),
        name="attn_sample",
    )(q.reshape(steps, nseq, ATTN_W), kn.reshape(steps, nseq, KV_W), vn.reshape(steps, nseq, KV_W),
      ckt, cvt, sinks)


def _mix_sample_kernel(x1_ref, attn_ref, u_ref, sga_ref, sgs_ref, x0r_ref, x0i_ref,
                       wao_ref, wglu_ref, bglu_ref, wso_ref, wout_ref,
                       are_ref, aim_ref, ldt_ref, wbre_s, wbim_s, ccat_ref, d_ref,
                       x2_ref, sre_ref, sim_ref,
                       bur_s, bui_s):
    nseq = x0r_ref.shape[0]
    steps = x1_ref.shape[0] // nseq
    ab_re, ab_im, _, _ = _ssm_discretize(are_ref, aim_ref, ldt_ref)
    u = u_ref[...]
    _expand_state(u.astype(BF16), wbre_s, wbim_s, bur_s, bui_s)
    for c0 in range(0, SSM_S, LANES):
        cs = slice(c0, c0 + LANES)
        ar, ai = ab_re[:, cs], ab_im[:, cs]
        xr, xi = x0r_ref[:, cs], x0i_ref[:, cs]
        for t in range(steps):
            rs = slice(t * nseq, (t + 1) * nseq)
            nr = ar * xr - ai * xi + bur_s[rs, cs]
            ni = ar * xi + ai * xr + bui_s[rs, cs]
            bur_s[rs, cs] = nr
            bui_s[rs, cs] = ni
            xr, xi = nr, ni
        sre_ref[:, cs] = xr
        sim_ref[:, cs] = xi
    y_ssm = _project_state(bur_s, bui_s, ccat_ref) + d_ref[...] * u
    attn_gated = _gated_attn(attn_ref[...].astype(BF16), sga_ref[...], wao_ref)
    x2_ref[...] = _gated_merge(x1_ref[...], attn_gated, y_ssm, sgs_ref[...],
                               wglu_ref, bglu_ref, wso_ref, wout_ref)


def _mix_sample(x1, attn, u, sga, sgs, x0r, x0i, p):
    n, nseq = x1.shape[0], x0r.shape[0]
    args = [x1, attn, u, sga, sgs, x0r, x0i, p['wao'], p['wglu'], p['bglu'], p['wso'], p['wout'],
            p['are'], p['aim'], p['ldt'], p['wbre'], p['wbim'], p['ccat'], p['dskip']]
    return pl.pallas_call(
        _mix_sample_kernel,
        grid=(1,),
        in_specs=[_const_spec(a.shape) for a in args],
        out_specs=[pl.BlockSpec((n, D_MODEL), lambda i: (0, 0)), pl.BlockSpec((nseq, SSM_S), lambda i: (0, 0)),
                   pl.BlockSpec((nseq, SSM_S), lambda i: (0, 0))],
        out_shape=[jax.ShapeDtypeStruct((n, D_MODEL), F32),
                   jax.ShapeDtypeStruct((nseq, SSM_S), F32),
                   jax.ShapeDtypeStruct((nseq, SSM_S), F32)],
        scratch_shapes=[pltpu.VMEM((n, SSM_S), F32), pltpu.VMEM((n, SSM_S), F32)],
        compiler_params=pltpu.CompilerParams(dimension_semantics=("arbitrary",),
                                             vmem_limit_bytes=VMEM_LIMIT),
        name="mix_sample",
    )(*args)


def _rope_tables(base_pos, off_pos):
    half = HEAD_DIM // 2
    lane = np.arange(LANES)
    inv = ROPE_THETA ** (-2.0 * (lane % half) / HEAD_DIM)
    first = (lane % HEAD_DIM) < half
    ab = np.asarray(base_pos, np.float64)[:, None] * inv[None, :]
    ao = np.asarray(off_pos, np.float64)[:, None] * inv[None, :]
    base = np.stack([np.cos(ab), np.sin(ab)], axis=1)
    co, so = np.cos(ao), np.sin(ao)
    lo = np.where(first, -1.0, 0.0)
    hi = np.where(first, 0.0, 1.0)
    off = np.stack([co, so, co * lo, so * lo, co * hi, so * hi], axis=0)
    return jnp.asarray(base, F32), jnp.asarray(off, F32)


def _block_diag_halves(m):
    g, r, c = m.shape
    hg = g // 2
    strip = jnp.transpose(m.reshape(2, hg, r, c), (0, 2, 1, 3)).reshape(2, 1, r, hg * c)
    full = jnp.broadcast_to(strip, (2, hg, r, hg * c)).reshape(2, hg * r, hg * c)
    mask = jnp.asarray(np.kron(np.eye(hg), np.ones((r, c))), m.dtype)
    return full * mask


def _layout_params(ffn1_norm, ffn1_w1, ffn1_w3, ffn1_w2, mix_norm, w_in, q_norm, k_norm, attn_sinks,
                   w_attn_out, ssm_a_re, ssm_a_im, ssm_log_dt, ssm_b_re, ssm_b_im, ssm_c_re, ssm_c_im,
                   ssm_d, w_glu, b_glu, w_ssm_out, w_out, ffn2_norm, ffn2_w1, ffn2_w3, ffn2_w2):
    l = 0
    row = lambda a: a.reshape(1, -1).astype(F32)
    head_ones = jnp.asarray(np.kron(np.eye(LANES // HEAD_DIM), np.ones((HEAD_DIM, HEAD_DIM))), F32)
    r = np.arange(MIX_TM)
    tok = (r % SUBLANES) * SCAN_SEG + r // SUBLANES
    perm = jnp.asarray(tok[:, None] == np.arange(MIX_TM)[None, :], BF16)
    c_cat = jnp.concatenate([_block_diag_halves(jnp.swapaxes(ssm_c_re[l], 1, 2)),
                             -_block_diag_halves(jnp.swapaxes(ssm_c_im[l], 1, 2))], axis=1)
    return dict(
        n1=row(ffn1_norm[l]), w1a=ffn1_w1[l].astype(BF16), w3a=ffn1_w3[l].astype(BF16), w2a=ffn1_w2[l].astype(BF16),
        nm=row(mix_norm[l]), win=w_in[l].astype(BF16),
        qn=row(jnp.tile(q_norm[l], LANES // HEAD_DIM)), kn=row(jnp.tile(k_norm[l], LANES // HEAD_DIM)),
        ones=head_ones.astype(BF16), sinks=attn_sinks[l].astype(F32),
        bglu=row(b_glu[l]),
        are=row(ssm_a_re[l]), aim=row(ssm_a_im[l]), ldt=row(jnp.repeat(ssm_log_dt[l], SSM_N)),
        bbre=_block_diag_halves(jnp.swapaxes(ssm_b_re[l], 1, 2).astype(F32)),
        bbim=_block_diag_halves(jnp.swapaxes(ssm_b_im[l], 1, 2).astype(F32)),
        ccat=c_cat.astype(BF16), dskip=row(ssm_d[l]), perm=perm, permt=perm.T,
        n2=row(ffn2_norm[l]),
    )


def kernel(x_prompt, x_sample, cache_k, cache_v, state_ssm_re, state_ssm_im, ffn1_norm, ffn1_w1, ffn1_w3, ffn1_w2, mix_norm, w_in, q_norm, k_norm, attn_sinks, w_attn_out, ssm_a_re, ssm_a_im, ssm_log_dt, ssm_b_re, ssm_b_im, ssm_c_re, ssm_c_im, ssm_d, w_glu, b_glu, w_ssm_out, w_out, ffn2_norm, ffn2_w1, ffn2_w3, ffn2_w2):
    assert ffn1_norm.shape[0] == 1, "single trunk layer"
    batch, seq, _ = x_prompt.shape
    nseq, steps, _ = x_sample.shape
    p = _layout_params(ffn1_norm, ffn1_w1, ffn1_w3, ffn1_w2, mix_norm, w_in, q_norm, k_norm, attn_sinks,
                       w_attn_out, ssm_a_re, ssm_a_im, ssm_log_dt, ssm_b_re, ssm_b_im, ssm_c_re, ssm_c_im,
                       ssm_d, w_glu, b_glu, w_ssm_out, w_out, ffn2_norm, ffn2_w1, ffn2_w3, ffn2_w2)

    p['wbre'], p['wbim'] = _ssm_weights(p)

    tile0 = np.arange(batch * seq // FRONT_TM) * FRONT_TM
    tabs_p = _rope_tables(tile0 % seq, np.arange(FRONT_TM))
    later = dict(wao=w_attn_out, wglu=w_glu, wso=w_ssm_out, wout=w_out, w1b=ffn2_w1, w3b=ffn2_w3, w2b=ffn2_w2)
    fr = _front(x_prompt.reshape(batch * seq, D_MODEL), tabs_p, p, BF16, cast=list(later.values()))
    p.update(zip(later, fr[7:]))
    fr = fr[:7]
    x2p, x2p_last, re_p, im_p = _mix_prompt(fr, p, batch, seq)
    y_prompt = _ffn(x2p, p, x2p_last, seq).reshape(batch, seq, D_MODEL)
    tail_kv = lambda a: a.reshape(batch, seq, KV_W)[:, seq - WINDOW:].reshape(batch, WINDOW, N_KV, HEAD_DIM)
    k_p, v_p = tail_kv(fr[2]), tail_kv(fr[3])

    xs = jnp.swapaxes(x_sample, 0, 1).reshape(steps * nseq, D_MODEL)
    tile0 = np.arange(steps * nseq // FRONT_TM) * FRONT_TM
    tabs_s = _rope_tables(PAST_LEN + tile0 // nseq, np.arange(FRONT_TM) // nseq)
    x1s, qs, ks, vs, us, sgas, sgss = _front(xs, tabs_s, p, F32)
    to_lanes = lambda c: jnp.transpose(c[0], (0, 2, 3, 1))
    attn_s, nkt, nvt = _attn_sample(qs, ks, vs, to_lanes(cache_k), to_lanes(cache_v), p['sinks'], steps, nseq)
    x2s, re_s, im_s = _mix_sample(x1s, attn_s.reshape(steps * nseq, ATTN_W), us, sgas, sgss,
                                  state_ssm_re[0].reshape(nseq, SSM_S), state_ssm_im[0].reshape(nseq, SSM_S), p)
    y_sample = jnp.swapaxes(_ffn(x2s, p).reshape(steps, nseq, D_MODEL), 0, 1)
    k_s, v_s = jnp.transpose(nkt, (0, 3, 1, 2)), jnp.transpose(nvt, (0, 3, 1, 2))

    st = lambda a, n: a.reshape(1, n, SSM_G, SSM_N)
    return (y_prompt, y_sample, k_p[None], v_p[None], st(re_p, batch), st(im_p, batch),
            k_s[None], v_s[None], st(re_s, nseq), st(im_s, nseq))
```

```python
import functools

import jax
import jax.numpy as jnp
import numpy as np
from jax import lax
from jax.experimental import pallas as pl
from jax.experimental.pallas import tpu as pltpu

F32 = jnp.float32
BF16 = jnp.bfloat16

D_MODEL = 1024
HEAD_DIM = 64
N_HEADS = 8
N_KV = 2
WINDOW = 128
ATTN_W = N_HEADS * HEAD_DIM
KV_W = N_KV * HEAD_DIM
SSM_W = 512
SSM_G = 32
SSM_CH = 16
SSM_N = 64
SSM_S = SSM_G * SSM_N
D_FF = 2816
IN_COLS = ATTN_W + 2 * KV_W + SSM_W + 2 * D_MODEL
ROPE_THETA = 10000.0
RMS_EPS = 1e-6
MASK_VALUE = -1e30
PAST_LEN = 16384

LANES = 128
SUBLANES = 8
MXU_DIM = 256

FRONT_TM = 512
FFN_TM = 1024
MIX_TM = 512
SCAN_SEG = MIX_TM // SUBLANES
FF_CHUNK = 256
SCAN_LC = 512
SAMPLE_BS = 16
VMEM_LIMIT = 58 * 1024 * 1024


def _const_spec(shape):
    nd = len(shape)
    return pl.BlockSpec(shape, lambda *_: (0,) * nd, pipeline_mode=pl.Buffered(1))


def _rms(x, w):
    ms = jnp.mean(x * x, axis=-1, keepdims=True)
    return x * lax.rsqrt(ms + RMS_EPS) * w


def _swiglu_residual(x, nw_ref, w1_ref, w3_ref, w2_ref):
    h = _rms(x, nw_ref[...]).astype(BF16)
    acc = None
    for c0 in range(0, D_FF, FF_CHUNK):
        cs = slice(c0, min(c0 + FF_CHUNK, D_FF))
        a = jnp.dot(h, w1_ref[:, cs], preferred_element_type=F32)
        b = jnp.dot(h, w3_ref[:, cs], preferred_element_type=F32)
        g = (a * jax.nn.sigmoid(a) * b).astype(BF16)
        t = jnp.dot(g, w2_ref[cs, :], preferred_element_type=F32)
        acc = t if acc is None else acc + t
    return x + 0.5 * acc


def _head_norm_rope(t, ones_ref, g_ref, cos, sinlo, sinhi, scale):
    ssum = jnp.dot((t * t).astype(BF16), ones_ref[...], preferred_element_type=F32)
    tn = t * lax.rsqrt(ssum * (1.0 / HEAD_DIM) + RMS_EPS) * g_ref[...]
    up = pltpu.roll(tn, HEAD_DIM // 2, axis=1)
    dn = pltpu.roll(tn, LANES - HEAD_DIM // 2, axis=1)
    r = tn * cos + dn * sinlo + up * sinhi
    return r * scale if scale != 1.0 else r


def _front_kernel(n_cast, x_ref, base_ref, off_ref,
                  n1_ref, w1_ref, w3_ref, w2_ref, nm_ref, win_ref, qn_ref, kn_ref, ones_ref, *refs):
    cast_in, refs = refs[:n_cast], refs[n_cast:]
    x1_ref, q_ref, k_ref, v_ref, u_ref, sga_ref, sgs_ref = refs[:7]
    for src, dst in zip(cast_in, refs[7:]):
        dst[...] = src[...].astype(BF16)
    x1 = _swiglu_residual(x_ref[...], n1_ref, w1_ref, w3_ref, w2_ref)
    x1_ref[...] = x1
    h = _rms(x1, nm_ref[...]).astype(BF16)
    cb, sb = base_ref[0:1, :], base_ref[1:2, :]
    cos = cb * off_ref[0] - sb * off_ref[1]
    sinlo = sb * off_ref[2] + cb * off_ref[3]
    sinhi = sb * off_ref[4] + cb * off_ref[5]
    o = ATTN_W + 2 * KV_W + SSM_W
    qkvu = jnp.dot(h, win_ref[:, 0:o], preferred_element_type=F32)
    for j in range(ATTN_W // LANES):
        q_ref[:, j * LANES:(j + 1) * LANES] = _head_norm_rope(
            qkvu[:, j * LANES:(j + 1) * LANES], ones_ref, qn_ref, cos, sinlo, sinhi,
            HEAD_DIM ** -0.5).astype(q_ref.dtype)
    k_ref[...] = _head_norm_rope(qkvu[:, ATTN_W:ATTN_W + KV_W], ones_ref, kn_ref, cos, sinlo, sinhi, 1.0)
    v_ref[...] = qkvu[:, ATTN_W + KV_W:ATTN_W + 2 * KV_W]
    u_ref[...] = qkvu[:, ATTN_W + 2 * KV_W:o]
    ga = jnp.dot(h, win_ref[:, o:o + D_MODEL], preferred_element_type=F32)
    sga_ref[...] = jax.nn.sigmoid(ga).astype(BF16)
    o += D_MODEL
    gs = jnp.dot(h, win_ref[:, o:o + D_MODEL], preferred_element_type=F32)
    sgs_ref[...] = jax.nn.sigmoid(gs).astype(BF16)


def _cast_block_rows(rows, steps):
    for blocks in range(steps, 0, -1):
        if rows % blocks == 0 and (rows // blocks) % (2 * SUBLANES) == 0:
            return rows // blocks
    raise ValueError(rows)


def _front(x, tabs, p, q_dtype, cast=()):
    n = x.shape[0]
    tm = FRONT_TM
    steps = n // tm
    base_tab, off_tab = tabs
    row = lambda w: pl.BlockSpec((tm, w), lambda i: (i, 0))
    consts = [p['n1'], p['w1a'], p['w3a'], p['w2a'], p['nm'], p['win'], p['qn'], p['kn'], p['ones']]
    outs = [(D_MODEL, F32), (ATTN_W, q_dtype), (KV_W, F32), (KV_W, F32), (SSM_W, F32),
            (D_MODEL, BF16), (D_MODEL, BF16)]
    cast_in, cast_out = [], []
    for w in cast:
        _, rows, cols = w.shape
        rb = _cast_block_rows(rows, steps)
        last = rows // rb - 1
        cast_in.append(pl.BlockSpec((None, rb, cols), lambda i, last=last: (0, jnp.minimum(i, last), 0)))
        cast_out.append(pl.BlockSpec((rb, cols), lambda i, last=last: (jnp.minimum(i, last), 0)))
    return pl.pallas_call(
        functools.partial(_front_kernel, len(cast)),
        grid=(steps,),
        in_specs=[row(D_MODEL), pl.BlockSpec((None, 2, LANES), lambda i: (i, 0, 0)), _const_spec(off_tab.shape)]
        + [_const_spec(c.shape) for c in consts] + cast_in,
        out_specs=[row(w) for w, _ in outs] + cast_out,
        out_shape=[jax.ShapeDtypeStruct((n, w), dt) for w, dt in outs]
        + [jax.ShapeDtypeStruct(w.shape[1:], BF16) for w in cast],
        compiler_params=pltpu.CompilerParams(dimension_semantics=("arbitrary",),
                                             vmem_limit_bytes=VMEM_LIMIT),
        name="front",
    )(x, base_tab, off_tab, *consts, *cast)


def _ffn_kernel(tiles_per_seq, x_ref, xl_ref, n_ref, w1_ref, w3_ref, w2_ref, y_ref):
    x = x_ref[...]
    if tiles_per_seq:
        keep = x.shape[0] - xl_ref.shape[0]
        is_last = pl.program_id(0) % tiles_per_seq == tiles_per_seq - 1
        x = jnp.concatenate([x[:keep], jnp.where(is_last, xl_ref[...], x[keep:])], axis=0)
    y_ref[...] = _swiglu_residual(x, n_ref, w1_ref, w3_ref, w2_ref)


def _ffn(x, p, x_last=None, seq=None):
    n = x.shape[0]
    tm = min(FFN_TM, n)
    weights = [p['w1b'], p['w3b'], p['w2b']]
    if x_last is None:
        tiles_per_seq, x_last = 0, x
        last_spec = pl.BlockSpec((SUBLANES, D_MODEL), lambda i: (0, 0))
    else:
        tiles_per_seq = seq // tm
        last_spec = pl.BlockSpec((MIX_TM, D_MODEL), lambda i: (i // tiles_per_seq, 0))
    return pl.pallas_call(
        functools.partial(_ffn_kernel, tiles_per_seq),
        grid=(n // tm,),
        in_specs=[pl.BlockSpec((tm, D_MODEL), lambda i: (i, 0)), last_spec, _const_spec(p['n2'].shape)]
        + [_const_spec(w.shape) for w in weights],
        out_specs=pl.BlockSpec((tm, D_MODEL), lambda i: (i, 0)),
        out_shape=jax.ShapeDtypeStruct((n, D_MODEL), F32),
        compiler_params=pltpu.CompilerParams(dimension_semantics=("arbitrary",),
                                             vmem_limit_bytes=VMEM_LIMIT),
        name="ffn",
    )(x, x_last, p['n2'], *weights)


def _cmul(ar, ai, br, bi):
    return ar * br - ai * bi, ar * bi + ai * br


def _ssm_discretize(are_ref, aim_ref, ldt_ref):
    a_re, a_im = are_ref[...], aim_ref[...]
    dt = jnp.exp(ldt_ref[...])
    mag = jnp.exp(dt * a_re)
    ab_re = mag * jnp.cos(dt * a_im)
    ab_im = mag * jnp.sin(dt * a_im)
    den = a_re * a_re + a_im * a_im
    nr, ni = ab_re - 1.0, ab_im
    f_re = (nr * a_re + ni * a_im) / den
    f_im = (ni * a_re - nr * a_im) / den
    return ab_re, ab_im, f_re, f_im


def _build_wb(bbre_ref, bbim_ref, f_re, f_im, wbre_s, wbim_s):
    half = SSM_S // 2
    for hh in range(2):
        fr, fi = f_re[:, hh * half:(hh + 1) * half], f_im[:, hh * half:(hh + 1) * half]
        br, bi = bbre_ref[hh], bbim_ref[hh]
        wbre_s[hh] = (br * fr - bi * fi).astype(BF16)
        wbim_s[hh] = (br * fi + bi * fr).astype(BF16)


def _ssm_weights_kernel(are_ref, aim_ref, ldt_ref, bbre_ref, bbim_ref, wbre_ref, wbim_ref):
    _, _, f_re, f_im = _ssm_discretize(are_ref, aim_ref, ldt_ref)
    _build_wb(bbre_ref, bbim_ref, f_re, f_im, wbre_ref, wbim_ref)


def _ssm_weights(p):
    shape = jax.ShapeDtypeStruct(p['bbre'].shape, BF16)
    return pl.pallas_call(_ssm_weights_kernel, out_shape=[shape, shape], name="ssm_weights")(
        p['are'], p['aim'], p['ldt'], p['bbre'], p['bbim'])


def _expand_state(ub, wbre_s, wbim_s, bur_s, bui_s):
    half_in, half_st = SSM_W // 2, SSM_S // 2
    for hh in range(2):
        uh = ub[:, hh * half_in:(hh + 1) * half_in]
        bur_s[:, hh * half_st:(hh + 1) * half_st] = jnp.dot(uh, wbre_s[hh], preferred_element_type=F32)
        bui_s[:, hh * half_st:(hh + 1) * half_st] = jnp.dot(uh, wbim_s[hh], preferred_element_type=F32)


def _project_state(xr_s, xi_s, ccat_ref):
    half_st = SSM_S // 2
    outs = []
    for hh in range(2):
        sl = slice(hh * half_st, (hh + 1) * half_st)
        xcat = jnp.concatenate([xr_s[:, sl].astype(BF16), xi_s[:, sl].astype(BF16)], axis=1)
        outs.append(jnp.dot(xcat, ccat_ref[hh], preferred_element_type=F32))
    return jnp.concatenate(outs, axis=1)


def _gated_attn(attn_b, sga, wao_ref):
    return sga.astype(F32) * jnp.dot(attn_b, wao_ref[...], preferred_element_type=F32)


def _gated_sum(attn_gated, y_ssm, sgs, wglu_ref, bglu_ref, wso_ref):
    z = jax.nn.gelu(y_ssm, approximate=True)
    gl = jnp.dot(z.astype(BF16), wglu_ref[...], preferred_element_type=F32) + bglu_ref[...]
    ssm = (z * jax.nn.sigmoid(gl)).astype(BF16)
    merged = attn_gated + sgs.astype(F32) * jnp.dot(ssm, wso_ref[...], preferred_element_type=F32)
    return merged.astype(BF16)


def _gated_merge(x1, attn_gated, y_ssm, sgs, wglu_ref, bglu_ref, wso_ref, wout_ref):
    merged = _gated_sum(attn_gated, y_ssm, sgs, wglu_ref, bglu_ref, wso_ref)
    return x1 + jnp.dot(merged, wout_ref[...], preferred_element_type=F32)


def _replicate_heads(kv):
    sw = pltpu.roll(kv, HEAD_DIM, axis=1)
    lo = lax.broadcasted_iota(jnp.int32, kv.shape, 1) < HEAD_DIM
    return jnp.where(lo, kv, sw).astype(BF16), jnp.where(lo, sw, kv).astype(BF16)


def _attn_scores(qs, kreps, keys_on_lanes=False):
    qb = qs[0].shape[0]
    lo = lax.broadcasted_iota(jnp.int32, (qb, LANES), 1) < HEAD_DIM
    dims = (((1,), (0,)), ((), ())) if keys_on_lanes else (((1,), (1,)), ((), ()))
    scores = []
    for q, krep in zip(qs, kreps):
        if q.dtype != BF16 or qb % (2 * SUBLANES):
            q = q.astype(F32)
        zero = jnp.zeros((), q.dtype)
        for kv in range(N_KV):
            parts = []
            for jj in range(2):
                t = q[:, (2 * kv + jj) * LANES:(2 * kv + jj + 1) * LANES]
                parts += [jnp.where(lo, t, zero), jnp.where(lo, zero, t)]
            stacked = jnp.concatenate(parts, axis=0).astype(BF16)
            scores.append(lax.dot_general(stacked, krep[kv], dims, preferred_element_type=F32))
    return jnp.concatenate(scores, axis=0)


def _window_valid(rows, qb, first):
    qi = lax.broadcasted_iota(jnp.int32, (rows, 2 * WINDOW), 0) & (qb - 1)
    sj = lax.broadcasted_iota(jnp.int32, (rows, 2 * WINDOW), 1)
    band = (sj <= WINDOW + qi) & (sj > qi)
    return band if first is False else band & ((sj >= WINDOW) | jnp.logical_not(first))


def _attn_softmax(s, sinks_ref, qb, firsts, bias=None):
    per_block = s.shape[0] // len(firsts)
    sink_parts = [jnp.full((qb, 1), sinks_ref[h], F32) for h in range(N_HEADS)]
    sink = jnp.concatenate(sink_parts * len(firsts), axis=0)
    if bias is None:
        valid = jnp.concatenate([_window_valid(per_block, qb, f) for f in firsts], axis=0)
        s = jnp.where(valid, s, MASK_VALUE)
    else:
        s = s + bias
    m = jnp.maximum(jnp.max(s, axis=-1, keepdims=True), sink)
    e = jnp.exp(s - m)
    denom = jnp.sum(e, axis=-1, keepdims=True) + jnp.exp(sink - m)
    return (e * (1.0 / denom)).astype(BF16)


def _attn_values(probs, vreps, qb, keys_on_lanes=False):
    lo = lax.broadcasted_iota(jnp.int32, (qb, LANES), 1) < HEAD_DIM
    rows = 2 * (N_HEADS // N_KV // 2) * qb
    dims = (((1,), (1,)), ((), ())) if keys_on_lanes else (((1,), (0,)), ((), ()))
    outs = []
    for b, vrep in enumerate(vreps):
        tiles = []
        for kv in range(N_KV):
            r0 = (b * N_KV + kv) * rows
            o = lax.dot_general(probs[r0:r0 + rows], vrep[kv], dims, preferred_element_type=F32)
            for jj in range(2):
                oa = o[(2 * jj) * qb:(2 * jj + 1) * qb]
                ob = o[(2 * jj + 1) * qb:(2 * jj + 2) * qb]
                tiles.append(jnp.where(lo, oa, ob))
        outs.append(jnp.concatenate(tiles, axis=1))
    return outs


def _mix_prompt_kernel(x1_ref, q_ref, k_ref, v_ref, u_ref, sga_ref, sgs_ref, sinks_ref,
                       wao_ref, wglu_ref, bglu_ref, wso_ref, wout_ref,
                       are_ref, aim_ref, ldt_ref, wbre_s, wbim_s, ccat_ref, d_ref, perm_ref, permt_ref,
                       x2_ref, x2_last_ref, sre_ref, sim_ref,
                       apr_s, api_s, a1_s, aseg_s, carry_s, cin_s,
                       bur_s, bui_s, xb_s, kprev_s, vprev_s, bias_s, yssm_s, attg_s, sgsp_s, x1p_s):
    b, i = pl.program_id(0), pl.program_id(1)
    nt = pl.num_programs(1)

    @pl.when((b == 0) & (i == 0))
    def _():
        ab_re, ab_im, _, _ = _ssm_discretize(are_ref, aim_ref, ldt_ref)
        a1_s[0:1, :] = ab_re
        a1_s[1:2, :] = ab_im
        pr, pi = ab_re, ab_im
        held = None
        for j in range(SCAN_SEG):
            rows = [jnp.broadcast_to(pr, (SUBLANES, SSM_S)), jnp.broadcast_to(pi, (SUBLANES, SSM_S))]
            if j % 2 == 0:
                held = rows
            else:
                rs = slice((j - 1) * SUBLANES, (j + 1) * SUBLANES)
                apr_s[rs, :] = jnp.concatenate([held[0], rows[0]], axis=0).astype(BF16)
                api_s[rs, :] = jnp.concatenate([held[1], rows[1]], axis=0).astype(BF16)
            if j == SCAN_SEG - 1:
                aseg_s[0:1, :] = pr
                aseg_s[1:2, :] = pi
            pr, pi = _cmul(pr, pi, ab_re, ab_im)
        for t, first in enumerate((False, True)):
            bias_s[t] = jnp.where(_window_valid(bias_s.shape[1], WINDOW, first), 0.0, MASK_VALUE)

    @pl.when(i == 0)
    def _():
        carry_s[...] = jnp.zeros_like(carry_s)
        kprev_s[...] = jnp.zeros_like(kprev_s)
        vprev_s[...] = jnp.zeros_like(vprev_s)
        for lag_s in (yssm_s, attg_s, sgsp_s, x1p_s):
            lag_s[...] = jnp.zeros_like(lag_s)

    lagged = [yssm_s[...], attg_s[...], sgsp_s[...], x1p_s[...]]

    k0, k1 = _replicate_heads(k_ref[...])
    v0, v1 = _replicate_heads(v_ref[...])
    qs, kreps, vreps, firsts = [], [], [], []
    for blk in range(MIX_TM // WINDOW):
        cur = slice(blk * WINDOW, (blk + 1) * WINDOW)
        if blk == 0:
            prev = [kprev_s[0], kprev_s[1], vprev_s[0], vprev_s[1]]
            firsts.append(i == 0)
        else:
            ps = slice((blk - 1) * WINDOW, blk * WINDOW)
            prev = [k0[ps], k1[ps], v0[ps], v1[ps]]
            firsts.append(False)
        qs.append(q_ref[cur, :])
        kreps.append([jnp.concatenate([prev[0], k0[cur]], axis=0), jnp.concatenate([prev[1], k1[cur]], axis=0)])
        vreps.append([jnp.concatenate([prev[2], v0[cur]], axis=0), jnp.concatenate([prev[3], v1[cur]], axis=0)])
    tail = slice(MIX_TM - WINDOW, MIX_TM)
    kprev_s[0], kprev_s[1] = k0[tail], k1[tail]
    vprev_s[0], vprev_s[1] = v0[tail], v1[tail]
    scores = _attn_scores(qs, kreps)

    half_in, half_st = SSM_W // 2, SSM_S // 2
    chunks = list(range(0, SSM_S, SCAN_LC))
    nblk = len(qs)
    blk_rows = scores.shape[0] // nblk
    u = u_ref[...]
    up = jnp.dot(perm_ref[...], u.astype(BF16), preferred_element_type=F32).astype(BF16)

    def expand_half(hh):
        uh = up[:, hh * half_in:(hh + 1) * half_in]
        bur_s[:, hh * half_st:(hh + 1) * half_st] = jnp.dot(uh, wbre_s[hh], preferred_element_type=F32)
        bui_s[:, hh * half_st:(hh + 1) * half_st] = jnp.dot(uh, wbim_s[hh], preferred_element_type=F32)

    def softmax_blocks(lo, hi):
        no_prev = (i == 0).astype(jnp.int32)
        bias = jnp.concatenate([bias_s[no_prev] if blk == 0 else bias_s[0]
                                for blk in range(lo, hi) for _ in range(N_HEADS)], axis=0)
        return _attn_softmax(scores[lo * blk_rows:hi * blk_rows], sinks_ref, WINDOW, firsts[lo:hi], bias)

    def scan_chunk(c0):
        cs = slice(c0, c0 + SCAN_LC)
        ar = jnp.broadcast_to(a1_s[0:1, cs], (SUBLANES, SCAN_LC))
        ai = jnp.broadcast_to(a1_s[1:2, cs], (SUBLANES, SCAN_LC))
        xr, xi = bur_s[0:SUBLANES, cs], bui_s[0:SUBLANES, cs]
        for j in range(1, SCAN_SEG):
            rs = slice(j * SUBLANES, (j + 1) * SUBLANES)
            xr, xi = (ar * xr - ai * xi + bur_s[rs, cs], ar * xi + ai * xr + bui_s[rs, cs])
            bur_s[rs, cs] = xr
            bui_s[rs, cs] = xi
        return xr

    mid = nblk // 2
    expand_half(0)
    probs_a = softmax_blocks(0, mid)
    expand_half(1)
    probs_b = softmax_blocks(mid, nblk)
    quarter = len(chunks) // 4
    scan_group = lambda g: [scan_chunk(c0) for c0 in chunks[g * quarter:(g + 1) * quarter]]
    started = scan_group(0)[-1]
    bits = lax.bitcast_convert_type(started[0:1, 0:1], jnp.uint32)
    zero = lax.bitcast_convert_type(lax.shift_right_logical(lax.shift_right_logical(bits, jnp.uint32(16)),
                                                            jnp.uint32(16)), F32)
    yssm_prev, attg_prev, sgs_prev, x1_prev = lagged
    x2_ref[...] = _gated_merge(x1_prev, attg_prev, yssm_prev + zero, sgs_prev,
                               wglu_ref, bglu_ref, wso_ref, wout_ref)
    attn = _attn_values(probs_a, vreps[0:mid], WINDOW)
    scan_group(1)
    attn += _attn_values(probs_b, vreps[mid:nblk], WINDOW)
    scan_group(2)
    attn_gated = _gated_attn(jnp.concatenate(attn, axis=0).astype(BF16), sga_ref[...], wao_ref)
    scan_group(3)

    last = MIX_TM - SUBLANES
    sr, si = aseg_s[0:1, :], aseg_s[1:2, :]
    cr, ci = carry_s[0:1, :], carry_s[1:2, :]
    for s in range(SUBLANES):
        cin_s[0, s:s + 1, :] = cr
        cin_s[1, s:s + 1, :] = ci
        pr, pi = _cmul(sr, si, cr, ci)
        cr = pr + bur_s[last + s:last + s + 1, :]
        ci = pi + bui_s[last + s:last + s + 1, :]
    carry_s[0:1, :] = cr
    carry_s[1:2, :] = ci
    pack = 2 * SUBLANES

    def fix_chunk(c0):
        cs = slice(c0, c0 + SCAN_LC)
        cinr = jnp.concatenate([cin_s[0, :, cs]] * 2, axis=0).astype(BF16)
        cini = jnp.concatenate([cin_s[1, :, cs]] * 2, axis=0).astype(BF16)
        col = (c0 // half_st) * SSM_S + c0 % half_st
        for r0 in range(0, MIX_TM, pack):
            rs = slice(r0, r0 + pack)
            pr, pi = apr_s[rs, cs], api_s[rs, cs]
            xb_s[rs, col:col + SCAN_LC] = bur_s[rs, cs].astype(BF16) + pr * cinr - pi * cini
            xb_s[rs, col + half_st:col + half_st + SCAN_LC] = bui_s[rs, cs].astype(BF16) + pr * cini + pi * cinr

    def project_half(hh):
        return jnp.dot(xb_s[:, hh * SSM_S:(hh + 1) * SSM_S], ccat_ref[hh], preferred_element_type=F32)

    per_half = len(chunks) // 2
    for c0 in chunks[:per_half]:
        fix_chunk(c0)
    yp0 = project_half(0)
    for c0 in chunks[per_half:]:
        fix_chunk(c0)
    yp = jnp.concatenate([yp0, project_half(1)], axis=1)
    y_ssm = jnp.dot(permt_ref[...], yp.astype(BF16), preferred_element_type=F32) + d_ref[...] * u

    yssm_s[...] = y_ssm
    attg_s[...] = attn_gated
    sgsp_s[...] = sgs_ref[...]
    x1p_s[...] = x1_ref[...]

    @pl.when(i == nt - 1)
    def _():
        x2_last_ref[...] = _gated_merge(x1_ref[...], attn_gated, y_ssm, sgs_ref[...],
                                        wglu_ref, bglu_ref, wso_ref, wout_ref)
        sre_ref[...] = carry_s[0:1, :]
        sim_ref[...] = carry_s[1:2, :]


def _mix_prompt(fr, p, batch, seq):
    x1, q, k, v, u, sga, sgs = fr
    tm = MIX_TM
    nt = seq // tm
    row = lambda w: pl.BlockSpec((tm, w), lambda b, i: (b * nt + i, 0))
    lag = pl.BlockSpec((tm, D_MODEL), lambda b, i: (b * nt + (i + nt - 1) % nt, 0))
    last = pl.BlockSpec((tm, D_MODEL), lambda b, i: (b, 0))
    consts = [p['wao'], p['wglu'], p['bglu'], p['wso'], p['wout'],
              p['are'], p['aim'], p['ldt'], p['wbre'], p['wbim'], p['ccat'], p['dskip'], p['perm'], p['permt']]
    st_spec = pl.BlockSpec((None, 1, SSM_S), lambda b, i: (b, 0, 0))
    scratch = [pltpu.VMEM((tm, SSM_S), BF16), pltpu.VMEM((tm, SSM_S), BF16),
               pltpu.VMEM((SUBLANES, SSM_S), F32), pltpu.VMEM((SUBLANES, SSM_S), F32),
               pltpu.VMEM((SUBLANES, SSM_S), F32), pltpu.VMEM((2, SUBLANES, SSM_S), F32),
               pltpu.VMEM((tm, SSM_S), F32), pltpu.VMEM((tm, SSM_S), F32), pltpu.VMEM((tm, 2 * SSM_S), BF16),
               pltpu.VMEM((N_KV, WINDOW, LANES), BF16), pltpu.VMEM((N_KV, WINDOW, LANES), BF16),
               pltpu.VMEM((2, WINDOW, 2 * WINDOW), F32),
               pltpu.VMEM((tm, SSM_W), F32), pltpu.VMEM((tm, D_MODEL), F32), pltpu.VMEM((tm, D_MODEL), BF16),
               pltpu.VMEM((tm, D_MODEL), F32)]
    return pl.pallas_call(
        _mix_prompt_kernel,
        grid=(batch, nt),
        in_specs=[row(D_MODEL), row(ATTN_W), row(KV_W), row(KV_W), row(SSM_W), row(D_MODEL), row(D_MODEL),
                  pl.BlockSpec(memory_space=pltpu.SMEM)] + [_const_spec(c.shape) for c in consts],
        out_specs=[lag, last, st_spec, st_spec],
        out_shape=[jax.ShapeDtypeStruct((batch * seq, D_MODEL), F32),
                   jax.ShapeDtypeStruct((batch * tm, D_MODEL), F32),
                   jax.ShapeDtypeStruct((batch, 1, SSM_S), F32),
                   jax.ShapeDtypeStruct((batch, 1, SSM_S), F32)],
        scratch_shapes=scratch,
        compiler_params=pltpu.CompilerParams(dimension_semantics=("arbitrary", "arbitrary"),
                                             vmem_limit_bytes=VMEM_LIMIT),
        name="mix_prompt",
    )(x1, q, k, v, u, sga, sgs, p['sinks'], *consts)


def _attn_sample_kernel(q_ref, kn_ref, vn_ref, ckt_ref, cvt_ref, sinks_ref, o_ref, nkt_ref, nvt_ref,
                        q8_s, knp_s, vnp_s):
    steps = q_ref.shape[0]
    shift = WINDOW - steps
    keep = lax.broadcasted_iota(jnp.int32, (HEAD_DIM, WINDOW), 1) < shift
    q8_s[...] = jnp.zeros_like(q8_s)
    knp_s[...] = jnp.zeros_like(knp_s)
    vnp_s[...] = jnp.zeros_like(vnp_s)
    qs, kreps, vreps = [], [], []
    for s in range(SAMPLE_BS):
        for t in range(steps):
            q8_s[s, t:t + 1, :] = q_ref[t, s:s + 1, :]
            knp_s[s, t:t + 1, :] = kn_ref[t, s:s + 1, :]
            vnp_s[s, t:t + 1, :] = vn_ref[t, s:s + 1, :]
        news = [knp_s[s].T, vnp_s[s].T]
        ends = [pltpu.roll(n, shift, axis=1) for n in news]
        krep, vrep = [], []
        for kv in range(N_KV):
            hs = slice(kv * HEAD_DIM, (kv + 1) * HEAD_DIM)
            for cache_ref, out_ref, new, end, reps in ((ckt_ref, nkt_ref, news[0], ends[0], krep),
                                                       (cvt_ref, nvt_ref, news[1], ends[1], vrep)):
                old = cache_ref[s, kv]
                out_ref[s, kv] = jnp.where(keep, pltpu.roll(old, shift, axis=1), end[hs])
                both = jnp.concatenate([old, new[hs]], axis=1).astype(BF16)
                reps.append(jnp.concatenate([both, both], axis=0))
        qs.append(q8_s[s])
        kreps.append(krep)
        vreps.append(vrep)
    probs = _attn_softmax(_attn_scores(qs, kreps, keys_on_lanes=True), sinks_ref, SUBLANES, [False] * SAMPLE_BS)
    outs = _attn_values(probs, vreps, SUBLANES, keys_on_lanes=True)
    for s in range(SAMPLE_BS):
        for t in range(steps):
            o_ref[t, s:s + 1, :] = outs[s][t:t + 1, :]


def _attn_sample(q, kn, vn, ckt, cvt, sinks, steps, nseq):
    bs = SAMPLE_BS
    blk3 = lambda w: pl.BlockSpec((steps, bs, w), lambda i: (0, i, 0))
    cache = pl.BlockSpec((bs, N_KV, HEAD_DIM, WINDOW), lambda i: (i, 0, 0, 0))
    cache_shape = jax.ShapeDtypeStruct((nseq, N_KV, HEAD_DIM, WINDOW), F32)
    return pl.pallas_call(
        _attn_sample_kernel,
        grid=(nseq // bs,),
        in_specs=[blk3(ATTN_W), blk3(KV_W), blk3(KV_W), cache, cache, pl.BlockSpec(memory_space=pltpu.SMEM)],
        out_specs=[blk3(ATTN_W), cache, cache],
        out_shape=[jax.ShapeDtypeStruct((steps, nseq, ATTN_W), F32), cache_shape, cache_shape],
        scratch_shapes=[pltpu.VMEM((bs, SUBLANES, ATTN_W), F32), pltpu.VMEM((bs, LANES, KV_W), F32),
                        pltpu.VMEM((bs, LANES, KV_W), F32)],
        compiler_params=pltpu.CompilerParams(dimension_semantics=("arbitrary",)),
        name="attn_sample",
    )(q.reshape(steps, nseq, ATTN_W), kn.reshape(steps, nseq, KV_W), vn.reshape(steps, nseq, KV_W),
      ckt, cvt, sinks)


def _mix_sample_kernel(x1_ref, attn_ref, u_ref, sga_ref, sgs_ref, x0r_ref, x0i_ref,
                       wao_ref, wglu_ref, bglu_ref, wso_ref, wout_ref,
                       are_ref, aim_ref, ldt_ref, wbre_s, wbim_s, ccat_ref, d_ref,
                       x2_ref, sre_ref, sim_ref,
                       bur_s, bui_s):
    nseq = x0r_ref.shape[0]
    steps = x1_ref.shape[0] // nseq
    ab_re, ab_im, _, _ = _ssm_discretize(are_ref, aim_ref, ldt_ref)
    u = u_ref[...]
    _expand_state(u.astype(BF16), wbre_s, wbim_s, bur_s, bui_s)
    for c0 in range(0, SSM_S, LANES):
        cs = slice(c0, c0 + LANES)
        ar, ai = ab_re[:, cs], ab_im[:, cs]
        xr, xi = x0r_ref[:, cs], x0i_ref[:, cs]
        for t in range(steps):
            rs = slice(t * nseq, (t + 1) * nseq)
            nr = ar * xr - ai * xi + bur_s[rs, cs]
            ni = ar * xi + ai * xr + bui_s[rs, cs]
            bur_s[rs, cs] = nr
            bui_s[rs, cs] = ni
            xr, xi = nr, ni
        sre_ref[:, cs] = xr
        sim_ref[:, cs] = xi
    y_ssm = _project_state(bur_s, bui_s, ccat_ref) + d_ref[...] * u
    attn_gated = _gated_attn(attn_ref[...].astype(BF16), sga_ref[...], wao_ref)
    x2_ref[...] = _gated_merge(x1_ref[...], attn_gated, y_ssm, sgs_ref[...],
                               wglu_ref, bglu_ref, wso_ref, wout_ref)


def _mix_sample(x1, attn, u, sga, sgs, x0r, x0i, p):
    n, nseq = x1.shape[0], x0r.shape[0]
    args = [x1, attn, u, sga, sgs, x0r, x0i, p['wao'], p['wglu'], p['bglu'], p['wso'], p['wout'],
            p['are'], p['aim'], p['ldt'], p['wbre'], p['wbim'], p['ccat'], p['dskip']]
    return pl.pallas_call(
        _mix_sample_kernel,
        grid=(1,),
        in_specs=[_const_spec(a.shape) for a in args],
        out_specs=[pl.BlockSpec((n, D_MODEL), lambda i: (0, 0)), pl.BlockSpec((nseq, SSM_S), lambda i: (0, 0)),
                   pl.BlockSpec((nseq, SSM_S), lambda i: (0, 0))],
        out_shape=[jax.ShapeDtypeStruct((n, D_MODEL), F32),
                   jax.ShapeDtypeStruct((nseq, SSM_S), F32),
                   jax.ShapeDtypeStruct((nseq, SSM_S), F32)],
        scratch_shapes=[pltpu.VMEM((n, SSM_S), F32), pltpu.VMEM((n, SSM_S), F32)],
        compiler_params=pltpu.CompilerParams(dimension_semantics=("arbitrary",),
                                             vmem_limit_bytes=VMEM_LIMIT),
        name="mix_sample",
    )(*args)


def _rope_tables(base_pos, off_pos):
    half = HEAD_DIM // 2
    lane = np.arange(LANES)
    inv = ROPE_THETA ** (-2.0 * (lane % half) / HEAD_DIM)
    first = (lane % HEAD_DIM) < half
    ab = np.asarray(base_pos, np.float64)[:, None] * inv[None, :]
    ao = np.asarray(off_pos, np.float64)[:, None] * inv[None, :]
    base = np.stack([np.cos(ab), np.sin(ab)], axis=1)
    co, so = np.cos(ao), np.sin(ao)
    lo = np.where(first, -1.0, 0.0)
    hi = np.where(first, 0.0, 1.0)
    off = np.stack([co, so, co * lo, so * lo, co * hi, so * hi], axis=0)
    return jnp.asarray(base, F32), jnp.asarray(off, F32)


def _block_diag_halves(m):
    g, r, c = m.shape
    hg = g // 2
    strip = jnp.transpose(m.reshape(2, hg, r, c), (0, 2, 1, 3)).reshape(2, 1, r, hg * c)
    full = jnp.broadcast_to(strip, (2, hg, r, hg * c)).reshape(2, hg * r, hg * c)
    mask = jnp.asarray(np.kron(np.eye(hg), np.ones((r, c))), m.dtype)
    return full * mask


def _layout_params(ffn1_norm, ffn1_w1, ffn1_w3, ffn1_w2, mix_norm, w_in, q_norm, k_norm, attn_sinks,
                   w_attn_out, ssm_a_re, ssm_a_im, ssm_log_dt, ssm_b_re, ssm_b_im, ssm_c_re, ssm_c_im,
                   ssm_d, w_glu, b_glu, w_ssm_out, w_out, ffn2_norm, ffn2_w1, ffn2_w3, ffn2_w2):
    l = 0
    row = lambda a: a.reshape(1, -1).astype(F32)
    head_ones = jnp.asarray(np.kron(np.eye(LANES // HEAD_DIM), np.ones((HEAD_DIM, HEAD_DIM))), F32)
    r = np.arange(MIX_TM)
    tok = (r % SUBLANES) * SCAN_SEG + r // SUBLANES
    perm = jnp.asarray(tok[:, None] == np.arange(MIX_TM)[None, :], BF16)
    c_cat = jnp.concatenate([_block_diag_halves(jnp.swapaxes(ssm_c_re[l], 1, 2)),
                             -_block_diag_halves(jnp.swapaxes(ssm_c_im[l], 1, 2))], axis=1)
    return dict(
        n1=row(ffn1_norm[l]), w1a=ffn1_w1[l].astype(BF16), w3a=ffn1_w3[l].astype(BF16), w2a=ffn1_w2[l].astype(BF16),
        nm=row(mix_norm[l]), win=w_in[l].astype(BF16),
        qn=row(jnp.tile(q_norm[l], LANES // HEAD_DIM)), kn=row(jnp.tile(k_norm[l], LANES // HEAD_DIM)),
        ones=head_ones.astype(BF16), sinks=attn_sinks[l].astype(F32),
        bglu=row(b_glu[l]),
        are=row(ssm_a_re[l]), aim=row(ssm_a_im[l]), ldt=row(jnp.repeat(ssm_log_dt[l], SSM_N)),
        bbre=_block_diag_halves(jnp.swapaxes(ssm_b_re[l], 1, 2).astype(F32)),
        bbim=_block_diag_halves(jnp.swapaxes(ssm_b_im[l], 1, 2).astype(F32)),
        ccat=c_cat.astype(BF16), dskip=row(ssm_d[l]), perm=perm, permt=perm.T,
        n2=row(ffn2_norm[l]),
    )


def kernel(x_prompt, x_sample, cache_k, cache_v, state_ssm_re, state_ssm_im, ffn1_norm, ffn1_w1, ffn1_w3, ffn1_w2, mix_norm, w_in, q_norm, k_norm, attn_sinks, w_attn_out, ssm_a_re, ssm_a_im, ssm_log_dt, ssm_b_re, ssm_b_im, ssm_c_re, ssm_c_im, ssm_d, w_glu, b_glu, w_ssm_out, w_out, ffn2_norm, ffn2_w1, ffn2_w3, ffn2_w2):
    assert ffn1_norm.shape[0] == 1, "single trunk layer"
    batch, seq, _ = x_prompt.shape
    nseq, steps, _ = x_sample.shape
    p = _layout_params(ffn1_norm, ffn1_w1, ffn1_w3, ffn1_w2, mix_norm, w_in, q_norm, k_norm, attn_sinks,
                       w_attn_out, ssm_a_re, ssm_a_im, ssm_log_dt, ssm_b_re, ssm_b_im, ssm_c_re, ssm_c_im,
                       ssm_d, w_glu, b_glu, w_ssm_out, w_out, ffn2_norm, ffn2_w1, ffn2_w3, ffn2_w2)

    p['wbre'], p['wbim'] = _ssm_weights(p)

    tile0 = np.arange(batch * seq // FRONT_TM) * FRONT_TM
    tabs_p = _rope_tables(tile0 % seq, np.arange(FRONT_TM))
    later = dict(wao=w_attn_out, wglu=w_glu, wso=w_ssm_out, wout=w_out, w1b=ffn2_w1, w3b=ffn2_w3, w2b=ffn2_w2)
    fr = _front(x_prompt.reshape(batch * seq, D_MODEL), tabs_p, p, BF16, cast=list(later.values()))
    p.update(zip(later, fr[7:]))
    fr = fr[:7]
    x2p, x2p_last, re_p, im_p = _mix_prompt(fr, p, batch, seq)
    y_prompt = _ffn(x2p, p, x2p_last, seq).reshape(batch, seq, D_MODEL)
    tail_kv = lambda a: a.reshape(batch, seq, KV_W)[:, seq - WINDOW:].reshape(batch, WINDOW, N_KV, HEAD_DIM)
    k_p, v_p = tail_kv(fr[2]), tail_kv(fr[3])

    xs = jnp.swapaxes(x_sample, 0, 1).reshape(steps * nseq, D_MODEL)
    tile0 = np.arange(steps * nseq // FRONT_TM) * FRONT_TM
    tabs_s = _rope_tables(PAST_LEN + tile0 // nseq, np.arange(FRONT_TM) // nseq)
    x1s, qs, ks, vs, us, sgas, sgss = _front(xs, tabs_s, p, F32)
    to_lanes = lambda c: jnp.transpose(c[0], (0, 2, 3, 1))
    attn_s, nkt, nvt = _attn_sample(qs, ks, vs, to_lanes(cache_k), to_lanes(cache_v), p['sinks'], steps, nseq)
    x2s, re_s, im_s = _mix_sample(x1s, attn_s.reshape(steps * nseq, ATTN_W), us, sgas, sgss,
                                  state_ssm_re[0].reshape(nseq, SSM_S), state_ssm_im[0].reshape(nseq, SSM_S), p)
    y_sample = jnp.swapaxes(_ffn(x2s, p).reshape(steps, nseq, D_MODEL), 0, 1)
    k_s, v_s = jnp.transpose(nkt, (0, 3, 1, 2)), jnp.transpose(nvt, (0, 3, 1, 2))

    st = lambda a, n: a.reshape(1, n, SSM_G, SSM_N)
    return (y_prompt, y_sample, k_p[None], v_p[None], st(re_p, batch), st(im_p, batch),
            k_s[None], v_s[None], st(re_s, nseq), st(im_s, nseq))
```

```python
import functools

import jax
import jax.numpy as jnp
import numpy as np
from jax import lax
from jax.experimental import pallas as pl
from jax.experimental.pallas import tpu as pltpu

F32 = jnp.float32
BF16 = jnp.bfloat16

D_MODEL = 1024
HEAD_DIM = 64
N_HEADS = 8
N_KV = 2
WINDOW = 128
ATTN_W = N_HEADS * HEAD_DIM
KV_W = N_KV * HEAD_DIM
SSM_W = 512
SSM_G = 32
SSM_CH = 16
SSM_N = 64
SSM_S = SSM_G * SSM_N
D_FF = 2816
IN_COLS = ATTN_W + 2 * KV_W + SSM_W + 2 * D_MODEL
ROPE_THETA = 10000.0
RMS_EPS = 1e-6
MASK_VALUE = -1e30
PAST_LEN = 16384

LANES = 128
SUBLANES = 8
MXU_DIM = 256

FRONT_TM = 512
FFN_TM = 1024
MIX_TM = 512
SCAN_SEG = MIX_TM // SUBLANES
FF_CHUNK = 256
SCAN_LC = 512
SAMPLE_BS = 32
VMEM_LIMIT = 58 * 1024 * 1024


def _const_spec(shape):
    nd = len(shape)
    return pl.BlockSpec(shape, lambda *_: (0,) * nd, pipeline_mode=pl.Buffered(1))


def _rms(x, w):
    ms = jnp.mean(x * x, axis=-1, keepdims=True)
    return x * lax.rsqrt(ms + RMS_EPS) * w


def _swiglu_residual(x, nw_ref, w1_ref, w3_ref, w2_ref):
    h = _rms(x, nw_ref[...]).astype(BF16)
    acc = None
    for c0 in range(0, D_FF, FF_CHUNK):
        cs = slice(c0, min(c0 + FF_CHUNK, D_FF))
        a = jnp.dot(h, w1_ref[:, cs], preferred_element_type=F32)
        b = jnp.dot(h, w3_ref[:, cs], preferred_element_type=F32)
        g = (a * jax.nn.sigmoid(a) * b).astype(BF16)
        t = jnp.dot(g, w2_ref[cs, :], preferred_element_type=F32)
        acc = t if acc is None else acc + t
    return x + 0.5 * acc


def _head_norm_rope(t, ones_ref, g_ref, cos, sinlo, sinhi, scale):
    ssum = jnp.dot((t * t).astype(BF16), ones_ref[...], preferred_element_type=F32)
    tn = t * lax.rsqrt(ssum * (1.0 / HEAD_DIM) + RMS_EPS) * g_ref[...]
    up = pltpu.roll(tn, HEAD_DIM // 2, axis=1)
    dn = pltpu.roll(tn, LANES - HEAD_DIM // 2, axis=1)
    r = tn * cos + dn * sinlo + up * sinhi
    return r * scale if scale != 1.0 else r


def _front_kernel(n_cast, x_ref, base_ref, off_ref,
                  n1_ref, w1_ref, w3_ref, w2_ref, nm_ref, win_ref, qn_ref, kn_ref, ones_ref, *refs):
    cast_in, refs = refs[:n_cast], refs[n_cast:]
    x1_ref, q_ref, k_ref, v_ref, u_ref, sga_ref, sgs_ref = refs[:7]
    for src, dst in zip(cast_in, refs[7:]):
        dst[...] = src[...].astype(BF16)
    x1 = _swiglu_residual(x_ref[...], n1_ref, w1_ref, w3_ref, w2_ref)
    x1_ref[...] = x1
    h = _rms(x1, nm_ref[...]).astype(BF16)
    cb, sb = base_ref[0:1, :], base_ref[1:2, :]
    cos = cb * off_ref[0] - sb * off_ref[1]
    sinlo = sb * off_ref[2] + cb * off_ref[3]
    sinhi = sb * off_ref[4] + cb * off_ref[5]
    o = ATTN_W + 2 * KV_W + SSM_W
    qkvu = jnp.dot(h, win_ref[:, 0:o], preferred_element_type=F32)
    for j in range(ATTN_W // LANES):
        q_ref[:, j * LANES:(j + 1) * LANES] = _head_norm_rope(
            qkvu[:, j * LANES:(j + 1) * LANES], ones_ref, qn_ref, cos, sinlo, sinhi,
            HEAD_DIM ** -0.5).astype(q_ref.dtype)
    k_ref[...] = _head_norm_rope(qkvu[:, ATTN_W:ATTN_W + KV_W], ones_ref, kn_ref, cos, sinlo, sinhi, 1.0)
    v_ref[...] = qkvu[:, ATTN_W + KV_W:ATTN_W + 2 * KV_W]
    u_ref[...] = qkvu[:, ATTN_W + 2 * KV_W:o]
    ga = jnp.dot(h, win_ref[:, o:o + D_MODEL], preferred_element_type=F32)
    sga_ref[...] = jax.nn.sigmoid(ga).astype(BF16)
    o += D_MODEL
    gs = jnp.dot(h, win_ref[:, o:o + D_MODEL], preferred_element_type=F32)
    sgs_ref[...] = jax.nn.sigmoid(gs).astype(BF16)


def _cast_block_rows(rows, steps):
    for blocks in range(steps, 0, -1):
        if rows % blocks == 0 and (rows // blocks) % (2 * SUBLANES) == 0:
            return rows // blocks
    raise ValueError(rows)


def _front(x, tabs, p, q_dtype, cast=()):
    n = x.shape[0]
    tm = FRONT_TM
    steps = n // tm
    base_tab, off_tab = tabs
    row = lambda w: pl.BlockSpec((tm, w), lambda i: (i, 0))
    consts = [p['n1'], p['w1a'], p['w3a'], p['w2a'], p['nm'], p['win'], p['qn'], p['kn'], p['ones']]
    outs = [(D_MODEL, F32), (ATTN_W, q_dtype), (KV_W, F32), (KV_W, F32), (SSM_W, F32),
            (D_MODEL, BF16), (D_MODEL, BF16)]
    cast_in, cast_out = [], []
    for w in cast:
        _, rows, cols = w.shape
        rb = _cast_block_rows(rows, steps)
        last = rows // rb - 1
        cast_in.append(pl.BlockSpec((None, rb, cols), lambda i, last=last: (0, jnp.minimum(i, last), 0)))
        cast_out.append(pl.BlockSpec((rb, cols), lambda i, last=last: (jnp.minimum(i, last), 0)))
    return pl.pallas_call(
        functools.partial(_front_kernel, len(cast)),
        grid=(steps,),
        in_specs=[row(D_MODEL), pl.BlockSpec((None, 2, LANES), lambda i: (i, 0, 0)), _const_spec(off_tab.shape)]
        + [_const_spec(c.shape) for c in consts] + cast_in,
        out_specs=[row(w) for w, _ in outs] + cast_out,
        out_shape=[jax.ShapeDtypeStruct((n, w), dt) for w, dt in outs]
        + [jax.ShapeDtypeStruct(w.shape[1:], BF16) for w in cast],
        compiler_params=pltpu.CompilerParams(dimension_semantics=("arbitrary",),
                                             vmem_limit_bytes=VMEM_LIMIT),
        name="front",
    )(x, base_tab, off_tab, *consts, *cast)


def _ffn_kernel(tiles_per_seq, x_ref, xl_ref, n_ref, w1_ref, w3_ref, w2_ref, y_ref):
    x = x_ref[...]
    if tiles_per_seq:
        keep = x.shape[0] - xl_ref.shape[0]
        is_last = pl.program_id(0) % tiles_per_seq == tiles_per_seq - 1
        x = jnp.concatenate([x[:keep], jnp.where(is_last, xl_ref[...], x[keep:])], axis=0)
    y_ref[...] = _swiglu_residual(x, n_ref, w1_ref, w3_ref, w2_ref)


def _ffn(x, p, x_last=None, seq=None):
    n = x.shape[0]
    tm = min(FFN_TM, n)
    weights = [p['w1b'], p['w3b'], p['w2b']]
    if x_last is None:
        tiles_per_seq, x_last = 0, x
        last_spec = pl.BlockSpec((SUBLANES, D_MODEL), lambda i: (0, 0))
    else:
        tiles_per_seq = seq // tm
        last_spec = pl.BlockSpec((MIX_TM, D_MODEL), lambda i: (i // tiles_per_seq, 0))
    return pl.pallas_call(
        functools.partial(_ffn_kernel, tiles_per_seq),
        grid=(n // tm,),
        in_specs=[pl.BlockSpec((tm, D_MODEL), lambda i: (i, 0)), last_spec, _const_spec(p['n2'].shape)]
        + [_const_spec(w.shape) for w in weights],
        out_specs=pl.BlockSpec((tm, D_MODEL), lambda i: (i, 0)),
        out_shape=jax.ShapeDtypeStruct((n, D_MODEL), F32),
        compiler_params=pltpu.CompilerParams(dimension_semantics=("arbitrary",),
                                             vmem_limit_bytes=VMEM_LIMIT),
        name="ffn",
    )(x, x_last, p['n2'], *weights)


def _cmul(ar, ai, br, bi):
    return ar * br - ai * bi, ar * bi + ai * br


def _ssm_discretize(are_ref, aim_ref, ldt_ref):
    a_re, a_im = are_ref[...], aim_ref[...]
    dt = jnp.exp(ldt_ref[...])
    mag = jnp.exp(dt * a_re)
    ab_re = mag * jnp.cos(dt * a_im)
    ab_im = mag * jnp.sin(dt * a_im)
    den = a_re * a_re + a_im * a_im
    nr, ni = ab_re - 1.0, ab_im
    f_re = (nr * a_re + ni * a_im) / den
    f_im = (ni * a_re - nr * a_im) / den
    return ab_re, ab_im, f_re, f_im


def _build_wb(bbre_ref, bbim_ref, f_re, f_im, wbre_s, wbim_s):
    half = SSM_S // 2
    for hh in range(2):
        fr, fi = f_re[:, hh * half:(hh + 1) * half], f_im[:, hh * half:(hh + 1) * half]
        br, bi = bbre_ref[hh], bbim_ref[hh]
        wbre_s[hh] = (br * fr - bi * fi).astype(BF16)
        wbim_s[hh] = (br * fi + bi * fr).astype(BF16)


def _ssm_weights_kernel(are_ref, aim_ref, ldt_ref, bbre_ref, bbim_ref, wbre_ref, wbim_ref):
    _, _, f_re, f_im = _ssm_discretize(are_ref, aim_ref, ldt_ref)
    _build_wb(bbre_ref, bbim_ref, f_re, f_im, wbre_ref, wbim_ref)


def _ssm_weights(p):
    shape = jax.ShapeDtypeStruct(p['bbre'].shape, BF16)
    return pl.pallas_call(_ssm_weights_kernel, out_shape=[shape, shape], name="ssm_weights")(
        p['are'], p['aim'], p['ldt'], p['bbre'], p['bbim'])


def _expand_state(ub, wbre_s, wbim_s, bur_s, bui_s):
    half_in, half_st = SSM_W // 2, SSM_S // 2
    for hh in range(2):
        uh = ub[:, hh * half_in:(hh + 1) * half_in]
        bur_s[:, hh * half_st:(hh + 1) * half_st] = jnp.dot(uh, wbre_s[hh], preferred_element_type=F32)
        bui_s[:, hh * half_st:(hh + 1) * half_st] = jnp.dot(uh, wbim_s[hh], preferred_element_type=F32)


def _project_state(xr_s, xi_s, ccat_ref):
    half_st = SSM_S // 2
    outs = []
    for hh in range(2):
        sl = slice(hh * half_st, (hh + 1) * half_st)
        xcat = jnp.concatenate([xr_s[:, sl].astype(BF16), xi_s[:, sl].astype(BF16)], axis=1)
        outs.append(jnp.dot(xcat, ccat_ref[hh], preferred_element_type=F32))
    return jnp.concatenate(outs, axis=1)


def _gated_attn(attn_b, sga, wao_ref):
    return sga.astype(F32) * jnp.dot(attn_b, wao_ref[...], preferred_element_type=F32)


def _gated_sum(attn_gated, y_ssm, sgs, wglu_ref, bglu_ref, wso_ref):
    z = jax.nn.gelu(y_ssm, approximate=True)
    gl = jnp.dot(z.astype(BF16), wglu_ref[...], preferred_element_type=F32) + bglu_ref[...]
    ssm = (z * jax.nn.sigmoid(gl)).astype(BF16)
    merged = attn_gated + sgs.astype(F32) * jnp.dot(ssm, wso_ref[...], preferred_element_type=F32)
    return merged.astype(BF16)


def _gated_merge(x1, attn_gated, y_ssm, sgs, wglu_ref, bglu_ref, wso_ref, wout_ref):
    merged = _gated_sum(attn_gated, y_ssm, sgs, wglu_ref, bglu_ref, wso_ref)
    return x1 + jnp.dot(merged, wout_ref[...], preferred_element_type=F32)


def _replicate_heads(kv):
    sw = pltpu.roll(kv, HEAD_DIM, axis=1)
    lo = lax.broadcasted_iota(jnp.int32, kv.shape, 1) < HEAD_DIM
    return jnp.where(lo, kv, sw).astype(BF16), jnp.where(lo, sw, kv).astype(BF16)


def _attn_scores(qs, kreps, keys_on_lanes=False):
    qb = qs[0].shape[0]
    lo = lax.broadcasted_iota(jnp.int32, (qb, LANES), 1) < HEAD_DIM
    dims = (((1,), (0,)), ((), ())) if keys_on_lanes else (((1,), (1,)), ((), ()))
    scores = []
    for q, krep in zip(qs, kreps):
        if q.dtype != BF16 or qb % (2 * SUBLANES):
            q = q.astype(F32)
        zero = jnp.zeros((), q.dtype)
        for kv in range(N_KV):
            parts = []
            for jj in range(2):
                t = q[:, (2 * kv + jj) * LANES:(2 * kv + jj + 1) * LANES]
                parts += [jnp.where(lo, t, zero), jnp.where(lo, zero, t)]
            stacked = jnp.concatenate(parts, axis=0).astype(BF16)
            scores.append(lax.dot_general(stacked, krep[kv], dims, preferred_element_type=F32))
    return jnp.concatenate(scores, axis=0)


def _window_valid(rows, qb, first):
    qi = lax.broadcasted_iota(jnp.int32, (rows, 2 * WINDOW), 0) & (qb - 1)
    sj = lax.broadcasted_iota(jnp.int32, (rows, 2 * WINDOW), 1)
    band = (sj <= WINDOW + qi) & (sj > qi)
    return band if first is False else band & ((sj >= WINDOW) | jnp.logical_not(first))


def _attn_softmax(s, sinks_ref, qb, firsts, bias=None):
    per_block = s.shape[0] // len(firsts)
    sink_parts = [jnp.full((qb, 1), sinks_ref[h], F32) for h in range(N_HEADS)]
    sink = jnp.concatenate(sink_parts * len(firsts), axis=0)
    if bias is None:
        valid = jnp.concatenate([_window_valid(per_block, qb, f) for f in firsts], axis=0)
        s = jnp.where(valid, s, MASK_VALUE)
    else:
        s = s + bias
    m = jnp.maximum(jnp.max(s, axis=-1, keepdims=True), sink)
    e = jnp.exp(s - m)
    denom = jnp.sum(e, axis=-1, keepdims=True) + jnp.exp(sink - m)
    return (e * (1.0 / denom)).astype(BF16)


def _attn_values(probs, vreps, qb, keys_on_lanes=False):
    lo = lax.broadcasted_iota(jnp.int32, (qb, LANES), 1) < HEAD_DIM
    rows = 2 * (N_HEADS // N_KV // 2) * qb
    dims = (((1,), (1,)), ((), ())) if keys_on_lanes else (((1,), (0,)), ((), ()))
    outs = []
    for b, vrep in enumerate(vreps):
        tiles = []
        for kv in range(N_KV):
            r0 = (b * N_KV + kv) * rows
            o = lax.dot_general(probs[r0:r0 + rows], vrep[kv], dims, preferred_element_type=F32)
            for jj in range(2):
                oa = o[(2 * jj) * qb:(2 * jj + 1) * qb]
                ob = o[(2 * jj + 1) * qb:(2 * jj + 2) * qb]
                tiles.append(jnp.where(lo, oa, ob))
        outs.append(jnp.concatenate(tiles, axis=1))
    return outs


def _mix_prompt_kernel(x1_ref, q_ref, k_ref, v_ref, u_ref, sga_ref, sgs_ref, sinks_ref,
                       wao_ref, wglu_ref, bglu_ref, wso_ref, wout_ref,
                       are_ref, aim_ref, ldt_ref, wbre_s, wbim_s, ccat_ref, d_ref, perm_ref, permt_ref,
                       x2_ref, x2_last_ref, sre_ref, sim_ref,
                       apr_s, api_s, a1_s, aseg_s, carry_s, cin_s,
                       bur_s, bui_s, xb_s, kprev_s, vprev_s, bias_s, yssm_s, attg_s, sgsp_s, x1p_s):
    b, i = pl.program_id(0), pl.program_id(1)
    nt = pl.num_programs(1)

    @pl.when((b == 0) & (i == 0))
    def _():
        ab_re, ab_im, _, _ = _ssm_discretize(are_ref, aim_ref, ldt_ref)
        a1_s[0:1, :] = ab_re
        a1_s[1:2, :] = ab_im
        pr, pi = ab_re, ab_im
        held = None
        for j in range(SCAN_SEG):
            rows = [jnp.broadcast_to(pr, (SUBLANES, SSM_S)), jnp.broadcast_to(pi, (SUBLANES, SSM_S))]
            if j % 2 == 0:
                held = rows
            else:
                rs = slice((j - 1) * SUBLANES, (j + 1) * SUBLANES)
                apr_s[rs, :] = jnp.concatenate([held[0], rows[0]], axis=0).astype(BF16)
                api_s[rs, :] = jnp.concatenate([held[1], rows[1]], axis=0).astype(BF16)
            if j == SCAN_SEG - 1:
                aseg_s[0:1, :] = pr
                aseg_s[1:2, :] = pi
            pr, pi = _cmul(pr, pi, ab_re, ab_im)
        for t, first in enumerate((False, True)):
            bias_s[t] = jnp.where(_window_valid(bias_s.shape[1], WINDOW, first), 0.0, MASK_VALUE)

    @pl.when(i == 0)
    def _():
        carry_s[...] = jnp.zeros_like(carry_s)
        kprev_s[...] = jnp.zeros_like(kprev_s)
        vprev_s[...] = jnp.zeros_like(vprev_s)
        for lag_s in (yssm_s, attg_s, sgsp_s, x1p_s):
            lag_s[...] = jnp.zeros_like(lag_s)

    lagged = [yssm_s[...], attg_s[...], sgsp_s[...], x1p_s[...]]

    k0, k1 = _replicate_heads(k_ref[...])
    v0, v1 = _replicate_heads(v_ref[...])
    qs, kreps, vreps, firsts = [], [], [], []
    for blk in range(MIX_TM // WINDOW):
        cur = slice(blk * WINDOW, (blk + 1) * WINDOW)
        if blk == 0:
            prev = [kprev_s[0], kprev_s[1], vprev_s[0], vprev_s[1]]
            firsts.append(i == 0)
        else:
            ps = slice((blk - 1) * WINDOW, blk * WINDOW)
            prev = [k0[ps], k1[ps], v0[ps], v1[ps]]
            firsts.append(False)
        qs.append(q_ref[cur, :])
        kreps.append([jnp.concatenate([prev[0], k0[cur]], axis=0), jnp.concatenate([prev[1], k1[cur]], axis=0)])
        vreps.append([jnp.concatenate([prev[2], v0[cur]], axis=0), jnp.concatenate([prev[3], v1[cur]], axis=0)])
    tail = slice(MIX_TM - WINDOW, MIX_TM)
    kprev_s[0], kprev_s[1] = k0[tail], k1[tail]
    vprev_s[0], vprev_s[1] = v0[tail], v1[tail]
    scores = _attn_scores(qs, kreps)

    half_in, half_st = SSM_W // 2, SSM_S // 2
    chunks = list(range(0, SSM_S, SCAN_LC))
    nblk = len(qs)
    blk_rows = scores.shape[0] // nblk
    u = u_ref[...]
    up = jnp.dot(perm_ref[...], u.astype(BF16), preferred_element_type=F32).astype(BF16)

    def expand_half(hh):
        uh = up[:, hh * half_in:(hh + 1) * half_in]
        bur_s[:, hh * half_st:(hh + 1) * half_st] = jnp.dot(uh, wbre_s[hh], preferred_element_type=F32)
        bui_s[:, hh * half_st:(hh + 1) * half_st] = jnp.dot(uh, wbim_s[hh], preferred_element_type=F32)

    def softmax_blocks(lo, hi):
        no_prev = (i == 0).astype(jnp.int32)
        bias = jnp.concatenate([bias_s[no_prev] if blk == 0 else bias_s[0]
                                for blk in range(lo, hi) for _ in range(N_HEADS)], axis=0)
        return _attn_softmax(scores[lo * blk_rows:hi * blk_rows], sinks_ref, WINDOW, firsts[lo:hi], bias)

    def scan_chunk(c0):
        cs = slice(c0, c0 + SCAN_LC)
        ar = jnp.broadcast_to(a1_s[0:1, cs], (SUBLANES, SCAN_LC))
        ai = jnp.broadcast_to(a1_s[1:2, cs], (SUBLANES, SCAN_LC))
        xr, xi = bur_s[0:SUBLANES, cs], bui_s[0:SUBLANES, cs]
        for j in range(1, SCAN_SEG):
            rs = slice(j * SUBLANES, (j + 1) * SUBLANES)
            xr, xi = (ar * xr - ai * xi + bur_s[rs, cs], ar * xi + ai * xr + bui_s[rs, cs])
            bur_s[rs, cs] = xr
            bui_s[rs, cs] = xi
        return xr

    mid = nblk // 2
    expand_half(0)
    probs_a = softmax_blocks(0, mid)
    expand_half(1)
    probs_b = softmax_blocks(mid, nblk)
    quarter = len(chunks) // 4
    scan_group = lambda g: [scan_chunk(c0) for c0 in chunks[g * quarter:(g + 1) * quarter]]
    started = scan_group(0)[-1]
    bits = lax.bitcast_convert_type(started[0:1, 0:1], jnp.uint32)
    zero = lax.bitcast_convert_type(lax.shift_right_logical(lax.shift_right_logical(bits, jnp.uint32(16)),
                                                            jnp.uint32(16)), F32)
    yssm_prev, attg_prev, sgs_prev, x1_prev = lagged
    x2_ref[...] = _gated_merge(x1_prev, attg_prev, yssm_prev + zero, sgs_prev,
                               wglu_ref, bglu_ref, wso_ref, wout_ref)
    attn = _attn_values(probs_a, vreps[0:mid], WINDOW)
    scan_group(1)
    attn += _attn_values(probs_b, vreps[mid:nblk], WINDOW)
    scan_group(2)
    attn_gated = _gated_attn(jnp.concatenate(attn, axis=0).astype(BF16), sga_ref[...], wao_ref)
    scan_group(3)

    last = MIX_TM - SUBLANES
    sr, si = aseg_s[0:1, :], aseg_s[1:2, :]
    cr, ci = carry_s[0:1, :], carry_s[1:2, :]
    for s in range(SUBLANES):
        cin_s[0, s:s + 1, :] = cr
        cin_s[1, s:s + 1, :] = ci
        pr, pi = _cmul(sr, si, cr, ci)
        cr = pr + bur_s[last + s:last + s + 1, :]
        ci = pi + bui_s[last + s:last + s + 1, :]
    carry_s[0:1, :] = cr
    carry_s[1:2, :] = ci
    pack = 2 * SUBLANES

    def fix_chunk(c0):
        cs = slice(c0, c0 + SCAN_LC)
        cinr = jnp.concatenate([cin_s[0, :, cs]] * 2, axis=0).astype(BF16)
        cini = jnp.concatenate([cin_s[1, :, cs]] * 2, axis=0).astype(BF16)
        col = (c0 // half_st) * SSM_S + c0 % half_st
        for r0 in range(0, MIX_TM, pack):
            rs = slice(r0, r0 + pack)
            pr, pi = apr_s[rs, cs], api_s[rs, cs]
            xb_s[rs, col:col + SCAN_LC] = bur_s[rs, cs].astype(BF16) + pr * cinr - pi * cini
            xb_s[rs, col + half_st:col + half_st + SCAN_LC] = bui_s[rs, cs].astype(BF16) + pr * cini + pi * cinr

    def project_half(hh):
        return jnp.dot(xb_s[:, hh * SSM_S:(hh + 1) * SSM_S], ccat_ref[hh], preferred_element_type=F32)

    per_half = len(chunks) // 2
    for c0 in chunks[:per_half]:
        fix_chunk(c0)
    yp0 = project_half(0)
    for c0 in chunks[per_half:]:
        fix_chunk(c0)
    yp = jnp.concatenate([yp0, project_half(1)], axis=1)
    y_ssm = jnp.dot(permt_ref[...], yp.astype(BF16), preferred_element_type=F32) + d_ref[...] * u

    yssm_s[...] = y_ssm
    attg_s[...] = attn_gated
    sgsp_s[...] = sgs_ref[...]
    x1p_s[...] = x1_ref[...]

    @pl.when(i == nt - 1)
    def _():
        x2_last_ref[...] = _gated_merge(x1_ref[...], attn_gated, y_ssm, sgs_ref[...],
                                        wglu_ref, bglu_ref, wso_ref, wout_ref)
        sre_ref[...] = carry_s[0:1, :]
        sim_ref[...] = carry_s[1:2, :]


def _mix_prompt(fr, p, batch, seq):
    x1, q, k, v, u, sga, sgs = fr
    tm = MIX_TM
    nt = seq // tm
    row = lambda w: pl.BlockSpec((tm, w), lambda b, i: (b * nt + i, 0))
    lag = pl.BlockSpec((tm, D_MODEL), lambda b, i: (b * nt + (i + nt - 1) % nt, 0))
    last = pl.BlockSpec((tm, D_MODEL), lambda b, i: (b, 0))
    consts = [p['wao'], p['wglu'], p['bglu'], p['wso'], p['wout'],
              p['are'], p['aim'], p['ldt'], p['wbre'], p['wbim'], p['ccat'], p['dskip'], p['perm'], p['permt']]
    st_spec = pl.BlockSpec((None, 1, SSM_S), lambda b, i: (b, 0, 0))
    scratch = [pltpu.VMEM((tm, SSM_S), BF16), pltpu.VMEM((tm, SSM_S), BF16),
               pltpu.VMEM((SUBLANES, SSM_S), F32), pltpu.VMEM((SUBLANES, SSM_S), F32),
               pltpu.VMEM((SUBLANES, SSM_S), F32), pltpu.VMEM((2, SUBLANES, SSM_S), F32),
               pltpu.VMEM((tm, SSM_S), F32), pltpu.VMEM((tm, SSM_S), F32), pltpu.VMEM((tm, 2 * SSM_S), BF16),
               pltpu.VMEM((N_KV, WINDOW, LANES), BF16), pltpu.VMEM((N_KV, WINDOW, LANES), BF16),
               pltpu.VMEM((2, WINDOW, 2 * WINDOW), F32),
               pltpu.VMEM((tm, SSM_W), F32), pltpu.VMEM((tm, D_MODEL), F32), pltpu.VMEM((tm, D_MODEL), BF16),
               pltpu.VMEM((tm, D_MODEL), F32)]
    return pl.pallas_call(
        _mix_prompt_kernel,
        grid=(batch, nt),
        in_specs=[row(D_MODEL), row(ATTN_W), row(KV_W), row(KV_W), row(SSM_W), row(D_MODEL), row(D_MODEL),
                  pl.BlockSpec(memory_space=pltpu.SMEM)] + [_const_spec(c.shape) for c in consts],
        out_specs=[lag, last, st_spec, st_spec],
        out_shape=[jax.ShapeDtypeStruct((batch * seq, D_MODEL), F32),
                   jax.ShapeDtypeStruct((batch * tm, D_MODEL), F32),
                   jax.ShapeDtypeStruct((batch, 1, SSM_S), F32),
                   jax.ShapeDtypeStruct((batch, 1, SSM_S), F32)],
        scratch_shapes=scratch,
        compiler_params=pltpu.CompilerParams(dimension_semantics=("arbitrary", "arbitrary"),
                                             vmem_limit_bytes=VMEM_LIMIT),
        name="mix_prompt",
    )(x1, q, k, v, u, sga, sgs, p['sinks'], *consts)


def _attn_sample_kernel(q_ref, kn_ref, vn_ref, ckt_ref, cvt_ref, sinks_ref, o_ref, nkt_ref, nvt_ref,
                        q8_s, knp_s, vnp_s):
    steps = q_ref.shape[0]
    shift = WINDOW - steps
    keep = lax.broadcasted_iota(jnp.int32, (HEAD_DIM, WINDOW), 1) < shift
    q8_s[...] = jnp.zeros_like(q8_s)
    knp_s[...] = jnp.zeros_like(knp_s)
    vnp_s[...] = jnp.zeros_like(vnp_s)
    qs, kreps, vreps = [], [], []
    for s in range(SAMPLE_BS):
        for t in range(steps):
            q8_s[s, t:t + 1, :] = q_ref[t, s:s + 1, :]
            knp_s[s, t:t + 1, :] = kn_ref[t, s:s + 1, :]
            vnp_s[s, t:t + 1, :] = vn_ref[t, s:s + 1, :]
        news = [knp_s[s].T, vnp_s[s].T]
        ends = [pltpu.roll(n, shift, axis=1) for n in news]
        krep, vrep = [], []
        for kv in range(N_KV):
            hs = slice(kv * HEAD_DIM, (kv + 1) * HEAD_DIM)
            for cache_ref, out_ref, new, end, reps in ((ckt_ref, nkt_ref, news[0], ends[0], krep),
                                                       (cvt_ref, nvt_ref, news[1], ends[1], vrep)):
                old = cache_ref[s, kv]
                out_ref[s, kv] = jnp.where(keep, pltpu.roll(old, shift, axis=1), end[hs])
                both = jnp.concatenate([old, new[hs]], axis=1).astype(BF16)
                reps.append(jnp.concatenate([both, both], axis=0))
        qs.append(q8_s[s])
        kreps.append(krep)
        vreps.append(vrep)
    probs = _attn_softmax(_attn_scores(qs, kreps, keys_on_lanes=True), sinks_ref, SUBLANES, [False] * SAMPLE_BS)
    outs = _attn_values(probs, vreps, SUBLANES, keys_on_lanes=True)
    for s in range(SAMPLE_BS):
        for t in range(steps):
            o_ref[t, s:s + 1, :] = outs[s][t:t + 1, :]


def _attn_sample(q, kn, vn, ckt, cvt, sinks, steps, nseq):
    bs = SAMPLE_BS
    blk3 = lambda w: pl.BlockSpec((steps, bs, w), lambda i: (0, i, 0))
    cache = pl.BlockSpec((bs, N_KV, HEAD_DIM, WINDOW), lambda i: (i, 0, 0, 0))
    cache_shape = jax.ShapeDtypeStruct((nseq, N_KV, HEAD_DIM, WINDOW), F32)
    return pl.pallas_call(
        _attn_sample_kernel,
        grid=(nseq // bs,),
        in_specs=[blk3(ATTN_W), blk3(KV_W), blk3(KV_W), cache, cache, pl.BlockSpec(memory_space=pltpu.SMEM)],
        out_specs=[blk3(ATTN_W), cache, cache],
        out_shape=[jax.ShapeDtypeStruct((steps, nseq, ATTN_W), F32), cache_shape, cache_shape],
        scratch_shapes=[pltpu.VMEM((bs, SUBLANES, ATTN_W), F32), pltpu.VMEM((bs, LANES, KV_W), F32),
                        pltpu.VMEM((bs, LANES, KV_W), F32)],
        compiler_params=pltpu.CompilerParams(dimension_semantics=("arbitrary",)),
        name="attn_sample",
    )(q.reshape(steps, nseq, ATTN_W), kn.reshape(steps, nseq, KV_W), vn.reshape(steps, nseq, KV_W),
      ckt, cvt, sinks)


def _mix_sample_kernel(x1_ref, attn_ref, u_ref, sga_ref, sgs_ref, x0r_ref, x0i_ref,
                       wao_ref, wglu_ref, bglu_ref, wso_ref, wout_ref,
                       are_ref, aim_ref, ldt_ref, wbre_s, wbim_s, ccat_ref, d_ref,
                       x2_ref, sre_ref, sim_ref,
                       bur_s, bui_s):
    nseq = x0r_ref.shape[0]
    steps = x1_ref.shape[0] // nseq
    ab_re, ab_im, _, _ = _ssm_discretize(are_ref, aim_ref, ldt_ref)
    u = u_ref[...]
    _expand_state(u.astype(BF16), wbre_s, wbim_s, bur_s, bui_s)
    for c0 in range(0, SSM_S, LANES):
        cs = slice(c0, c0 + LANES)
        ar, ai = ab_re[:, cs], ab_im[:, cs]
        xr, xi = x0r_ref[:, cs], x0i_ref[:, cs]
        for t in range(steps):
            rs = slice(t * nseq, (t + 1) * nseq)
            nr = ar * xr - ai * xi + bur_s[rs, cs]
            ni = ar * xi + ai * xr + bui_s[rs, cs]
            bur_s[rs, cs] = nr
            bui_s[rs, cs] = ni
            xr, xi = nr, ni
        sre_ref[:, cs] = xr
        sim_ref[:, cs] = xi
    y_ssm = _project_state(bur_s, bui_s, ccat_ref) + d_ref[...] * u
    attn_gated = _gated_attn(attn_ref[...].astype(BF16), sga_ref[...], wao_ref)
    x2_ref[...] = _gated_merge(x1_ref[...], attn_gated, y_ssm, sgs_ref[...],
                               wglu_ref, bglu_ref, wso_ref, wout_ref)


def _mix_sample(x1, attn, u, sga, sgs, x0r, x0i, p):
    n, nseq = x1.shape[0], x0r.shape[0]
    args = [x1, attn, u, sga, sgs, x0r, x0i, p['wao'], p['wglu'], p['bglu'], p['wso'], p['wout'],
            p['are'], p['aim'], p['ldt'], p['wbre'], p['wbim'], p['ccat'], p['dskip']]
    return pl.pallas_call(
        _mix_sample_kernel,
        grid=(1,),
        in_specs=[_const_spec(a.shape) for a in args],
        out_specs=[pl.BlockSpec((n, D_MODEL), lambda i: (0, 0)), pl.BlockSpec((nseq, SSM_S), lambda i: (0, 0)),
                   pl.BlockSpec((nseq, SSM_S), lambda i: (0, 0))],
        out_shape=[jax.ShapeDtypeStruct((n, D_MODEL), F32),
                   jax.ShapeDtypeStruct((nseq, SSM_S), F32),
                   jax.ShapeDtypeStruct((nseq, SSM_S), F32)],
        scratch_shapes=[pltpu.VMEM((n, SSM_S), F32), pltpu.VMEM((n, SSM_S), F32)],
        compiler_params=pltpu.CompilerParams(dimension_semantics=("arbitrary",),
                                             vmem_limit_bytes=VMEM_LIMIT),
        name="mix_sample",
    )(*args)


def _rope_tables(base_pos, off_pos):
    half = HEAD_DIM // 2
    lane = np.arange(LANES)
    inv = ROPE_THETA ** (-2.0 * (lane % half) / HEAD_DIM)
    first = (lane % HEAD_DIM) < half
    ab = np.asarray(base_pos, np.float64)[:, None] * inv[None, :]
    ao = np.asarray(off_pos, np.float64)[:, None] * inv[None, :]
    base = np.stack([np.cos(ab), np.sin(ab)], axis=1)
    co, so = np.cos(ao), np.sin(ao)
    lo = np.where(first, -1.0, 0.0)
    hi = np.where(first, 0.0, 1.0)
    off = np.stack([co, so, co * lo, so * lo, co * hi, so * hi], axis=0)
    return jnp.asarray(base, F32), jnp.asarray(off, F32)


def _block_diag_halves(m):
    g, r, c = m.shape
    hg = g // 2
    strip = jnp.transpose(m.reshape(2, hg, r, c), (0, 2, 1, 3)).reshape(2, 1, r, hg * c)
    full = jnp.broadcast_to(strip, (2, hg, r, hg * c)).reshape(2, hg * r, hg * c)
    mask = jnp.asarray(np.kron(np.eye(hg), np.ones((r, c))), m.dtype)
    return full * mask


def _layout_params(ffn1_norm, ffn1_w1, ffn1_w3, ffn1_w2, mix_norm, w_in, q_norm, k_norm, attn_sinks,
                   w_attn_out, ssm_a_re, ssm_a_im, ssm_log_dt, ssm_b_re, ssm_b_im, ssm_c_re, ssm_c_im,
                   ssm_d, w_glu, b_glu, w_ssm_out, w_out, ffn2_norm, ffn2_w1, ffn2_w3, ffn2_w2):
    l = 0
    row = lambda a: a.reshape(1, -1).astype(F32)
    head_ones = jnp.asarray(np.kron(np.eye(LANES // HEAD_DIM), np.ones((HEAD_DIM, HEAD_DIM))), F32)
    r = np.arange(MIX_TM)
    tok = (r % SUBLANES) * SCAN_SEG + r // SUBLANES
    perm = jnp.asarray(tok[:, None] == np.arange(MIX_TM)[None, :], BF16)
    c_cat = jnp.concatenate([_block_diag_halves(jnp.swapaxes(ssm_c_re[l], 1, 2)),
                             -_block_diag_halves(jnp.swapaxes(ssm_c_im[l], 1, 2))], axis=1)
    return dict(
        n1=row(ffn1_norm[l]), w1a=ffn1_w1[l].astype(BF16), w3a=ffn1_w3[l].astype(BF16), w2a=ffn1_w2[l].astype(BF16),
        nm=row(mix_norm[l]), win=w_in[l].astype(BF16),
        qn=row(jnp.tile(q_norm[l], LANES // HEAD_DIM)), kn=row(jnp.tile(k_norm[l], LANES // HEAD_DIM)),
        ones=head_ones.astype(BF16), sinks=attn_sinks[l].astype(F32),
        bglu=row(b_glu[l]),
        are=row(ssm_a_re[l]), aim=row(ssm_a_im[l]), ldt=row(jnp.repeat(ssm_log_dt[l], SSM_N)),
        bbre=_block_diag_halves(jnp.swapaxes(ssm_b_re[l], 1, 2).astype(F32)),
        bbim=_block_diag_halves(jnp.swapaxes(ssm_b_im[l], 1, 2).astype(F32)),
        ccat=c_cat.astype(BF16), dskip=row(ssm_d[l]), perm=perm, permt=perm.T,
        n2=row(ffn2_norm[l]),
    )


def kernel(x_prompt, x_sample, cache_k, cache_v, state_ssm_re, state_ssm_im, ffn1_norm, ffn1_w1, ffn1_w3, ffn1_w2, mix_norm, w_in, q_norm, k_norm, attn_sinks, w_attn_out, ssm_a_re, ssm_a_im, ssm_log_dt, ssm_b_re, ssm_b_im, ssm_c_re, ssm_c_im, ssm_d, w_glu, b_glu, w_ssm_out, w_out, ffn2_norm, ffn2_w1, ffn2_w3, ffn2_w2):
    assert ffn1_norm.shape[0] == 1, "single trunk layer"
    batch, seq, _ = x_prompt.shape
    nseq, steps, _ = x_sample.shape
    p = _layout_params(ffn1_norm, ffn1_w1, ffn1_w3, ffn1_w2, mix_norm, w_in, q_norm, k_norm, attn_sinks,
                       w_attn_out, ssm_a_re, ssm_a_im, ssm_log_dt, ssm_b_re, ssm_b_im, ssm_c_re, ssm_c_im,
                       ssm_d, w_glu, b_glu, w_ssm_out, w_out, ffn2_norm, ffn2_w1, ffn2_w3, ffn2_w2)

    p['wbre'], p['wbim'] = _ssm_weights(p)

    tile0 = np.arange(batch * seq // FRONT_TM) * FRONT_TM
    tabs_p = _rope_tables(tile0 % seq, np.arange(FRONT_TM))
    later = dict(wao=w_attn_out, wglu=w_glu, wso=w_ssm_out, wout=w_out, w1b=ffn2_w1, w3b=ffn2_w3, w2b=ffn2_w2)
    fr = _front(x_prompt.reshape(batch * seq, D_MODEL), tabs_p, p, BF16, cast=list(later.values()))
    p.update(zip(later, fr[7:]))
    fr = fr[:7]
    x2p, x2p_last, re_p, im_p = _mix_prompt(fr, p, batch, seq)
    y_prompt = _ffn(x2p, p, x2p_last, seq).reshape(batch, seq, D_MODEL)
    tail_kv = lambda a: a.reshape(batch, seq, KV_W)[:, seq - WINDOW:].reshape(batch, WINDOW, N_KV, HEAD_DIM)
    k_p, v_p = tail_kv(fr[2]), tail_kv(fr[3])

    xs = jnp.swapaxes(x_sample, 0, 1).reshape(steps * nseq, D_MODEL)
    tile0 = np.arange(steps * nseq // FRONT_TM) * FRONT_TM
    tabs_s = _rope_tables(PAST_LEN + tile0 // nseq, np.arange(FRONT_TM) // nseq)
    x1s, qs, ks, vs, us, sgas, sgss = _front(xs, tabs_s, p, F32)
    to_lanes = lambda c: jnp.transpose(c[0], (0, 2, 3, 1))
    attn_s, nkt, nvt = _attn_sample(qs, ks, vs, to_lanes(cache_k), to_lanes(cache_v), p['sinks'], steps, nseq)
    x2s, re_s, im_s = _mix_sample(x1s, attn_s.reshape(steps * nseq, ATTN_W), us, sgas, sgss,
                                  state_ssm_re[0].reshape(nseq, SSM_S), state_ssm_im[0].reshape(nseq, SSM_S), p)
    y_sample = jnp.swapaxes(_ffn(x2s, p).reshape(steps, nseq, D_MODEL), 0, 1)
    k_s, v_s = jnp.transpose(nkt, (0, 3, 1, 2)), jnp.transpose(nvt, (0, 3, 1, 2))

    st = lambda a, n: a.reshape(1, n, SSM_G, SSM_N)
    return (y_prompt, y_sample, k_p[None], v_p[None], st(re_p, batch), st(im_p, batch),
            k_s[None], v_s[None], st(re_s, nseq), st(im_s, nseq))
```

```python
import functools

import jax
import jax.numpy as jnp
import numpy as np
from jax import lax
from jax.experimental import pallas as pl
from jax.experimental.pallas import tpu as pltpu

F32 = jnp.float32
BF16 = jnp.bfloat16

D_MODEL = 1024
HEAD_DIM = 64
N_HEADS = 8
N_KV = 2
WINDOW = 128
ATTN_W = N_HEADS * HEAD_DIM
KV_W = N_KV * HEAD_DIM
SSM_W = 512
SSM_G = 32
SSM_CH = 16
SSM_N = 64
SSM_S = SSM_G * SSM_N
D_FF = 2816
IN_COLS = ATTN_W + 2 * KV_W + SSM_W + 2 * D_MODEL
ROPE_THETA = 10000.0
RMS_EPS = 1e-6
MASK_VALUE = -1e30
PAST_LEN = 16384

LANES = 128
SUBLANES = 8
MXU_DIM = 256

FRONT_TM = 512
FFN_TM = 1024
MIX_TM = 512
SCAN_ROWS = 256
SCAN_GROUPS = MIX_TM // SCAN_ROWS
SCAN_SEG = SCAN_ROWS // SUBLANES
FF_CHUNK = 256
SCAN_LC = 512
SAMPLE_BS = 32
VMEM_LIMIT = 58 * 1024 * 1024


def _const_spec(shape):
    nd = len(shape)
    return pl.BlockSpec(shape, lambda *_: (0,) * nd, pipeline_mode=pl.Buffered(1))


def _rms(x, w):
    ms = jnp.mean(x * x, axis=-1, keepdims=True)
    return x * lax.rsqrt(ms + RMS_EPS) * w


def _swiglu_residual(x, nw_ref, w1_ref, w3_ref, w2_ref):
    h = _rms(x, nw_ref[...]).astype(BF16)
    acc = None
    for c0 in range(0, D_FF, FF_CHUNK):
        cs = slice(c0, min(c0 + FF_CHUNK, D_FF))
        a = jnp.dot(h, w1_ref[:, cs], preferred_element_type=F32)
        b = jnp.dot(h, w3_ref[:, cs], preferred_element_type=F32)
        g = (a * jax.nn.sigmoid(a) * b).astype(BF16)
        t = jnp.dot(g, w2_ref[cs, :], preferred_element_type=F32)
        acc = t if acc is None else acc + t
    return x + 0.5 * acc


def _head_norm_rope(t, ones_ref, g_ref, cos, sinlo, sinhi, scale):
    ssum = jnp.dot((t * t).astype(BF16), ones_ref[...], preferred_element_type=F32)
    tn = t * lax.rsqrt(ssum * (1.0 / HEAD_DIM) + RMS_EPS) * g_ref[...]
    up = pltpu.roll(tn, HEAD_DIM // 2, axis=1)
    dn = pltpu.roll(tn, LANES - HEAD_DIM // 2, axis=1)
    r = tn * cos + dn * sinlo + up * sinhi
    return r * scale if scale != 1.0 else r


def _front_kernel(n_cast, x_ref, base_ref, off_ref,
                  n1_ref, w1_ref, w3_ref, w2_ref, nm_ref, win_ref, qn_ref, kn_ref, ones_ref, *refs):
    cast_in, refs = refs[:n_cast], refs[n_cast:]
    x1_ref, q_ref, k_ref, v_ref, u_ref, sga_ref, sgs_ref = refs[:7]
    for src, dst in zip(cast_in, refs[7:]):
        dst[...] = src[...].astype(BF16)
    x1 = _swiglu_residual(x_ref[...], n1_ref, w1_ref, w3_ref, w2_ref)
    x1_ref[...] = x1
    h = _rms(x1, nm_ref[...]).astype(BF16)
    cb, sb = base_ref[0:1, :], base_ref[1:2, :]
    cos = cb * off_ref[0] - sb * off_ref[1]
    sinlo = sb * off_ref[2] + cb * off_ref[3]
    sinhi = sb * off_ref[4] + cb * off_ref[5]
    o = ATTN_W + 2 * KV_W + SSM_W
    qkvu = jnp.dot(h, win_ref[:, 0:o], preferred_element_type=F32)
    for j in range(ATTN_W // LANES):
        q_ref[:, j * LANES:(j + 1) * LANES] = _head_norm_rope(
            qkvu[:, j * LANES:(j + 1) * LANES], ones_ref, qn_ref, cos, sinlo, sinhi,
            HEAD_DIM ** -0.5).astype(q_ref.dtype)
    k_ref[...] = _head_norm_rope(qkvu[:, ATTN_W:ATTN_W + KV_W], ones_ref, kn_ref, cos, sinlo, sinhi, 1.0)
    v_ref[...] = qkvu[:, ATTN_W + KV_W:ATTN_W + 2 * KV_W]
    u_ref[...] = qkvu[:, ATTN_W + 2 * KV_W:o]
    ga = jnp.dot(h, win_ref[:, o:o + D_MODEL], preferred_element_type=F32)
    sga_ref[...] = jax.nn.sigmoid(ga).astype(BF16)
    o += D_MODEL
    gs = jnp.dot(h, win_ref[:, o:o + D_MODEL], preferred_element_type=F32)
    sgs_ref[...] = jax.nn.sigmoid(gs).astype(BF16)


def _cast_block_rows(rows, steps):
    for blocks in range(steps, 0, -1):
        if rows % blocks == 0 and (rows // blocks) % (2 * SUBLANES) == 0:
            return rows // blocks
    raise ValueError(rows)


def _front(x, tabs, p, q_dtype, cast=()):
    n = x.shape[0]
    tm = FRONT_TM
    steps = n // tm
    base_tab, off_tab = tabs
    row = lambda w: pl.BlockSpec((tm, w), lambda i: (i, 0))
    consts = [p['n1'], p['w1a'], p['w3a'], p['w2a'], p['nm'], p['win'], p['qn'], p['kn'], p['ones']]
    outs = [(D_MODEL, F32), (ATTN_W, q_dtype), (KV_W, F32), (KV_W, F32), (SSM_W, F32),
            (D_MODEL, BF16), (D_MODEL, BF16)]
    cast_in, cast_out = [], []
    for w in cast:
        _, rows, cols = w.shape
        rb = _cast_block_rows(rows, steps)
        last = rows // rb - 1
        cast_in.append(pl.BlockSpec((None, rb, cols), lambda i, last=last: (0, jnp.minimum(i, last), 0)))
        cast_out.append(pl.BlockSpec((rb, cols), lambda i, last=last: (jnp.minimum(i, last), 0)))
    return pl.pallas_call(
        functools.partial(_front_kernel, len(cast)),
        grid=(steps,),
        in_specs=[row(D_MODEL), pl.BlockSpec((None, 2, LANES), lambda i: (i, 0, 0)), _const_spec(off_tab.shape)]
        + [_const_spec(c.shape) for c in consts] + cast_in,
        out_specs=[row(w) for w, _ in outs] + cast_out,
        out_shape=[jax.ShapeDtypeStruct((n, w), dt) for w, dt in outs]
        + [jax.ShapeDtypeStruct(w.shape[1:], BF16) for w in cast],
        compiler_params=pltpu.CompilerParams(dimension_semantics=("arbitrary",),
                                             vmem_limit_bytes=VMEM_LIMIT),
        name="front",
    )(x, base_tab, off_tab, *consts, *cast)


def _ffn_kernel(tiles_per_seq, x_ref, xl_ref, n_ref, w1_ref, w3_ref, w2_ref, y_ref):
    x = x_ref[...]
    if tiles_per_seq:
        keep = x.shape[0] - xl_ref.shape[0]
        is_last = pl.program_id(0) % tiles_per_seq == tiles_per_seq - 1
        x = jnp.concatenate([x[:keep], jnp.where(is_last, xl_ref[...], x[keep:])], axis=0)
    y_ref[...] = _swiglu_residual(x, n_ref, w1_ref, w3_ref, w2_ref)


def _ffn(x, p, x_last=None, seq=None):
    n = x.shape[0]
    tm = min(FFN_TM, n)
    weights = [p['w1b'], p['w3b'], p['w2b']]
    if x_last is None:
        tiles_per_seq, x_last = 0, x
        last_spec = pl.BlockSpec((SUBLANES, D_MODEL), lambda i: (0, 0))
    else:
        tiles_per_seq = seq // tm
        last_spec = pl.BlockSpec((MIX_TM, D_MODEL), lambda i: (i // tiles_per_seq, 0))
    return pl.pallas_call(
        functools.partial(_ffn_kernel, tiles_per_seq),
        grid=(n // tm,),
        in_specs=[pl.BlockSpec((tm, D_MODEL), lambda i: (i, 0)), last_spec, _const_spec(p['n2'].shape)]
        + [_const_spec(w.shape) for w in weights],
        out_specs=pl.BlockSpec((tm, D_MODEL), lambda i: (i, 0)),
        out_shape=jax.ShapeDtypeStruct((n, D_MODEL), F32),
        compiler_params=pltpu.CompilerParams(dimension_semantics=("arbitrary",),
                                             vmem_limit_bytes=VMEM_LIMIT),
        name="ffn",
    )(x, x_last, p['n2'], *weights)


def _cmul(ar, ai, br, bi):
    return ar * br - ai * bi, ar * bi + ai * br


def _ssm_discretize(are_ref, aim_ref, ldt_ref):
    a_re, a_im = are_ref[...], aim_ref[...]
    dt = jnp.exp(ldt_ref[...])
    mag = jnp.exp(dt * a_re)
    ab_re = mag * jnp.cos(dt * a_im)
    ab_im = mag * jnp.sin(dt * a_im)
    den = a_re * a_re + a_im * a_im
    nr, ni = ab_re - 1.0, ab_im
    f_re = (nr * a_re + ni * a_im) / den
    f_im = (ni * a_re - nr * a_im) / den
    return ab_re, ab_im, f_re, f_im


def _build_wb(bbre_ref, bbim_ref, f_re, f_im, wbre_s, wbim_s):
    half = SSM_S // 2
    for hh in range(2):
        fr, fi = f_re[:, hh * half:(hh + 1) * half], f_im[:, hh * half:(hh + 1) * half]
        br, bi = bbre_ref[hh], bbim_ref[hh]
        wbre_s[hh] = (br * fr - bi * fi).astype(BF16)
        wbim_s[hh] = (br * fi + bi * fr).astype(BF16)


def _ssm_weights_kernel(are_ref, aim_ref, ldt_ref, bbre_ref, bbim_ref, wbre_ref, wbim_ref):
    _, _, f_re, f_im = _ssm_discretize(are_ref, aim_ref, ldt_ref)
    _build_wb(bbre_ref, bbim_ref, f_re, f_im, wbre_ref, wbim_ref)


def _ssm_weights(p):
    shape = jax.ShapeDtypeStruct(p['bbre'].shape, BF16)
    return pl.pallas_call(_ssm_weights_kernel, out_shape=[shape, shape], name="ssm_weights")(
        p['are'], p['aim'], p['ldt'], p['bbre'], p['bbim'])


def _expand_state(ub, wbre_s, wbim_s, bur_s, bui_s):
    half_in, half_st = SSM_W // 2, SSM_S // 2
    for hh in range(2):
        uh = ub[:, hh * half_in:(hh + 1) * half_in]
        bur_s[:, hh * half_st:(hh + 1) * half_st] = jnp.dot(uh, wbre_s[hh], preferred_element_type=F32)
        bui_s[:, hh * half_st:(hh + 1) * half_st] = jnp.dot(uh, wbim_s[hh], preferred_element_type=F32)


def _project_state(xr_s, xi_s, ccat_ref):
    half_st = SSM_S // 2
    outs = []
    for hh in range(2):
        sl = slice(hh * half_st, (hh + 1) * half_st)
        xcat = jnp.concatenate([xr_s[:, sl].astype(BF16), xi_s[:, sl].astype(BF16)], axis=1)
        outs.append(jnp.dot(xcat, ccat_ref[hh], preferred_element_type=F32))
    return jnp.concatenate(outs, axis=1)


def _gated_attn(attn_b, sga, wao_ref):
    return sga.astype(F32) * jnp.dot(attn_b, wao_ref[...], preferred_element_type=F32)


def _gated_sum(attn_gated, y_ssm, sgs, wglu_ref, bglu_ref, wso_ref):
    z = jax.nn.gelu(y_ssm, approximate=True)
    gl = jnp.dot(z.astype(BF16), wglu_ref[...], preferred_element_type=F32) + bglu_ref[...]
    ssm = (z * jax.nn.sigmoid(gl)).astype(BF16)
    merged = attn_gated + sgs.astype(F32) * jnp.dot(ssm, wso_ref[...], preferred_element_type=F32)
    return merged.astype(BF16)


def _gated_merge(x1, attn_gated, y_ssm, sgs, wglu_ref, bglu_ref, wso_ref, wout_ref):
    merged = _gated_sum(attn_gated, y_ssm, sgs, wglu_ref, bglu_ref, wso_ref)
    return x1 + jnp.dot(merged, wout_ref[...], preferred_element_type=F32)


def _replicate_heads(kv):
    sw = pltpu.roll(kv, HEAD_DIM, axis=1)
    lo = lax.broadcasted_iota(jnp.int32, kv.shape, 1) < HEAD_DIM
    return jnp.where(lo, kv, sw).astype(BF16), jnp.where(lo, sw, kv).astype(BF16)


def _attn_scores(qs, kreps, keys_on_lanes=False):
    qb = qs[0].shape[0]
    lo = lax.broadcasted_iota(jnp.int32, (qb, LANES), 1) < HEAD_DIM
    dims = (((1,), (0,)), ((), ())) if keys_on_lanes else (((1,), (1,)), ((), ()))
    scores = []
    for q, krep in zip(qs, kreps):
        if q.dtype != BF16 or qb % (2 * SUBLANES):
            q = q.astype(F32)
        zero = jnp.zeros((), q.dtype)
        for kv in range(N_KV):
            parts = []
            for jj in range(2):
                t = q[:, (2 * kv + jj) * LANES:(2 * kv + jj + 1) * LANES]
                parts += [jnp.where(lo, t, zero), jnp.where(lo, zero, t)]
            stacked = jnp.concatenate(parts, axis=0).astype(BF16)
            scores.append(lax.dot_general(stacked, krep[kv], dims, preferred_element_type=F32))
    return jnp.concatenate(scores, axis=0)


def _window_valid(rows, qb, first):
    qi = lax.broadcasted_iota(jnp.int32, (rows, 2 * WINDOW), 0) & (qb - 1)
    sj = lax.broadcasted_iota(jnp.int32, (rows, 2 * WINDOW), 1)
    band = (sj <= WINDOW + qi) & (sj > qi)
    return band if first is False else band & ((sj >= WINDOW) | jnp.logical_not(first))


def _attn_softmax(s, sinks_ref, qb, firsts, bias=None):
    per_block = s.shape[0] // len(firsts)
    sink_parts = [jnp.full((qb, 1), sinks_ref[h], F32) for h in range(N_HEADS)]
    sink = jnp.concatenate(sink_parts * len(firsts), axis=0)
    if bias is None:
        valid = jnp.concatenate([_window_valid(per_block, qb, f) for f in firsts], axis=0)
        s = jnp.where(valid, s, MASK_VALUE)
    else:
        s = s + bias
    m = jnp.maximum(jnp.max(s, axis=-1, keepdims=True), sink)
    e = jnp.exp(s - m)
    denom = jnp.sum(e, axis=-1, keepdims=True) + jnp.exp(sink - m)
    return (e * (1.0 / denom)).astype(BF16)


def _attn_values(probs, vreps, qb, keys_on_lanes=False):
    lo = lax.broadcasted_iota(jnp.int32, (qb, LANES), 1) < HEAD_DIM
    rows = 2 * (N_HEADS // N_KV // 2) * qb
    dims = (((1,), (1,)), ((), ())) if keys_on_lanes else (((1,), (0,)), ((), ()))
    outs = []
    for b, vrep in enumerate(vreps):
        tiles = []
        for kv in range(N_KV):
            r0 = (b * N_KV + kv) * rows
            o = lax.dot_general(probs[r0:r0 + rows], vrep[kv], dims, preferred_element_type=F32)
            for jj in range(2):
                oa = o[(2 * jj) * qb:(2 * jj + 1) * qb]
                ob = o[(2 * jj + 1) * qb:(2 * jj + 2) * qb]
                tiles.append(jnp.where(lo, oa, ob))
        outs.append(jnp.concatenate(tiles, axis=1))
    return outs


def _mix_prompt_kernel(x1_ref, q_ref, k_ref, v_ref, u_ref, sga_ref, sgs_ref, sinks_ref,
                       wao_ref, wglu_ref, bglu_ref, wso_ref, wout_ref,
                       are_ref, aim_ref, ldt_ref, wbre_s, wbim_s, ccat_ref, d_ref, perm_ref, permt_ref,
                       x2_ref, x2_last_ref, sre_ref, sim_ref,
                       apr_s, api_s, a1_s, aseg_s, carry_s, cin_s,
                       bur_s, bui_s, xb_s, kprev_s, vprev_s, bias_s, yssm_s, attg_s, sgsp_s, x1p_s):
    b, i = pl.program_id(0), pl.program_id(1)
    nt = pl.num_programs(1)

    @pl.when((b == 0) & (i == 0))
    def _():
        ab_re, ab_im, _, _ = _ssm_discretize(are_ref, aim_ref, ldt_ref)
        a1_s[0:1, :] = ab_re
        a1_s[1:2, :] = ab_im
        pr, pi = ab_re, ab_im
        held = None
        for j in range(SCAN_SEG):
            rows = [jnp.broadcast_to(pr, (SUBLANES, SSM_S)), jnp.broadcast_to(pi, (SUBLANES, SSM_S))]
            if j % 2 == 0:
                held = rows
            else:
                rs = slice((j - 1) * SUBLANES, (j + 1) * SUBLANES)
                apr_s[rs, :] = jnp.concatenate([held[0], rows[0]], axis=0).astype(BF16)
                api_s[rs, :] = jnp.concatenate([held[1], rows[1]], axis=0).astype(BF16)
            if j == SCAN_SEG - 1:
                aseg_s[0:1, :] = pr
                aseg_s[1:2, :] = pi
            pr, pi = _cmul(pr, pi, ab_re, ab_im)
        for t, first in enumerate((False, True)):
            bias_s[t] = jnp.where(_window_valid(bias_s.shape[1], WINDOW, first), 0.0, MASK_VALUE)

    @pl.when(i == 0)
    def _():
        carry_s[...] = jnp.zeros_like(carry_s)
        kprev_s[...] = jnp.zeros_like(kprev_s)
        vprev_s[...] = jnp.zeros_like(vprev_s)
        for lag_s in (yssm_s, attg_s, sgsp_s, x1p_s):
            lag_s[...] = jnp.zeros_like(lag_s)

    lagged = [yssm_s[...], attg_s[...], sgsp_s[...], x1p_s[...]]

    k0, k1 = _replicate_heads(k_ref[...])
    v0, v1 = _replicate_heads(v_ref[...])
    qs, kreps, vreps, firsts = [], [], [], []
    for blk in range(MIX_TM // WINDOW):
        cur = slice(blk * WINDOW, (blk + 1) * WINDOW)
        if blk == 0:
            prev = [kprev_s[0], kprev_s[1], vprev_s[0], vprev_s[1]]
            firsts.append(i == 0)
        else:
            ps = slice((blk - 1) * WINDOW, blk * WINDOW)
            prev = [k0[ps], k1[ps], v0[ps], v1[ps]]
            firsts.append(False)
        qs.append(q_ref[cur, :])
        kreps.append([jnp.concatenate([prev[0], k0[cur]], axis=0), jnp.concatenate([prev[1], k1[cur]], axis=0)])
        vreps.append([jnp.concatenate([prev[2], v0[cur]], axis=0), jnp.concatenate([prev[3], v1[cur]], axis=0)])
    tail = slice(MIX_TM - WINDOW, MIX_TM)
    kprev_s[0], kprev_s[1] = k0[tail], k1[tail]
    vprev_s[0], vprev_s[1] = v0[tail], v1[tail]
    scores = _attn_scores(qs, kreps)

    half_in, half_st = SSM_W // 2, SSM_S // 2
    chunks = list(range(0, SSM_S, SCAN_LC))
    groups = [g * SCAN_ROWS for g in range(SCAN_GROUPS)]
    nblk = len(qs)
    blk_rows = scores.shape[0] // nblk
    u = u_ref[...]
    ub = u.astype(BF16)
    up = jnp.concatenate([jnp.dot(perm_ref[...], ub[g0:g0 + SCAN_ROWS], preferred_element_type=F32)
                          for g0 in groups], axis=0).astype(BF16)

    def expand_half(hh):
        uh = up[:, hh * half_in:(hh + 1) * half_in]
        bur_s[:, hh * half_st:(hh + 1) * half_st] = jnp.dot(uh, wbre_s[hh], preferred_element_type=F32)
        bui_s[:, hh * half_st:(hh + 1) * half_st] = jnp.dot(uh, wbim_s[hh], preferred_element_type=F32)

    def softmax_blocks(lo, hi):
        no_prev = (i == 0).astype(jnp.int32)
        bias = jnp.concatenate([bias_s[no_prev] if blk == 0 else bias_s[0]
                                for blk in range(lo, hi) for _ in range(N_HEADS)], axis=0)
        return _attn_softmax(scores[lo * blk_rows:hi * blk_rows], sinks_ref, WINDOW, firsts[lo:hi], bias)

    def scan_chunk(c0):
        cs = slice(c0, c0 + SCAN_LC)
        ar = jnp.broadcast_to(a1_s[0:1, cs], (SUBLANES, SCAN_LC))
        ai = jnp.broadcast_to(a1_s[1:2, cs], (SUBLANES, SCAN_LC))
        state = [(bur_s[g0:g0 + SUBLANES, cs], bui_s[g0:g0 + SUBLANES, cs]) for g0 in groups]
        for j in range(1, SCAN_SEG):
            for g, g0 in enumerate(groups):
                rs = slice(g0 + j * SUBLANES, g0 + (j + 1) * SUBLANES)
                xr, xi = state[g]
                xr, xi = (ar * xr - ai * xi + bur_s[rs, cs], ar * xi + ai * xr + bui_s[rs, cs])
                bur_s[rs, cs] = xr
                bui_s[rs, cs] = xi
                state[g] = (xr, xi)
        return state[0][0]

    mid = nblk // 2
    expand_half(0)
    probs_a = softmax_blocks(0, mid)
    expand_half(1)
    probs_b = softmax_blocks(mid, nblk)
    quarter = len(chunks) // 4
    scan_group = lambda g: [scan_chunk(c0) for c0 in chunks[g * quarter:(g + 1) * quarter]]
    started = scan_group(0)[-1]
    bits = lax.bitcast_convert_type(started[0:1, 0:1], jnp.uint32)
    zero = lax.bitcast_convert_type(lax.shift_right_logical(lax.shift_right_logical(bits, jnp.uint32(16)),
                                                            jnp.uint32(16)), F32)
    yssm_prev, attg_prev, sgs_prev, x1_prev = lagged
    x2_ref[...] = _gated_merge(x1_prev, attg_prev, yssm_prev + zero, sgs_prev,
                               wglu_ref, bglu_ref, wso_ref, wout_ref)
    attn = _attn_values(probs_a, vreps[0:mid], WINDOW)
    scan_group(1)
    attn += _attn_values(probs_b, vreps[mid:nblk], WINDOW)
    scan_group(2)
    attn_gated = _gated_attn(jnp.concatenate(attn, axis=0).astype(BF16), sga_ref[...], wao_ref)
    scan_group(3)

    sr, si = aseg_s[0:1, :], aseg_s[1:2, :]
    cr, ci = carry_s[0:1, :], carry_s[1:2, :]
    for g, g0 in enumerate(groups):
        last = g0 + SCAN_ROWS - SUBLANES
        for s in range(SUBLANES):
            k = g * SUBLANES + s
            cin_s[0, k:k + 1, :] = cr
            cin_s[1, k:k + 1, :] = ci
            pr, pi = _cmul(sr, si, cr, ci)
            cr = pr + bur_s[last + s:last + s + 1, :]
            ci = pi + bui_s[last + s:last + s + 1, :]
    carry_s[0:1, :] = cr
    carry_s[1:2, :] = ci
    pack = 2 * SUBLANES

    def fix_chunk(c0):
        cs = slice(c0, c0 + SCAN_LC)
        col = (c0 // half_st) * SSM_S + c0 % half_st
        for g, g0 in enumerate(groups):
            seg = slice(g * SUBLANES, (g + 1) * SUBLANES)
            cinr = jnp.concatenate([cin_s[0, seg, cs]] * 2, axis=0).astype(BF16)
            cini = jnp.concatenate([cin_s[1, seg, cs]] * 2, axis=0).astype(BF16)
            for r0 in range(0, SCAN_ROWS, pack):
                rs, ts = slice(g0 + r0, g0 + r0 + pack), slice(r0, r0 + pack)
                pr, pi = apr_s[ts, cs], api_s[ts, cs]
                xb_s[rs, col:col + SCAN_LC] = bur_s[rs, cs].astype(BF16) + pr * cinr - pi * cini
                xb_s[rs, col + half_st:col + half_st + SCAN_LC] = (bui_s[rs, cs].astype(BF16)
                                                                   + pr * cini + pi * cinr)

    def project_half(hh):
        return jnp.dot(xb_s[:, hh * SSM_S:(hh + 1) * SSM_S], ccat_ref[hh], preferred_element_type=F32)

    per_half = len(chunks) // 2
    for c0 in chunks[:per_half]:
        fix_chunk(c0)
    yp0 = project_half(0)
    for c0 in chunks[per_half:]:
        fix_chunk(c0)
    yp = jnp.concatenate([yp0, project_half(1)], axis=1)
    ypb = yp.astype(BF16)
    y_ssm = jnp.concatenate([jnp.dot(permt_ref[...], ypb[g0:g0 + SCAN_ROWS], preferred_element_type=F32)
                             for g0 in groups], axis=0) + d_ref[...] * u

    yssm_s[...] = y_ssm
    attg_s[...] = attn_gated
    sgsp_s[...] = sgs_ref[...]
    x1p_s[...] = x1_ref[...]

    @pl.when(i == nt - 1)
    def _():
        x2_last_ref[...] = _gated_merge(x1_ref[...], attn_gated, y_ssm, sgs_ref[...],
                                        wglu_ref, bglu_ref, wso_ref, wout_ref)
        sre_ref[...] = carry_s[0:1, :]
        sim_ref[...] = carry_s[1:2, :]


def _mix_prompt(fr, p, batch, seq):
    x1, q, k, v, u, sga, sgs = fr
    tm = MIX_TM
    nt = seq // tm
    row = lambda w: pl.BlockSpec((tm, w), lambda b, i: (b * nt + i, 0))
    lag = pl.BlockSpec((tm, D_MODEL), lambda b, i: (b * nt + (i + nt - 1) % nt, 0))
    last = pl.BlockSpec((tm, D_MODEL), lambda b, i: (b, 0))
    consts = [p['wao'], p['wglu'], p['bglu'], p['wso'], p['wout'],
              p['are'], p['aim'], p['ldt'], p['wbre'], p['wbim'], p['ccat'], p['dskip'], p['perm'], p['permt']]
    st_spec = pl.BlockSpec((None, 1, SSM_S), lambda b, i: (b, 0, 0))
    scratch = [pltpu.VMEM((SCAN_ROWS, SSM_S), BF16), pltpu.VMEM((SCAN_ROWS, SSM_S), BF16),
               pltpu.VMEM((SUBLANES, SSM_S), F32), pltpu.VMEM((SUBLANES, SSM_S), F32),
               pltpu.VMEM((SUBLANES, SSM_S), F32), pltpu.VMEM((2, SCAN_GROUPS * SUBLANES, SSM_S), F32),
               pltpu.VMEM((tm, SSM_S), F32), pltpu.VMEM((tm, SSM_S), F32), pltpu.VMEM((tm, 2 * SSM_S), BF16),
               pltpu.VMEM((N_KV, WINDOW, LANES), BF16), pltpu.VMEM((N_KV, WINDOW, LANES), BF16),
               pltpu.VMEM((2, WINDOW, 2 * WINDOW), F32),
               pltpu.VMEM((tm, SSM_W), F32), pltpu.VMEM((tm, D_MODEL), F32), pltpu.VMEM((tm, D_MODEL), BF16),
               pltpu.VMEM((tm, D_MODEL), F32)]
    return pl.pallas_call(
        _mix_prompt_kernel,
        grid=(batch, nt),
        in_specs=[row(D_MODEL), row(ATTN_W), row(KV_W), row(KV_W), row(SSM_W), row(D_MODEL), row(D_MODEL),
                  pl.BlockSpec(memory_space=pltpu.SMEM)] + [_const_spec(c.shape) for c in consts],
        out_specs=[lag, last, st_spec, st_spec],
        out_shape=[jax.ShapeDtypeStruct((batch * seq, D_MODEL), F32),
                   jax.ShapeDtypeStruct((batch * tm, D_MODEL), F32),
                   jax.ShapeDtypeStruct((batch, 1, SSM_S), F32),
                   jax.ShapeDtypeStruct((batch, 1, SSM_S), F32)],
        scratch_shapes=scratch,
        compiler_params=pltpu.CompilerParams(dimension_semantics=("arbitrary", "arbitrary"),
                                             vmem_limit_bytes=VMEM_LIMIT),
        name="mix_prompt",
    )(x1, q, k, v, u, sga, sgs, p['sinks'], *consts)


def _attn_sample_kernel(q_ref, kn_ref, vn_ref, ckt_ref, cvt_ref, sinks_ref, o_ref, nkt_ref, nvt_ref,
                        q8_s, knp_s, vnp_s):
    steps = q_ref.shape[0]
    shift = WINDOW - steps
    keep = lax.broadcasted_iota(jnp.int32, (HEAD_DIM, WINDOW), 1) < shift
    q8_s[...] = jnp.zeros_like(q8_s)
    knp_s[...] = jnp.zeros_like(knp_s)
    vnp_s[...] = jnp.zeros_like(vnp_s)
    qs, kreps, vreps = [], [], []
    for s in range(SAMPLE_BS):
        for t in range(steps):
            q8_s[s, t:t + 1, :] = q_ref[t, s:s + 1, :]
            knp_s[s, t:t + 1, :] = kn_ref[t, s:s + 1, :]
            vnp_s[s, t:t + 1, :] = vn_ref[t, s:s + 1, :]
        news = [knp_s[s].T, vnp_s[s].T]
        ends = [pltpu.roll(n, shift, axis=1) for n in news]
        krep, vrep = [], []
        for kv in range(N_KV):
            hs = slice(kv * HEAD_DIM, (kv + 1) * HEAD_DIM)
            for cache_ref, out_ref, new, end, reps in ((ckt_ref, nkt_ref, news[0], ends[0], krep),
                                                       (cvt_ref, nvt_ref, news[1], ends[1], vrep)):
                old = cache_ref[s, kv]
                out_ref[s, kv] = jnp.where(keep, pltpu.roll(old, shift, axis=1), end[hs])
                both = jnp.concatenate([old, new[hs]], axis=1).astype(BF16)
                reps.append(jnp.concatenate([both, both], axis=0))
        qs.append(q8_s[s])
        kreps.append(krep)
        vreps.append(vrep)
    probs = _attn_softmax(_attn_scores(qs, kreps, keys_on_lanes=True), sinks_ref, SUBLANES, [False] * SAMPLE_BS)
    outs = _attn_values(probs, vreps, SUBLANES, keys_on_lanes=True)
    for s in range(SAMPLE_BS):
        for t in range(steps):
            o_ref[t, s:s + 1, :] = outs[s][t:t + 1, :]


def _attn_sample(q, kn, vn, ckt, cvt, sinks, steps, nseq):
    bs = SAMPLE_BS
    blk3 = lambda w: pl.BlockSpec((steps, bs, w), lambda i: (0, i, 0))
    cache = pl.BlockSpec((bs, N_KV, HEAD_DIM, WINDOW), lambda i: (i, 0, 0, 0))
    cache_shape = jax.ShapeDtypeStruct((nseq, N_KV, HEAD_DIM, WINDOW), F32)
    return pl.pallas_call(
        _attn_sample_kernel,
        grid=(nseq // bs,),
        in_specs=[blk3(ATTN_W), blk3(KV_W), blk3(KV_W), cache, cache, pl.BlockSpec(memory_space=pltpu.SMEM)],
        out_specs=[blk3(ATTN_W), cache, cache],
        out_shape=[jax.ShapeDtypeStruct((steps, nseq, ATTN_W), F32), cache_shape, cache_shape],
        scratch_shapes=[pltpu.VMEM((bs, SUBLANES, ATTN_W), F32), pltpu.VMEM((bs, LANES, KV_W), F32),
                        pltpu.VMEM((bs, LANES, KV_W), F32)],
        compiler_params=pltpu.CompilerParams(dimension_semantics=("arbitrary",)),
        name="attn_sample",
    )(q.reshape(steps, nseq, ATTN_W), kn.reshape(steps, nseq, KV_W), vn.reshape(steps, nseq, KV_W),
      ckt, cvt, sinks)


def _mix_sample_kernel(x1_ref, attn_ref, u_ref, sga_ref, sgs_ref, x0r_ref, x0i_ref,
                       wao_ref, wglu_ref, bglu_ref, wso_ref, wout_ref,
                       are_ref, aim_ref, ldt_ref, wbre_s, wbim_s, ccat_ref, d_ref,
                       x2_ref, sre_ref, sim_ref,
                       bur_s, bui_s):
    nseq = x0r_ref.shape[0]
    steps = x1_ref.shape[0] // nseq
    ab_re, ab_im, _, _ = _ssm_discretize(are_ref, aim_ref, ldt_ref)
    u = u_ref[...]
    _expand_state(u.astype(BF16), wbre_s, wbim_s, bur_s, bui_s)
    for c0 in range(0, SSM_S, LANES):
        cs = slice(c0, c0 + LANES)
        ar, ai = ab_re[:, cs], ab_im[:, cs]
        xr, xi = x0r_ref[:, cs], x0i_ref[:, cs]
        for t in range(steps):
            rs = slice(t * nseq, (t + 1) * nseq)
            nr = ar * xr - ai * xi + bur_s[rs, cs]
            ni = ar * xi + ai * xr + bui_s[rs, cs]
            bur_s[rs, cs] = nr
            bui_s[rs, cs] = ni
            xr, xi = nr, ni
        sre_ref[:, cs] = xr
        sim_ref[:, cs] = xi
    y_ssm = _project_state(bur_s, bui_s, ccat_ref) + d_ref[...] * u
    attn_gated = _gated_attn(attn_ref[...].astype(BF16), sga_ref[...], wao_ref)
    x2_ref[...] = _gated_merge(x1_ref[...], attn_gated, y_ssm, sgs_ref[...],
                               wglu_ref, bglu_ref, wso_ref, wout_ref)


def _mix_sample(x1, attn, u, sga, sgs, x0r, x0i, p):
    n, nseq = x1.shape[0], x0r.shape[0]
    args = [x1, attn, u, sga, sgs, x0r, x0i, p['wao'], p['wglu'], p['bglu'], p['wso'], p['wout'],
            p['are'], p['aim'], p['ldt'], p['wbre'], p['wbim'], p['ccat'], p['dskip']]
    return pl.pallas_call(
        _mix_sample_kernel,
        grid=(1,),
        in_specs=[_const_spec(a.shape) for a in args],
        out_specs=[pl.BlockSpec((n, D_MODEL), lambda i: (0, 0)), pl.BlockSpec((nseq, SSM_S), lambda i: (0, 0)),
                   pl.BlockSpec((nseq, SSM_S), lambda i: (0, 0))],
        out_shape=[jax.ShapeDtypeStruct((n, D_MODEL), F32),
                   jax.ShapeDtypeStruct((nseq, SSM_S), F32),
                   jax.ShapeDtypeStruct((nseq, SSM_S), F32)],
        scratch_shapes=[pltpu.VMEM((n, SSM_S), F32), pltpu.VMEM((n, SSM_S), F32)],
        compiler_params=pltpu.CompilerParams(dimension_semantics=("arbitrary",),
                                             vmem_limit_bytes=VMEM_LIMIT),
        name="mix_sample",
    )(*args)


def _rope_tables(base_pos, off_pos):
    half = HEAD_DIM // 2
    lane = np.arange(LANES)
    inv = ROPE_THETA ** (-2.0 * (lane % half) / HEAD_DIM)
    first = (lane % HEAD_DIM) < half
    ab = np.asarray(base_pos, np.float64)[:, None] * inv[None, :]
    ao = np.asarray(off_pos, np.float64)[:, None] * inv[None, :]
    base = np.stack([np.cos(ab), np.sin(ab)], axis=1)
    co, so = np.cos(ao), np.sin(ao)
    lo = np.where(first, -1.0, 0.0)
    hi = np.where(first, 0.0, 1.0)
    off = np.stack([co, so, co * lo, so * lo, co * hi, so * hi], axis=0)
    return jnp.asarray(base, F32), jnp.asarray(off, F32)


def _block_diag_halves(m):
    g, r, c = m.shape
    hg = g // 2
    strip = jnp.transpose(m.reshape(2, hg, r, c), (0, 2, 1, 3)).reshape(2, 1, r, hg * c)
    full = jnp.broadcast_to(strip, (2, hg, r, hg * c)).reshape(2, hg * r, hg * c)
    mask = jnp.asarray(np.kron(np.eye(hg), np.ones((r, c))), m.dtype)
    return full * mask


def _layout_params(ffn1_norm, ffn1_w1, ffn1_w3, ffn1_w2, mix_norm, w_in, q_norm, k_norm, attn_sinks,
                   w_attn_out, ssm_a_re, ssm_a_im, ssm_log_dt, ssm_b_re, ssm_b_im, ssm_c_re, ssm_c_im,
                   ssm_d, w_glu, b_glu, w_ssm_out, w_out, ffn2_norm, ffn2_w1, ffn2_w3, ffn2_w2):
    l = 0
    row = lambda a: a.reshape(1, -1).astype(F32)
    head_ones = jnp.asarray(np.kron(np.eye(LANES // HEAD_DIM), np.ones((HEAD_DIM, HEAD_DIM))), F32)
    r = np.arange(SCAN_ROWS)
    tok = (r % SUBLANES) * SCAN_SEG + r // SUBLANES
    perm = jnp.asarray(tok[:, None] == np.arange(SCAN_ROWS)[None, :], BF16)
    c_cat = jnp.concatenate([_block_diag_halves(jnp.swapaxes(ssm_c_re[l], 1, 2)),
                             -_block_diag_halves(jnp.swapaxes(ssm_c_im[l], 1, 2))], axis=1)
    return dict(
        n1=row(ffn1_norm[l]), w1a=ffn1_w1[l].astype(BF16), w3a=ffn1_w3[l].astype(BF16), w2a=ffn1_w2[l].astype(BF16),
        nm=row(mix_norm[l]), win=w_in[l].astype(BF16),
        qn=row(jnp.tile(q_norm[l], LANES // HEAD_DIM)), kn=row(jnp.tile(k_norm[l], LANES // HEAD_DIM)),
        ones=head_ones.astype(BF16), sinks=attn_sinks[l].astype(F32),
        bglu=row(b_glu[l]),
        are=row(ssm_a_re[l]), aim=row(ssm_a_im[l]), ldt=row(jnp.repeat(ssm_log_dt[l], SSM_N)),
        bbre=_block_diag_halves(jnp.swapaxes(ssm_b_re[l], 1, 2).astype(F32)),
        bbim=_block_diag_halves(jnp.swapaxes(ssm_b_im[l], 1, 2).astype(F32)),
        ccat=c_cat.astype(BF16), dskip=row(ssm_d[l]), perm=perm, permt=perm.T,
        n2=row(ffn2_norm[l]),
    )


def kernel(x_prompt, x_sample, cache_k, cache_v, state_ssm_re, state_ssm_im, ffn1_norm, ffn1_w1, ffn1_w3, ffn1_w2, mix_norm, w_in, q_norm, k_norm, attn_sinks, w_attn_out, ssm_a_re, ssm_a_im, ssm_log_dt, ssm_b_re, ssm_b_im, ssm_c_re, ssm_c_im, ssm_d, w_glu, b_glu, w_ssm_out, w_out, ffn2_norm, ffn2_w1, ffn2_w3, ffn2_w2):
    assert ffn1_norm.shape[0] == 1, "single trunk layer"
    batch, seq, _ = x_prompt.shape
    nseq, steps, _ = x_sample.shape
    p = _layout_params(ffn1_norm, ffn1_w1, ffn1_w3, ffn1_w2, mix_norm, w_in, q_norm, k_norm, attn_sinks,
                       w_attn_out, ssm_a_re, ssm_a_im, ssm_log_dt, ssm_b_re, ssm_b_im, ssm_c_re, ssm_c_im,
                       ssm_d, w_glu, b_glu, w_ssm_out, w_out, ffn2_norm, ffn2_w1, ffn2_w3, ffn2_w2)

    p['wbre'], p['wbim'] = _ssm_weights(p)

    tile0 = np.arange(batch * seq // FRONT_TM) * FRONT_TM
    tabs_p = _rope_tables(tile0 % seq, np.arange(FRONT_TM))
    later = dict(wao=w_attn_out, wglu=w_glu, wso=w_ssm_out, wout=w_out, w1b=ffn2_w1, w3b=ffn2_w3, w2b=ffn2_w2)
    fr = _front(x_prompt.reshape(batch * seq, D_MODEL), tabs_p, p, BF16, cast=list(later.values()))
    p.update(zip(later, fr[7:]))
    fr = fr[:7]
    x2p, x2p_last, re_p, im_p = _mix_prompt(fr, p, batch, seq)
    y_prompt = _ffn(x2p, p, x2p_last, seq).reshape(batch, seq, D_MODEL)
    tail_kv = lambda a: a.reshape(batch, seq, KV_W)[:, seq - WINDOW:].reshape(batch, WINDOW, N_KV, HEAD_DIM)
    k_p, v_p = tail_kv(fr[2]), tail_kv(fr[3])

    xs = jnp.swapaxes(x_sample, 0, 1).reshape(steps * nseq, D_MODEL)
    tile0 = np.arange(steps * nseq // FRONT_TM) * FRONT_TM
    tabs_s = _rope_tables(PAST_LEN + tile0 // nseq, np.arange(FRONT_TM) // nseq)
    x1s, qs, ks, vs, us, sgas, sgss = _front(xs, tabs_s, p, F32)
    to_lanes = lambda c: jnp.transpose(c[0], (0, 2, 3, 1))
    attn_s, nkt, nvt = _attn_sample(qs, ks, vs, to_lanes(cache_k), to_lanes(cache_v), p['sinks'], steps, nseq)
    x2s, re_s, im_s = _mix_sample(x1s, attn_s.reshape(steps * nseq, ATTN_W), us, sgas, sgss,
                                  state_ssm_re[0].reshape(nseq, SSM_S), state_ssm_im[0].reshape(nseq, SSM_S), p)
    y_sample = jnp.swapaxes(_ffn(x2s, p).reshape(steps, nseq, D_MODEL), 0, 1)
    k_s, v_s = jnp.transpose(nkt, (0, 3, 1, 2)), jnp.transpose(nvt, (0, 3, 1, 2))

    st = lambda a, n: a.reshape(1, n, SSM_G, SSM_N)
    return (y_prompt, y_sample, k_p[None], v_p[None], st(re_p, batch), st(im_p, batch),
            k_s[None], v_s[None], st(re_s, nseq), st(im_s, nseq))
```

```python
import functools

import jax
import jax.numpy as jnp
import numpy as np
from jax import lax
from jax.experimental import pallas as pl
from jax.experimental.pallas import tpu as pltpu

F32 = jnp.float32
BF16 = jnp.bfloat16

D_MODEL = 1024
HEAD_DIM = 64
N_HEADS = 8
N_KV = 2
WINDOW = 128
ATTN_W = N_HEADS * HEAD_DIM
KV_W = N_KV * HEAD_DIM
SSM_W = 512
SSM_G = 32
SSM_CH = 16
SSM_N = 64
SSM_S = SSM_G * SSM_N
D_FF = 2816
IN_COLS = ATTN_W + 2 * KV_W + SSM_W + 2 * D_MODEL
ROPE_THETA = 10000.0
RMS_EPS = 1e-6
MASK_VALUE = -1e30
PAST_LEN = 16384

LANES = 128
SUBLANES = 8
MXU_DIM = 256

FRONT_TM = 512
FFN_TM = 1024
MIX_TM = 512
SCAN_ROWS = 256
SCAN_GROUPS = MIX_TM // SCAN_ROWS
SCAN_SEG = SCAN_ROWS // SUBLANES
FF_CHUNK = 256
SCAN_LC = 512
SAMPLE_BS = 32
VMEM_LIMIT = 58 * 1024 * 1024


def _const_spec(shape):
    nd = len(shape)
    return pl.BlockSpec(shape, lambda *_: (0,) * nd, pipeline_mode=pl.Buffered(1))


def _rms(x, w):
    ms = jnp.mean(x * x, axis=-1, keepdims=True)
    return x * lax.rsqrt(ms + RMS_EPS) * w


def _swiglu_residual(x, nw_ref, w1_ref, w3_ref, w2_ref):
    h = _rms(x, nw_ref[...]).astype(BF16)
    acc = None
    for c0 in range(0, D_FF, FF_CHUNK):
        cs = slice(c0, min(c0 + FF_CHUNK, D_FF))
        a = jnp.dot(h, w1_ref[:, cs], preferred_element_type=F32)
        b = jnp.dot(h, w3_ref[:, cs], preferred_element_type=F32)
        g = (a * jax.nn.sigmoid(a) * b).astype(BF16)
        t = jnp.dot(g, w2_ref[cs, :], preferred_element_type=F32)
        acc = t if acc is None else acc + t
    return x + 0.5 * acc


def _head_norm_rope(t, ones_ref, g_ref, cos, sinlo, sinhi, scale):
    ssum = jnp.dot((t * t).astype(BF16), ones_ref[...], preferred_element_type=F32)
    tn = t * lax.rsqrt(ssum * (1.0 / HEAD_DIM) + RMS_EPS) * g_ref[...]
    up = pltpu.roll(tn, HEAD_DIM // 2, axis=1)
    dn = pltpu.roll(tn, LANES - HEAD_DIM // 2, axis=1)
    r = tn * cos + dn * sinlo + up * sinhi
    return r * scale if scale != 1.0 else r


def _front_kernel(n_cast, x_ref, base_ref, off_ref,
                  n1_ref, w1_ref, w3_ref, w2_ref, nm_ref, win_ref, qn_ref, kn_ref, ones_ref, *refs):
    cast_in, refs = refs[:n_cast], refs[n_cast:]
    x1_ref, q_ref, k_ref, v_ref, u_ref, sga_ref, sgs_ref = refs[:7]
    for src, dst in zip(cast_in, refs[7:]):
        dst[...] = src[...].astype(BF16)
    x1 = _swiglu_residual(x_ref[...], n1_ref, w1_ref, w3_ref, w2_ref)
    x1_ref[...] = x1
    h = _rms(x1, nm_ref[...]).astype(BF16)
    cb, sb = base_ref[0:1, :], base_ref[1:2, :]
    cos = cb * off_ref[0] - sb * off_ref[1]
    sinlo = sb * off_ref[2] + cb * off_ref[3]
    sinhi = sb * off_ref[4] + cb * off_ref[5]
    o = ATTN_W + 2 * KV_W + SSM_W
    qkvu = jnp.dot(h, win_ref[:, 0:o], preferred_element_type=F32)
    for j in range(ATTN_W // LANES):
        q_ref[:, j * LANES:(j + 1) * LANES] = _head_norm_rope(
            qkvu[:, j * LANES:(j + 1) * LANES], ones_ref, qn_ref, cos, sinlo, sinhi,
            HEAD_DIM ** -0.5).astype(q_ref.dtype)
    k_ref[...] = _head_norm_rope(qkvu[:, ATTN_W:ATTN_W + KV_W], ones_ref, kn_ref, cos, sinlo, sinhi, 1.0)
    v_ref[...] = qkvu[:, ATTN_W + KV_W:ATTN_W + 2 * KV_W]
    u_ref[...] = qkvu[:, ATTN_W + 2 * KV_W:o]
    ga = jnp.dot(h, win_ref[:, o:o + D_MODEL], preferred_element_type=F32)
    sga_ref[...] = jax.nn.sigmoid(ga).astype(BF16)
    o += D_MODEL
    gs = jnp.dot(h, win_ref[:, o:o + D_MODEL], preferred_element_type=F32)
    sgs_ref[...] = jax.nn.sigmoid(gs).astype(BF16)


def _cast_block_rows(rows, steps):
    for blocks in range(steps, 0, -1):
        if rows % blocks == 0 and (rows // blocks) % (2 * SUBLANES) == 0:
            return rows // blocks
    raise ValueError(rows)


def _front(x, tabs, p, q_dtype, cast=()):
    n = x.shape[0]
    tm = FRONT_TM
    steps = n // tm
    base_tab, off_tab = tabs
    row = lambda w: pl.BlockSpec((tm, w), lambda i: (i, 0))
    consts = [p['n1'], p['w1a'], p['w3a'], p['w2a'], p['nm'], p['win'], p['qn'], p['kn'], p['ones']]
    outs = [(D_MODEL, F32), (ATTN_W, q_dtype), (KV_W, F32), (KV_W, F32), (SSM_W, F32),
            (D_MODEL, BF16), (D_MODEL, BF16)]
    cast_in, cast_out = [], []
    for w in cast:
        _, rows, cols = w.shape
        rb = _cast_block_rows(rows, steps)
        last = rows // rb - 1
        cast_in.append(pl.BlockSpec((None, rb, cols), lambda i, last=last: (0, jnp.minimum(i, last), 0)))
        cast_out.append(pl.BlockSpec((rb, cols), lambda i, last=last: (jnp.minimum(i, last), 0)))
    return pl.pallas_call(
        functools.partial(_front_kernel, len(cast)),
        grid=(steps,),
        in_specs=[row(D_MODEL), pl.BlockSpec((None, 2, LANES), lambda i: (i, 0, 0)), _const_spec(off_tab.shape)]
        + [_const_spec(c.shape) for c in consts] + cast_in,
        out_specs=[row(w) for w, _ in outs] + cast_out,
        out_shape=[jax.ShapeDtypeStruct((n, w), dt) for w, dt in outs]
        + [jax.ShapeDtypeStruct(w.shape[1:], BF16) for w in cast],
        compiler_params=pltpu.CompilerParams(dimension_semantics=("arbitrary",),
                                             vmem_limit_bytes=VMEM_LIMIT),
        name="front",
    )(x, base_tab, off_tab, *consts, *cast)


def _ffn_kernel(tiles_per_seq, x_ref, xl_ref, n_ref, w1_ref, w3_ref, w2_ref, y_ref):
    x = x_ref[...]
    if tiles_per_seq:
        keep = x.shape[0] - xl_ref.shape[0]
        is_last = pl.program_id(0) % tiles_per_seq == tiles_per_seq - 1
        x = jnp.concatenate([x[:keep], jnp.where(is_last, xl_ref[...], x[keep:])], axis=0)
    y_ref[...] = _swiglu_residual(x, n_ref, w1_ref, w3_ref, w2_ref)


def _ffn(x, p, x_last=None, seq=None):
    n = x.shape[0]
    tm = min(FFN_TM, n)
    weights = [p['w1b'], p['w3b'], p['w2b']]
    if x_last is None:
        tiles_per_seq, x_last = 0, x
        last_spec = pl.BlockSpec((SUBLANES, D_MODEL), lambda i: (0, 0))
    else:
        tiles_per_seq = seq // tm
        last_spec = pl.BlockSpec((MIX_TM, D_MODEL), lambda i: (i // tiles_per_seq, 0))
    return pl.pallas_call(
        functools.partial(_ffn_kernel, tiles_per_seq),
        grid=(n // tm,),
        in_specs=[pl.BlockSpec((tm, D_MODEL), lambda i: (i, 0)), last_spec, _const_spec(p['n2'].shape)]
        + [_const_spec(w.shape) for w in weights],
        out_specs=pl.BlockSpec((tm, D_MODEL), lambda i: (i, 0)),
        out_shape=jax.ShapeDtypeStruct((n, D_MODEL), F32),
        compiler_params=pltpu.CompilerParams(dimension_semantics=("arbitrary",),
                                             vmem_limit_bytes=VMEM_LIMIT),
        name="ffn",
    )(x, x_last, p['n2'], *weights)


def _cmul(ar, ai, br, bi):
    return ar * br - ai * bi, ar * bi + ai * br


def _ssm_discretize(are_ref, aim_ref, ldt_ref):
    a_re, a_im = are_ref[...], aim_ref[...]
    dt = jnp.exp(ldt_ref[...])
    mag = jnp.exp(dt * a_re)
    ab_re = mag * jnp.cos(dt * a_im)
    ab_im = mag * jnp.sin(dt * a_im)
    den = a_re * a_re + a_im * a_im
    nr, ni = ab_re - 1.0, ab_im
    f_re = (nr * a_re + ni * a_im) / den
    f_im = (ni * a_re - nr * a_im) / den
    return ab_re, ab_im, f_re, f_im


def _build_wb(bbre_ref, bbim_ref, f_re, f_im, wbre_s, wbim_s):
    half = SSM_S // 2
    for hh in range(2):
        fr, fi = f_re[:, hh * half:(hh + 1) * half], f_im[:, hh * half:(hh + 1) * half]
        br, bi = bbre_ref[hh], bbim_ref[hh]
        wbre_s[hh] = (br * fr - bi * fi).astype(BF16)
        wbim_s[hh] = (br * fi + bi * fr).astype(BF16)


def _ssm_weights_kernel(are_ref, aim_ref, ldt_ref, bbre_ref, bbim_ref, wbre_ref, wbim_ref):
    _, _, f_re, f_im = _ssm_discretize(are_ref, aim_ref, ldt_ref)
    _build_wb(bbre_ref, bbim_ref, f_re, f_im, wbre_ref, wbim_ref)


def _ssm_weights(p):
    shape = jax.ShapeDtypeStruct(p['bbre'].shape, BF16)
    return pl.pallas_call(_ssm_weights_kernel, out_shape=[shape, shape], name="ssm_weights")(
        p['are'], p['aim'], p['ldt'], p['bbre'], p['bbim'])


def _expand_state(ub, wbre_s, wbim_s, bur_s, bui_s):
    half_in, half_st = SSM_W // 2, SSM_S // 2
    for hh in range(2):
        uh = ub[:, hh * half_in:(hh + 1) * half_in]
        bur_s[:, hh * half_st:(hh + 1) * half_st] = jnp.dot(uh, wbre_s[hh], preferred_element_type=F32)
        bui_s[:, hh * half_st:(hh + 1) * half_st] = jnp.dot(uh, wbim_s[hh], preferred_element_type=F32)


def _project_state(xr_s, xi_s, ccat_ref):
    half_st = SSM_S // 2
    outs = []
    for hh in range(2):
        sl = slice(hh * half_st, (hh + 1) * half_st)
        xcat = jnp.concatenate([xr_s[:, sl].astype(BF16), xi_s[:, sl].astype(BF16)], axis=1)
        outs.append(jnp.dot(xcat, ccat_ref[hh], preferred_element_type=F32))
    return jnp.concatenate(outs, axis=1)


def _gated_attn(attn_b, sga, wao_ref):
    return sga.astype(F32) * jnp.dot(attn_b, wao_ref[...], preferred_element_type=F32)


def _gated_sum(attn_gated, y_ssm, sgs, wglu_ref, bglu_ref, wso_ref):
    z = jax.nn.gelu(y_ssm, approximate=True)
    gl = jnp.dot(z.astype(BF16), wglu_ref[...], preferred_element_type=F32) + bglu_ref[...]
    ssm = (z * jax.nn.sigmoid(gl)).astype(BF16)
    merged = attn_gated + sgs.astype(F32) * jnp.dot(ssm, wso_ref[...], preferred_element_type=F32)
    return merged.astype(BF16)


def _gated_merge(x1, attn_gated, y_ssm, sgs, wglu_ref, bglu_ref, wso_ref, wout_ref):
    merged = _gated_sum(attn_gated, y_ssm, sgs, wglu_ref, bglu_ref, wso_ref)
    return x1 + jnp.dot(merged, wout_ref[...], preferred_element_type=F32)


def _replicate_heads(kv):
    sw = pltpu.roll(kv, HEAD_DIM, axis=1)
    lo = lax.broadcasted_iota(jnp.int32, kv.shape, 1) < HEAD_DIM
    return jnp.where(lo, kv, sw).astype(BF16), jnp.where(lo, sw, kv).astype(BF16)


def _attn_scores(qs, kreps, keys_on_lanes=False):
    qb = qs[0].shape[0]
    lo = lax.broadcasted_iota(jnp.int32, (qb, LANES), 1) < HEAD_DIM
    dims = (((1,), (0,)), ((), ())) if keys_on_lanes else (((1,), (1,)), ((), ()))
    scores = []
    for q, krep in zip(qs, kreps):
        if q.dtype != BF16 or qb % (2 * SUBLANES):
            q = q.astype(F32)
        zero = jnp.zeros((), q.dtype)
        for kv in range(N_KV):
            parts = []
            for jj in range(2):
                t = q[:, (2 * kv + jj) * LANES:(2 * kv + jj + 1) * LANES]
                parts += [jnp.where(lo, t, zero), jnp.where(lo, zero, t)]
            stacked = jnp.concatenate(parts, axis=0).astype(BF16)
            scores.append(lax.dot_general(stacked, krep[kv], dims, preferred_element_type=F32))
    return jnp.concatenate(scores, axis=0)


def _window_valid(rows, qb, first):
    qi = lax.broadcasted_iota(jnp.int32, (rows, 2 * WINDOW), 0) & (qb - 1)
    sj = lax.broadcasted_iota(jnp.int32, (rows, 2 * WINDOW), 1)
    band = (sj <= WINDOW + qi) & (sj > qi)
    return band if first is False else band & ((sj >= WINDOW) | jnp.logical_not(first))


def _attn_softmax(s, sinks_ref, qb, firsts, bias=None):
    per_block = s.shape[0] // len(firsts)
    sink_parts = [jnp.full((qb, 1), sinks_ref[h], F32) for h in range(N_HEADS)]
    sink = jnp.concatenate(sink_parts * len(firsts), axis=0)
    if bias is None:
        valid = jnp.concatenate([_window_valid(per_block, qb, f) for f in firsts], axis=0)
        s = jnp.where(valid, s, MASK_VALUE)
    else:
        s = s + bias
    m = jnp.maximum(jnp.max(s, axis=-1, keepdims=True), sink)
    e = jnp.exp(s - m)
    denom = jnp.sum(e, axis=-1, keepdims=True) + jnp.exp(sink - m)
    return (e * (1.0 / denom)).astype(BF16)


def _attn_values(probs, vreps, qb, keys_on_lanes=False):
    lo = lax.broadcasted_iota(jnp.int32, (qb, LANES), 1) < HEAD_DIM
    rows = 2 * (N_HEADS // N_KV // 2) * qb
    dims = (((1,), (1,)), ((), ())) if keys_on_lanes else (((1,), (0,)), ((), ()))
    outs = []
    for b, vrep in enumerate(vreps):
        tiles = []
        for kv in range(N_KV):
            r0 = (b * N_KV + kv) * rows
            o = lax.dot_general(probs[r0:r0 + rows], vrep[kv], dims, preferred_element_type=F32)
            for jj in range(2):
                oa = o[(2 * jj) * qb:(2 * jj + 1) * qb]
                ob = o[(2 * jj + 1) * qb:(2 * jj + 2) * qb]
                tiles.append(jnp.where(lo, oa, ob))
        outs.append(jnp.concatenate(tiles, axis=1))
    return outs


def _mix_prompt_kernel(x1_ref, q_ref, k_ref, v_ref, u_ref, sga_ref, sgs_ref, sinks_ref,
                       wao_ref, wglu_ref, bglu_ref, wso_ref, wout_ref,
                       are_ref, aim_ref, ldt_ref, wbre_s, wbim_s, ccat_ref, d_ref, perm_ref, permt_ref,
                       x2_ref, x2_last_ref, sre_ref, sim_ref,
                       apr_s, api_s, a1_s, aseg_s, carry_s, cin_s,
                       bur_s, bui_s, xb_s, kprev_s, vprev_s, bias_s, yssm_s, attg_s, sgsp_s, x1p_s):
    b, i = pl.program_id(0), pl.program_id(1)
    nt = pl.num_programs(1)

    @pl.when((b == 0) & (i == 0))
    def _():
        ab_re, ab_im, _, _ = _ssm_discretize(are_ref, aim_ref, ldt_ref)
        a1_s[0:1, :] = ab_re
        a1_s[1:2, :] = ab_im
        pr, pi = ab_re, ab_im
        held = None
        for j in range(SCAN_SEG):
            rows = [jnp.broadcast_to(pr, (SUBLANES, SSM_S)), jnp.broadcast_to(pi, (SUBLANES, SSM_S))]
            if j % 2 == 0:
                held = rows
            else:
                rs = slice((j - 1) * SUBLANES, (j + 1) * SUBLANES)
                apr_s[rs, :] = jnp.concatenate([held[0], rows[0]], axis=0).astype(BF16)
                api_s[rs, :] = jnp.concatenate([held[1], rows[1]], axis=0).astype(BF16)
            if j == SCAN_SEG - 1:
                aseg_s[0:1, :] = pr
                aseg_s[1:2, :] = pi
            pr, pi = _cmul(pr, pi, ab_re, ab_im)
        for t, first in enumerate((False, True)):
            bias_s[t] = jnp.where(_window_valid(bias_s.shape[1], WINDOW, first), 0.0, MASK_VALUE)

    @pl.when(i == 0)
    def _():
        carry_s[...] = jnp.zeros_like(carry_s)
        kprev_s[...] = jnp.zeros_like(kprev_s)
        vprev_s[...] = jnp.zeros_like(vprev_s)
        for lag_s in (yssm_s, attg_s, sgsp_s, x1p_s):
            lag_s[...] = jnp.zeros_like(lag_s)

    lagged = [yssm_s[...], attg_s[...], sgsp_s[...], x1p_s[...]]

    k0, k1 = _replicate_heads(k_ref[...])
    v0, v1 = _replicate_heads(v_ref[...])
    qs, kreps, vreps, firsts = [], [], [], []
    for blk in range(MIX_TM // WINDOW):
        cur = slice(blk * WINDOW, (blk + 1) * WINDOW)
        if blk == 0:
            prev = [kprev_s[0], kprev_s[1], vprev_s[0], vprev_s[1]]
            firsts.append(i == 0)
        else:
            ps = slice((blk - 1) * WINDOW, blk * WINDOW)
            prev = [k0[ps], k1[ps], v0[ps], v1[ps]]
            firsts.append(False)
        qs.append(q_ref[cur, :])
        kreps.append([jnp.concatenate([prev[0], k0[cur]], axis=0), jnp.concatenate([prev[1], k1[cur]], axis=0)])
        vreps.append([jnp.concatenate([prev[2], v0[cur]], axis=0), jnp.concatenate([prev[3], v1[cur]], axis=0)])
    tail = slice(MIX_TM - WINDOW, MIX_TM)
    kprev_s[0], kprev_s[1] = k0[tail], k1[tail]
    vprev_s[0], vprev_s[1] = v0[tail], v1[tail]
    scores = _attn_scores(qs, kreps)

    half_in, half_st = SSM_W // 2, SSM_S // 2
    chunks = list(range(0, SSM_S, SCAN_LC))
    groups = [g * SCAN_ROWS for g in range(SCAN_GROUPS)]
    nblk = len(qs)
    blk_rows = scores.shape[0] // nblk
    u = u_ref[...]
    ub = u.astype(BF16)
    up = jnp.concatenate([jnp.dot(perm_ref[...], ub[g0:g0 + SCAN_ROWS], preferred_element_type=F32)
                          for g0 in groups], axis=0).astype(BF16)

    def expand_half(hh):
        uh = up[:, hh * half_in:(hh + 1) * half_in]
        bur_s[:, hh * half_st:(hh + 1) * half_st] = jnp.dot(uh, wbre_s[hh], preferred_element_type=F32)
        bui_s[:, hh * half_st:(hh + 1) * half_st] = jnp.dot(uh, wbim_s[hh], preferred_element_type=F32)

    def softmax_blocks(lo, hi):
        no_prev = (i == 0).astype(jnp.int32)
        bias = jnp.concatenate([bias_s[no_prev] if blk == 0 else bias_s[0]
                                for blk in range(lo, hi) for _ in range(N_HEADS)], axis=0)
        return _attn_softmax(scores[lo * blk_rows:hi * blk_rows], sinks_ref, WINDOW, firsts[lo:hi], bias)

    def scan_chunk(c0):
        cs = slice(c0, c0 + SCAN_LC)
        ar = jnp.broadcast_to(a1_s[0:1, cs], (SUBLANES, SCAN_LC))
        ai = jnp.broadcast_to(a1_s[1:2, cs], (SUBLANES, SCAN_LC))
        state = [(bur_s[g0:g0 + SUBLANES, cs], bui_s[g0:g0 + SUBLANES, cs]) for g0 in groups]
        for j in range(1, SCAN_SEG):
            for g, g0 in enumerate(groups):
                rs = slice(g0 + j * SUBLANES, g0 + (j + 1) * SUBLANES)
                xr, xi = state[g]
                xr, xi = (ar * xr - ai * xi + bur_s[rs, cs], ar * xi + ai * xr + bui_s[rs, cs])
                bur_s[rs, cs] = xr
                bui_s[rs, cs] = xi
                state[g] = (xr, xi)
        return state[0][0]

    mid = nblk // 2
    expand_half(0)
    probs_a = softmax_blocks(0, mid)
    expand_half(1)
    probs_b = softmax_blocks(mid, nblk)
    quarter = len(chunks) // 4
    scan_group = lambda g: [scan_chunk(c0) for c0 in chunks[g * quarter:(g + 1) * quarter]]
    started = scan_group(0)[-1]
    scan_group(1)
    scan_group(2)
    scan_group(3)

    sr, si = aseg_s[0:1, :], aseg_s[1:2, :]
    cr, ci = carry_s[0:1, :], carry_s[1:2, :]
    for g, g0 in enumerate(groups):
        last = g0 + SCAN_ROWS - SUBLANES
        for s in range(SUBLANES):
            k = g * SUBLANES + s
            cin_s[0, k:k + 1, :] = cr
            cin_s[1, k:k + 1, :] = ci
            pr, pi = _cmul(sr, si, cr, ci)
            cr = pr + bur_s[last + s:last + s + 1, :]
            ci = pi + bui_s[last + s:last + s + 1, :]
    carry_s[0:1, :] = cr
    carry_s[1:2, :] = ci
    pack = 2 * SUBLANES

    def fix_chunk(c0):
        cs = slice(c0, c0 + SCAN_LC)
        col = (c0 // half_st) * SSM_S + c0 % half_st
        for g, g0 in enumerate(groups):
            seg = slice(g * SUBLANES, (g + 1) * SUBLANES)
            cinr = jnp.concatenate([cin_s[0, seg, cs]] * 2, axis=0).astype(BF16)
            cini = jnp.concatenate([cin_s[1, seg, cs]] * 2, axis=0).astype(BF16)
            for r0 in range(0, SCAN_ROWS, pack):
                rs, ts = slice(g0 + r0, g0 + r0 + pack), slice(r0, r0 + pack)
                pr, pi = apr_s[ts, cs], api_s[ts, cs]
                xb_s[rs, col:col + SCAN_LC] = bur_s[rs, cs].astype(BF16) + pr * cinr - pi * cini
                xb_s[rs, col + half_st:col + half_st + SCAN_LC] = (bui_s[rs, cs].astype(BF16)
                                                                   + pr * cini + pi * cinr)

    def project_half(hh):
        return jnp.dot(xb_s[:, hh * SSM_S:(hh + 1) * SSM_S], ccat_ref[hh], preferred_element_type=F32)

    per_half = len(chunks) // 2
    for c0 in chunks[:per_half]:
        fix_chunk(c0)
    yp0 = project_half(0)
    for c0 in chunks[per_half:]:
        fix_chunk(c0)
    yp = jnp.concatenate([yp0, project_half(1)], axis=1)
    ypb = yp.astype(BF16)
    y_ssm = jnp.concatenate([jnp.dot(permt_ref[...], ypb[g0:g0 + SCAN_ROWS], preferred_element_type=F32)
                             for g0 in groups], axis=0) + d_ref[...] * u

    attn = _attn_values(probs_a, vreps[0:mid], WINDOW) + _attn_values(probs_b, vreps[mid:nblk], WINDOW)
    attn_gated = _gated_attn(jnp.concatenate(attn, axis=0).astype(BF16), sga_ref[...], wao_ref)
    bits = lax.bitcast_convert_type(started[0:1, 0:1], jnp.uint32)
    zero = lax.bitcast_convert_type(lax.shift_right_logical(lax.shift_right_logical(bits, jnp.uint32(16)),
                                                            jnp.uint32(16)), F32)
    yssm_prev, attg_prev, sgs_prev, x1_prev = lagged
    x2_ref[...] = _gated_merge(x1_prev, attg_prev, yssm_prev + zero, sgs_prev,
                               wglu_ref, bglu_ref, wso_ref, wout_ref)

    yssm_s[...] = y_ssm
    attg_s[...] = attn_gated
    sgsp_s[...] = sgs_ref[...]
    x1p_s[...] = x1_ref[...]

    @pl.when(i == nt - 1)
    def _():
        x2_last_ref[...] = _gated_merge(x1_ref[...], attn_gated, y_ssm, sgs_ref[...],
                                        wglu_ref, bglu_ref, wso_ref, wout_ref)
        sre_ref[...] = carry_s[0:1, :]
        sim_ref[...] = carry_s[1:2, :]


def _mix_prompt(fr, p, batch, seq):
    x1, q, k, v, u, sga, sgs = fr
    tm = MIX_TM
    nt = seq // tm
    row = lambda w: pl.BlockSpec((tm, w), lambda b, i: (b * nt + i, 0))
    lag = pl.BlockSpec((tm, D_MODEL), lambda b, i: (b * nt + (i + nt - 1) % nt, 0))
    last = pl.BlockSpec((tm, D_MODEL), lambda b, i: (b, 0))
    consts = [p['wao'], p['wglu'], p['bglu'], p['wso'], p['wout'],
              p['are'], p['aim'], p['ldt'], p['wbre'], p['wbim'], p['ccat'], p['dskip'], p['perm'], p['permt']]
    st_spec = pl.BlockSpec((None, 1, SSM_S), lambda b, i: (b, 0, 0))
    scratch = [pltpu.VMEM((SCAN_ROWS, SSM_S), BF16), pltpu.VMEM((SCAN_ROWS, SSM_S), BF16),
               pltpu.VMEM((SUBLANES, SSM_S), F32), pltpu.VMEM((SUBLANES, SSM_S), F32),
               pltpu.VMEM((SUBLANES, SSM_S), F32), pltpu.VMEM((2, SCAN_GROUPS * SUBLANES, SSM_S), F32),
               pltpu.VMEM((tm, SSM_S), F32), pltpu.VMEM((tm, SSM_S), F32), pltpu.VMEM((tm, 2 * SSM_S), BF16),
               pltpu.VMEM((N_KV, WINDOW, LANES), BF16), pltpu.VMEM((N_KV, WINDOW, LANES), BF16),
               pltpu.VMEM((2, WINDOW, 2 * WINDOW), F32),
               pltpu.VMEM((tm, SSM_W), F32), pltpu.VMEM((tm, D_MODEL), F32), pltpu.VMEM((tm, D_MODEL), BF16),
               pltpu.VMEM((tm, D_MODEL), F32)]
    return pl.pallas_call(
        _mix_prompt_kernel,
        grid=(batch, nt),
        in_specs=[row(D_MODEL), row(ATTN_W), row(KV_W), row(KV_W), row(SSM_W), row(D_MODEL), row(D_MODEL),
                  pl.BlockSpec(memory_space=pltpu.SMEM)] + [_const_spec(c.shape) for c in consts],
        out_specs=[lag, last, st_spec, st_spec],
        out_shape=[jax.ShapeDtypeStruct((batch * seq, D_MODEL), F32),
                   jax.ShapeDtypeStruct((batch * tm, D_MODEL), F32),
                   jax.ShapeDtypeStruct((batch, 1, SSM_S), F32),
                   jax.ShapeDtypeStruct((batch, 1, SSM_S), F32)],
        scratch_shapes=scratch,
        compiler_params=pltpu.CompilerParams(dimension_semantics=("arbitrary", "arbitrary"),
                                             vmem_limit_bytes=VMEM_LIMIT),
        name="mix_prompt",
    )(x1, q, k, v, u, sga, sgs, p['sinks'], *consts)


def _attn_sample_kernel(q_ref, kn_ref, vn_ref, ckt_ref, cvt_ref, sinks_ref, o_ref, nkt_ref, nvt_ref,
                        q8_s, knp_s, vnp_s):
    steps = q_ref.shape[0]
    shift = WINDOW - steps
    keep = lax.broadcasted_iota(jnp.int32, (HEAD_DIM, WINDOW), 1) < shift
    q8_s[...] = jnp.zeros_like(q8_s)
    knp_s[...] = jnp.zeros_like(knp_s)
    vnp_s[...] = jnp.zeros_like(vnp_s)
    qs, kreps, vreps = [], [], []
    for s in range(SAMPLE_BS):
        for t in range(steps):
            q8_s[s, t:t + 1, :] = q_ref[t, s:s + 1, :]
            knp_s[s, t:t + 1, :] = kn_ref[t, s:s + 1, :]
            vnp_s[s, t:t + 1, :] = vn_ref[t, s:s + 1, :]
        news = [knp_s[s].T, vnp_s[s].T]
        ends = [pltpu.roll(n, shift, axis=1) for n in news]
        krep, vrep = [], []
        for kv in range(N_KV):
            hs = slice(kv * HEAD_DIM, (kv + 1) * HEAD_DIM)
            for cache_ref, out_ref, new, end, reps in ((ckt_ref, nkt_ref, news[0], ends[0], krep),
                                                       (cvt_ref, nvt_ref, news[1], ends[1], vrep)):
                old = cache_ref[s, kv]
                out_ref[s, kv] = jnp.where(keep, pltpu.roll(old, shift, axis=1), end[hs])
                both = jnp.concatenate([old, new[hs]], axis=1).astype(BF16)
                reps.append(jnp.concatenate([both, both], axis=0))
        qs.append(q8_s[s])
        kreps.append(krep)
        vreps.append(vrep)
    probs = _attn_softmax(_attn_scores(qs, kreps, keys_on_lanes=True), sinks_ref, SUBLANES, [False] * SAMPLE_BS)
    outs = _attn_values(probs, vreps, SUBLANES, keys_on_lanes=True)
    for s in range(SAMPLE_BS):
        for t in range(steps):
            o_ref[t, s:s + 1, :] = outs[s][t:t + 1, :]


def _attn_sample(q, kn, vn, ckt, cvt, sinks, steps, nseq):
    bs = SAMPLE_BS
    blk3 = lambda w: pl.BlockSpec((steps, bs, w), lambda i: (0, i, 0))
    cache = pl.BlockSpec((bs, N_KV, HEAD_DIM, WINDOW), lambda i: (i, 0, 0, 0))
    cache_shape = jax.ShapeDtypeStruct((nseq, N_KV, HEAD_DIM, WINDOW), F32)
    return pl.pallas_call(
        _attn_sample_kernel,
        grid=(nseq // bs,),
        in_specs=[blk3(ATTN_W), blk3(KV_W), blk3(KV_W), cache, cache, pl.BlockSpec(memory_space=pltpu.SMEM)],
        out_specs=[blk3(ATTN_W), cache, cache],
        out_shape=[jax.ShapeDtypeStruct((steps, nseq, ATTN_W), F32), cache_shape, cache_shape],
        scratch_shapes=[pltpu.VMEM((bs, SUBLANES, ATTN_W), F32), pltpu.VMEM((bs, LANES, KV_W), F32),
                        pltpu.VMEM((bs, LANES, KV_W), F32)],
        compiler_params=pltpu.CompilerParams(dimension_semantics=("arbitrary",)),
        name="attn_sample",
    )(q.reshape(steps, nseq, ATTN_W), kn.reshape(steps, nseq, KV_W), vn.reshape(steps, nseq, KV_W),
      ckt, cvt, sinks)


def _mix_sample_kernel(x1_ref, attn_ref, u_ref, sga_ref, sgs_ref, x0r_ref, x0i_ref,
                       wao_ref, wglu_ref, bglu_ref, wso_ref, wout_ref,
                       are_ref, aim_ref, ldt_ref, wbre_s, wbim_s, ccat_ref, d_ref,
                       x2_ref, sre_ref, sim_ref,
                       bur_s, bui_s):
    nseq = x0r_ref.shape[0]
    steps = x1_ref.shape[0] // nseq
    ab_re, ab_im, _, _ = _ssm_discretize(are_ref, aim_ref, ldt_ref)
    u = u_ref[...]
    _expand_state(u.astype(BF16), wbre_s, wbim_s, bur_s, bui_s)
    for c0 in range(0, SSM_S, LANES):
        cs = slice(c0, c0 + LANES)
        ar, ai = ab_re[:, cs], ab_im[:, cs]
        xr, xi = x0r_ref[:, cs], x0i_ref[:, cs]
        for t in range(steps):
            rs = slice(t * nseq, (t + 1) * nseq)
            nr = ar * xr - ai * xi + bur_s[rs, cs]
            ni = ar * xi + ai * xr + bui_s[rs, cs]
            bur_s[rs, cs] = nr
            bui_s[rs, cs] = ni
            xr, xi = nr, ni
        sre_ref[:, cs] = xr
        sim_ref[:, cs] = xi
    y_ssm = _project_state(bur_s, bui_s, ccat_ref) + d_ref[...] * u
    attn_gated = _gated_attn(attn_ref[...].astype(BF16), sga_ref[...], wao_ref)
    x2_ref[...] = _gated_merge(x1_ref[...], attn_gated, y_ssm, sgs_ref[...],
                               wglu_ref, bglu_ref, wso_ref, wout_ref)


def _mix_sample(x1, attn, u, sga, sgs, x0r, x0i, p):
    n, nseq = x1.shape[0], x0r.shape[0]
    args = [x1, attn, u, sga, sgs, x0r, x0i, p['wao'], p['wglu'], p['bglu'], p['wso'], p['wout'],
            p['are'], p['aim'], p['ldt'], p['wbre'], p['wbim'], p['ccat'], p['dskip']]
    return pl.pallas_call(
        _mix_sample_kernel,
        grid=(1,),
        in_specs=[_const_spec(a.shape) for a in args],
        out_specs=[pl.BlockSpec((n, D_MODEL), lambda i: (0, 0)), pl.BlockSpec((nseq, SSM_S), lambda i: (0, 0)),
                   pl.BlockSpec((nseq, SSM_S), lambda i: (0, 0))],
        out_shape=[jax.ShapeDtypeStruct((n, D_MODEL), F32),
                   jax.ShapeDtypeStruct((nseq, SSM_S), F32),
                   jax.ShapeDtypeStruct((nseq, SSM_S), F32)],
        scratch_shapes=[pltpu.VMEM((n, SSM_S), F32), pltpu.VMEM((n, SSM_S), F32)],
        compiler_params=pltpu.CompilerParams(dimension_semantics=("arbitrary",),
                                             vmem_limit_bytes=VMEM_LIMIT),
        name="mix_sample",
    )(*args)


def _rope_tables(base_pos, off_pos):
    half = HEAD_DIM // 2
    lane = np.arange(LANES)
    inv = ROPE_THETA ** (-2.0 * (lane % half) / HEAD_DIM)
    first = (lane % HEAD_DIM) < half
    ab = np.asarray(base_pos, np.float64)[:, None] * inv[None, :]
    ao = np.asarray(off_pos, np.float64)[:, None] * inv[None, :]
    base = np.stack([np.cos(ab), np.sin(ab)], axis=1)
    co, so = np.cos(ao), np.sin(ao)
    lo = np.where(first, -1.0, 0.0)
    hi = np.where(first, 0.0, 1.0)
    off = np.stack([co, so, co * lo, so * lo, co * hi, so * hi], axis=0)
    return jnp.asarray(base, F32), jnp.asarray(off, F32)


def _block_diag_halves(m):
    g, r, c = m.shape
    hg = g // 2
    strip = jnp.transpose(m.reshape(2, hg, r, c), (0, 2, 1, 3)).reshape(2, 1, r, hg * c)
    full = jnp.broadcast_to(strip, (2, hg, r, hg * c)).reshape(2, hg * r, hg * c)
    mask = jnp.asarray(np.kron(np.eye(hg), np.ones((r, c))), m.dtype)
    return full * mask


def _layout_params(ffn1_norm, ffn1_w1, ffn1_w3, ffn1_w2, mix_norm, w_in, q_norm, k_norm, attn_sinks,
                   w_attn_out, ssm_a_re, ssm_a_im, ssm_log_dt, ssm_b_re, ssm_b_im, ssm_c_re, ssm_c_im,
                   ssm_d, w_glu, b_glu, w_ssm_out, w_out, ffn2_norm, ffn2_w1, ffn2_w3, ffn2_w2):
    l = 0
    row = lambda a: a.reshape(1, -1).astype(F32)
    head_ones = jnp.asarray(np.kron(np.eye(LANES // HEAD_DIM), np.ones((HEAD_DIM, HEAD_DIM))), F32)
    r = np.arange(SCAN_ROWS)
    tok = (r % SUBLANES) * SCAN_SEG + r // SUBLANES
    perm = jnp.asarray(tok[:, None] == np.arange(SCAN_ROWS)[None, :], BF16)
    c_cat = jnp.concatenate([_block_diag_halves(jnp.swapaxes(ssm_c_re[l], 1, 2)),
                             -_block_diag_halves(jnp.swapaxes(ssm_c_im[l], 1, 2))], axis=1)
    return dict(
        n1=row(ffn1_norm[l]), w1a=ffn1_w1[l].astype(BF16), w3a=ffn1_w3[l].astype(BF16), w2a=ffn1_w2[l].astype(BF16),
        nm=row(mix_norm[l]), win=w_in[l].astype(BF16),
        qn=row(jnp.tile(q_norm[l], LANES // HEAD_DIM)), kn=row(jnp.tile(k_norm[l], LANES // HEAD_DIM)),
        ones=head_ones.astype(BF16), sinks=attn_sinks[l].astype(F32),
        bglu=row(b_glu[l]),
        are=row(ssm_a_re[l]), aim=row(ssm_a_im[l]), ldt=row(jnp.repeat(ssm_log_dt[l], SSM_N)),
        bbre=_block_diag_halves(jnp.swapaxes(ssm_b_re[l], 1, 2).astype(F32)),
        bbim=_block_diag_halves(jnp.swapaxes(ssm_b_im[l], 1, 2).astype(F32)),
        ccat=c_cat.astype(BF16), dskip=row(ssm_d[l]), perm=perm, permt=perm.T,
        n2=row(ffn2_norm[l]),
    )


def kernel(x_prompt, x_sample, cache_k, cache_v, state_ssm_re, state_ssm_im, ffn1_norm, ffn1_w1, ffn1_w3, ffn1_w2, mix_norm, w_in, q_norm, k_norm, attn_sinks, w_attn_out, ssm_a_re, ssm_a_im, ssm_log_dt, ssm_b_re, ssm_b_im, ssm_c_re, ssm_c_im, ssm_d, w_glu, b_glu, w_ssm_out, w_out, ffn2_norm, ffn2_w1, ffn2_w3, ffn2_w2):
    assert ffn1_norm.shape[0] == 1, "single trunk layer"
    batch, seq, _ = x_prompt.shape
    nseq, steps, _ = x_sample.shape
    p = _layout_params(ffn1_norm, ffn1_w1, ffn1_w3, ffn1_w2, mix_norm, w_in, q_norm, k_norm, attn_sinks,
                       w_attn_out, ssm_a_re, ssm_a_im, ssm_log_dt, ssm_b_re, ssm_b_im, ssm_c_re, ssm_c_im,
                       ssm_d, w_glu, b_glu, w_ssm_out, w_out, ffn2_norm, ffn2_w1, ffn2_w3, ffn2_w2)

    p['wbre'], p['wbim'] = _ssm_weights(p)

    tile0 = np.arange(batch * seq // FRONT_TM) * FRONT_TM
    tabs_p = _rope_tables(tile0 % seq, np.arange(FRONT_TM))
    later = dict(wao=w_attn_out, wglu=w_glu, wso=w_ssm_out, wout=w_out, w1b=ffn2_w1, w3b=ffn2_w3, w2b=ffn2_w2)
    fr = _front(x_prompt.reshape(batch * seq, D_MODEL), tabs_p, p, BF16, cast=list(later.values()))
    p.update(zip(later, fr[7:]))
    fr = fr[:7]
    x2p, x2p_last, re_p, im_p = _mix_prompt(fr, p, batch, seq)
    y_prompt = _ffn(x2p, p, x2p_last, seq).reshape(batch, seq, D_MODEL)
    tail_kv = lambda a: a.reshape(batch, seq, KV_W)[:, seq - WINDOW:].reshape(batch, WINDOW, N_KV, HEAD_DIM)
    k_p, v_p = tail_kv(fr[2]), tail_kv(fr[3])

    xs = jnp.swapaxes(x_sample, 0, 1).reshape(steps * nseq, D_MODEL)
    tile0 = np.arange(steps * nseq // FRONT_TM) * FRONT_TM
    tabs_s = _rope_tables(PAST_LEN + tile0 // nseq, np.arange(FRONT_TM) // nseq)
    x1s, qs, ks, vs, us, sgas, sgss = _front(xs, tabs_s, p, F32)
    to_lanes = lambda c: jnp.transpose(c[0], (0, 2, 3, 1))
    attn_s, nkt, nvt = _attn_sample(qs, ks, vs, to_lanes(cache_k), to_lanes(cache_v), p['sinks'], steps, nseq)
    x2s, re_s, im_s = _mix_sample(x1s, attn_s.reshape(steps * nseq, ATTN_W), us, sgas, sgss,
                                  state_ssm_re[0].reshape(nseq, SSM_S), state_ssm_im[0].reshape(nseq, SSM_S), p)
    y_sample = jnp.swapaxes(_ffn(x2s, p).reshape(steps, nseq, D_MODEL), 0, 1)
    k_s, v_s = jnp.transpose(nkt, (0, 3, 1, 2)), jnp.transpose(nvt, (0, 3, 1, 2))

    st = lambda a, n: a.reshape(1, n, SSM_G, SSM_N)
    return (y_prompt, y_sample, k_p[None], v_p[None], st(re_p, batch), st(im_p, batch),
            k_s[None], v_s[None], st(re_s, nseq), st(im_s, nseq))
```

```python
import functools

import jax
import jax.numpy as jnp
import numpy as np
from jax import lax
from jax.experimental import pallas as pl
from jax.experimental.pallas import tpu as pltpu

F32 = jnp.float32
BF16 = jnp.bfloat16

D_MODEL = 1024
HEAD_DIM = 64
N_HEADS = 8
N_KV = 2
WINDOW = 128
ATTN_W = N_HEADS * HEAD_DIM
KV_W = N_KV * HEAD_DIM
SSM_W = 512
SSM_G = 32
SSM_CH = 16
SSM_N = 64
SSM_S = SSM_G * SSM_N
D_FF = 2816
IN_COLS = ATTN_W + 2 * KV_W + SSM_W + 2 * D_MODEL
ROPE_THETA = 10000.0
RMS_EPS = 1e-6
MASK_VALUE = -1e30
PAST_LEN = 16384

LANES = 128
SUBLANES = 8
MXU_DIM = 256

FRONT_TM = 512
FFN_TM = 1024
MIX_TM = 512
SCAN_ROWS = 256
SCAN_GROUPS = MIX_TM // SCAN_ROWS
SCAN_SEG = SCAN_ROWS // SUBLANES
FF_CHUNK = 256
SCAN_LC = 512
SAMPLE_BS = 32
VMEM_LIMIT = 58 * 1024 * 1024


def _const_spec(shape):
    nd = len(shape)
    return pl.BlockSpec(shape, lambda *_: (0,) * nd, pipeline_mode=pl.Buffered(1))


def _rms(x, w):
    ms = jnp.mean(x * x, axis=-1, keepdims=True)
    return x * lax.rsqrt(ms + RMS_EPS) * w


def _swiglu_residual(x, nw_ref, w1_ref, w3_ref, w2_ref):
    h = _rms(x, nw_ref[...]).astype(BF16)
    acc = None
    for c0 in range(0, D_FF, FF_CHUNK):
        cs = slice(c0, min(c0 + FF_CHUNK, D_FF))
        a = jnp.dot(h, w1_ref[:, cs], preferred_element_type=F32)
        b = jnp.dot(h, w3_ref[:, cs], preferred_element_type=F32)
        g = (a * jax.nn.sigmoid(a) * b).astype(BF16)
        t = jnp.dot(g, w2_ref[cs, :], preferred_element_type=F32)
        acc = t if acc is None else acc + t
    return x + 0.5 * acc


def _head_norm_rope(t, ones_ref, g_ref, cos, sinlo, sinhi, scale):
    ssum = jnp.dot((t * t).astype(BF16), ones_ref[...], preferred_element_type=F32)
    tn = t * lax.rsqrt(ssum * (1.0 / HEAD_DIM) + RMS_EPS) * g_ref[...]
    up = pltpu.roll(tn, HEAD_DIM // 2, axis=1)
    dn = pltpu.roll(tn, LANES - HEAD_DIM // 2, axis=1)
    r = tn * cos + dn * sinlo + up * sinhi
    return r * scale if scale != 1.0 else r


def _front_kernel(n_cast, x_ref, base_ref, off_ref,
                  n1_ref, w1_ref, w3_ref, w2_ref, nm_ref, win_ref, qn_ref, kn_ref, ones_ref, *refs):
    cast_in, refs = refs[:n_cast], refs[n_cast:]
    x1_ref, q_ref, k_ref, v_ref, u_ref, sga_ref, sgs_ref = refs[:7]
    for src, dst in zip(cast_in, refs[7:]):
        dst[...] = src[...].astype(BF16)
    x1 = _swiglu_residual(x_ref[...], n1_ref, w1_ref, w3_ref, w2_ref)
    x1_ref[...] = x1
    h = _rms(x1, nm_ref[...]).astype(BF16)
    cb, sb = base_ref[0:1, :], base_ref[1:2, :]
    cos = cb * off_ref[0] - sb * off_ref[1]
    sinlo = sb * off_ref[2] + cb * off_ref[3]
    sinhi = sb * off_ref[4] + cb * off_ref[5]
    o = ATTN_W + 2 * KV_W + SSM_W
    qkvu = jnp.dot(h, win_ref[:, 0:o], preferred_element_type=F32)
    for j in range(ATTN_W // LANES):
        q_ref[:, j * LANES:(j + 1) * LANES] = _head_norm_rope(
            qkvu[:, j * LANES:(j + 1) * LANES], ones_ref, qn_ref, cos, sinlo, sinhi,
            HEAD_DIM ** -0.5).astype(q_ref.dtype)
    k_ref[...] = _head_norm_rope(qkvu[:, ATTN_W:ATTN_W + KV_W], ones_ref, kn_ref, cos, sinlo, sinhi, 1.0)
    v_ref[...] = qkvu[:, ATTN_W + KV_W:ATTN_W + 2 * KV_W]
    u_ref[...] = qkvu[:, ATTN_W + 2 * KV_W:o]
    ga = jnp.dot(h, win_ref[:, o:o + D_MODEL], preferred_element_type=F32)
    sga_ref[...] = jax.nn.sigmoid(ga).astype(BF16)
    o += D_MODEL
    gs = jnp.dot(h, win_ref[:, o:o + D_MODEL], preferred_element_type=F32)
    sgs_ref[...] = jax.nn.sigmoid(gs).astype(BF16)


def _cast_block_rows(rows, steps):
    for blocks in range(steps, 0, -1):
        if rows % blocks == 0 and (rows // blocks) % (2 * SUBLANES) == 0:
            return rows // blocks
    raise ValueError(rows)


def _front(x, tabs, p, q_dtype, cast=()):
    n = x.shape[0]
    tm = FRONT_TM
    steps = n // tm
    base_tab, off_tab = tabs
    row = lambda w: pl.BlockSpec((tm, w), lambda i: (i, 0))
    consts = [p['n1'], p['w1a'], p['w3a'], p['w2a'], p['nm'], p['win'], p['qn'], p['kn'], p['ones']]
    outs = [(D_MODEL, F32), (ATTN_W, q_dtype), (KV_W, F32), (KV_W, F32), (SSM_W, F32),
            (D_MODEL, BF16), (D_MODEL, BF16)]
    cast_in, cast_out = [], []
    for w in cast:
        _, rows, cols = w.shape
        rb = _cast_block_rows(rows, steps)
        last = rows // rb - 1
        cast_in.append(pl.BlockSpec((None, rb, cols), lambda i, last=last: (0, jnp.minimum(i, last), 0)))
        cast_out.append(pl.BlockSpec((rb, cols), lambda i, last=last: (jnp.minimum(i, last), 0)))
    return pl.pallas_call(
        functools.partial(_front_kernel, len(cast)),
        grid=(steps,),
        in_specs=[row(D_MODEL), pl.BlockSpec((None, 2, LANES), lambda i: (i, 0, 0)), _const_spec(off_tab.shape)]
        + [_const_spec(c.shape) for c in consts] + cast_in,
        out_specs=[row(w) for w, _ in outs] + cast_out,
        out_shape=[jax.ShapeDtypeStruct((n, w), dt) for w, dt in outs]
        + [jax.ShapeDtypeStruct(w.shape[1:], BF16) for w in cast],
        compiler_params=pltpu.CompilerParams(dimension_semantics=("arbitrary",),
                                             vmem_limit_bytes=VMEM_LIMIT),
        name="front",
    )(x, base_tab, off_tab, *consts, *cast)


def _ffn_kernel(x_ref, n_ref, w1_ref, w3_ref, w2_ref, y_ref):
    y_ref[...] = _swiglu_residual(x_ref[...], n_ref, w1_ref, w3_ref, w2_ref)


def _ffn(x, p):
    n = x.shape[0]
    tm = min(FFN_TM, n)
    weights = [p['w1b'], p['w3b'], p['w2b']]
    return pl.pallas_call(
        _ffn_kernel,
        grid=(n // tm,),
        in_specs=[pl.BlockSpec((tm, D_MODEL), lambda i: (i, 0)), _const_spec(p['n2'].shape)]
        + [_const_spec(w.shape) for w in weights],
        out_specs=pl.BlockSpec((tm, D_MODEL), lambda i: (i, 0)),
        out_shape=jax.ShapeDtypeStruct((n, D_MODEL), F32),
        compiler_params=pltpu.CompilerParams(dimension_semantics=("arbitrary",),
                                             vmem_limit_bytes=VMEM_LIMIT),
        name="ffn",
    )(x, p['n2'], *weights)


def _cmul(ar, ai, br, bi):
    return ar * br - ai * bi, ar * bi + ai * br


def _ssm_discretize(are_ref, aim_ref, ldt_ref):
    a_re, a_im = are_ref[...], aim_ref[...]
    dt = jnp.exp(ldt_ref[...])
    mag = jnp.exp(dt * a_re)
    ab_re = mag * jnp.cos(dt * a_im)
    ab_im = mag * jnp.sin(dt * a_im)
    den = a_re * a_re + a_im * a_im
    nr, ni = ab_re - 1.0, ab_im
    f_re = (nr * a_re + ni * a_im) / den
    f_im = (ni * a_re - nr * a_im) / den
    return ab_re, ab_im, f_re, f_im


def _build_wb(bbre_ref, bbim_ref, f_re, f_im, wbre_s, wbim_s):
    half = SSM_S // 2
    for hh in range(2):
        fr, fi = f_re[:, hh * half:(hh + 1) * half], f_im[:, hh * half:(hh + 1) * half]
        br, bi = bbre_ref[hh], bbim_ref[hh]
        wbre_s[hh] = (br * fr - bi * fi).astype(BF16)
        wbim_s[hh] = (br * fi + bi * fr).astype(BF16)


def _ssm_weights_kernel(are_ref, aim_ref, ldt_ref, bbre_ref, bbim_ref, wbre_ref, wbim_ref):
    _, _, f_re, f_im = _ssm_discretize(are_ref, aim_ref, ldt_ref)
    _build_wb(bbre_ref, bbim_ref, f_re, f_im, wbre_ref, wbim_ref)


def _ssm_weights(p):
    shape = jax.ShapeDtypeStruct(p['bbre'].shape, BF16)
    return pl.pallas_call(_ssm_weights_kernel, out_shape=[shape, shape], name="ssm_weights")(
        p['are'], p['aim'], p['ldt'], p['bbre'], p['bbim'])


def _expand_state(ub, wbre_s, wbim_s, bur_s, bui_s):
    half_in, half_st = SSM_W // 2, SSM_S // 2
    for hh in range(2):
        uh = ub[:, hh * half_in:(hh + 1) * half_in]
        bur_s[:, hh * half_st:(hh + 1) * half_st] = jnp.dot(uh, wbre_s[hh], preferred_element_type=F32)
        bui_s[:, hh * half_st:(hh + 1) * half_st] = jnp.dot(uh, wbim_s[hh], preferred_element_type=F32)


def _project_state(xr_s, xi_s, ccat_ref):
    half_st = SSM_S // 2
    outs = []
    for hh in range(2):
        sl = slice(hh * half_st, (hh + 1) * half_st)
        xcat = jnp.concatenate([xr_s[:, sl].astype(BF16), xi_s[:, sl].astype(BF16)], axis=1)
        outs.append(jnp.dot(xcat, ccat_ref[hh], preferred_element_type=F32))
    return jnp.concatenate(outs, axis=1)


def _gated_attn(attn_b, sga, wao_ref):
    return sga.astype(F32) * jnp.dot(attn_b, wao_ref[...], preferred_element_type=F32)


def _gated_sum(attn_gated, y_ssm, sgs, wglu_ref, bglu_ref, wso_ref):
    z = jax.nn.gelu(y_ssm, approximate=True)
    gl = jnp.dot(z.astype(BF16), wglu_ref[...], preferred_element_type=F32) + bglu_ref[...]
    ssm = (z * jax.nn.sigmoid(gl)).astype(BF16)
    merged = attn_gated + sgs.astype(F32) * jnp.dot(ssm, wso_ref[...], preferred_element_type=F32)
    return merged.astype(BF16)


def _gated_merge(x1, attn_gated, y_ssm, sgs, wglu_ref, bglu_ref, wso_ref, wout_ref):
    merged = _gated_sum(attn_gated, y_ssm, sgs, wglu_ref, bglu_ref, wso_ref)
    return x1 + jnp.dot(merged, wout_ref[...], preferred_element_type=F32)


def _replicate_heads(kv):
    sw = pltpu.roll(kv, HEAD_DIM, axis=1)
    lo = lax.broadcasted_iota(jnp.int32, kv.shape, 1) < HEAD_DIM
    return jnp.where(lo, kv, sw).astype(BF16), jnp.where(lo, sw, kv).astype(BF16)


def _attn_scores(qs, kreps, keys_on_lanes=False):
    qb = qs[0].shape[0]
    lo = lax.broadcasted_iota(jnp.int32, (qb, LANES), 1) < HEAD_DIM
    dims = (((1,), (0,)), ((), ())) if keys_on_lanes else (((1,), (1,)), ((), ()))
    scores = []
    for q, krep in zip(qs, kreps):
        if q.dtype != BF16 or qb % (2 * SUBLANES):
            q = q.astype(F32)
        zero = jnp.zeros((), q.dtype)
        for kv in range(N_KV):
            parts = []
            for jj in range(2):
                t = q[:, (2 * kv + jj) * LANES:(2 * kv + jj + 1) * LANES]
                parts += [jnp.where(lo, t, zero), jnp.where(lo, zero, t)]
            stacked = jnp.concatenate(parts, axis=0).astype(BF16)
            scores.append(lax.dot_general(stacked, krep[kv], dims, preferred_element_type=F32))
    return jnp.concatenate(scores, axis=0)


def _window_valid(rows, qb, first):
    qi = lax.broadcasted_iota(jnp.int32, (rows, 2 * WINDOW), 0) & (qb - 1)
    sj = lax.broadcasted_iota(jnp.int32, (rows, 2 * WINDOW), 1)
    band = (sj <= WINDOW + qi) & (sj > qi)
    return band if first is False else band & ((sj >= WINDOW) | jnp.logical_not(first))


def _attn_softmax(s, sinks_ref, qb, firsts, bias=None):
    per_block = s.shape[0] // len(firsts)
    sink_parts = [jnp.full((qb, 1), sinks_ref[h], F32) for h in range(N_HEADS)]
    sink = jnp.concatenate(sink_parts * len(firsts), axis=0)
    if bias is None:
        valid = jnp.concatenate([_window_valid(per_block, qb, f) for f in firsts], axis=0)
        s = jnp.where(valid, s, MASK_VALUE)
    else:
        s = s + bias
    m = jnp.maximum(jnp.max(s, axis=-1, keepdims=True), sink)
    e = jnp.exp(s - m)
    denom = jnp.sum(e, axis=-1, keepdims=True) + jnp.exp(sink - m)
    return (e * (1.0 / denom)).astype(BF16)


def _attn_values(probs, vreps, qb, keys_on_lanes=False):
    lo = lax.broadcasted_iota(jnp.int32, (qb, LANES), 1) < HEAD_DIM
    rows = 2 * (N_HEADS // N_KV // 2) * qb
    dims = (((1,), (1,)), ((), ())) if keys_on_lanes else (((1,), (0,)), ((), ()))
    outs = []
    for b, vrep in enumerate(vreps):
        tiles = []
        for kv in range(N_KV):
            r0 = (b * N_KV + kv) * rows
            o = lax.dot_general(probs[r0:r0 + rows], vrep[kv], dims, preferred_element_type=F32)
            for jj in range(2):
                oa = o[(2 * jj) * qb:(2 * jj + 1) * qb]
                ob = o[(2 * jj + 1) * qb:(2 * jj + 2) * qb]
                tiles.append(jnp.where(lo, oa, ob))
        outs.append(jnp.concatenate(tiles, axis=1))
    return outs


def _mix_prompt_kernel(x1_ref, q_ref, k_ref, v_ref, u_ref, sga_ref, sgs_ref, sinks_ref,
                       wao_ref, wglu_ref, bglu_ref, wso_ref, wout_ref,
                       are_ref, aim_ref, ldt_ref, wbre_s, wbim_s, ccat_ref, d_ref, perm_ref, permt_ref,
                       x2_ref, sre_ref, sim_ref,
                       apr_s, api_s, a1_s, aseg_s, carry_s, cin_s,
                       bur_s, bui_s, xb_s, kprev_s, vprev_s, bias_s):
    b, i = pl.program_id(0), pl.program_id(1)
    nt = pl.num_programs(1)

    @pl.when((b == 0) & (i == 0))
    def _():
        ab_re, ab_im, _, _ = _ssm_discretize(are_ref, aim_ref, ldt_ref)
        a1_s[0:1, :] = ab_re
        a1_s[1:2, :] = ab_im
        pr, pi = ab_re, ab_im
        held = None
        for j in range(SCAN_SEG):
            rows = [jnp.broadcast_to(pr, (SUBLANES, SSM_S)), jnp.broadcast_to(pi, (SUBLANES, SSM_S))]
            if j % 2 == 0:
                held = rows
            else:
                rs = slice((j - 1) * SUBLANES, (j + 1) * SUBLANES)
                apr_s[rs, :] = jnp.concatenate([held[0], rows[0]], axis=0).astype(BF16)
                api_s[rs, :] = jnp.concatenate([held[1], rows[1]], axis=0).astype(BF16)
            if j == SCAN_SEG - 1:
                aseg_s[0:1, :] = pr
                aseg_s[1:2, :] = pi
            pr, pi = _cmul(pr, pi, ab_re, ab_im)
        for t, first in enumerate((False, True)):
            bias_s[t] = jnp.where(_window_valid(bias_s.shape[1], WINDOW, first), 0.0, MASK_VALUE)

    @pl.when(i == 0)
    def _():
        carry_s[...] = jnp.zeros_like(carry_s)
        kprev_s[...] = jnp.zeros_like(kprev_s)
        vprev_s[...] = jnp.zeros_like(vprev_s)

    k0, k1 = _replicate_heads(k_ref[...])
    v0, v1 = _replicate_heads(v_ref[...])
    qs, kreps, vreps, firsts = [], [], [], []
    for blk in range(MIX_TM // WINDOW):
        cur = slice(blk * WINDOW, (blk + 1) * WINDOW)
        if blk == 0:
            prev = [kprev_s[0], kprev_s[1], vprev_s[0], vprev_s[1]]
            firsts.append(i == 0)
        else:
            ps = slice((blk - 1) * WINDOW, blk * WINDOW)
            prev = [k0[ps], k1[ps], v0[ps], v1[ps]]
            firsts.append(False)
        qs.append(q_ref[cur, :])
        kreps.append([jnp.concatenate([prev[0], k0[cur]], axis=0), jnp.concatenate([prev[1], k1[cur]], axis=0)])
        vreps.append([jnp.concatenate([prev[2], v0[cur]], axis=0), jnp.concatenate([prev[3], v1[cur]], axis=0)])
    tail = slice(MIX_TM - WINDOW, MIX_TM)
    kprev_s[0], kprev_s[1] = k0[tail], k1[tail]
    vprev_s[0], vprev_s[1] = v0[tail], v1[tail]
    scores = _attn_scores(qs, kreps)

    half_in, half_st = SSM_W // 2, SSM_S // 2
    chunks = list(range(0, SSM_S, SCAN_LC))
    groups = [g * SCAN_ROWS for g in range(SCAN_GROUPS)]
    nblk = len(qs)
    blk_rows = scores.shape[0] // nblk
    u = u_ref[...]
    ub = u.astype(BF16)
    up = jnp.concatenate([jnp.dot(perm_ref[...], ub[g0:g0 + SCAN_ROWS], preferred_element_type=F32)
                          for g0 in groups], axis=0).astype(BF16)

    def expand_half(hh):
        uh = up[:, hh * half_in:(hh + 1) * half_in]
        bur_s[:, hh * half_st:(hh + 1) * half_st] = jnp.dot(uh, wbre_s[hh], preferred_element_type=F32)
        bui_s[:, hh * half_st:(hh + 1) * half_st] = jnp.dot(uh, wbim_s[hh], preferred_element_type=F32)

    def softmax_blocks(lo, hi):
        no_prev = (i == 0).astype(jnp.int32)
        bias = jnp.concatenate([bias_s[no_prev] if blk == 0 else bias_s[0]
                                for blk in range(lo, hi) for _ in range(N_HEADS)], axis=0)
        return _attn_softmax(scores[lo * blk_rows:hi * blk_rows], sinks_ref, WINDOW, firsts[lo:hi], bias)

    def scan_chunk(c0):
        cs = slice(c0, c0 + SCAN_LC)
        ar = jnp.broadcast_to(a1_s[0:1, cs], (SUBLANES, SCAN_LC))
        ai = jnp.broadcast_to(a1_s[1:2, cs], (SUBLANES, SCAN_LC))
        state = [(bur_s[g0:g0 + SUBLANES, cs], bui_s[g0:g0 + SUBLANES, cs]) for g0 in groups]
        for j in range(1, SCAN_SEG):
            for g, g0 in enumerate(groups):
                rs = slice(g0 + j * SUBLANES, g0 + (j + 1) * SUBLANES)
                xr, xi = state[g]
                xr, xi = (ar * xr - ai * xi + bur_s[rs, cs], ar * xi + ai * xr + bui_s[rs, cs])
                bur_s[rs, cs] = xr
                bui_s[rs, cs] = xi
                state[g] = (xr, xi)
        return state[0][0]

    mid = nblk // 2
    expand_half(0)
    probs_a = softmax_blocks(0, mid)
    expand_half(1)
    probs_b = softmax_blocks(mid, nblk)
    quarter = len(chunks) // 4
    scan_group = lambda g: [scan_chunk(c0) for c0 in chunks[g * quarter:(g + 1) * quarter]]
    scan_group(0)
    scan_group(1)
    scan_group(2)
    scan_group(3)

    sr, si = aseg_s[0:1, :], aseg_s[1:2, :]
    cr, ci = carry_s[0:1, :], carry_s[1:2, :]
    for g, g0 in enumerate(groups):
        last = g0 + SCAN_ROWS - SUBLANES
        for s in range(SUBLANES):
            k = g * SUBLANES + s
            cin_s[0, k:k + 1, :] = cr
            cin_s[1, k:k + 1, :] = ci
            pr, pi = _cmul(sr, si, cr, ci)
            cr = pr + bur_s[last + s:last + s + 1, :]
            ci = pi + bui_s[last + s:last + s + 1, :]
    carry_s[0:1, :] = cr
    carry_s[1:2, :] = ci
    pack = 2 * SUBLANES

    def fix_chunk(c0):
        cs = slice(c0, c0 + SCAN_LC)
        col = (c0 // half_st) * SSM_S + c0 % half_st
        for g, g0 in enumerate(groups):
            seg = slice(g * SUBLANES, (g + 1) * SUBLANES)
            cinr = jnp.concatenate([cin_s[0, seg, cs]] * 2, axis=0).astype(BF16)
            cini = jnp.concatenate([cin_s[1, seg, cs]] * 2, axis=0).astype(BF16)
            for r0 in range(0, SCAN_ROWS, pack):
                rs, ts = slice(g0 + r0, g0 + r0 + pack), slice(r0, r0 + pack)
                pr, pi = apr_s[ts, cs], api_s[ts, cs]
                xb_s[rs, col:col + SCAN_LC] = bur_s[rs, cs].astype(BF16) + pr * cinr - pi * cini
                xb_s[rs, col + half_st:col + half_st + SCAN_LC] = (bui_s[rs, cs].astype(BF16)
                                                                   + pr * cini + pi * cinr)

    def project_half(hh):
        return jnp.dot(xb_s[:, hh * SSM_S:(hh + 1) * SSM_S], ccat_ref[hh], preferred_element_type=F32)

    per_half = len(chunks) // 2
    for c0 in chunks[:per_half]:
        fix_chunk(c0)
    yp0 = project_half(0)
    for c0 in chunks[per_half:]:
        fix_chunk(c0)
    yp = jnp.concatenate([yp0, project_half(1)], axis=1)
    ypb = yp.astype(BF16)
    y_ssm = jnp.concatenate([jnp.dot(permt_ref[...], ypb[g0:g0 + SCAN_ROWS], preferred_element_type=F32)
                             for g0 in groups], axis=0) + d_ref[...] * u

    attn = _attn_values(probs_a, vreps[0:mid], WINDOW) + _attn_values(probs_b, vreps[mid:nblk], WINDOW)
    attn_gated = _gated_attn(jnp.concatenate(attn, axis=0).astype(BF16), sga_ref[...], wao_ref)
    x2_ref[...] = _gated_merge(x1_ref[...], attn_gated, y_ssm, sgs_ref[...],
                               wglu_ref, bglu_ref, wso_ref, wout_ref)

    @pl.when(i == nt - 1)
    def _():
        sre_ref[...] = carry_s[0:1, :]
        sim_ref[...] = carry_s[1:2, :]


def _mix_prompt(fr, p, batch, seq):
    x1, q, k, v, u, sga, sgs = fr
    tm = MIX_TM
    nt = seq // tm
    row = lambda w: pl.BlockSpec((tm, w), lambda b, i: (b * nt + i, 0))
    consts = [p['wao'], p['wglu'], p['bglu'], p['wso'], p['wout'],
              p['are'], p['aim'], p['ldt'], p['wbre'], p['wbim'], p['ccat'], p['dskip'], p['perm'], p['permt']]
    st_spec = pl.BlockSpec((None, 1, SSM_S), lambda b, i: (b, 0, 0))
    scratch = [pltpu.VMEM((SCAN_ROWS, SSM_S), BF16), pltpu.VMEM((SCAN_ROWS, SSM_S), BF16),
               pltpu.VMEM((SUBLANES, SSM_S), F32), pltpu.VMEM((SUBLANES, SSM_S), F32),
               pltpu.VMEM((SUBLANES, SSM_S), F32), pltpu.VMEM((2, SCAN_GROUPS * SUBLANES, SSM_S), F32),
               pltpu.VMEM((tm, SSM_S), F32), pltpu.VMEM((tm, SSM_S), F32), pltpu.VMEM((tm, 2 * SSM_S), BF16),
               pltpu.VMEM((N_KV, WINDOW, LANES), BF16), pltpu.VMEM((N_KV, WINDOW, LANES), BF16),
               pltpu.VMEM((2, WINDOW, 2 * WINDOW), F32)]
    return pl.pallas_call(
        _mix_prompt_kernel,
        grid=(batch, nt),
        in_specs=[row(D_MODEL), row(ATTN_W), row(KV_W), row(KV_W), row(SSM_W), row(D_MODEL), row(D_MODEL),
                  pl.BlockSpec(memory_space=pltpu.SMEM)] + [_const_spec(c.shape) for c in consts],
        out_specs=[row(D_MODEL), st_spec, st_spec],
        out_shape=[jax.ShapeDtypeStruct((batch * seq, D_MODEL), F32),
                   jax.ShapeDtypeStruct((batch, 1, SSM_S), F32),
                   jax.ShapeDtypeStruct((batch, 1, SSM_S), F32)],
        scratch_shapes=scratch,
        compiler_params=pltpu.CompilerParams(dimension_semantics=("arbitrary", "arbitrary"),
                                             vmem_limit_bytes=VMEM_LIMIT),
        name="mix_prompt",
    )(x1, q, k, v, u, sga, sgs, p['sinks'], *consts)


def _attn_sample_kernel(q_ref, kn_ref, vn_ref, ckt_ref, cvt_ref, sinks_ref, o_ref, nkt_ref, nvt_ref,
                        q8_s, knp_s, vnp_s):
    steps = q_ref.shape[0]
    shift = WINDOW - steps
    keep = lax.broadcasted_iota(jnp.int32, (HEAD_DIM, WINDOW), 1) < shift
    q8_s[...] = jnp.zeros_like(q8_s)
    knp_s[...] = jnp.zeros_like(knp_s)
    vnp_s[...] = jnp.zeros_like(vnp_s)
    qs, kreps, vreps = [], [], []
    for s in range(SAMPLE_BS):
        for t in range(steps):
            q8_s[s, t:t + 1, :] = q_ref[t, s:s + 1, :]
            knp_s[s, t:t + 1, :] = kn_ref[t, s:s + 1, :]
            vnp_s[s, t:t + 1, :] = vn_ref[t, s:s + 1, :]
        news = [knp_s[s].T, vnp_s[s].T]
        ends = [pltpu.roll(n, shift, axis=1) for n in news]
        krep, vrep = [], []
        for kv in range(N_KV):
            hs = slice(kv * HEAD_DIM, (kv + 1) * HEAD_DIM)
            for cache_ref, out_ref, new, end, reps in ((ckt_ref, nkt_ref, news[0], ends[0], krep),
                                                       (cvt_ref, nvt_ref, news[1], ends[1], vrep)):
                old = cache_ref[s, kv]
                out_ref[s, kv] = jnp.where(keep, pltpu.roll(old, shift, axis=1), end[hs])
                both = jnp.concatenate([old, new[hs]], axis=1).astype(BF16)
                reps.append(jnp.concatenate([both, both], axis=0))
        qs.append(q8_s[s])
        kreps.append(krep)
        vreps.append(vrep)
    probs = _attn_softmax(_attn_scores(qs, kreps, keys_on_lanes=True), sinks_ref, SUBLANES, [False] * SAMPLE_BS)
    outs = _attn_values(probs, vreps, SUBLANES, keys_on_lanes=True)
    for s in range(SAMPLE_BS):
        for t in range(steps):
            o_ref[t, s:s + 1, :] = outs[s][t:t + 1, :]


def _attn_sample(q, kn, vn, ckt, cvt, sinks, steps, nseq):
    bs = SAMPLE_BS
    blk3 = lambda w: pl.BlockSpec((steps, bs, w), lambda i: (0, i, 0))
    cache = pl.BlockSpec((bs, N_KV, HEAD_DIM, WINDOW), lambda i: (i, 0, 0, 0))
    cache_shape = jax.ShapeDtypeStruct((nseq, N_KV, HEAD_DIM, WINDOW), F32)
    return pl.pallas_call(
        _attn_sample_kernel,
        grid=(nseq // bs,),
        in_specs=[blk3(ATTN_W), blk3(KV_W), blk3(KV_W), cache, cache, pl.BlockSpec(memory_space=pltpu.SMEM)],
        out_specs=[blk3(ATTN_W), cache, cache],
        out_shape=[jax.ShapeDtypeStruct((steps, nseq, ATTN_W), F32), cache_shape, cache_shape],
        scratch_shapes=[pltpu.VMEM((bs, SUBLANES, ATTN_W), F32), pltpu.VMEM((bs, LANES, KV_W), F32),
                        pltpu.VMEM((bs, LANES, KV_W), F32)],
        compiler_params=pltpu.CompilerParams(dimension_semantics=("arbitrary",)),
        name="attn_sample",
    )(q.reshape(steps, nseq, ATTN_W), kn.reshape(steps, nseq, KV_W), vn.reshape(steps, nseq, KV_W),
      ckt, cvt, sinks)


def _mix_sample_kernel(x1_ref, attn_ref, u_ref, sga_ref, sgs_ref, x0r_ref, x0i_ref,
                       wao_ref, wglu_ref, bglu_ref, wso_ref, wout_ref,
                       are_ref, aim_ref, ldt_ref, wbre_s, wbim_s, ccat_ref, d_ref,
                       x2_ref, sre_ref, sim_ref,
                       bur_s, bui_s):
    nseq = x0r_ref.shape[0]
    steps = x1_ref.shape[0] // nseq
    ab_re, ab_im, _, _ = _ssm_discretize(are_ref, aim_ref, ldt_ref)
    u = u_ref[...]
    _expand_state(u.astype(BF16), wbre_s, wbim_s, bur_s, bui_s)
    for c0 in range(0, SSM_S, LANES):
        cs = slice(c0, c0 + LANES)
        ar, ai = ab_re[:, cs], ab_im[:, cs]
        xr, xi = x0r_ref[:, cs], x0i_ref[:, cs]
        for t in range(steps):
            rs = slice(t * nseq, (t + 1) * nseq)
            nr = ar * xr - ai * xi + bur_s[rs, cs]
            ni = ar * xi + ai * xr + bui_s[rs, cs]
            bur_s[rs, cs] = nr
            bui_s[rs, cs] = ni
            xr, xi = nr, ni
        sre_ref[:, cs] = xr
        sim_ref[:, cs] = xi
    y_ssm = _project_state(bur_s, bui_s, ccat_ref) + d_ref[...] * u
    attn_gated = _gated_attn(attn_ref[...].astype(BF16), sga_ref[...], wao_ref)
    x2_ref[...] = _gated_merge(x1_ref[...], attn_gated, y_ssm, sgs_ref[...],
                               wglu_ref, bglu_ref, wso_ref, wout_ref)


def _mix_sample(x1, attn, u, sga, sgs, x0r, x0i, p):
    n, nseq = x1.shape[0], x0r.shape[0]
    args = [x1, attn, u, sga, sgs, x0r, x0i, p['wao'], p['wglu'], p['bglu'], p['wso'], p['wout'],
            p['are'], p['aim'], p['ldt'], p['wbre'], p['wbim'], p['ccat'], p['dskip']]
    return pl.pallas_call(
        _mix_sample_kernel,
        grid=(1,),
        in_specs=[_const_spec(a.shape) for a in args],
        out_specs=[pl.BlockSpec((n, D_MODEL), lambda i: (0, 0)), pl.BlockSpec((nseq, SSM_S), lambda i: (0, 0)),
                   pl.BlockSpec((nseq, SSM_S), lambda i: (0, 0))],
        out_shape=[jax.ShapeDtypeStruct((n, D_MODEL), F32),
                   jax.ShapeDtypeStruct((nseq, SSM_S), F32),
                   jax.ShapeDtypeStruct((nseq, SSM_S), F32)],
        scratch_shapes=[pltpu.VMEM((n, SSM_S), F32), pltpu.VMEM((n, SSM_S), F32)],
        compiler_params=pltpu.CompilerParams(dimension_semantics=("arbitrary",),
                                             vmem_limit_bytes=VMEM_LIMIT),
        name="mix_sample",
    )(*args)


def _rope_tables(base_pos, off_pos):
    half = HEAD_DIM // 2
    lane = np.arange(LANES)
    inv = ROPE_THETA ** (-2.0 * (lane % half) / HEAD_DIM)
    first = (lane % HEAD_DIM) < half
    ab = np.asarray(base_pos, np.float64)[:, None] * inv[None, :]
    ao = np.asarray(off_pos, np.float64)[:, None] * inv[None, :]
    base = np.stack([np.cos(ab), np.sin(ab)], axis=1)
    co, so = np.cos(ao), np.sin(ao)
    lo = np.where(first, -1.0, 0.0)
    hi = np.where(first, 0.0, 1.0)
    off = np.stack([co, so, co * lo, so * lo, co * hi, so * hi], axis=0)
    return jnp.asarray(base, F32), jnp.asarray(off, F32)


def _block_diag_halves(m):
    g, r, c = m.shape
    hg = g // 2
    strip = jnp.transpose(m.reshape(2, hg, r, c), (0, 2, 1, 3)).reshape(2, 1, r, hg * c)
    full = jnp.broadcast_to(strip, (2, hg, r, hg * c)).reshape(2, hg * r, hg * c)
    mask = jnp.asarray(np.kron(np.eye(hg), np.ones((r, c))), m.dtype)
    return full * mask


def _layout_params(ffn1_norm, ffn1_w1, ffn1_w3, ffn1_w2, mix_norm, w_in, q_norm, k_norm, attn_sinks,
                   w_attn_out, ssm_a_re, ssm_a_im, ssm_log_dt, ssm_b_re, ssm_b_im, ssm_c_re, ssm_c_im,
                   ssm_d, w_glu, b_glu, w_ssm_out, w_out, ffn2_norm, ffn2_w1, ffn2_w3, ffn2_w2):
    l = 0
    row = lambda a: a.reshape(1, -1).astype(F32)
    head_ones = jnp.asarray(np.kron(np.eye(LANES // HEAD_DIM), np.ones((HEAD_DIM, HEAD_DIM))), F32)
    r = np.arange(SCAN_ROWS)
    tok = (r % SUBLANES) * SCAN_SEG + r // SUBLANES
    perm = jnp.asarray(tok[:, None] == np.arange(SCAN_ROWS)[None, :], BF16)
    c_cat = jnp.concatenate([_block_diag_halves(jnp.swapaxes(ssm_c_re[l], 1, 2)),
                             -_block_diag_halves(jnp.swapaxes(ssm_c_im[l], 1, 2))], axis=1)
    return dict(
        n1=row(ffn1_norm[l]), w1a=ffn1_w1[l].astype(BF16), w3a=ffn1_w3[l].astype(BF16), w2a=ffn1_w2[l].astype(BF16),
        nm=row(mix_norm[l]), win=w_in[l].astype(BF16),
        qn=row(jnp.tile(q_norm[l], LANES // HEAD_DIM)), kn=row(jnp.tile(k_norm[l], LANES // HEAD_DIM)),
        ones=head_ones.astype(BF16), sinks=attn_sinks[l].astype(F32),
        bglu=row(b_glu[l]),
        are=row(ssm_a_re[l]), aim=row(ssm_a_im[l]), ldt=row(jnp.repeat(ssm_log_dt[l], SSM_N)),
        bbre=_block_diag_halves(jnp.swapaxes(ssm_b_re[l], 1, 2).astype(F32)),
        bbim=_block_diag_halves(jnp.swapaxes(ssm_b_im[l], 1, 2).astype(F32)),
        ccat=c_cat.astype(BF16), dskip=row(ssm_d[l]), perm=perm, permt=perm.T,
        n2=row(ffn2_norm[l]),
    )


def kernel(x_prompt, x_sample, cache_k, cache_v, state_ssm_re, state_ssm_im, ffn1_norm, ffn1_w1, ffn1_w3, ffn1_w2, mix_norm, w_in, q_norm, k_norm, attn_sinks, w_attn_out, ssm_a_re, ssm_a_im, ssm_log_dt, ssm_b_re, ssm_b_im, ssm_c_re, ssm_c_im, ssm_d, w_glu, b_glu, w_ssm_out, w_out, ffn2_norm, ffn2_w1, ffn2_w3, ffn2_w2):
    assert ffn1_norm.shape[0] == 1, "single trunk layer"
    batch, seq, _ = x_prompt.shape
    nseq, steps, _ = x_sample.shape
    p = _layout_params(ffn1_norm, ffn1_w1, ffn1_w3, ffn1_w2, mix_norm, w_in, q_norm, k_norm, attn_sinks,
                       w_attn_out, ssm_a_re, ssm_a_im, ssm_log_dt, ssm_b_re, ssm_b_im, ssm_c_re, ssm_c_im,
                       ssm_d, w_glu, b_glu, w_ssm_out, w_out, ffn2_norm, ffn2_w1, ffn2_w3, ffn2_w2)

    p['wbre'], p['wbim'] = _ssm_weights(p)

    tile0 = np.arange(batch * seq // FRONT_TM) * FRONT_TM
    tabs_p = _rope_tables(tile0 % seq, np.arange(FRONT_TM))
    later = dict(wao=w_attn_out, wglu=w_glu, wso=w_ssm_out, wout=w_out, w1b=ffn2_w1, w3b=ffn2_w3, w2b=ffn2_w2)
    fr = _front(x_prompt.reshape(batch * seq, D_MODEL), tabs_p, p, BF16, cast=list(later.values()))
    p.update(zip(later, fr[7:]))
    fr = fr[:7]
    x2p, re_p, im_p = _mix_prompt(fr, p, batch, seq)
    y_prompt = _ffn(x2p, p).reshape(batch, seq, D_MODEL)
    tail_kv = lambda a: a.reshape(batch, seq, KV_W)[:, seq - WINDOW:].reshape(batch, WINDOW, N_KV, HEAD_DIM)
    k_p, v_p = tail_kv(fr[2]), tail_kv(fr[3])

    xs = jnp.swapaxes(x_sample, 0, 1).reshape(steps * nseq, D_MODEL)
    tile0 = np.arange(steps * nseq // FRONT_TM) * FRONT_TM
    tabs_s = _rope_tables(PAST_LEN + tile0 // nseq, np.arange(FRONT_TM) // nseq)
    x1s, qs, ks, vs, us, sgas, sgss = _front(xs, tabs_s, p, F32)
    to_lanes = lambda c: jnp.transpose(c[0], (0, 2, 3, 1))
    attn_s, nkt, nvt = _attn_sample(qs, ks, vs, to_lanes(cache_k), to_lanes(cache_v), p['sinks'], steps, nseq)
    x2s, re_s, im_s = _mix_sample(x1s, attn_s.reshape(steps * nseq, ATTN_W), us, sgas, sgss,
                                  state_ssm_re[0].reshape(nseq, SSM_S), state_ssm_im[0].reshape(nseq, SSM_S), p)
    y_sample = jnp.swapaxes(_ffn(x2s, p).reshape(steps, nseq, D_MODEL), 0, 1)
    k_s, v_s = jnp.transpose(nkt, (0, 3, 1, 2)), jnp.transpose(nvt, (0, 3, 1, 2))

    st = lambda a, n: a.reshape(1, n, SSM_G, SSM_N)
    return (y_prompt, y_sample, k_p[None], v_p[None], st(re_p, batch), st(im_p, batch),
            k_s[None], v_s[None], st(re_s, nseq), st(im_s, nseq))
```

```python
import functools

import jax
import jax.numpy as jnp
import numpy as np
from jax import lax
from jax.experimental import pallas as pl
from jax.experimental.pallas import tpu as pltpu

F32 = jnp.float32
BF16 = jnp.bfloat16

D_MODEL = 1024
HEAD_DIM = 64
N_HEADS = 8
N_KV = 2
WINDOW = 128
ATTN_W = N_HEADS * HEAD_DIM
KV_W = N_KV * HEAD_DIM
SSM_W = 512
SSM_G = 32
SSM_CH = 16
SSM_N = 64
SSM_S = SSM_G * SSM_N
D_FF = 2816
IN_COLS = ATTN_W + 2 * KV_W + SSM_W + 2 * D_MODEL
ROPE_THETA = 10000.0
RMS_EPS = 1e-6
MASK_VALUE = -1e30
PAST_LEN = 16384

LANES = 128
SUBLANES = 8
MXU_DIM = 256

FRONT_TM = 512
FFN_TM = 1024
MIX_TM = 512
SCAN_ROWS = 256
SCAN_GROUPS = MIX_TM // SCAN_ROWS
SCAN_SEG = SCAN_ROWS // SUBLANES
FF_CHUNK = 256
SCAN_LC = 512
SAMPLE_BS = 32
VMEM_LIMIT = 58 * 1024 * 1024


def _const_spec(shape):
    nd = len(shape)
    return pl.BlockSpec(shape, lambda *_: (0,) * nd, pipeline_mode=pl.Buffered(1))


def _rms(x, w):
    ms = jnp.mean(x * x, axis=-1, keepdims=True)
    return x * lax.rsqrt(ms + RMS_EPS) * w


def _swiglu_residual(x, nw_ref, w1_ref, w3_ref, w2_ref):
    h = _rms(x, nw_ref[...]).astype(BF16)
    acc = None
    for c0 in range(0, D_FF, FF_CHUNK):
        cs = slice(c0, min(c0 + FF_CHUNK, D_FF))
        a = jnp.dot(h, w1_ref[:, cs], preferred_element_type=F32)
        b = jnp.dot(h, w3_ref[:, cs], preferred_element_type=F32)
        g = (a * jax.nn.sigmoid(a) * b).astype(BF16)
        t = jnp.dot(g, w2_ref[cs, :], preferred_element_type=F32)
        acc = t if acc is None else acc + t
    return x + 0.5 * acc


def _head_norm_rope(t, ones_ref, g_ref, cos, sinlo, sinhi, scale):
    ssum = jnp.dot((t * t).astype(BF16), ones_ref[...], preferred_element_type=F32)
    tn = t * lax.rsqrt(ssum * (1.0 / HEAD_DIM) + RMS_EPS) * g_ref[...]
    up = pltpu.roll(tn, HEAD_DIM // 2, axis=1)
    dn = pltpu.roll(tn, LANES - HEAD_DIM // 2, axis=1)
    r = tn * cos + dn * sinlo + up * sinhi
    return r * scale if scale != 1.0 else r


def _front_kernel(n_cast, x_ref, base_ref, off_ref,
                  n1_ref, w1_ref, w3_ref, w2_ref, nm_ref, win_ref, qn_ref, kn_ref, ones_ref, *refs):
    cast_in, refs = refs[:n_cast], refs[n_cast:]
    x1_ref, q_ref, k_ref, v_ref, u_ref, sga_ref, sgs_ref = refs[:7]
    for src, dst in zip(cast_in, refs[7:]):
        dst[...] = src[...].astype(BF16)
    x1 = _swiglu_residual(x_ref[...], n1_ref, w1_ref, w3_ref, w2_ref)
    x1_ref[...] = x1
    h = _rms(x1, nm_ref[...]).astype(BF16)
    cb, sb = base_ref[0:1, :], base_ref[1:2, :]
    cos = cb * off_ref[0] - sb * off_ref[1]
    sinlo = sb * off_ref[2] + cb * off_ref[3]
    sinhi = sb * off_ref[4] + cb * off_ref[5]
    o = ATTN_W + 2 * KV_W + SSM_W
    qkvu = jnp.dot(h, win_ref[:, 0:o], preferred_element_type=F32)
    for j in range(ATTN_W // LANES):
        q_ref[:, j * LANES:(j + 1) * LANES] = _head_norm_rope(
            qkvu[:, j * LANES:(j + 1) * LANES], ones_ref, qn_ref, cos, sinlo, sinhi,
            HEAD_DIM ** -0.5).astype(q_ref.dtype)
    k_ref[...] = _head_norm_rope(qkvu[:, ATTN_W:ATTN_W + KV_W], ones_ref, kn_ref, cos, sinlo, sinhi, 1.0)
    v_ref[...] = qkvu[:, ATTN_W + KV_W:ATTN_W + 2 * KV_W]
    u_ref[...] = qkvu[:, ATTN_W + 2 * KV_W:o]
    ga = jnp.dot(h, win_ref[:, o:o + D_MODEL], preferred_element_type=F32)
    sga_ref[...] = jax.nn.sigmoid(ga).astype(BF16)
    o += D_MODEL
    gs = jnp.dot(h, win_ref[:, o:o + D_MODEL], preferred_element_type=F32)
    sgs_ref[...] = jax.nn.sigmoid(gs).astype(BF16)


def _cast_block_rows(rows, steps):
    for blocks in range(steps, 0, -1):
        if rows % blocks == 0 and (rows // blocks) % (2 * SUBLANES) == 0:
            return rows // blocks
    raise ValueError(rows)


def _front(x, tabs, p, q_dtype, cast=()):
    n = x.shape[0]
    tm = FRONT_TM
    steps = n // tm
    base_tab, off_tab = tabs
    row = lambda w: pl.BlockSpec((tm, w), lambda i: (i, 0))
    consts = [p['n1'], p['w1a'], p['w3a'], p['w2a'], p['nm'], p['win'], p['qn'], p['kn'], p['ones']]
    outs = [(D_MODEL, F32), (ATTN_W, q_dtype), (KV_W, F32), (KV_W, F32), (SSM_W, F32),
            (D_MODEL, BF16), (D_MODEL, BF16)]
    cast_in, cast_out = [], []
    for w in cast:
        _, rows, cols = w.shape
        rb = _cast_block_rows(rows, steps)
        last = rows // rb - 1
        cast_in.append(pl.BlockSpec((None, rb, cols), lambda i, last=last: (0, jnp.minimum(i, last), 0)))
        cast_out.append(pl.BlockSpec((rb, cols), lambda i, last=last: (jnp.minimum(i, last), 0)))
    return pl.pallas_call(
        functools.partial(_front_kernel, len(cast)),
        grid=(steps,),
        in_specs=[row(D_MODEL), pl.BlockSpec((None, 2, LANES), lambda i: (i, 0, 0)), _const_spec(off_tab.shape)]
        + [_const_spec(c.shape) for c in consts] + cast_in,
        out_specs=[row(w) for w, _ in outs] + cast_out,
        out_shape=[jax.ShapeDtypeStruct((n, w), dt) for w, dt in outs]
        + [jax.ShapeDtypeStruct(w.shape[1:], BF16) for w in cast],
        compiler_params=pltpu.CompilerParams(dimension_semantics=("arbitrary",),
                                             vmem_limit_bytes=VMEM_LIMIT),
        name="front",
    )(x, base_tab, off_tab, *consts, *cast)


def _ffn_kernel(x_ref, n_ref, w1_ref, w3_ref, w2_ref, y_ref):
    y_ref[...] = _swiglu_residual(x_ref[...], n_ref, w1_ref, w3_ref, w2_ref)


def _ffn(x, p):
    n = x.shape[0]
    tm = min(FFN_TM, n)
    weights = [p['w1b'], p['w3b'], p['w2b']]
    return pl.pallas_call(
        _ffn_kernel,
        grid=(n // tm,),
        in_specs=[pl.BlockSpec((tm, D_MODEL), lambda i: (i, 0)), _const_spec(p['n2'].shape)]
        + [_const_spec(w.shape) for w in weights],
        out_specs=pl.BlockSpec((tm, D_MODEL), lambda i: (i, 0)),
        out_shape=jax.ShapeDtypeStruct((n, D_MODEL), F32),
        compiler_params=pltpu.CompilerParams(dimension_semantics=("arbitrary",),
                                             vmem_limit_bytes=VMEM_LIMIT),
        name="ffn",
    )(x, p['n2'], *weights)


def _ffn_stream_kernel(x_ref, n_ref, w1_ref, w3_ref, w2_ref, y_ref, h_s, acc_s):
    j = pl.program_id(0)
    blk = w1_ref.shape[1]

    @pl.when(j == 0)
    def _():
        h_s[...] = _rms(x_ref[...], n_ref[...]).astype(BF16)
        acc_s[...] = jnp.zeros_like(acc_s)

    h = h_s[...]
    acc = acc_s[...]
    for c0 in range(0, blk, FF_CHUNK):
        cs = slice(c0, min(c0 + FF_CHUNK, blk))
        a = jnp.dot(h, w1_ref[:, cs], preferred_element_type=F32)
        b = jnp.dot(h, w3_ref[:, cs], preferred_element_type=F32)
        g = (a * jax.nn.sigmoid(a) * b).astype(BF16)
        acc = acc + jnp.dot(g, w2_ref[cs, :], preferred_element_type=F32)
    acc_s[...] = acc

    @pl.when(j == pl.num_programs(0) - 1)
    def _():
        y_ref[...] = x_ref[...] + 0.5 * acc


def _ffn_stream(x, p):
    n = x.shape[0]
    blk = D_FF // 2
    assert blk % LANES == 0
    full = pl.BlockSpec((n, D_MODEL), lambda j: (0, 0))
    return pl.pallas_call(
        _ffn_stream_kernel,
        grid=(D_FF // blk,),
        in_specs=[full, _const_spec(p['n2'].shape),
                  pl.BlockSpec((D_MODEL, blk), lambda j: (0, j)), pl.BlockSpec((D_MODEL, blk), lambda j: (0, j)),
                  pl.BlockSpec((blk, D_MODEL), lambda j: (j, 0))],
        out_specs=full,
        out_shape=jax.ShapeDtypeStruct((n, D_MODEL), F32),
        scratch_shapes=[pltpu.VMEM((n, D_MODEL), BF16), pltpu.VMEM((n, D_MODEL), F32)],
        compiler_params=pltpu.CompilerParams(dimension_semantics=("arbitrary",),
                                             vmem_limit_bytes=VMEM_LIMIT),
        name="ffn_stream",
    )(x, p['n2'], p['w1b'], p['w3b'], p['w2b'])


def _cmul(ar, ai, br, bi):
    return ar * br - ai * bi, ar * bi + ai * br


def _ssm_discretize(are_ref, aim_ref, ldt_ref):
    a_re, a_im = are_ref[...], aim_ref[...]
    dt = jnp.exp(ldt_ref[...])
    mag = jnp.exp(dt * a_re)
    ab_re = mag * jnp.cos(dt * a_im)
    ab_im = mag * jnp.sin(dt * a_im)
    den = a_re * a_re + a_im * a_im
    nr, ni = ab_re - 1.0, ab_im
    f_re = (nr * a_re + ni * a_im) / den
    f_im = (ni * a_re - nr * a_im) / den
    return ab_re, ab_im, f_re, f_im


def _build_wb(bbre_ref, bbim_ref, f_re, f_im, wbre_s, wbim_s):
    half = SSM_S // 2
    for hh in range(2):
        fr, fi = f_re[:, hh * half:(hh + 1) * half], f_im[:, hh * half:(hh + 1) * half]
        br, bi = bbre_ref[hh], bbim_ref[hh]
        wbre_s[hh] = (br * fr - bi * fi).astype(BF16)
        wbim_s[hh] = (br * fi + bi * fr).astype(BF16)


def _ssm_weights_kernel(are_ref, aim_ref, ldt_ref, bbre_ref, bbim_ref, wbre_ref, wbim_ref):
    _, _, f_re, f_im = _ssm_discretize(are_ref, aim_ref, ldt_ref)
    _build_wb(bbre_ref, bbim_ref, f_re, f_im, wbre_ref, wbim_ref)


def _ssm_weights(p):
    shape = jax.ShapeDtypeStruct(p['bbre'].shape, BF16)
    return pl.pallas_call(_ssm_weights_kernel, out_shape=[shape, shape], name="ssm_weights")(
        p['are'], p['aim'], p['ldt'], p['bbre'], p['bbim'])


def _expand_state(ub, wbre_s, wbim_s, bur_s, bui_s):
    half_in, half_st = SSM_W // 2, SSM_S // 2
    for hh in range(2):
        uh = ub[:, hh * half_in:(hh + 1) * half_in]
        bur_s[:, hh * half_st:(hh + 1) * half_st] = jnp.dot(uh, wbre_s[hh], preferred_element_type=F32)
        bui_s[:, hh * half_st:(hh + 1) * half_st] = jnp.dot(uh, wbim_s[hh], preferred_element_type=F32)


def _project_state(xr_s, xi_s, ccat_ref):
    half_st = SSM_S // 2
    outs = []
    for hh in range(2):
        sl = slice(hh * half_st, (hh + 1) * half_st)
        xcat = jnp.concatenate([xr_s[:, sl].astype(BF16), xi_s[:, sl].astype(BF16)], axis=1)
        outs.append(jnp.dot(xcat, ccat_ref[hh], preferred_element_type=F32))
    return jnp.concatenate(outs, axis=1)


def _gated_attn(attn_b, sga, wao_ref):
    return sga.astype(F32) * jnp.dot(attn_b, wao_ref[...], preferred_element_type=F32)


def _gated_sum(attn_gated, y_ssm, sgs, wglu_ref, bglu_ref, wso_ref):
    z = jax.nn.gelu(y_ssm, approximate=True)
    gl = jnp.dot(z.astype(BF16), wglu_ref[...], preferred_element_type=F32) + bglu_ref[...]
    ssm = (z * jax.nn.sigmoid(gl)).astype(BF16)
    merged = attn_gated + sgs.astype(F32) * jnp.dot(ssm, wso_ref[...], preferred_element_type=F32)
    return merged.astype(BF16)


def _gated_merge(x1, attn_gated, y_ssm, sgs, wglu_ref, bglu_ref, wso_ref, wout_ref):
    merged = _gated_sum(attn_gated, y_ssm, sgs, wglu_ref, bglu_ref, wso_ref)
    return x1 + jnp.dot(merged, wout_ref[...], preferred_element_type=F32)


def _replicate_heads(kv):
    sw = pltpu.roll(kv, HEAD_DIM, axis=1)
    lo = lax.broadcasted_iota(jnp.int32, kv.shape, 1) < HEAD_DIM
    return jnp.where(lo, kv, sw).astype(BF16), jnp.where(lo, sw, kv).astype(BF16)


def _attn_scores(qs, kreps, keys_on_lanes=False):
    qb = qs[0].shape[0]
    lo = lax.broadcasted_iota(jnp.int32, (qb, LANES), 1) < HEAD_DIM
    dims = (((1,), (0,)), ((), ())) if keys_on_lanes else (((1,), (1,)), ((), ()))
    scores = []
    for q, krep in zip(qs, kreps):
        if q.dtype != BF16 or qb % (2 * SUBLANES):
            q = q.astype(F32)
        zero = jnp.zeros((), q.dtype)
        for kv in range(N_KV):
            parts = []
            for jj in range(2):
                t = q[:, (2 * kv + jj) * LANES:(2 * kv + jj + 1) * LANES]
                parts += [jnp.where(lo, t, zero), jnp.where(lo, zero, t)]
            stacked = jnp.concatenate(parts, axis=0).astype(BF16)
            scores.append(lax.dot_general(stacked, krep[kv], dims, preferred_element_type=F32))
    return jnp.concatenate(scores, axis=0)


def _window_valid(rows, qb, first):
    qi = lax.broadcasted_iota(jnp.int32, (rows, 2 * WINDOW), 0) & (qb - 1)
    sj = lax.broadcasted_iota(jnp.int32, (rows, 2 * WINDOW), 1)
    band = (sj <= WINDOW + qi) & (sj > qi)
    return band if first is False else band & ((sj >= WINDOW) | jnp.logical_not(first))


def _attn_softmax(s, sinks_ref, qb, firsts, bias=None):
    per_block = s.shape[0] // len(firsts)
    sink_parts = [jnp.full((qb, 1), sinks_ref[h], F32) for h in range(N_HEADS)]
    sink = jnp.concatenate(sink_parts * len(firsts), axis=0)
    if bias is None:
        valid = jnp.concatenate([_window_valid(per_block, qb, f) for f in firsts], axis=0)
        s = jnp.where(valid, s, MASK_VALUE)
    else:
        s = s + bias
    m = jnp.maximum(jnp.max(s, axis=-1, keepdims=True), sink)
    e = jnp.exp(s - m)
    denom = jnp.sum(e, axis=-1, keepdims=True) + jnp.exp(sink - m)
    return (e * (1.0 / denom)).astype(BF16)


def _attn_values(probs, vreps, qb, keys_on_lanes=False):
    lo = lax.broadcasted_iota(jnp.int32, (qb, LANES), 1) < HEAD_DIM
    rows = 2 * (N_HEADS // N_KV // 2) * qb
    dims = (((1,), (1,)), ((), ())) if keys_on_lanes else (((1,), (0,)), ((), ()))
    outs = []
    for b, vrep in enumerate(vreps):
        tiles = []
        for kv in range(N_KV):
            r0 = (b * N_KV + kv) * rows
            o = lax.dot_general(probs[r0:r0 + rows], vrep[kv], dims, preferred_element_type=F32)
            for jj in range(2):
                oa = o[(2 * jj) * qb:(2 * jj + 1) * qb]
                ob = o[(2 * jj + 1) * qb:(2 * jj + 2) * qb]
                tiles.append(jnp.where(lo, oa, ob))
        outs.append(jnp.concatenate(tiles, axis=1))
    return outs


def _mix_prompt_kernel(x1_ref, q_ref, k_ref, v_ref, u_ref, sga_ref, sgs_ref, sinks_ref,
                       wao_ref, wglu_ref, bglu_ref, wso_ref, wout_ref,
                       are_ref, aim_ref, ldt_ref, wbre_s, wbim_s, ccat_ref, d_ref, perm_ref, permt_ref,
                       x2_ref, sre_ref, sim_ref,
                       apr_s, api_s, a1_s, aseg_s, carry_s, cin_s,
                       bur_s, bui_s, xb_s, kprev_s, vprev_s, bias_s):
    b, i = pl.program_id(0), pl.program_id(1)
    nt = pl.num_programs(1)

    @pl.when((b == 0) & (i == 0))
    def _():
        ab_re, ab_im, _, _ = _ssm_discretize(are_ref, aim_ref, ldt_ref)
        a1_s[0:1, :] = ab_re
        a1_s[1:2, :] = ab_im
        pr, pi = ab_re, ab_im
        held = None
        for j in range(SCAN_SEG):
            rows = [jnp.broadcast_to(pr, (SUBLANES, SSM_S)), jnp.broadcast_to(pi, (SUBLANES, SSM_S))]
            if j % 2 == 0:
                held = rows
            else:
                rs = slice((j - 1) * SUBLANES, (j + 1) * SUBLANES)
                apr_s[rs, :] = jnp.concatenate([held[0], rows[0]], axis=0).astype(BF16)
                api_s[rs, :] = jnp.concatenate([held[1], rows[1]], axis=0).astype(BF16)
            if j == SCAN_SEG - 1:
                aseg_s[0:1, :] = pr
                aseg_s[1:2, :] = pi
            pr, pi = _cmul(pr, pi, ab_re, ab_im)
        for t, first in enumerate((False, True)):
            bias_s[t] = jnp.where(_window_valid(bias_s.shape[1], WINDOW, first), 0.0, MASK_VALUE)

    @pl.when(i == 0)
    def _():
        carry_s[...] = jnp.zeros_like(carry_s)
        kprev_s[...] = jnp.zeros_like(kprev_s)
        vprev_s[...] = jnp.zeros_like(vprev_s)

    k0, k1 = _replicate_heads(k_ref[...])
    v0, v1 = _replicate_heads(v_ref[...])
    qs, kreps, vreps, firsts = [], [], [], []
    for blk in range(MIX_TM // WINDOW):
        cur = slice(blk * WINDOW, (blk + 1) * WINDOW)
        if blk == 0:
            prev = [kprev_s[0], kprev_s[1], vprev_s[0], vprev_s[1]]
            firsts.append(i == 0)
        else:
            ps = slice((blk - 1) * WINDOW, blk * WINDOW)
            prev = [k0[ps], k1[ps], v0[ps], v1[ps]]
            firsts.append(False)
        qs.append(q_ref[cur, :])
        kreps.append([jnp.concatenate([prev[0], k0[cur]], axis=0), jnp.concatenate([prev[1], k1[cur]], axis=0)])
        vreps.append([jnp.concatenate([prev[2], v0[cur]], axis=0), jnp.concatenate([prev[3], v1[cur]], axis=0)])
    tail = slice(MIX_TM - WINDOW, MIX_TM)
    kprev_s[0], kprev_s[1] = k0[tail], k1[tail]
    vprev_s[0], vprev_s[1] = v0[tail], v1[tail]
    scores = _attn_scores(qs, kreps)

    half_in, half_st = SSM_W // 2, SSM_S // 2
    chunks = list(range(0, SSM_S, SCAN_LC))
    groups = [g * SCAN_ROWS for g in range(SCAN_GROUPS)]
    nblk = len(qs)
    blk_rows = scores.shape[0] // nblk
    u = u_ref[...]
    ub = u.astype(BF16)
    up = jnp.concatenate([jnp.dot(perm_ref[...], ub[g0:g0 + SCAN_ROWS], preferred_element_type=F32)
                          for g0 in groups], axis=0).astype(BF16)

    def expand_half(hh):
        uh = up[:, hh * half_in:(hh + 1) * half_in]
        bur_s[:, hh * half_st:(hh + 1) * half_st] = jnp.dot(uh, wbre_s[hh], preferred_element_type=F32)
        bui_s[:, hh * half_st:(hh + 1) * half_st] = jnp.dot(uh, wbim_s[hh], preferred_element_type=F32)

    def softmax_blocks(lo, hi):
        no_prev = (i == 0).astype(jnp.int32)
        bias = jnp.concatenate([bias_s[no_prev] if blk == 0 else bias_s[0]
                                for blk in range(lo, hi) for _ in range(N_HEADS)], axis=0)
        return _attn_softmax(scores[lo * blk_rows:hi * blk_rows], sinks_ref, WINDOW, firsts[lo:hi], bias)

    def scan_chunk(c0):
        cs = slice(c0, c0 + SCAN_LC)
        ar = jnp.broadcast_to(a1_s[0:1, cs], (SUBLANES, SCAN_LC))
        ai = jnp.broadcast_to(a1_s[1:2, cs], (SUBLANES, SCAN_LC))
        state = [(bur_s[g0:g0 + SUBLANES, cs], bui_s[g0:g0 + SUBLANES, cs]) for g0 in groups]
        for j in range(1, SCAN_SEG):
            for g, g0 in enumerate(groups):
                rs = slice(g0 + j * SUBLANES, g0 + (j + 1) * SUBLANES)
                xr, xi = state[g]
                xr, xi = (ar * xr - ai * xi + bur_s[rs, cs], ar * xi + ai * xr + bui_s[rs, cs])
                bur_s[rs, cs] = xr
                bui_s[rs, cs] = xi
                state[g] = (xr, xi)
        return state[0][0]

    mid = nblk // 2
    expand_half(0)
    probs_a = softmax_blocks(0, mid)
    expand_half(1)
    probs_b = softmax_blocks(mid, nblk)
    quarter = len(chunks) // 4
    scan_group = lambda g: [scan_chunk(c0) for c0 in chunks[g * quarter:(g + 1) * quarter]]
    scan_group(0)
    scan_group(1)
    scan_group(2)
    scan_group(3)

    sr, si = aseg_s[0:1, :], aseg_s[1:2, :]
    cr, ci = carry_s[0:1, :], carry_s[1:2, :]
    for g, g0 in enumerate(groups):
        last = g0 + SCAN_ROWS - SUBLANES
        for s in range(SUBLANES):
            k = g * SUBLANES + s
            cin_s[0, k:k + 1, :] = cr
            cin_s[1, k:k + 1, :] = ci
            pr, pi = _cmul(sr, si, cr, ci)
            cr = pr + bur_s[last + s:last + s + 1, :]
            ci = pi + bui_s[last + s:last + s + 1, :]
    carry_s[0:1, :] = cr
    carry_s[1:2, :] = ci
    pack = 2 * SUBLANES

    def fix_chunk(c0):
        cs = slice(c0, c0 + SCAN_LC)
        col = (c0 // half_st) * SSM_S + c0 % half_st
        for g, g0 in enumerate(groups):
            seg = slice(g * SUBLANES, (g + 1) * SUBLANES)
            cinr = jnp.concatenate([cin_s[0, seg, cs]] * 2, axis=0).astype(BF16)
            cini = jnp.concatenate([cin_s[1, seg, cs]] * 2, axis=0).astype(BF16)
            for r0 in range(0, SCAN_ROWS, pack):
                rs, ts = slice(g0 + r0, g0 + r0 + pack), slice(r0, r0 + pack)
                pr, pi = apr_s[ts, cs], api_s[ts, cs]
                xb_s[rs, col:col + SCAN_LC] = bur_s[rs, cs].astype(BF16) + pr * cinr - pi * cini
                xb_s[rs, col + half_st:col + half_st + SCAN_LC] = (bui_s[rs, cs].astype(BF16)
                                                                   + pr * cini + pi * cinr)

    def project_half(hh):
        return jnp.dot(xb_s[:, hh * SSM_S:(hh + 1) * SSM_S], ccat_ref[hh], preferred_element_type=F32)

    per_half = len(chunks) // 2
    for c0 in chunks[:per_half]:
        fix_chunk(c0)
    yp0 = project_half(0)
    for c0 in chunks[per_half:]:
        fix_chunk(c0)
    yp = jnp.concatenate([yp0, project_half(1)], axis=1)
    ypb = yp.astype(BF16)
    y_ssm = jnp.concatenate([jnp.dot(permt_ref[...], ypb[g0:g0 + SCAN_ROWS], preferred_element_type=F32)
                             for g0 in groups], axis=0) + d_ref[...] * u

    attn = _attn_values(probs_a, vreps[0:mid], WINDOW) + _attn_values(probs_b, vreps[mid:nblk], WINDOW)
    attn_gated = _gated_attn(jnp.concatenate(attn, axis=0).astype(BF16), sga_ref[...], wao_ref)
    x2_ref[...] = _gated_merge(x1_ref[...], attn_gated, y_ssm, sgs_ref[...],
                               wglu_ref, bglu_ref, wso_ref, wout_ref)

    @pl.when(i == nt - 1)
    def _():
        sre_ref[...] = carry_s[0:1, :]
        sim_ref[...] = carry_s[1:2, :]


def _mix_prompt(fr, p, batch, seq):
    x1, q, k, v, u, sga, sgs = fr
    tm = MIX_TM
    nt = seq // tm
    row = lambda w: pl.BlockSpec((tm, w), lambda b, i: (b * nt + i, 0))
    consts = [p['wao'], p['wglu'], p['bglu'], p['wso'], p['wout'],
              p['are'], p['aim'], p['ldt'], p['wbre'], p['wbim'], p['ccat'], p['dskip'], p['perm'], p['permt']]
    st_spec = pl.BlockSpec((None, 1, SSM_S), lambda b, i: (b, 0, 0))
    scratch = [pltpu.VMEM((SCAN_ROWS, SSM_S), BF16), pltpu.VMEM((SCAN_ROWS, SSM_S), BF16),
               pltpu.VMEM((SUBLANES, SSM_S), F32), pltpu.VMEM((SUBLANES, SSM_S), F32),
               pltpu.VMEM((SUBLANES, SSM_S), F32), pltpu.VMEM((2, SCAN_GROUPS * SUBLANES, SSM_S), F32),
               pltpu.VMEM((tm, SSM_S), F32), pltpu.VMEM((tm, SSM_S), F32), pltpu.VMEM((tm, 2 * SSM_S), BF16),
               pltpu.VMEM((N_KV, WINDOW, LANES), BF16), pltpu.VMEM((N_KV, WINDOW, LANES), BF16),
               pltpu.VMEM((2, WINDOW, 2 * WINDOW), F32)]
    return pl.pallas_call(
        _mix_prompt_kernel,
        grid=(batch, nt),
        in_specs=[row(D_MODEL), row(ATTN_W), row(KV_W), row(KV_W), row(SSM_W), row(D_MODEL), row(D_MODEL),
                  pl.BlockSpec(memory_space=pltpu.SMEM)] + [_const_spec(c.shape) for c in consts],
        out_specs=[row(D_MODEL), st_spec, st_spec],
        out_shape=[jax.ShapeDtypeStruct((batch * seq, D_MODEL), F32),
                   jax.ShapeDtypeStruct((batch, 1, SSM_S), F32),
                   jax.ShapeDtypeStruct((batch, 1, SSM_S), F32)],
        scratch_shapes=scratch,
        compiler_params=pltpu.CompilerParams(dimension_semantics=("arbitrary", "arbitrary"),
                                             vmem_limit_bytes=VMEM_LIMIT),
        name="mix_prompt",
    )(x1, q, k, v, u, sga, sgs, p['sinks'], *consts)


def _attn_sample_kernel(q_ref, kn_ref, vn_ref, ckt_ref, cvt_ref, sinks_ref, o_ref, nkt_ref, nvt_ref,
                        q8_s, knp_s, vnp_s):
    steps = q_ref.shape[0]
    shift = WINDOW - steps
    keep = lax.broadcasted_iota(jnp.int32, (HEAD_DIM, WINDOW), 1) < shift
    q8_s[...] = jnp.zeros_like(q8_s)
    knp_s[...] = jnp.zeros_like(knp_s)
    vnp_s[...] = jnp.zeros_like(vnp_s)
    qs, kreps, vreps = [], [], []
    for s in range(SAMPLE_BS):
        for t in range(steps):
            q8_s[s, t:t + 1, :] = q_ref[t, s:s + 1, :]
            knp_s[s, t:t + 1, :] = kn_ref[t, s:s + 1, :]
            vnp_s[s, t:t + 1, :] = vn_ref[t, s:s + 1, :]
        news = [knp_s[s].T, vnp_s[s].T]
        ends = [pltpu.roll(n, shift, axis=1) for n in news]
        krep, vrep = [], []
        for kv in range(N_KV):
            hs = slice(kv * HEAD_DIM, (kv + 1) * HEAD_DIM)
            for cache_ref, out_ref, new, end, reps in ((ckt_ref, nkt_ref, news[0], ends[0], krep),
                                                       (cvt_ref, nvt_ref, news[1], ends[1], vrep)):
                old = cache_ref[s, kv]
                out_ref[s, kv] = jnp.where(keep, pltpu.roll(old, shift, axis=1), end[hs])
                both = jnp.concatenate([old, new[hs]], axis=1).astype(BF16)
                reps.append(jnp.concatenate([both, both], axis=0))
        qs.append(q8_s[s])
        kreps.append(krep)
        vreps.append(vrep)
    probs = _attn_softmax(_attn_scores(qs, kreps, keys_on_lanes=True), sinks_ref, SUBLANES, [False] * SAMPLE_BS)
    outs = _attn_values(probs, vreps, SUBLANES, keys_on_lanes=True)
    for s in range(SAMPLE_BS):
        for t in range(steps):
            o_ref[t, s:s + 1, :] = outs[s][t:t + 1, :]


def _attn_sample(q, kn, vn, ckt, cvt, sinks, steps, nseq):
    bs = SAMPLE_BS
    blk3 = lambda w: pl.BlockSpec((steps, bs, w), lambda i: (0, i, 0))
    cache = pl.BlockSpec((bs, N_KV, HEAD_DIM, WINDOW), lambda i: (i, 0, 0, 0))
    cache_shape = jax.ShapeDtypeStruct((nseq, N_KV, HEAD_DIM, WINDOW), F32)
    return pl.pallas_call(
        _attn_sample_kernel,
        grid=(nseq // bs,),
        in_specs=[blk3(ATTN_W), blk3(KV_W), blk3(KV_W), cache, cache, pl.BlockSpec(memory_space=pltpu.SMEM)],
        out_specs=[blk3(ATTN_W), cache, cache],
        out_shape=[jax.ShapeDtypeStruct((steps, nseq, ATTN_W), F32), cache_shape, cache_shape],
        scratch_shapes=[pltpu.VMEM((bs, SUBLANES, ATTN_W), F32), pltpu.VMEM((bs, LANES, KV_W), F32),
                        pltpu.VMEM((bs, LANES, KV_W), F32)],
        compiler_params=pltpu.CompilerParams(dimension_semantics=("arbitrary",)),
        name="attn_sample",
    )(q.reshape(steps, nseq, ATTN_W), kn.reshape(steps, nseq, KV_W), vn.reshape(steps, nseq, KV_W),
      ckt, cvt, sinks)


def _mix_sample_kernel(x1_ref, attn_ref, u_ref, sga_ref, sgs_ref, x0r_ref, x0i_ref,
                       wao_ref, wglu_ref, bglu_ref, wso_ref, wout_ref,
                       are_ref, aim_ref, ldt_ref, wbre_s, wbim_s, ccat_ref, d_ref,
                       x2_ref, sre_ref, sim_ref,
                       bur_s, bui_s):
    nseq = x0r_ref.shape[0]
    steps = x1_ref.shape[0] // nseq
    ab_re, ab_im, _, _ = _ssm_discretize(are_ref, aim_ref, ldt_ref)
    u = u_ref[...]
    _expand_state(u.astype(BF16), wbre_s, wbim_s, bur_s, bui_s)
    for c0 in range(0, SSM_S, LANES):
        cs = slice(c0, c0 + LANES)
        ar, ai = ab_re[:, cs], ab_im[:, cs]
        xr, xi = x0r_ref[:, cs], x0i_ref[:, cs]
        for t in range(steps):
            rs = slice(t * nseq, (t + 1) * nseq)
            nr = ar * xr - ai * xi + bur_s[rs, cs]
            ni = ar * xi + ai * xr + bui_s[rs, cs]
            bur_s[rs, cs] = nr
            bui_s[rs, cs] = ni
            xr, xi = nr, ni
        sre_ref[:, cs] = xr
        sim_ref[:, cs] = xi
    y_ssm = _project_state(bur_s, bui_s, ccat_ref) + d_ref[...] * u
    attn_gated = _gated_attn(attn_ref[...].astype(BF16), sga_ref[...], wao_ref)
    x2_ref[...] = _gated_merge(x1_ref[...], attn_gated, y_ssm, sgs_ref[...],
                               wglu_ref, bglu_ref, wso_ref, wout_ref)


def _mix_sample(x1, attn, u, sga, sgs, x0r, x0i, p):
    n, nseq = x1.shape[0], x0r.shape[0]
    args = [x1, attn, u, sga, sgs, x0r, x0i, p['wao'], p['wglu'], p['bglu'], p['wso'], p['wout'],
            p['are'], p['aim'], p['ldt'], p['wbre'], p['wbim'], p['ccat'], p['dskip']]
    return pl.pallas_call(
        _mix_sample_kernel,
        grid=(1,),
        in_specs=[_const_spec(a.shape) for a in args],
        out_specs=[pl.BlockSpec((n, D_MODEL), lambda i: (0, 0)), pl.BlockSpec((nseq, SSM_S), lambda i: (0, 0)),
                   pl.BlockSpec((nseq, SSM_S), lambda i: (0, 0))],
        out_shape=[jax.ShapeDtypeStruct((n, D_MODEL), F32),
                   jax.ShapeDtypeStruct((nseq, SSM_S), F32),
                   jax.ShapeDtypeStruct((nseq, SSM_S), F32)],
        scratch_shapes=[pltpu.VMEM((n, SSM_S), F32), pltpu.VMEM((n, SSM_S), F32)],
        compiler_params=pltpu.CompilerParams(dimension_semantics=("arbitrary",),
                                             vmem_limit_bytes=VMEM_LIMIT),
        name="mix_sample",
    )(*args)


def _rope_tables(base_pos, off_pos):
    half = HEAD_DIM // 2
    lane = np.arange(LANES)
    inv = ROPE_THETA ** (-2.0 * (lane % half) / HEAD_DIM)
    first = (lane % HEAD_DIM) < half
    ab = np.asarray(base_pos, np.float64)[:, None] * inv[None, :]
    ao = np.asarray(off_pos, np.float64)[:, None] * inv[None, :]
    base = np.stack([np.cos(ab), np.sin(ab)], axis=1)
    co, so = np.cos(ao), np.sin(ao)
    lo = np.where(first, -1.0, 0.0)
    hi = np.where(first, 0.0, 1.0)
    off = np.stack([co, so, co * lo, so * lo, co * hi, so * hi], axis=0)
    return jnp.asarray(base, F32), jnp.asarray(off, F32)


def _block_diag_halves(m):
    g, r, c = m.shape
    hg = g // 2
    strip = jnp.transpose(m.reshape(2, hg, r, c), (0, 2, 1, 3)).reshape(2, 1, r, hg * c)
    full = jnp.broadcast_to(strip, (2, hg, r, hg * c)).reshape(2, hg * r, hg * c)
    mask = jnp.asarray(np.kron(np.eye(hg), np.ones((r, c))), m.dtype)
    return full * mask


def _layout_params(ffn1_norm, ffn1_w1, ffn1_w3, ffn1_w2, mix_norm, w_in, q_norm, k_norm, attn_sinks,
                   w_attn_out, ssm_a_re, ssm_a_im, ssm_log_dt, ssm_b_re, ssm_b_im, ssm_c_re, ssm_c_im,
                   ssm_d, w_glu, b_glu, w_ssm_out, w_out, ffn2_norm, ffn2_w1, ffn2_w3, ffn2_w2):
    l = 0
    row = lambda a: a.reshape(1, -1).astype(F32)
    head_ones = jnp.asarray(np.kron(np.eye(LANES // HEAD_DIM), np.ones((HEAD_DIM, HEAD_DIM))), F32)
    r = np.arange(SCAN_ROWS)
    tok = (r % SUBLANES) * SCAN_SEG + r // SUBLANES
    perm = jnp.asarray(tok[:, None] == np.arange(SCAN_ROWS)[None, :], BF16)
    c_cat = jnp.concatenate([_block_diag_halves(jnp.swapaxes(ssm_c_re[l], 1, 2)),
                             -_block_diag_halves(jnp.swapaxes(ssm_c_im[l], 1, 2))], axis=1)
    return dict(
        n1=row(ffn1_norm[l]), w1a=ffn1_w1[l].astype(BF16), w3a=ffn1_w3[l].astype(BF16), w2a=ffn1_w2[l].astype(BF16),
        nm=row(mix_norm[l]), win=w_in[l].astype(BF16),
        qn=row(jnp.tile(q_norm[l], LANES // HEAD_DIM)), kn=row(jnp.tile(k_norm[l], LANES // HEAD_DIM)),
        ones=head_ones.astype(BF16), sinks=attn_sinks[l].astype(F32),
        bglu=row(b_glu[l]),
        are=row(ssm_a_re[l]), aim=row(ssm_a_im[l]), ldt=row(jnp.repeat(ssm_log_dt[l], SSM_N)),
        bbre=_block_diag_halves(jnp.swapaxes(ssm_b_re[l], 1, 2).astype(F32)),
        bbim=_block_diag_halves(jnp.swapaxes(ssm_b_im[l], 1, 2).astype(F32)),
        ccat=c_cat.astype(BF16), dskip=row(ssm_d[l]), perm=perm, permt=perm.T,
        n2=row(ffn2_norm[l]),
    )


def kernel(x_prompt, x_sample, cache_k, cache_v, state_ssm_re, state_ssm_im, ffn1_norm, ffn1_w1, ffn1_w3, ffn1_w2, mix_norm, w_in, q_norm, k_norm, attn_sinks, w_attn_out, ssm_a_re, ssm_a_im, ssm_log_dt, ssm_b_re, ssm_b_im, ssm_c_re, ssm_c_im, ssm_d, w_glu, b_glu, w_ssm_out, w_out, ffn2_norm, ffn2_w1, ffn2_w3, ffn2_w2):
    assert ffn1_norm.shape[0] == 1, "single trunk layer"
    batch, seq, _ = x_prompt.shape
    nseq, steps, _ = x_sample.shape
    p = _layout_params(ffn1_norm, ffn1_w1, ffn1_w3, ffn1_w2, mix_norm, w_in, q_norm, k_norm, attn_sinks,
                       w_attn_out, ssm_a_re, ssm_a_im, ssm_log_dt, ssm_b_re, ssm_b_im, ssm_c_re, ssm_c_im,
                       ssm_d, w_glu, b_glu, w_ssm_out, w_out, ffn2_norm, ffn2_w1, ffn2_w3, ffn2_w2)

    p['wbre'], p['wbim'] = _ssm_weights(p)

    tile0 = np.arange(batch * seq // FRONT_TM) * FRONT_TM
    tabs_p = _rope_tables(tile0 % seq, np.arange(FRONT_TM))
    later = dict(wao=w_attn_out, wglu=w_glu, wso=w_ssm_out, wout=w_out, w1b=ffn2_w1, w3b=ffn2_w3, w2b=ffn2_w2)
    fr = _front(x_prompt.reshape(batch * seq, D_MODEL), tabs_p, p, BF16, cast=list(later.values()))
    p.update(zip(later, fr[7:]))
    fr = fr[:7]
    x2p, re_p, im_p = _mix_prompt(fr, p, batch, seq)
    y_prompt = _ffn(x2p, p).reshape(batch, seq, D_MODEL)
    tail_kv = lambda a: a.reshape(batch, seq, KV_W)[:, seq - WINDOW:].reshape(batch, WINDOW, N_KV, HEAD_DIM)
    k_p, v_p = tail_kv(fr[2]), tail_kv(fr[3])

    xs = jnp.swapaxes(x_sample, 0, 1).reshape(steps * nseq, D_MODEL)
    tile0 = np.arange(steps * nseq // FRONT_TM) * FRONT_TM
    tabs_s = _rope_tables(PAST_LEN + tile0 // nseq, np.arange(FRONT_TM) // nseq)
    x1s, qs, ks, vs, us, sgas, sgss = _front(xs, tabs_s, p, F32)
    to_lanes = lambda c: jnp.transpose(c[0], (0, 2, 3, 1))
    attn_s, nkt, nvt = _attn_sample(qs, ks, vs, to_lanes(cache_k), to_lanes(cache_v), p['sinks'], steps, nseq)
    x2s, re_s, im_s = _mix_sample(x1s, attn_s.reshape(steps * nseq, ATTN_W), us, sgas, sgss,
                                  state_ssm_re[0].reshape(nseq, SSM_S), state_ssm_im[0].reshape(nseq, SSM_S), p)
    y_sample = jnp.swapaxes(_ffn_stream(x2s, p).reshape(steps, nseq, D_MODEL), 0, 1)
    k_s, v_s = jnp.transpose(nkt, (0, 3, 1, 2)), jnp.transpose(nvt, (0, 3, 1, 2))

    st = lambda a, n: a.reshape(1, n, SSM_G, SSM_N)
    return (y_prompt, y_sample, k_p[None], v_p[None], st(re_p, batch), st(im_p, batch),
            k_s[None], v_s[None], st(re_s, nseq), st(im_s, nseq))
```
